```python
import numpy as np
import jax, jax.numpy as jnp
from jax import lax

D_MODEL = 1024
BATCH = 8
SEQ = 4096
DEPTH = 2

N_MIXERS = 2
N_HEADS = 16
HEAD_DIM = D_MODEL // N_HEADS
Q_BLOCK = 128
NSA_KV_GROUPS = 2
NSA_HPG = N_HEADS // NSA_KV_GROUPS
NSA_KV_W = NSA_KV_GROUPS * HEAD_DIM
CMP_LEN = 32
CMP_STRIDE = 16
CMP_HIDDEN = 2 * HEAD_DIM
SLC_LEN = 64
SLC_TOP_N = 16
WINDOW = 512
NSA_IN_W = D_MODEL + 6 * NSA_KV_W + 3 * N_HEADS
ROPE_THETA = 10000.0
D_FF = 2816
CONV_W = 3
RMS_EPS = 1e-6
NEG = -1e30
FORCE_BONUS = 1e4

kernel_name = "hybrid_stickbreak_nsa_convffn"


def rmsnorm(x, g):
    xf = x.astype(jnp.float32)
    y = xf * lax.rsqrt(jnp.mean(xf * xf, axis=-1, keepdims=True) + RMS_EPS)
    return (y * g.astype(jnp.float32)).astype(x.dtype)


def rope(x, pos):
    half = HEAD_DIM // 2
    inv = ROPE_THETA ** (-jnp.arange(half, dtype=jnp.float32) / half)
    ang = pos.astype(jnp.float32)[..., None] * inv
    cos = jnp.cos(ang)[:, :, None, :]
    sin = jnp.sin(ang)[:, :, None, :]
    xf = x.astype(jnp.float32)
    x1, x2 = xf[..., :half], xf[..., half:]
    return jnp.concatenate([x1 * cos - x2 * sin, x2 * cos + x1 * sin], axis=-1).astype(x.dtype)


def stick_breaking_attention(h, w_in, w_out):
    B, S, _ = h.shape
    nb = S // Q_BLOCK
    scale = HEAD_DIM ** -0.5
    q, k, v = jnp.split(h @ w_in, 3, axis=-1)
    q, k, v = [t.reshape(B, S, N_HEADS, HEAD_DIM).transpose(0, 2, 1, 3) for t in (q, k, v)]
    q_blocks = q.reshape(B, N_HEADS, nb, Q_BLOCK, HEAD_DIM).transpose(2, 0, 1, 3, 4)
    key_pos = jnp.arange(S)

    def block(args):
        qb, q0 = args
        z = jnp.einsum('bhqd,bhkd->bhqk', qb, k).astype(jnp.float32) * scale
        qpos = q0 + jnp.arange(Q_BLOCK)
        strict = key_pos[None, :] < qpos[:, None]
        log_beta = jax.nn.log_sigmoid(z)
        log_1m = jnp.where(strict, jax.nn.log_sigmoid(-z), 0.0)
        after = lax.cumsum(log_1m, axis=3, reverse=True) - log_1m
        a = jnp.where(strict, jnp.exp(log_beta + after), 0.0)
        return jnp.einsum('bhqk,bhkd->bhqd', a.astype(v.dtype), v)

    o = lax.map(block, (q_blocks, jnp.arange(nb, dtype=jnp.int32) * Q_BLOCK))
    o = o.transpose(1, 0, 3, 2, 4).reshape(B, S, D_MODEL)
    return o @ w_out


def native_sparse_attention(h, pos, w_in, cmp_pos_k, cmp_pos_v, cmp_k_w1, cmp_k_w2,
                            cmp_v_w1, cmp_v_w2, w_out):
    B, S, _ = h.shape
    G, HG, dh = NSA_KV_GROUPS, NSA_HPG, HEAD_DIM
    nb = S // Q_BLOCK
    n_cmp = (S - CMP_LEN) // CMP_STRIDE + 1
    n_slc = S // SLC_LEN
    top_n = min(SLC_TOP_N, n_slc)
    scale = dh ** -0.5

    splits = np.cumsum([D_MODEL] + [NSA_KV_W] * 6).tolist()
    q, kc, vc, ks, vs, kw, vw, gates = jnp.split(h @ w_in, splits, axis=-1)
    q = rope(q.reshape(B, S, N_HEADS, dh), pos)
    kvh = lambda t: t.reshape(B, S, G, dh)
    ks = rope(kvh(ks), pos)
    kw = rope(kvh(kw), pos)
    vs, vw = kvh(vs), kvh(vw)

    idx = np.arange(n_cmp)[:, None] * CMP_STRIDE + np.arange(CMP_LEN)[None, :]
    cmp_end = idx[:, -1]

    def compress(t, pe, w1, w2):
        blk = kvh(t)[:, idx] + pe[None, None, :, None, :]
        blk = blk.transpose(0, 1, 3, 2, 4).reshape(B, n_cmp, G, CMP_LEN * dh)
        return jax.nn.gelu(blk @ w1) @ w2

    kc = rope(compress(kc, cmp_pos_k, cmp_k_w1, cmp_k_w2), pos[:, cmp_end]).transpose(0, 2, 1, 3)
    vc = compress(vc, cmp_pos_v, cmp_v_w1, cmp_v_w2).transpose(0, 2, 1, 3)
    cmp_end_j = jnp.asarray(cmp_end)

    c0 = np.arange(n_cmp)[:, None] * CMP_STRIDE
    s0 = np.arange(n_slc)[None, :] * SLC_LEN
    overlap = np.clip(np.minimum(c0 + CMP_LEN, s0 + SLC_LEN) - np.maximum(c0, s0), 0, None) / CMP_LEN
    overlap = jnp.asarray(overlap, jnp.float32)

    ks_blocks = ks.transpose(0, 2, 1, 3).reshape(B, G, n_slc, SLC_LEN, dh)
    vs_blocks = vs.transpose(0, 2, 1, 3).reshape(B, G, n_slc, SLC_LEN, dh)
    pad = ((0, 0), (0, 0), (WINDOW, 0), (0, 0))
    kw_pad = jnp.pad(kw.transpose(0, 2, 1, 3), pad)
    vw_pad = jnp.pad(vw.transpose(0, 2, 1, 3), pad)

    q_blocks = q.reshape(B, nb, Q_BLOCK, G, HG, dh).transpose(1, 0, 3, 4, 2, 5)
    g_all = jax.nn.sigmoid(gates.astype(jnp.float32)).reshape(B, nb, Q_BLOCK, G, HG, 3)
    g_blocks = g_all.transpose(1, 0, 3, 4, 2, 5)
    bi = jnp.arange(B)[:, None, None, None]
    gi = jnp.arange(G)[None, :, None, None]
    blk_ids = jnp.arange(n_slc)

    def block(args):
        qb, gb, q0 = args
        qpos = q0 + jnp.arange(Q_BLOCK)
        s_c = jnp.einsum('bghqd,bgcd->bghqc', qb, kc).astype(jnp.float32) * scale
        m_c = cmp_end_j[None, :] <= qpos[:, None]
        p_c = jax.nn.softmax(jnp.where(m_c, s_c, NEG), axis=-1) * m_c
        o_c = jnp.einsum('bghqc,bgcd->bghqd', p_c.astype(vc.dtype), vc)
        imp = jnp.einsum('bghqc,cn->bgqn', p_c, overlap)
        cur = qpos // SLC_LEN
        forced = (blk_ids[None] == 0) | (blk_ids[None] == cur[:, None]) | (blk_ids[None] == cur[:, None] - 1)
        causal_blk = blk_ids[None] * SLC_LEN <= qpos[:, None]
        score = jnp.where(causal_blk, imp + FORCE_BONUS * forced, NEG)
        _, sel = lax.top_k(score, top_n)
        k_sel = ks_blocks[bi, gi, sel]
        v_sel = vs_blocks[bi, gi, sel]
        tok = sel[..., None] * SLC_LEN + jnp.arange(SLC_LEN)
        m_s = tok <= qpos[None, None, :, None, None]
        s_s = jnp.einsum('bghqd,bgqnld->bghqnl', qb, k_sel).astype(jnp.float32) * scale
        p_s = jax.nn.softmax(jnp.where(m_s[:, :, None], s_s, NEG), axis=(-2, -1))
        o_s = jnp.einsum('bghqnl,bgqnld->bghqd', p_s.astype(v_sel.dtype), v_sel)
        k_win = lax.dynamic_slice_in_dim(kw_pad, q0, WINDOW + Q_BLOCK, axis=2)
        v_win = lax.dynamic_slice_in_dim(vw_pad, q0, WINDOW + Q_BLOCK, axis=2)
        kpos = q0 - WINDOW + jnp.arange(WINDOW + Q_BLOCK)
        diff = qpos[:, None] - kpos[None, :]
        m_w = (diff >= 0) & (diff < WINDOW) & (kpos[None, :] >= 0)
        s_w = jnp.einsum('bghqd,bgkd->bghqk', qb, k_win).astype(jnp.float32) * scale
        p_w = jax.nn.softmax(jnp.where(m_w, s_w, NEG), axis=-1)
        o_w = jnp.einsum('bghqk,bgkd->bghqd', p_w.astype(v_win.dtype), v_win)
        out = gb[..., 0:1] * o_c + gb[..., 1:2] * o_s + gb[..., 2:3] * o_w
        return out.astype(qb.dtype)

    o = lax.map(block, (q_blocks, g_blocks, jnp.arange(nb, dtype=jnp.int32) * Q_BLOCK))
    o = o.transpose(1, 0, 4, 2, 3, 5).reshape(B, S, D_MODEL)
    return o @ w_out


def conv_ffn(h, w_up, conv_w, conv_b, w_down):
    S = h.shape[1]
    u = h @ w_up
    up = jnp.pad(u, ((0, 0), (CONV_W - 1, 0), (0, 0)))
    c = conv_b
    for j in range(CONV_W):
        c = c + up[:, j:j + S] * conv_w[j]
    gate, val = jnp.split(c, 2, axis=-1)
    return (jax.nn.silu(gate) * val) @ w_down


def setup_inputs(seed: int = 0) -> dict:
    key = jax.random.key(seed)
    ks = jax.random.split(key, 24)
    n_sba = len(range(0, DEPTH, N_MIXERS))
    n_nsa = len(range(1, DEPTH, N_MIXERS))
    f32 = jnp.float32
    nrm = lambda k, shape, fan_in: jax.random.normal(k, shape, f32) * fan_in ** -0.5
    gain = lambda k, shape: 1.0 + 0.05 * jax.random.normal(k, shape, f32)
    x = jax.random.normal(ks[0], (BATCH, SEQ, D_MODEL), f32)
    offs = jax.random.randint(ks[1], (BATCH, 1), 0, 1024, dtype=jnp.int32)
    positions = (jnp.arange(SEQ, dtype=jnp.int32)[None, :] + offs).astype(jnp.int32)
    return {
        "x": x,
        "positions": positions,
        "norm_mix": gain(ks[2], (DEPTH, D_MODEL)),
        "sba_w_in": nrm(ks[3], (n_sba, D_MODEL, 3 * D_MODEL), D_MODEL),
        "sba_w_out": nrm(ks[4], (n_sba, D_MODEL, D_MODEL), D_MODEL),
        "nsa_w_in": nrm(ks[5], (n_nsa, D_MODEL, NSA_IN_W), D_MODEL),
        "nsa_cmp_pos_k": 0.1 * jax.random.normal(ks[6], (n_nsa, CMP_LEN, HEAD_DIM), f32),
        "nsa_cmp_pos_v": 0.1 * jax.random.normal(ks[7], (n_nsa, CMP_LEN, HEAD_DIM), f32),
        "nsa_cmp_k_w1": nrm(ks[8], (n_nsa, CMP_LEN * HEAD_DIM, CMP_HIDDEN), CMP_LEN * HEAD_DIM),
        "nsa_cmp_k_w2": nrm(ks[9], (n_nsa, CMP_HIDDEN, HEAD_DIM), CMP_HIDDEN),
        "nsa_cmp_v_w1": nrm(ks[10], (n_nsa, CMP_LEN * HEAD_DIM, CMP_HIDDEN), CMP_LEN * HEAD_DIM),
        "nsa_cmp_v_w2": nrm(ks[11], (n_nsa, CMP_HIDDEN, HEAD_DIM), CMP_HIDDEN),
        "nsa_w_out": nrm(ks[12], (n_nsa, D_MODEL, D_MODEL), D_MODEL),
        "norm_ffn": gain(ks[13], (DEPTH, D_MODEL)),
        "ffn_w_up": nrm(ks[14], (DEPTH, D_MODEL, 2 * D_FF), D_MODEL),
        "ffn_conv_w": nrm(ks[15], (DEPTH, CONV_W, 2 * D_FF), CONV_W),
        "ffn_conv_b": 0.02 * jax.random.normal(ks[16], (DEPTH, 2 * D_FF), f32),
        "ffn_w_down": nrm(ks[17], (DEPTH, D_FF, D_MODEL), D_FF),
        "norm_final": gain(ks[18], (D_MODEL,)),
    }


def reference(x, positions, norm_mix, sba_w_in, sba_w_out, nsa_w_in, nsa_cmp_pos_k,
              nsa_cmp_pos_v, nsa_cmp_k_w1, nsa_cmp_k_w2, nsa_cmp_v_w1, nsa_cmp_v_w2,
              nsa_w_out, norm_ffn, ffn_w_up, ffn_conv_w, ffn_conv_b, ffn_w_down, norm_final):
    for i in range(DEPTH):
        hn = rmsnorm(x, norm_mix[i])
        j = i // N_MIXERS
        if i % N_MIXERS == 0:
            mix = stick_breaking_attention(hn, sba_w_in[j], sba_w_out[j])
        else:
            mix = native_sparse_attention(hn, positions, nsa_w_in[j], nsa_cmp_pos_k[j],
                                          nsa_cmp_pos_v[j], nsa_cmp_k_w1[j], nsa_cmp_k_w2[j],
                                          nsa_cmp_v_w1[j], nsa_cmp_v_w2[j], nsa_w_out[j])
        x = x + mix
        x = x + conv_ffn(rmsnorm(x, norm_ffn[i]), ffn_w_up[i], ffn_conv_w[i],
                         ffn_conv_b[i], ffn_w_down[i])
    return rmsnorm(x, norm_final)
```

```python
import functools

import numpy as np
import jax
import jax.numpy as jnp
from jax import lax
from jax.experimental import pallas as pl
from jax.experimental.pallas import tpu as pltpu

D_MODEL = 1024
N_HEADS = 16
HEAD_DIM = 64
HALF = HEAD_DIM // 2
NSA_KV_GROUPS = 2
NSA_HPG = N_HEADS // NSA_KV_GROUPS
CMP_LEN = 32
CMP_STRIDE = 16
CMP_HIDDEN = 2 * HEAD_DIM
SLC_LEN = 64
SLC_TOP_N = 16
WINDOW = 512
ROPE_THETA = 10000.0
D_FF = 2816
RMS_EPS = 1e-6
NEG = -1e30
FORCE_BONUS = 1e4

LANES = 128
VMEM_LIMIT = 56 * 1024 * 1024

F32 = jnp.float32
BF16 = jnp.bfloat16
HIGHEST = lax.Precision.HIGHEST
NT_DIMS = (((1,), (1,)), ((), ()))


def _params(semantics):
    return pltpu.CompilerParams(dimension_semantics=semantics, vmem_limit_bytes=VMEM_LIMIT)


def _rmsnorm(x, g):
    return x * lax.rsqrt(jnp.mean(x * x, axis=-1, keepdims=True) + RMS_EPS) * g


def _const_spec(shape):
    return pl.BlockSpec(shape, lambda *_: (0,) * len(shape))


def _sba_in_proj_kernel(x_ref, g_ref, w_ref, o_ref, *, n_chunk):
    hn = _rmsnorm(x_ref[...], g_ref[...]).astype(w_ref.dtype)
    n = w_ref.shape[1]
    for c in range(n // n_chunk):
        cols = slice(c * n_chunk, (c + 1) * n_chunk)
        y = jnp.dot(hn, w_ref[:, cols], preferred_element_type=F32)
        if c * n_chunk < D_MODEL:
            y = y * (HEAD_DIM ** -0.5)
        o_ref[:, cols] = y.astype(o_ref.dtype)


def _sba_in_proj(x2, g, w, tm=512):
    t, d = x2.shape
    n = w.shape[1]
    return pl.pallas_call(
        functools.partial(_sba_in_proj_kernel, n_chunk=512),
        grid=(t // tm,),
        in_specs=[pl.BlockSpec((tm, d), lambda i: (i, 0)),
                  _const_spec((1, d)),
                  _const_spec((d, n))],
        out_specs=pl.BlockSpec((tm, n), lambda i: (i, 0)),
        out_shape=jax.ShapeDtypeStruct((t, n), BF16),
        compiler_params=_params(("arbitrary",)),
        name="sba_in_proj",
    )(x2, g, w)


def _sba_attn_kernel(q_ref, k_ref, v_ref, uu_ref, o_ref, acc_ref, car_ref, *, tq, kb):
    i = pl.program_id(2)
    q = q_ref[0]
    lane = lax.broadcasted_iota(jnp.int32, (tq, LANES), 1)
    q_heads = (jnp.where(lane < HEAD_DIM, q, jnp.zeros_like(q)),
               jnp.where(lane >= HEAD_DIM, q, jnp.zeros_like(q)))
    acc_ref[...] = jnp.zeros_like(acc_ref)
    car_ref[...] = jnp.zeros_like(car_ref)
    uu = uu_ref[...]
    qpos = i * tq + lax.broadcasted_iota(jnp.int32, (tq, kb), 0)
    kcol = lax.broadcasted_iota(jnp.int32, (tq, kb), 1)

    def tile(j, masked):
        start = pl.multiple_of(j * kb, kb)
        k_t = k_ref[0, pl.ds(start, kb), :]
        v_t = v_ref[0, pl.ds(start, kb), :]
        if masked:
            strict = (kcol + j * kb) < qpos
        for h in range(2):
            z = lax.dot_general(q_heads[h], k_t, NT_DIMS, preferred_element_type=F32)
            sp = jnp.log1p(jnp.exp(-jnp.abs(z)))
            log_beta = jnp.minimum(z, 0.0) - sp
            log_1m = jnp.minimum(-z, 0.0) - sp
            if masked:
                log_1m = jnp.where(strict, log_1m, 0.0)
            hi = log_1m.astype(BF16)
            lo = (log_1m - hi.astype(F32)).astype(BF16)
            r = jnp.dot(jnp.concatenate([hi, lo], axis=1), uu, preferred_element_type=F32)
            car = car_ref[h]
            a = jnp.exp(log_beta + r[:, :kb] + car)
            if masked:
                a = jnp.where(strict, a, 0.0)
            acc_ref[h] += jnp.dot(a.astype(BF16), v_t, preferred_element_type=F32)
            car_ref[h] = car + r[:, kb:]

    n_diag = tq // kb
    for d in range(n_diag):
        tile(i * n_diag + (n_diag - 1 - d), True)

    def body(t, carry):
        tile(i * n_diag - 1 - t, False)
        return carry

    lax.fori_loop(0, i * n_diag, body, 0)
    o_ref[0] = jnp.where(lane < HEAD_DIM, acc_ref[0], acc_ref[1]).astype(o_ref.dtype)


def _sba_attention(qkv, b, s, tq=256, kb=LANES):
    n_pairs = D_MODEL // LANES
    jj = np.arange(kb)
    u = np.concatenate([(jj[:, None] > jj[None, :]).astype(np.float32),
                        np.ones((kb, kb), np.float32)], axis=1)
    uu = jnp.asarray(np.concatenate([u, u], axis=0), BF16)
    return pl.pallas_call(
        functools.partial(_sba_attn_kernel, tq=tq, kb=kb),
        grid=(b, n_pairs, s // tq),
        in_specs=[pl.BlockSpec((1, tq, LANES), lambda bi, p, i: (bi, i, p)),
                  pl.BlockSpec((1, s, LANES), lambda bi, p, i: (bi, 0, n_pairs + p)),
                  pl.BlockSpec((1, s, LANES), lambda bi, p, i: (bi, 0, 2 * n_pairs + p)),
                  _const_spec((2 * kb, 2 * kb))],
        out_specs=pl.BlockSpec((1, tq, LANES), lambda bi, p, i: (bi, i, p)),
        out_shape=jax.ShapeDtypeStruct((b, s, D_MODEL), BF16),
        scratch_shapes=[pltpu.VMEM((2, tq, LANES), F32), pltpu.VMEM((2, tq, LANES), F32)],
        compiler_params=_params(("arbitrary", "arbitrary", "arbitrary")),
        name="sba_attention",
    )(qkv, qkv, qkv, uu)


def _out_proj_kernel(x_ref, o_ref, w_ref, y_ref):
    y_ref[...] = x_ref[...] + jnp.dot(o_ref[...], w_ref[...], preferred_element_type=F32)


def _out_proj_residual(x2, o2, w, tm=512):
    t, d = x2.shape
    return pl.pallas_call(
        _out_proj_kernel,
        grid=(t // tm,),
        in_specs=[pl.BlockSpec((tm, d), lambda i: (i, 0)),
                  pl.BlockSpec((tm, d), lambda i: (i, 0)),
                  _const_spec((d, d))],
        out_specs=pl.BlockSpec((tm, d), lambda i: (i, 0)),
        out_shape=jax.ShapeDtypeStruct((t, d), F32),
        compiler_params=_params(("arbitrary",)),
        name="out_proj_residual",
    )(x2, o2, w)


def _ffn_kernel(x_ref, g_ref, wup_ref, cw_ref, cb_ref, wdn_ref, gf_ref, y_ref,
                carry_ref, sg_ref, sv_ref, *, tm, fc, tiles_per_seq, final_norm):
    @pl.when(pl.program_id(0) % tiles_per_seq == 0)
    def _():
        carry_ref[...] = jnp.zeros_like(carry_ref)

    x = x_ref[...]
    hn = _rmsnorm(x, g_ref[...]).astype(wup_ref.dtype)

    def conv(col0, s_ref):
        cols = slice(col0, col0 + fc)
        u = jnp.dot(hn, wup_ref[:, cols], preferred_element_type=F32)
        s_ref[0:8, :] = carry_ref[:, cols]
        s_ref[8:tm + 8, :] = u
        carry_ref[:, cols] = u[tm - 8:tm, :]
        cw = cw_ref[:, cols]
        c = cb_ref[:, cols] + s_ref[6:tm + 6, :] * cw[0:1]
        c = c + s_ref[7:tm + 7, :] * cw[1:2]
        return c + u * cw[2:3]

    acc = jnp.zeros((tm, D_MODEL), F32)
    for c in range(D_FF // fc):
        gate = conv(c * fc, sg_ref)
        val = conv(D_FF + c * fc, sv_ref)
        act = (gate * jax.nn.sigmoid(gate) * val).astype(wdn_ref.dtype)
        acc = acc + jnp.dot(act, wdn_ref[c * fc:(c + 1) * fc, :], preferred_element_type=F32)
    y = x + acc
    if final_norm:
        y = _rmsnorm(y, gf_ref[...])
    y_ref[...] = y


def _conv_ffn(x2, g, w_up, conv_w, conv_b, w_down, g_final, s, final_norm, tm=256, fc=256):
    t, d = x2.shape
    f2 = w_up.shape[1]
    return pl.pallas_call(
        functools.partial(_ffn_kernel, tm=tm, fc=fc, tiles_per_seq=s // tm, final_norm=final_norm),
        grid=(t // tm,),
        in_specs=[pl.BlockSpec((tm, d), lambda i: (i, 0)),
                  _const_spec((1, d)),
                  _const_spec((d, f2)),
                  _const_spec((3, f2)),
                  _const_spec((1, f2)),
                  _const_spec((D_FF, d)),
                  _const_spec((1, d))],
        out_specs=pl.BlockSpec((tm, d), lambda i: (i, 0)),
        out_shape=jax.ShapeDtypeStruct((t, d), F32),
        scratch_shapes=[pltpu.VMEM((8, f2), F32),
                        pltpu.VMEM((tm + 8, fc), F32),
                        pltpu.VMEM((tm + 8, fc), F32)],
        compiler_params=_params(("arbitrary",)),
        name="conv_ffn",
    )(x2, g, w_up, conv_w, conv_b, w_down, g_final)


NSA_Q_COLS = D_MODEL
NSA_KV_W = NSA_KV_GROUPS * HEAD_DIM
NSA_W_COLS = D_MODEL + 7 * LANES


def _swap_halves(y):
    lane = lax.broadcasted_iota(jnp.int32, y.shape, 1)
    first = (lane % HEAD_DIM) < HALF
    return jnp.where(first, pltpu.roll(y, LANES - HALF, 1), pltpu.roll(y, HALF, 1))


def _nsa_in_proj_kernel(x_ref, g_ref, w_ref, pos_ref, inv_ref, sgn_ref,
                        q_ref, kc_ref, vc_ref, ks_ref, vst_ref, kw_ref, vwt_ref, gt_ref):
    hn = _rmsnorm(x_ref[...], g_ref[...]).astype(w_ref.dtype)
    ang = pos_ref[...].astype(F32) * inv_ref[...]
    cos = jnp.cos(ang)
    sin = jnp.sin(ang) * sgn_ref[...]

    def rope(y):
        return y * cos + _swap_halves(y) * sin

    def split_groups(y, ref):
        for g in range(NSA_KV_GROUPS):
            ref[g] = y[:, g * HEAD_DIM:(g + 1) * HEAD_DIM].astype(ref.dtype)

    def split_groups_t(y, ref):
        yt = y.T
        for g in range(NSA_KV_GROUPS):
            ref[g] = yt[g * HEAD_DIM:(g + 1) * HEAD_DIM, :].astype(ref.dtype)

    q_chunk = 4 * LANES
    for c in range(NSA_Q_COLS // q_chunk):
        y = jnp.dot(hn, w_ref[:, c * q_chunk:(c + 1) * q_chunk], preferred_element_type=F32)
        for l in range(q_chunk // LANES):
            yl = rope(y[:, l * LANES:(l + 1) * LANES]) * (HEAD_DIM ** -0.5)
            q_ref[:, c * q_chunk + l * LANES:c * q_chunk + (l + 1) * LANES] = yl.astype(q_ref.dtype)
    y = jnp.dot(hn, w_ref[:, NSA_Q_COLS:], preferred_element_type=F32)
    part = lambda n: y[:, n * LANES:(n + 1) * LANES]
    split_groups(part(0), kc_ref)
    split_groups(part(1), vc_ref)
    split_groups(rope(part(2)), ks_ref)
    split_groups_t(part(3), vst_ref)
    split_groups(rope(part(4)), kw_ref)
    split_groups_t(part(5), vwt_ref)
    gt_ref[...] = jax.nn.sigmoid(part(6)).T


def _nsa_in_proj(x2, g, w, pos2, inv2, sgn2, tm=512):
    t, d = x2.shape
    G = NSA_KV_GROUPS
    row_g = lambda dt: jax.ShapeDtypeStruct((G, t, HEAD_DIM), dt)
    col_g = lambda dt: jax.ShapeDtypeStruct((G, HEAD_DIM, t), dt)
    row_spec = pl.BlockSpec((G, tm, HEAD_DIM), lambda i: (0, i, 0))
    col_spec = pl.BlockSpec((G, HEAD_DIM, tm), lambda i: (0, 0, i))
    return pl.pallas_call(
        _nsa_in_proj_kernel,
        grid=(t // tm,),
        in_specs=[pl.BlockSpec((tm, d), lambda i: (i, 0)),
                  _const_spec((1, d)),
                  _const_spec((d, NSA_W_COLS)),
                  pl.BlockSpec((tm, 1), lambda i: (i, 0)),
                  _const_spec((1, LANES)),
                  _const_spec((1, LANES))],
        out_specs=[pl.BlockSpec((tm, NSA_Q_COLS), lambda i: (i, 0)),
                   row_spec, row_spec, row_spec, col_spec, row_spec, col_spec,
                   pl.BlockSpec((LANES, tm), lambda i: (0, i))],
        out_shape=[jax.ShapeDtypeStruct((t, NSA_Q_COLS), BF16),
                   row_g(F32), row_g(F32), row_g(BF16), col_g(BF16), row_g(BF16), col_g(BF16),
                   jax.ShapeDtypeStruct((LANES, t), F32)],
        compiler_params=_params(("arbitrary",)),
        name="nsa_in_proj",
    )(x2, g, w, pos2, inv2, sgn2)


def _nsa_compress_kernel(k16_ref, v16_ref, pek_ref, pev_ref, w1k_ref, w2k_ref, w2kr_ref,
                         w1v_ref, w2v_ref, pos_ref, inv_ref, kc_ref, vct_ref):
    half_w = CMP_STRIDE * HEAD_DIM
    nrow = k16_ref.shape[2]

    def hidden(x16_ref, pe_ref, w1_ref):
        x = x16_ref[0, 0]
        y1 = jnp.dot(x, w1_ref[:half_w, :], precision=HIGHEST, preferred_element_type=F32)
        y2 = jnp.dot(x, w1_ref[half_w:, :], precision=HIGHEST, preferred_element_type=F32)
        bias = jnp.dot(pe_ref[...], w1_ref[...], precision=HIGHEST, preferred_element_type=F32)
        return jax.nn.gelu(y1 + pltpu.roll(y2, nrow - 1, 0) + bias[0:1])

    hk = hidden(k16_ref, pek_ref, w1k_ref)
    kc = jnp.dot(hk, w2k_ref[...], precision=HIGHEST, preferred_element_type=F32)
    kc_rot = jnp.dot(hk, w2kr_ref[...], precision=HIGHEST, preferred_element_type=F32)
    ang = pos_ref[0].astype(F32) * inv_ref[...]
    kc_ref[0, 0] = kc * jnp.cos(ang) + kc_rot * jnp.sin(ang)
    hv = hidden(v16_ref, pev_ref, w1v_ref)
    vc = jnp.dot(hv, w2v_ref[...], precision=HIGHEST, preferred_element_type=F32)
    vct_ref[0, 0] = vc.T.astype(vct_ref.dtype)


def _nsa_compress(k16, v16, pek, pev, w1k, w2k, w2kr, w1v, w2v, pos_cmp, inv64):
    G, b, nrow, wide = k16.shape
    x_spec = pl.BlockSpec((1, 1, nrow, wide), lambda bi, g: (g, bi, 0, 0))
    return pl.pallas_call(
        _nsa_compress_kernel,
        grid=(b, G),
        in_specs=[x_spec, x_spec,
                  _const_spec(pek.shape), _const_spec(pev.shape),
                  _const_spec(w1k.shape), _const_spec(w2k.shape), _const_spec(w2kr.shape),
                  _const_spec(w1v.shape), _const_spec(w2v.shape),
                  pl.BlockSpec((1, nrow, 1), lambda bi, g: (bi, 0, 0)),
                  _const_spec((1, HEAD_DIM))],
        out_specs=[pl.BlockSpec((1, 1, nrow, HEAD_DIM), lambda bi, g: (bi, g, 0, 0)),
                   pl.BlockSpec((1, 1, HEAD_DIM, nrow), lambda bi, g: (bi, g, 0, 0))],
        out_shape=[jax.ShapeDtypeStruct((b, G, nrow, HEAD_DIM), F32),
                   jax.ShapeDtypeStruct((b, G, HEAD_DIM, nrow), BF16)],
        compiler_params=_params(("arbitrary", "arbitrary")),
        name="nsa_compress",
    )(k16, v16, pek, pev, w1k, w2k, w2kr, w1v, w2v, pos_cmp, inv64)


def _nsa_attn_kernel(q_ref, kc_ref, vct_ref, ovt_ref, ks_ref, vst_ref, kw_ref, vwt_ref, gt_ref,
                     o_ref, oct_ref, score_ref, selb_ref, *, tq, kb, n_slc):
    g = pl.program_id(1)
    i = pl.program_id(2)
    q0 = i * tq
    qpos = q0 + lax.broadcasted_iota(jnp.int32, (1, tq), 1)
    n_cmp_rows = kc_ref.shape[2]

    kc = kc_ref[0, 0]
    vct = vct_ref[0, 0]
    cmp_end = CMP_STRIDE * lax.broadcasted_iota(jnp.int32, (n_cmp_rows, 1), 0) + (CMP_LEN - 1)
    m_c = cmp_end <= qpos
    p_sum = jnp.zeros((n_cmp_rows, tq), F32)
    for h in range(NSA_HPG):
        qh = q_ref[0, :, h * HEAD_DIM:(h + 1) * HEAD_DIM].astype(F32)
        s = lax.dot_general(kc, qh, NT_DIMS, precision=HIGHEST, preferred_element_type=F32)
        s = jnp.where(m_c, s, NEG)
        e = jnp.exp(s - jnp.max(s, axis=0, keepdims=True))
        p = jnp.where(m_c, e * (1.0 / jnp.sum(e, axis=0, keepdims=True)), 0.0)
        p_sum = p_sum + p
        oct_ref[h] = jnp.dot(vct, p.astype(vct.dtype), preferred_element_type=F32)
    imp = jnp.dot(ovt_ref[...], p_sum, precision=HIGHEST, preferred_element_type=F32)

    blk = lax.broadcasted_iota(jnp.int32, (n_slc, 1), 0)
    cur = qpos // SLC_LEN
    forced = (blk == 0) | (blk == cur) | (blk == cur - 1)
    causal_blk = blk * SLC_LEN <= qpos
    score = jnp.where(causal_blk, imp + FORCE_BONUS * forced.astype(F32), NEG)
    score_ref[...] = score

    def rank_body(m, cnt):
        row = score_ref[pl.ds(m, 1), :]
        ahead = (row > score) | ((row == score) & (m < blk))
        return cnt + jnp.where(ahead, 1.0, 0.0)

    rank = lax.fori_loop(0, n_slc, rank_body, jnp.zeros((n_slc, tq), F32))
    selb_ref[...] = jnp.where(rank < SLC_TOP_N, 0.0, NEG)

    n_chunks = (q0 + tq + kb - 1) // kb
    w_rows = WINDOW + tq
    w_start = pl.multiple_of(jnp.maximum(q0 - WINDOW, 0), LANES)
    w_kpos = w_start + lax.broadcasted_iota(jnp.int32, (w_rows, 1), 0)
    w_diff = qpos - w_kpos
    m_w = (w_diff >= 0) & (w_diff < WINDOW)
    k_rows = lax.broadcasted_iota(jnp.int32, (kb, 1), 0)

    def head_out(h):
        qh = q_ref[0, :, h * HEAD_DIM:(h + 1) * HEAD_DIM]

        def chunk(c, carry):
            m_run, l_run, acc = carry
            start = pl.multiple_of(c * kb, kb)
            k_t = ks_ref[0, 0, pl.ds(start, kb), :]
            vt_t = vst_ref[0, :, pl.ds(start, kb)]
            s = lax.dot_general(k_t, qh, NT_DIMS, preferred_element_type=F32)
            bias = jnp.concatenate(
                [jnp.broadcast_to(selb_ref[pl.ds(c * (kb // SLC_LEN) + r, 1), :], (SLC_LEN, tq))
                 for r in range(kb // SLC_LEN)], axis=0)
            s = jnp.where(start + k_rows <= qpos, s + bias, NEG)
            m_new = jnp.maximum(m_run, jnp.max(s, axis=0, keepdims=True))
            alpha = jnp.exp(m_run - m_new)
            p = jnp.exp(s - m_new)
            l_new = alpha * l_run + jnp.sum(p, axis=0, keepdims=True)
            acc = alpha * acc + jnp.dot(vt_t, p.astype(vt_t.dtype), preferred_element_type=F32)
            return m_new, l_new, acc

        init = (jnp.full((1, tq), NEG, F32), jnp.zeros((1, tq), F32), jnp.zeros((HEAD_DIM, tq), F32))
        _, l_s, acc_s = lax.fori_loop(0, n_chunks, chunk, init)
        o_s = acc_s * (1.0 / l_s)

        k_w = kw_ref[0, 0, pl.ds(w_start, w_rows), :]
        vt_w = vwt_ref[0, :, pl.ds(w_start, w_rows)]
        s = lax.dot_general(k_w, qh, NT_DIMS, preferred_element_type=F32)
        s = jnp.where(m_w, s, NEG)
        e = jnp.exp(s - jnp.max(s, axis=0, keepdims=True))
        o_w = jnp.dot(vt_w, e.astype(vt_w.dtype), preferred_element_type=F32)
        o_w = o_w * (1.0 / jnp.sum(e, axis=0, keepdims=True))

        gate_row = g * (NSA_HPG * 3) + h * 3
        g_c = gt_ref[pl.ds(gate_row, 1), :]
        g_s = gt_ref[pl.ds(gate_row + 1, 1), :]
        g_w = gt_ref[pl.ds(gate_row + 2, 1), :]
        return g_c * oct_ref[h] + g_s * o_s + g_w * o_w

    for hp in range(NSA_HPG // 2):
        pair = jnp.concatenate([head_out(2 * hp), head_out(2 * hp + 1)], axis=0)
        o_ref[0, :, hp * LANES:(hp + 1) * LANES] = pair.T.astype(o_ref.dtype)


def _nsa_attention(q3, kc, vct, ovt, ks, vst, kw, vwt, gt, tq=LANES, kb=256):
    b, s, _ = q3.shape
    G = NSA_KV_GROUPS
    n_slc = s // SLC_LEN
    nq = s // tq
    n_cmp_rows = kc.shape[2]
    gw = NSA_HPG * HEAD_DIM
    kv_row = pl.BlockSpec((1, 1, s, HEAD_DIM), lambda bi, g, i: (g, bi, 0, 0))
    kv_col = pl.BlockSpec((1, HEAD_DIM, s), lambda bi, g, i: (g, 0, bi))
    return pl.pallas_call(
        functools.partial(_nsa_attn_kernel, tq=tq, kb=kb, n_slc=n_slc),
        grid=(b, G, nq),
        in_specs=[pl.BlockSpec((1, tq, gw), lambda bi, g, i: (bi, i, g)),
                  pl.BlockSpec((1, 1, n_cmp_rows, HEAD_DIM), lambda bi, g, i: (bi, g, 0, 0)),
                  pl.BlockSpec((1, 1, HEAD_DIM, n_cmp_rows), lambda bi, g, i: (bi, g, 0, 0)),
                  _const_spec(ovt.shape),
                  kv_row, kv_col, kv_row, kv_col,
                  pl.BlockSpec((LANES, tq), lambda bi, g, i: (0, bi * nq + i))],
        out_specs=pl.BlockSpec((1, tq, gw), lambda bi, g, i: (bi, i, g)),
        out_shape=jax.ShapeDtypeStruct((b, s, D_MODEL), BF16),
        scratch_shapes=[pltpu.VMEM((NSA_HPG, HEAD_DIM, tq), F32),
                        pltpu.VMEM((n_slc, tq), F32),
                        pltpu.VMEM((n_slc, tq), F32)],
        compiler_params=_params(("arbitrary", "arbitrary", "arbitrary")),
        name="nsa_attention",
    )(q3, kc, vct, ovt, ks, vst, kw, vwt, gt)


def _nsa_layer_attention(hx2, norm_g, positions, w_in, pe_k, pe_v, w1k, w2k, w1v, w2v, b, s):
    t = b * s
    G = NSA_KV_GROUPS
    n_gates = 3 * N_HEADS
    w_pad = jnp.pad(w_in, ((0, 0), (0, NSA_W_COLS - w_in.shape[1]))).astype(BF16)
    inv = ROPE_THETA ** (-jnp.arange(HALF, dtype=F32) / HALF)
    inv2 = jnp.tile(inv, LANES // HALF)[None, :]
    sgn2 = jnp.tile(jnp.concatenate([-jnp.ones(HALF, F32), jnp.ones(HALF, F32)]), LANES // HEAD_DIM)[None, :]
    q, kc_tok, vc_tok, ks, vst, kw, vwt, gt = _nsa_in_proj(
        hx2, norm_g, w_pad, positions.reshape(t, 1), inv2, sgn2)
    del n_gates

    nrow = s // CMP_STRIDE
    wide = CMP_STRIDE * HEAD_DIM
    k16 = kc_tok.reshape(G, b, nrow, wide)
    v16 = vc_tok.reshape(G, b, nrow, wide)
    pek = jnp.broadcast_to(pe_k.reshape(1, CMP_LEN * HEAD_DIM), (8, CMP_LEN * HEAD_DIM))
    pev = jnp.broadcast_to(pe_v.reshape(1, CMP_LEN * HEAD_DIM), (8, CMP_LEN * HEAD_DIM))
    w2k_rot = jnp.concatenate([-w2k[:, HALF:], w2k[:, :HALF]], axis=1)
    end_idx = jnp.minimum(jnp.arange(nrow) * CMP_STRIDE + CMP_LEN - 1, s - 1)
    pos_cmp = positions[:, end_idx][:, :, None]
    inv64 = jnp.tile(inv, 2)[None, :]
    kc, vct = _nsa_compress(k16, v16, pek, pev, w1k, w2k, w2k_rot, w1v, w2v, pos_cmp, inv64)

    n_slc = s // SLC_LEN
    c0 = np.arange(nrow)[None, :] * CMP_STRIDE
    s0 = np.arange(n_slc)[:, None] * SLC_LEN
    ovt = np.clip(np.minimum(c0 + CMP_LEN, s0 + SLC_LEN) - np.maximum(c0, s0), 0, None) / CMP_LEN
    ovt[:, (s - CMP_LEN) // CMP_STRIDE + 1:] = 0.0
    ovt = jnp.asarray(ovt, F32)

    ks4 = ks.reshape(G, b, s, HEAD_DIM)
    kw4 = kw.reshape(G, b, s, HEAD_DIM)
    return _nsa_attention(q.reshape(b, s, D_MODEL), kc, vct, ovt, ks4, vst, kw4, vwt, gt)


def kernel(x, positions, norm_mix, sba_w_in, sba_w_out, nsa_w_in, nsa_cmp_pos_k, nsa_cmp_pos_v,
           nsa_cmp_k_w1, nsa_cmp_k_w2, nsa_cmp_v_w1, nsa_cmp_v_w2, nsa_w_out, norm_ffn,
           ffn_w_up, ffn_conv_w, ffn_conv_b, ffn_w_down, norm_final):
    b, s, d = x.shape
    t = b * s
    depth = norm_mix.shape[0]
    x2 = x.reshape(t, d)
    g_final = norm_final.reshape(1, d)
    for layer in range(depth):
        j = layer // 2
        g_mix = norm_mix[layer].reshape(1, d)
        if layer % 2 == 0:
            qkv = _sba_in_proj(x2, g_mix, sba_w_in[j].astype(BF16))
            o = _sba_attention(qkv.reshape(b, s, 3 * d), b, s)
            w_out = sba_w_out[j]
        else:
            o = _nsa_layer_attention(x2, g_mix, positions, nsa_w_in[j], nsa_cmp_pos_k[j],
                                     nsa_cmp_pos_v[j], nsa_cmp_k_w1[j], nsa_cmp_k_w2[j],
                                     nsa_cmp_v_w1[j], nsa_cmp_v_w2[j], b, s)
            w_out = nsa_w_out[j]
        x2 = _out_proj_residual(x2, o.reshape(t, d), w_out.astype(BF16))
        x2 = _conv_ffn(x2, norm_ffn[layer].reshape(1, d), ffn_w_up[layer].astype(BF16),
                       ffn_conv_w[layer], ffn_conv_b[layer].reshape(1, -1),
                       ffn_w_down[layer].astype(BF16), g_final, s,
                       final_norm=(layer == depth - 1))
    return x2.reshape(b, s, d)
```

```python
import functools

import numpy as np
import jax
import jax.numpy as jnp
from jax import lax
from jax.experimental import pallas as pl
from jax.experimental.pallas import tpu as pltpu

D_MODEL = 1024
N_HEADS = 16
HEAD_DIM = 64
HALF = HEAD_DIM // 2
NSA_KV_GROUPS = 2
NSA_HPG = N_HEADS // NSA_KV_GROUPS
CMP_LEN = 32
CMP_STRIDE = 16
CMP_HIDDEN = 2 * HEAD_DIM
SLC_LEN = 64
SLC_TOP_N = 16
WINDOW = 512
ROPE_THETA = 10000.0
D_FF = 2816
RMS_EPS = 1e-6
NEG = -1e30
FORCE_BONUS = 1e4
LOG2E = float(np.log2(np.e))
Q_SCALE = LOG2E * HEAD_DIM ** -0.5

LANES = 128
VMEM_LIMIT = 56 * 1024 * 1024

F32 = jnp.float32
BF16 = jnp.bfloat16
HIGHEST = lax.Precision.HIGHEST
NT_DIMS = (((1,), (1,)), ((), ()))


def _params(semantics):
    return pltpu.CompilerParams(dimension_semantics=semantics, vmem_limit_bytes=VMEM_LIMIT)


def _rmsnorm(x, g):
    return x * lax.rsqrt(jnp.mean(x * x, axis=-1, keepdims=True) + RMS_EPS) * g


def _const_spec(shape):
    return pl.BlockSpec(shape, lambda *_: (0,) * len(shape))


def _sba_in_proj_kernel(x_ref, g_ref, w_ref, o_ref, *, n_chunk):
    hn = _rmsnorm(x_ref[...], g_ref[...]).astype(w_ref.dtype)
    n = w_ref.shape[1]
    for c in range(n // n_chunk):
        cols = slice(c * n_chunk, (c + 1) * n_chunk)
        y = jnp.dot(hn, w_ref[:, cols], preferred_element_type=F32)
        if c * n_chunk < D_MODEL:
            y = y * Q_SCALE
        o_ref[:, cols] = y.astype(o_ref.dtype)


def _sba_in_proj(x2, g, w, tm=512):
    t, d = x2.shape
    n = w.shape[1]
    return pl.pallas_call(
        functools.partial(_sba_in_proj_kernel, n_chunk=512),
        grid=(t // tm,),
        in_specs=[pl.BlockSpec((tm, d), lambda i: (i, 0)),
                  _const_spec((1, d)),
                  _const_spec((d, n))],
        out_specs=pl.BlockSpec((tm, n), lambda i: (i, 0)),
        out_shape=jax.ShapeDtypeStruct((t, n), BF16),
        compiler_params=_params(("arbitrary",)),
        name="sba_in_proj",
    )(x2, g, w)


def _sba_attn_kernel(q_ref, k_ref, v_ref, uu_ref, o_ref, acc_ref, car_ref, hl_ref, lb_ref, a_ref,
                     *, tq, kb, nsub, rc):
    i = pl.program_id(2)
    kt = kb * nsub
    q = q_ref[0]
    lane = lax.broadcasted_iota(jnp.int32, (tq, LANES), 1)
    q_heads = (jnp.where(lane < HEAD_DIM, q, jnp.zeros_like(q)),
               jnp.where(lane >= HEAD_DIM, q, jnp.zeros_like(q)))
    acc_ref[...] = jnp.zeros_like(acc_ref)
    car_ref[...] = jnp.zeros_like(car_ref)
    uu = uu_ref[...]
    rel = (lax.broadcasted_iota(jnp.int32, (rc, kb), 0)
           - lax.broadcasted_iota(jnp.int32, (rc, kb), 1))

    def super_tile(st, diag):
        base = pl.multiple_of(st * kt, kt)
        k_all = k_ref[0, pl.ds(base, kt), :]
        v_all = v_ref[0, pl.ds(base, kt), :]

        def visibility(j, r0):
            if not diag:
                return 2, None
            if j * kb >= r0 + rc - 1:
                return 0, None
            if j * kb + kb - 1 < r0:
                return 2, None
            return 1, rel > (j * kb - r0)

        for h in range(2):
            totals = []
            cums = []
            for j in range(nsub):
                z = lax.dot_general(q_heads[h], k_all[j * kb:(j + 1) * kb], NT_DIMS,
                                    preferred_element_type=F32)
                for r0 in range(0, tq, rc):
                    rows = slice(r0, r0 + rc)
                    kind, strict = visibility(j, r0)
                    if kind == 0:
                        hl_ref[h, j, rows, :] = jnp.zeros((rc, 2 * kb), BF16)
                        continue
                    zc = z[rows]
                    sp = jnp.log2(1.0 + jnp.exp2(-jnp.abs(zc)))
                    lb_ref[h, j, rows, :] = jnp.minimum(zc, 0.0) - sp
                    nl = jnp.maximum(zc, 0.0) + sp
                    if kind == 1:
                        nl = jnp.where(strict, nl, 0.0)
                    hi = nl.astype(BF16)
                    hl_ref[h, j, rows, 0:kb] = hi
                    hl_ref[h, j, rows, kb:2 * kb] = (nl - hi.astype(F32)).astype(BF16)
                r = jnp.dot(hl_ref[h, j], uu, preferred_element_type=F32)
                cums.append(r[:, :kb])
                totals.append(r[:, kb:])
            car = car_ref[h]
            for j in reversed(range(nsub)):
                for r0 in range(0, tq, rc):
                    rows = slice(r0, r0 + rc)
                    kind, strict = visibility(j, r0)
                    if kind == 0:
                        a_ref[h, rows, j * kb:(j + 1) * kb] = jnp.zeros((rc, kb), BF16)
                        continue
                    a = jnp.exp2(lb_ref[h, j, rows, :] - cums[j][rows] - car[rows])
                    if kind == 1:
                        a = jnp.where(strict, a, 0.0)
                    a_ref[h, rows, j * kb:(j + 1) * kb] = a.astype(BF16)
                car = car + totals[j]
            car_ref[h] = car
            acc_ref[h] += jnp.dot(a_ref[h], v_all, preferred_element_type=F32)

    super_tile(i, True)

    def body(t, carry):
        super_tile(i - 1 - t, False)
        return carry

    lax.fori_loop(0, i, body, 0)
    o_ref[0] = jnp.where(lane < HEAD_DIM, acc_ref[0], acc_ref[1]).astype(o_ref.dtype)


def _sba_attention(qkv, b, s, tq=512, kb=LANES, nsub=4, rc=64):
    assert tq == kb * nsub and s % tq == 0 and tq % rc == 0
    n_pairs = D_MODEL // LANES
    jj = np.arange(kb)
    u = np.concatenate([(jj[:, None] > jj[None, :]).astype(np.float32),
                        np.ones((kb, kb), np.float32)], axis=1)
    uu = jnp.asarray(np.concatenate([u, u], axis=0), BF16)
    return pl.pallas_call(
        functools.partial(_sba_attn_kernel, tq=tq, kb=kb, nsub=nsub, rc=rc),
        grid=(b, n_pairs, s // tq),
        in_specs=[pl.BlockSpec((1, tq, LANES), lambda bi, p, i: (bi, i, p)),
                  pl.BlockSpec((1, s, LANES), lambda bi, p, i: (bi, 0, n_pairs + p)),
                  pl.BlockSpec((1, s, LANES), lambda bi, p, i: (bi, 0, 2 * n_pairs + p)),
                  _const_spec((2 * kb, 2 * kb))],
        out_specs=pl.BlockSpec((1, tq, LANES), lambda bi, p, i: (bi, i, p)),
        out_shape=jax.ShapeDtypeStruct((b, s, D_MODEL), BF16),
        scratch_shapes=[pltpu.VMEM((2, tq, LANES), F32),
                        pltpu.VMEM((2, tq, LANES), F32),
                        pltpu.VMEM((2, nsub, tq, 2 * kb), BF16),
                        pltpu.VMEM((2, nsub, tq, kb), F32),
                        pltpu.VMEM((2, tq, nsub * kb), BF16)],
        compiler_params=_params(("arbitrary", "arbitrary", "arbitrary")),
        name="sba_attention",
    )(qkv, qkv, qkv, uu)


def _out_proj_kernel(x_ref, o_ref, w_ref, y_ref):
    y_ref[...] = x_ref[...] + jnp.dot(o_ref[...], w_ref[...], preferred_element_type=F32)


def _out_proj_residual(x2, o2, w, tm=512):
    t, d = x2.shape
    return pl.pallas_call(
        _out_proj_kernel,
        grid=(t // tm,),
        in_specs=[pl.BlockSpec((tm, d), lambda i: (i, 0)),
                  pl.BlockSpec((tm, d), lambda i: (i, 0)),
                  _const_spec((d, d))],
        out_specs=pl.BlockSpec((tm, d), lambda i: (i, 0)),
        out_shape=jax.ShapeDtypeStruct((t, d), F32),
        compiler_params=_params(("arbitrary",)),
        name="out_proj_residual",
    )(x2, o2, w)


def _ffn_kernel(x_ref, g_ref, wup_ref, cw_ref, cb_ref, wdn_ref, gf_ref, y_ref,
                carry_ref, sg_ref, sv_ref, *, tm, fc, tiles_per_seq, final_norm):
    @pl.when(pl.program_id(0) % tiles_per_seq == 0)
    def _():
        carry_ref[...] = jnp.zeros_like(carry_ref)

    x = x_ref[...]
    hn = _rmsnorm(x, g_ref[...]).astype(wup_ref.dtype)

    def conv(col0, s_ref):
        cols = slice(col0, col0 + fc)
        u = jnp.dot(hn, wup_ref[:, cols], preferred_element_type=F32)
        s_ref[0:8, :] = carry_ref[:, cols]
        s_ref[8:tm + 8, :] = u
        carry_ref[:, cols] = u[tm - 8:tm, :]
        cw = cw_ref[:, cols]
        c = cb_ref[:, cols] + s_ref[6:tm + 6, :] * cw[0:1]
        c = c + s_ref[7:tm + 7, :] * cw[1:2]
        return c + u * cw[2:3]

    acc = jnp.zeros((tm, D_MODEL), F32)
    for c in range(D_FF // fc):
        gate = conv(c * fc, sg_ref)
        val = conv(D_FF + c * fc, sv_ref)
        act = (gate * jax.nn.sigmoid(gate) * val).astype(wdn_ref.dtype)
        acc = acc + jnp.dot(act, wdn_ref[c * fc:(c + 1) * fc, :], preferred_element_type=F32)
    y = x + acc
    if final_norm:
        y = _rmsnorm(y, gf_ref[...])
    y_ref[...] = y


def _conv_ffn(x2, g, w_up, conv_w, conv_b, w_down, g_final, s, final_norm, tm=256, fc=256):
    t, d = x2.shape
    f2 = w_up.shape[1]
    return pl.pallas_call(
        functools.partial(_ffn_kernel, tm=tm, fc=fc, tiles_per_seq=s // tm, final_norm=final_norm),
        grid=(t // tm,),
        in_specs=[pl.BlockSpec((tm, d), lambda i: (i, 0)),
                  _const_spec((1, d)),
                  _const_spec((d, f2)),
                  _const_spec((3, f2)),
                  _const_spec((1, f2)),
                  _const_spec((D_FF, d)),
                  _const_spec((1, d))],
        out_specs=pl.BlockSpec((tm, d), lambda i: (i, 0)),
        out_shape=jax.ShapeDtypeStruct((t, d), F32),
        scratch_shapes=[pltpu.VMEM((8, f2), F32),
                        pltpu.VMEM((tm + 8, fc), F32),
                        pltpu.VMEM((tm + 8, fc), F32)],
        compiler_params=_params(("arbitrary",)),
        name="conv_ffn",
    )(x2, g, w_up, conv_w, conv_b, w_down, g_final)


NSA_Q_COLS = D_MODEL
NSA_W_COLS = D_MODEL + 7 * LANES


def _swap_halves(y):
    lane = lax.broadcasted_iota(jnp.int32, y.shape, 1)
    first = (lane % HEAD_DIM) < HALF
    return jnp.where(first, pltpu.roll(y, LANES - HALF, 1), pltpu.roll(y, HALF, 1))


def _nsa_in_proj_kernel(x_ref, g_ref, w_ref, pos_ref, inv_ref, sgn_ref,
                        q_ref, kc_ref, vc_ref, ks_ref, vst_ref, kw_ref, vwt_ref, gt_ref):
    hn = _rmsnorm(x_ref[...], g_ref[...]).astype(w_ref.dtype)
    ang = pos_ref[...].astype(F32) * inv_ref[...]
    cos = jnp.cos(ang)
    sin = jnp.sin(ang) * sgn_ref[...]

    def rope(y):
        return y * cos + _swap_halves(y) * sin

    def split_groups(y, ref):
        for g in range(NSA_KV_GROUPS):
            ref[g] = y[:, g * HEAD_DIM:(g + 1) * HEAD_DIM].astype(ref.dtype)

    def split_groups_t(y, ref):
        yt = y.T
        for g in range(NSA_KV_GROUPS):
            ref[g] = yt[g * HEAD_DIM:(g + 1) * HEAD_DIM, :].astype(ref.dtype)

    q_chunk = 4 * LANES
    for c in range(NSA_Q_COLS // q_chunk):
        y = jnp.dot(hn, w_ref[:, c * q_chunk:(c + 1) * q_chunk], preferred_element_type=F32)
        for l in range(q_chunk // LANES):
            yl = rope(y[:, l * LANES:(l + 1) * LANES]) * Q_SCALE
            q_ref[:, c * q_chunk + l * LANES:c * q_chunk + (l + 1) * LANES] = yl.astype(q_ref.dtype)
    y = jnp.dot(hn, w_ref[:, NSA_Q_COLS:], preferred_element_type=F32)
    part = lambda n: y[:, n * LANES:(n + 1) * LANES]
    split_groups(part(0), kc_ref)
    split_groups(part(1), vc_ref)
    split_groups(rope(part(2)), ks_ref)
    split_groups_t(part(3), vst_ref)
    split_groups(rope(part(4)), kw_ref)
    split_groups_t(part(5), vwt_ref)
    gt_ref[...] = jax.nn.sigmoid(part(6)).T


def _nsa_in_proj(x2, g, w, pos2, inv2, sgn2, tm=512):
    t, d = x2.shape
    G = NSA_KV_GROUPS
    row_g = lambda dt: jax.ShapeDtypeStruct((G, t, HEAD_DIM), dt)
    col_g = lambda dt: jax.ShapeDtypeStruct((G, HEAD_DIM, t), dt)
    row_spec = pl.BlockSpec((G, tm, HEAD_DIM), lambda i: (0, i, 0))
    col_spec = pl.BlockSpec((G, HEAD_DIM, tm), lambda i: (0, 0, i))
    return pl.pallas_call(
        _nsa_in_proj_kernel,
        grid=(t // tm,),
        in_specs=[pl.BlockSpec((tm, d), lambda i: (i, 0)),
                  _const_spec((1, d)),
                  _const_spec((d, NSA_W_COLS)),
                  pl.BlockSpec((tm, 1), lambda i: (i, 0)),
                  _const_spec((1, LANES)),
                  _const_spec((1, LANES))],
        out_specs=[pl.BlockSpec((tm, NSA_Q_COLS), lambda i: (i, 0)),
                   row_spec, row_spec, row_spec, col_spec, row_spec, col_spec,
                   pl.BlockSpec((LANES, tm), lambda i: (0, i))],
        out_shape=[jax.ShapeDtypeStruct((t, NSA_Q_COLS), BF16),
                   row_g(F32), row_g(F32), row_g(BF16), col_g(BF16), row_g(BF16), col_g(BF16),
                   jax.ShapeDtypeStruct((LANES, t), F32)],
        compiler_params=_params(("arbitrary",)),
        name="nsa_in_proj",
    )(x2, g, w, pos2, inv2, sgn2)


def _nsa_compress_kernel(k16_ref, v16_ref, pek_ref, pev_ref, w1k_ref, w2k_ref, w2kr_ref,
                         w1v_ref, w2v_ref, pos_ref, inv_ref, kc_ref, vct_ref):
    half_w = CMP_STRIDE * HEAD_DIM
    nrow = k16_ref.shape[2]

    def hidden(x16_ref, pe_ref, w1_ref):
        x = x16_ref[0, 0]
        y1 = jnp.dot(x, w1_ref[:half_w, :], precision=HIGHEST, preferred_element_type=F32)
        y2 = jnp.dot(x, w1_ref[half_w:, :], precision=HIGHEST, preferred_element_type=F32)
        bias = jnp.dot(pe_ref[...], w1_ref[...], precision=HIGHEST, preferred_element_type=F32)
        return jax.nn.gelu(y1 + pltpu.roll(y2, nrow - 1, 0) + bias[0:1])

    hk = hidden(k16_ref, pek_ref, w1k_ref)
    kc = jnp.dot(hk, w2k_ref[...], precision=HIGHEST, preferred_element_type=F32)
    kc_rot = jnp.dot(hk, w2kr_ref[...], precision=HIGHEST, preferred_element_type=F32)
    ang = pos_ref[0].astype(F32) * inv_ref[...]
    kc_ref[0, 0] = kc * jnp.cos(ang) + kc_rot * jnp.sin(ang)
    hv = hidden(v16_ref, pev_ref, w1v_ref)
    vc = jnp.dot(hv, w2v_ref[...], precision=HIGHEST, preferred_element_type=F32)
    vct_ref[0, 0] = vc.T.astype(vct_ref.dtype)


def _nsa_compress(k16, v16, pek, pev, w1k, w2k, w2kr, w1v, w2v, pos_cmp, inv64):
    G, b, nrow, wide = k16.shape
    x_spec = pl.BlockSpec((1, 1, nrow, wide), lambda bi, g: (g, bi, 0, 0))
    return pl.pallas_call(
        _nsa_compress_kernel,
        grid=(b, G),
        in_specs=[x_spec, x_spec,
                  _const_spec(pek.shape), _const_spec(pev.shape),
                  _const_spec(w1k.shape), _const_spec(w2k.shape), _const_spec(w2kr.shape),
                  _const_spec(w1v.shape), _const_spec(w2v.shape),
                  pl.BlockSpec((1, nrow, 1), lambda bi, g: (bi, 0, 0)),
                  _const_spec((1, HEAD_DIM))],
        out_specs=[pl.BlockSpec((1, 1, nrow, HEAD_DIM), lambda bi, g: (bi, g, 0, 0)),
                   pl.BlockSpec((1, 1, HEAD_DIM, nrow), lambda bi, g: (bi, g, 0, 0))],
        out_shape=[jax.ShapeDtypeStruct((b, G, nrow, HEAD_DIM), F32),
                   jax.ShapeDtypeStruct((b, G, HEAD_DIM, nrow), BF16)],
        compiler_params=_params(("arbitrary", "arbitrary")),
        name="nsa_compress",
    )(k16, v16, pek, pev, w1k, w2k, w2kr, w1v, w2v, pos_cmp, inv64)


def _nsa_attn_kernel(q_ref, kc_ref, vct_ref, ovt_ref, ks_ref, vst_ref, kw_ref, vwt_ref, gt_ref,
                     o_ref, qs_ref, oct_ref, score_ref, selb_ref, m_ref, l_ref, acc_ref,
                     *, tq, kb, n_slc):
    g = pl.program_id(1)
    i = pl.program_id(2)
    q0 = pl.multiple_of(i * tq, tq)
    qlane = lax.broadcasted_iota(jnp.int32, (1, tq), 1)
    qpos = q0 + qlane
    n_cmp_rows = kc_ref.shape[2]
    heads = range(NSA_HPG)
    for h in heads:
        qs_ref[h] = q_ref[0, :, h * HEAD_DIM:(h + 1) * HEAD_DIM]

    def softmax_cols(s):
        m = jnp.max(s, axis=0, keepdims=True)
        p = jnp.exp2(s - m)
        return p, m, jnp.sum(p, axis=0, keepdims=True)

    kc = kc_ref[0, 0]
    vct = vct_ref[0, 0]
    cmp_end = CMP_STRIDE * lax.broadcasted_iota(jnp.int32, (n_cmp_rows, 1), 0) + (CMP_LEN - 1)
    m_c = cmp_end <= qpos
    p_sum = jnp.zeros((n_cmp_rows, tq), F32)
    for h in heads:
        s = lax.dot_general(kc, qs_ref[h].astype(F32), NT_DIMS, precision=HIGHEST,
                            preferred_element_type=F32)
        e, _, l = softmax_cols(jnp.where(m_c, s, NEG))
        p = jnp.where(m_c, e * (1.0 / l), 0.0)
        p_sum = p_sum + p
        oct_ref[h] = jnp.dot(vct, p.astype(vct.dtype), preferred_element_type=F32)
    imp = jnp.dot(ovt_ref[...], p_sum, precision=HIGHEST, preferred_element_type=F32)

    blk = lax.broadcasted_iota(jnp.int32, (n_slc, 1), 0)
    cur = qpos // SLC_LEN
    forced = (blk == 0) | (blk == cur) | (blk == cur - 1)
    causal_blk = blk * SLC_LEN <= qpos
    score = jnp.where(causal_blk, imp + FORCE_BONUS * forced.astype(F32), NEG)
    score_ref[...] = score

    def rank_body(m, cnt):
        row = score_ref[pl.ds(m, 1), :]
        ahead = (row > score) | ((row == score) & (m < blk))
        return cnt + jnp.where(ahead, 1.0, 0.0)

    rank = lax.fori_loop(0, n_slc, rank_body, jnp.zeros((n_slc, tq), F32))
    selb_ref[...] = jnp.where((rank < SLC_TOP_N) & (blk * SLC_LEN < q0), 0.0, NEG)

    k_d = ks_ref[0, 0, pl.ds(q0, tq), :]
    vt_d = vst_ref[0, :, pl.ds(q0, tq)]
    tok_bias = jnp.where(lax.broadcasted_iota(jnp.int32, (tq, 1), 0) <= qlane, 0.0, NEG)
    for h in heads:
        s = lax.dot_general(k_d, qs_ref[h], NT_DIMS, preferred_element_type=F32) + tok_bias
        p, m_ref[h], l_ref[h] = softmax_cols(s)
        acc_ref[h] = jnp.dot(vt_d, p.astype(vt_d.dtype), preferred_element_type=F32)

    blocks_per_chunk = kb // SLC_LEN

    def chunk(c, carry):
        start = pl.multiple_of(c * kb, kb)
        k_t = ks_ref[0, 0, pl.ds(start, kb), :]
        vt_t = vst_ref[0, :, pl.ds(start, kb)]
        bias = jnp.concatenate(
            [jnp.broadcast_to(selb_ref[pl.ds(c * blocks_per_chunk + r, 1), :], (SLC_LEN, tq))
             for r in range(blocks_per_chunk)], axis=0)
        for h in heads:
            s = lax.dot_general(k_t, qs_ref[h], NT_DIMS, preferred_element_type=F32) + bias
            m_old = m_ref[h]
            m_new = jnp.maximum(m_old, jnp.max(s, axis=0, keepdims=True))
            alpha = jnp.exp2(m_old - m_new)
            p = jnp.exp2(s - m_new)
            l_ref[h] = alpha * l_ref[h] + jnp.sum(p, axis=0, keepdims=True)
            acc_ref[h] = alpha * acc_ref[h] + jnp.dot(vt_t, p.astype(vt_t.dtype),
                                                      preferred_element_type=F32)
            m_ref[h] = m_new
        return carry

    lax.fori_loop(0, (q0 + kb - 1) // kb, chunk, 0)

    w_rows = WINDOW + tq
    w_start = pl.multiple_of(jnp.maximum(q0 - WINDOW, 0), LANES)
    w_diff = qpos - (w_start + lax.broadcasted_iota(jnp.int32, (w_rows, 1), 0))
    w_bias = jnp.where((w_diff >= 0) & (w_diff < WINDOW), 0.0, NEG)
    k_w = kw_ref[0, 0, pl.ds(w_start, w_rows), :]
    vt_w = vwt_ref[0, :, pl.ds(w_start, w_rows)]

    def head_out(h):
        s = lax.dot_general(k_w, qs_ref[h], NT_DIMS, preferred_element_type=F32) + w_bias
        p, _, l_w = softmax_cols(s)
        o_w = jnp.dot(vt_w, p.astype(vt_w.dtype), preferred_element_type=F32) * (1.0 / l_w)
        o_s = acc_ref[h] * (1.0 / l_ref[h])
        gate_row = g * (NSA_HPG * 3) + h * 3
        g_c = gt_ref[pl.ds(gate_row, 1), :]
        g_s = gt_ref[pl.ds(gate_row + 1, 1), :]
        g_w = gt_ref[pl.ds(gate_row + 2, 1), :]
        return g_c * oct_ref[h] + g_s * o_s + g_w * o_w

    for hp in range(NSA_HPG // 2):
        pair = jnp.concatenate([head_out(2 * hp), head_out(2 * hp + 1)], axis=0)
        o_ref[0, :, hp * LANES:(hp + 1) * LANES] = pair.T.astype(o_ref.dtype)


def _nsa_attention(q3, kc, vct, ovt, ks, vst, kw, vwt, gt, tq=LANES, kb=256):
    b, s, _ = q3.shape
    G = NSA_KV_GROUPS
    n_slc = s // SLC_LEN
    nq = s // tq
    n_cmp_rows = kc.shape[2]
    gw = NSA_HPG * HEAD_DIM
    assert tq == 2 * SLC_LEN and kb % SLC_LEN == 0 and s % kb == 0 and s >= WINDOW + tq
    kv_row = pl.BlockSpec((1, 1, s, HEAD_DIM), lambda bi, g, i: (g, bi, 0, 0))
    kv_col = pl.BlockSpec((1, HEAD_DIM, s), lambda bi, g, i: (g, 0, bi))
    stat = pltpu.VMEM((NSA_HPG, 1, tq), F32)
    return pl.pallas_call(
        functools.partial(_nsa_attn_kernel, tq=tq, kb=kb, n_slc=n_slc),
        grid=(b, G, nq),
        in_specs=[pl.BlockSpec((1, tq, gw), lambda bi, g, i: (bi, i, g)),
                  pl.BlockSpec((1, 1, n_cmp_rows, HEAD_DIM), lambda bi, g, i: (bi, g, 0, 0)),
                  pl.BlockSpec((1, 1, HEAD_DIM, n_cmp_rows), lambda bi, g, i: (bi, g, 0, 0)),
                  _const_spec(ovt.shape),
                  kv_row, kv_col, kv_row, kv_col,
                  pl.BlockSpec((LANES, tq), lambda bi, g, i: (0, bi * nq + i))],
        out_specs=pl.BlockSpec((1, tq, gw), lambda bi, g, i: (bi, i, g)),
        out_shape=jax.ShapeDtypeStruct((b, s, D_MODEL), BF16),
        scratch_shapes=[pltpu.VMEM((NSA_HPG, tq, HEAD_DIM), BF16),
                        pltpu.VMEM((NSA_HPG, HEAD_DIM, tq), F32),
                        pltpu.VMEM((n_slc, tq), F32),
                        pltpu.VMEM((n_slc, tq), F32),
                        stat, stat,
                        pltpu.VMEM((NSA_HPG, HEAD_DIM, tq), F32)],
        compiler_params=_params(("arbitrary", "arbitrary", "arbitrary")),
        name="nsa_attention",
    )(q3, kc, vct, ovt, ks, vst, kw, vwt, gt)


def _nsa_layer_attention(hx2, norm_g, positions, w_in, pe_k, pe_v, w1k, w2k, w1v, w2v, b, s):
    t = b * s
    G = NSA_KV_GROUPS
    w_pad = jnp.pad(w_in, ((0, 0), (0, NSA_W_COLS - w_in.shape[1]))).astype(BF16)
    inv = ROPE_THETA ** (-jnp.arange(HALF, dtype=F32) / HALF)
    inv2 = jnp.tile(inv, LANES // HALF)[None, :]
    sgn2 = jnp.tile(jnp.concatenate([-jnp.ones(HALF, F32), jnp.ones(HALF, F32)]), LANES // HEAD_DIM)[None, :]
    q, kc_tok, vc_tok, ks, vst, kw, vwt, gt = _nsa_in_proj(
        hx2, norm_g, w_pad, positions.reshape(t, 1), inv2, sgn2)

    nrow = s // CMP_STRIDE
    wide = CMP_STRIDE * HEAD_DIM
    k16 = kc_tok.reshape(G, b, nrow, wide)
    v16 = vc_tok.reshape(G, b, nrow, wide)
    pek = jnp.broadcast_to(pe_k.reshape(1, CMP_LEN * HEAD_DIM), (8, CMP_LEN * HEAD_DIM))
    pev = jnp.broadcast_to(pe_v.reshape(1, CMP_LEN * HEAD_DIM), (8, CMP_LEN * HEAD_DIM))
    w2k_rot = jnp.concatenate([-w2k[:, HALF:], w2k[:, :HALF]], axis=1)
    end_idx = jnp.minimum(jnp.arange(nrow) * CMP_STRIDE + CMP_LEN - 1, s - 1)
    pos_cmp = positions[:, end_idx][:, :, None]
    inv64 = jnp.tile(inv, 2)[None, :]
    kc, vct = _nsa_compress(k16, v16, pek, pev, w1k, w2k, w2k_rot, w1v, w2v, pos_cmp, inv64)

    n_slc = s // SLC_LEN
    c0 = np.arange(nrow)[None, :] * CMP_STRIDE
    s0 = np.arange(n_slc)[:, None] * SLC_LEN
    ovt = np.clip(np.minimum(c0 + CMP_LEN, s0 + SLC_LEN) - np.maximum(c0, s0), 0, None) / CMP_LEN
    ovt[:, (s - CMP_LEN) // CMP_STRIDE + 1:] = 0.0
    ovt = jnp.asarray(ovt, F32)

    ks4 = ks.reshape(G, b, s, HEAD_DIM)
    kw4 = kw.reshape(G, b, s, HEAD_DIM)
    return _nsa_attention(q.reshape(b, s, D_MODEL), kc, vct, ovt, ks4, vst, kw4, vwt, gt)


def kernel(x, positions, norm_mix, sba_w_in, sba_w_out, nsa_w_in, nsa_cmp_pos_k, nsa_cmp_pos_v,
           nsa_cmp_k_w1, nsa_cmp_k_w2, nsa_cmp_v_w1, nsa_cmp_v_w2, nsa_w_out, norm_ffn,
           ffn_w_up, ffn_conv_w, ffn_conv_b, ffn_w_down, norm_final):
    b, s, d = x.shape
    t = b * s
    depth = norm_mix.shape[0]
    x2 = x.reshape(t, d)
    g_final = norm_final.reshape(1, d)
    for layer in range(depth):
        j = layer // 2
        g_mix = norm_mix[layer].reshape(1, d)
        if layer % 2 == 0:
            qkv = _sba_in_proj(x2, g_mix, sba_w_in[j].astype(BF16))
            o = _sba_attention(qkv.reshape(b, s, 3 * d), b, s)
            w_out = sba_w_out[j]
        else:
            o = _nsa_layer_attention(x2, g_mix, positions, nsa_w_in[j], nsa_cmp_pos_k[j],
                                     nsa_cmp_pos_v[j], nsa_cmp_k_w1[j], nsa_cmp_k_w2[j],
                                     nsa_cmp_v_w1[j], nsa_cmp_v_w2[j], b, s)
            w_out = nsa_w_out[j]
        x2 = _out_proj_residual(x2, o.reshape(t, d), w_out.astype(BF16))
        x2 = _conv_ffn(x2, norm_ffn[layer].reshape(1, d), ffn_w_up[layer].astype(BF16),
                       ffn_conv_w[layer], ffn_conv_b[layer].reshape(1, -1),
                       ffn_w_down[layer].astype(BF16), g_final, s,
                       final_norm=(layer == depth - 1))
    return x2.reshape(b, s, d)
```

```python
import functools

import numpy as np
import jax
import jax.numpy as jnp
from jax import lax
from jax.experimental import pallas as pl
from jax.experimental.pallas import tpu as pltpu

D_MODEL = 1024
N_HEADS = 16
HEAD_DIM = 64
HALF = HEAD_DIM // 2
NSA_KV_GROUPS = 2
NSA_HPG = N_HEADS // NSA_KV_GROUPS
CMP_LEN = 32
CMP_STRIDE = 16
CMP_HIDDEN = 2 * HEAD_DIM
SLC_LEN = 64
SLC_SHIFT = SLC_LEN.bit_length() - 1
SLC_TOP_N = 16
WINDOW = 512
ROPE_THETA = 10000.0
D_FF = 2816
RMS_EPS = 1e-6
NEG = -1e30
FORCE_BONUS = 1e4
LOG2E = float(np.log2(np.e))
Q_SCALE = LOG2E * HEAD_DIM ** -0.5

LANES = 128
VMEM_LIMIT = 56 * 1024 * 1024

F32 = jnp.float32
BF16 = jnp.bfloat16
HIGHEST = lax.Precision.HIGHEST
NT_DIMS = (((1,), (1,)), ((), ()))


def _params(semantics):
    return pltpu.CompilerParams(dimension_semantics=semantics, vmem_limit_bytes=VMEM_LIMIT)


def _rmsnorm(x, g):
    return x * lax.rsqrt(jnp.mean(x * x, axis=-1, keepdims=True) + RMS_EPS) * g


def _const_spec(shape):
    return pl.BlockSpec(shape, lambda *_: (0,) * len(shape))


def _split_bf16(x):
    hi = x.astype(BF16)
    return hi, (x - hi.astype(F32)).astype(BF16)


def _sba_in_proj_kernel(x_ref, g_ref, w_ref, o_ref, *, n_chunk):
    hn = _rmsnorm(x_ref[...], g_ref[...]).astype(w_ref.dtype)
    n = w_ref.shape[1]
    for c in range(n // n_chunk):
        cols = slice(c * n_chunk, (c + 1) * n_chunk)
        y = jnp.dot(hn, w_ref[:, cols], preferred_element_type=F32)
        if c * n_chunk < D_MODEL:
            y = y * Q_SCALE
        o_ref[:, cols] = y.astype(o_ref.dtype)


def _sba_in_proj(x2, g, w, tm=512):
    t, d = x2.shape
    n = w.shape[1]
    return pl.pallas_call(
        functools.partial(_sba_in_proj_kernel, n_chunk=512),
        grid=(t // tm,),
        in_specs=[pl.BlockSpec((tm, d), lambda i: (i, 0)),
                  _const_spec((1, d)),
                  _const_spec((d, n))],
        out_specs=pl.BlockSpec((tm, n), lambda i: (i, 0)),
        out_shape=jax.ShapeDtypeStruct((t, n), BF16),
        compiler_params=_params(("arbitrary",)),
        name="sba_in_proj",
    )(x2, g, w)


def _sba_attn_kernel(q_ref, k_ref, v_ref, uu_ref, o_ref, acc_ref, car_ref, hl_ref, lb_ref, a_ref,
                     *, tq, kb, nsub, rc):
    i = pl.program_id(2)
    kt = kb * nsub
    q = q_ref[0]
    lane = lax.broadcasted_iota(jnp.int32, (tq, LANES), 1)
    q_heads = (jnp.where(lane < HEAD_DIM, q, jnp.zeros_like(q)),
               jnp.where(lane >= HEAD_DIM, q, jnp.zeros_like(q)))
    acc_ref[...] = jnp.zeros_like(acc_ref)
    car_ref[...] = jnp.zeros_like(car_ref)
    uu = uu_ref[...]
    rel = (lax.broadcasted_iota(jnp.int32, (rc, kb), 0)
           - lax.broadcasted_iota(jnp.int32, (rc, kb), 1))

    def super_tile(st, diag):
        base = pl.multiple_of(st * kt, kt)
        k_all = k_ref[0, pl.ds(base, kt), :]
        v_all = v_ref[0, pl.ds(base, kt), :]

        def visibility(j, r0):
            if not diag:
                return 2, None
            if j * kb >= r0 + rc - 1:
                return 0, None
            if j * kb + kb - 1 < r0:
                return 2, None
            return 1, rel > (j * kb - r0)

        for h in range(2):
            totals = []
            cums = []
            for j in range(nsub):
                z = lax.dot_general(q_heads[h], k_all[j * kb:(j + 1) * kb], NT_DIMS,
                                    preferred_element_type=F32)
                for r0 in range(0, tq, rc):
                    rows = slice(r0, r0 + rc)
                    kind, strict = visibility(j, r0)
                    if kind == 0:
                        hl_ref[h, j, rows, :] = jnp.zeros((rc, 2 * kb), BF16)
                        continue
                    zc = z[rows]
                    sp = jnp.log2(1.0 + jnp.exp2(-jnp.abs(zc)))
                    lb_ref[h, j, rows, :] = jnp.minimum(zc, 0.0) - sp
                    nl = jnp.maximum(zc, 0.0) + sp
                    if kind == 1:
                        nl = jnp.where(strict, nl, 0.0)
                    hi, lo = _split_bf16(nl)
                    hl_ref[h, j, rows, 0:kb] = hi
                    hl_ref[h, j, rows, kb:2 * kb] = lo
                r = jnp.dot(hl_ref[h, j], uu, preferred_element_type=F32)
                cums.append(r[:, :kb])
                totals.append(r[:, kb:])
            car = car_ref[h]
            for j in reversed(range(nsub)):
                for r0 in range(0, tq, rc):
                    rows = slice(r0, r0 + rc)
                    kind, strict = visibility(j, r0)
                    if kind == 0:
                        a_ref[h, rows, j * kb:(j + 1) * kb] = jnp.zeros((rc, kb), BF16)
                        continue
                    a = jnp.exp2(lb_ref[h, j, rows, :] - cums[j][rows] - car[rows])
                    if kind == 1:
                        a = jnp.where(strict, a, 0.0)
                    a_ref[h, rows, j * kb:(j + 1) * kb] = a.astype(BF16)
                car = car + totals[j]
            car_ref[h] = car
            acc_ref[h] += jnp.dot(a_ref[h], v_all, preferred_element_type=F32)

    super_tile(i, True)

    def body(t, carry):
        super_tile(i - 1 - t, False)
        return carry

    lax.fori_loop(0, i, body, 0)
    o_ref[0] = jnp.where(lane < HEAD_DIM, acc_ref[0], acc_ref[1]).astype(o_ref.dtype)


def _sba_attention(qkv, b, s, tq=512, kb=LANES, nsub=4, rc=64):
    assert tq == kb * nsub and s % tq == 0 and tq % rc == 0
    n_pairs = D_MODEL // LANES
    jj = np.arange(kb)
    u = np.concatenate([(jj[:, None] > jj[None, :]).astype(np.float32),
                        np.ones((kb, kb), np.float32)], axis=1)
    uu = jnp.asarray(np.concatenate([u, u], axis=0), BF16)
    return pl.pallas_call(
        functools.partial(_sba_attn_kernel, tq=tq, kb=kb, nsub=nsub, rc=rc),
        grid=(b, n_pairs, s // tq),
        in_specs=[pl.BlockSpec((1, tq, LANES), lambda bi, p, i: (bi, i, p)),
                  pl.BlockSpec((1, s, LANES), lambda bi, p, i: (bi, 0, n_pairs + p)),
                  pl.BlockSpec((1, s, LANES), lambda bi, p, i: (bi, 0, 2 * n_pairs + p)),
                  _const_spec((2 * kb, 2 * kb))],
        out_specs=pl.BlockSpec((1, tq, LANES), lambda bi, p, i: (bi, i, p)),
        out_shape=jax.ShapeDtypeStruct((b, s, D_MODEL), BF16),
        scratch_shapes=[pltpu.VMEM((2, tq, LANES), F32),
                        pltpu.VMEM((2, tq, LANES), F32),
                        pltpu.VMEM((2, nsub, tq, 2 * kb), BF16),
                        pltpu.VMEM((2, nsub, tq, kb), F32),
                        pltpu.VMEM((2, tq, nsub * kb), BF16)],
        compiler_params=_params(("arbitrary", "arbitrary", "arbitrary")),
        name="sba_attention",
    )(qkv, qkv, qkv, uu)


def _out_proj_kernel(x_ref, o_ref, w_ref, y_ref):
    y_ref[...] = x_ref[...] + jnp.dot(o_ref[...], w_ref[...], preferred_element_type=F32)


def _out_proj_residual(x2, o2, w, tm=512):
    t, d = x2.shape
    return pl.pallas_call(
        _out_proj_kernel,
        grid=(t // tm,),
        in_specs=[pl.BlockSpec((tm, d), lambda i: (i, 0)),
                  pl.BlockSpec((tm, d), lambda i: (i, 0)),
                  _const_spec((d, d))],
        out_specs=pl.BlockSpec((tm, d), lambda i: (i, 0)),
        out_shape=jax.ShapeDtypeStruct((t, d), F32),
        compiler_params=_params(("arbitrary",)),
        name="out_proj_residual",
    )(x2, o2, w)


def _ffn_kernel(x_ref, g_ref, wup_ref, cw_ref, cb_ref, wdn_ref, gf_ref, y_ref,
                carry_ref, sg_ref, sv_ref, *, tm, fc, tiles_per_seq, final_norm):
    @pl.when(pl.program_id(0) % tiles_per_seq == 0)
    def _():
        carry_ref[...] = jnp.zeros_like(carry_ref)

    x = x_ref[...]
    hn = _rmsnorm(x, g_ref[...]).astype(wup_ref.dtype)

    def conv(col0, s_ref):
        cols = slice(col0, col0 + fc)
        u = jnp.dot(hn, wup_ref[:, cols], preferred_element_type=F32)
        s_ref[0:8, :] = carry_ref[:, cols]
        s_ref[8:tm + 8, :] = u
        carry_ref[:, cols] = u[tm - 8:tm, :]
        cw = cw_ref[:, cols]
        c = cb_ref[:, cols] + s_ref[6:tm + 6, :] * cw[0:1]
        c = c + s_ref[7:tm + 7, :] * cw[1:2]
        return c + u * cw[2:3]

    acc = jnp.zeros((tm, D_MODEL), F32)
    for c in range(D_FF // fc):
        gate = conv(c * fc, sg_ref)
        val = conv(D_FF + c * fc, sv_ref)
        act = (gate * jax.nn.sigmoid(gate) * val).astype(wdn_ref.dtype)
        acc = acc + jnp.dot(act, wdn_ref[c * fc:(c + 1) * fc, :], preferred_element_type=F32)
    y = x + acc
    if final_norm:
        y = _rmsnorm(y, gf_ref[...])
    y_ref[...] = y


def _conv_ffn(x2, g, w_up, conv_w, conv_b, w_down, g_final, s, final_norm, tm=256, fc=256):
    t, d = x2.shape
    f2 = w_up.shape[1]
    return pl.pallas_call(
        functools.partial(_ffn_kernel, tm=tm, fc=fc, tiles_per_seq=s // tm, final_norm=final_norm),
        grid=(t // tm,),
        in_specs=[pl.BlockSpec((tm, d), lambda i: (i, 0)),
                  _const_spec((1, d)),
                  _const_spec((d, f2)),
                  _const_spec((3, f2)),
                  _const_spec((1, f2)),
                  _const_spec((D_FF, d)),
                  _const_spec((1, d))],
        out_specs=pl.BlockSpec((tm, d), lambda i: (i, 0)),
        out_shape=jax.ShapeDtypeStruct((t, d), F32),
        scratch_shapes=[pltpu.VMEM((8, f2), F32),
                        pltpu.VMEM((tm + 8, fc), F32),
                        pltpu.VMEM((tm + 8, fc), F32)],
        compiler_params=_params(("arbitrary",)),
        name="conv_ffn",
    )(x2, g, w_up, conv_w, conv_b, w_down, g_final)


NSA_Q_COLS = D_MODEL
NSA_W_COLS = D_MODEL + 7 * LANES
NSA_GATES_PER_GROUP = 3 * NSA_HPG


def _swap_halves(y):
    lane = lax.broadcasted_iota(jnp.int32, y.shape, 1)
    first = (lane % HEAD_DIM) < HALF
    return jnp.where(first, pltpu.roll(y, LANES - HALF, 1), pltpu.roll(y, HALF, 1))


def _nsa_in_proj_kernel(x_ref, g_ref, w_ref, pos_ref, inv_ref, sgn_ref,
                        q_ref, kc_ref, vc_ref, ksa_ref, vsa_ref, kwa_ref, vwa_ref, gt_ref,
                        *, tm, seq):
    hn = _rmsnorm(x_ref[...], g_ref[...]).astype(w_ref.dtype)
    ang = pos_ref[...].astype(F32) * inv_ref[...]
    cos = jnp.cos(ang)
    sin = jnp.sin(ang) * sgn_ref[...]

    def rope(y):
        return y * cos + _swap_halves(y) * sin

    q_chunk = 4 * LANES
    for c in range(NSA_Q_COLS // q_chunk):
        y = jnp.dot(hn, w_ref[:, c * q_chunk:(c + 1) * q_chunk], preferred_element_type=F32)
        for l in range(q_chunk // LANES):
            yl = rope(y[:, l * LANES:(l + 1) * LANES]) * Q_SCALE
            q_ref[:, c * q_chunk + l * LANES:c * q_chunk + (l + 1) * LANES] = yl.astype(q_ref.dtype)
    y = jnp.dot(hn, w_ref[:, NSA_Q_COLS:], preferred_element_type=F32)
    part = lambda n: y[:, n * LANES:(n + 1) * LANES]

    tok = ((pl.program_id(0) % (seq // tm)) * tm
           + lax.broadcasted_iota(jnp.int32, (tm, HEAD_DIM), 0))
    lane = lax.broadcasted_iota(jnp.int32, (tm, HEAD_DIM), 1)
    blk_onehot = jnp.where((tok >> SLC_SHIFT) == lane, 1.0, 0.0)
    ones_col = jnp.where(lane == 0, 1.0, 0.0)
    zeros = jnp.zeros((tm, HEAD_DIM), F32)
    ks, vs, kw, vw = rope(part(2)), part(3), rope(part(4)), part(5)
    for g in range(NSA_KV_GROUPS):
        cols = slice(g * HEAD_DIM, (g + 1) * HEAD_DIM)
        kc_ref[g] = part(0)[:, cols]
        vc_ref[g] = part(1)[:, cols]
        ksa_ref[g] = jnp.concatenate([ks[:, cols], blk_onehot], axis=1).astype(ksa_ref.dtype)
        vsa_ref[g] = jnp.concatenate([vs[:, cols], ones_col], axis=1).astype(vsa_ref.dtype)
        kwa_ref[g] = jnp.concatenate([kw[:, cols], zeros], axis=1).astype(kwa_ref.dtype)
        vwa_ref[g] = jnp.concatenate([vw[:, cols], ones_col], axis=1).astype(vwa_ref.dtype)
    gates = jax.nn.sigmoid(part(6))
    gt_ref[0] = gates
    gt_ref[1] = pltpu.roll(gates, LANES - NSA_GATES_PER_GROUP, 1)


def _nsa_in_proj(x2, g, w, pos2, inv2, sgn2, seq, tm=512):
    t, d = x2.shape
    G = NSA_KV_GROUPS
    tok_g = lambda dt: jax.ShapeDtypeStruct((G, t, HEAD_DIM), dt)
    aug_g = lambda dt: jax.ShapeDtypeStruct((G, t, LANES), dt)
    tok_spec = pl.BlockSpec((G, tm, HEAD_DIM), lambda i: (0, i, 0))
    aug_spec = pl.BlockSpec((G, tm, LANES), lambda i: (0, i, 0))
    return pl.pallas_call(
        functools.partial(_nsa_in_proj_kernel, tm=tm, seq=seq),
        grid=(t // tm,),
        in_specs=[pl.BlockSpec((tm, d), lambda i: (i, 0)),
                  _const_spec((1, d)),
                  _const_spec((d, NSA_W_COLS)),
                  pl.BlockSpec((tm, 1), lambda i: (i, 0)),
                  _const_spec((1, LANES)),
                  _const_spec((1, LANES))],
        out_specs=[pl.BlockSpec((tm, NSA_Q_COLS), lambda i: (i, 0)),
                   tok_spec, tok_spec, aug_spec, aug_spec, aug_spec, aug_spec, aug_spec],
        out_shape=[jax.ShapeDtypeStruct((t, NSA_Q_COLS), BF16),
                   tok_g(F32), tok_g(F32), aug_g(BF16), aug_g(BF16), aug_g(BF16), aug_g(BF16),
                   aug_g(F32)],
        compiler_params=_params(("arbitrary",)),
        name="nsa_in_proj",
    )(x2, g, w, pos2, inv2, sgn2)


def _nsa_compress_kernel(k16_ref, v16_ref, pek_ref, pev_ref, w1k_ref, w2k_ref, w2kr_ref,
                         w1v_ref, w2v_ref, pos_ref, inv_ref, kct_ref, vc_ref):
    half_w = CMP_STRIDE * HEAD_DIM
    nrow = k16_ref.shape[2]

    def hidden(x16_ref, pe_ref, w1_ref):
        x = x16_ref[0, 0]
        y1 = jnp.dot(x, w1_ref[:half_w, :], precision=HIGHEST, preferred_element_type=F32)
        y2 = jnp.dot(x, w1_ref[half_w:, :], precision=HIGHEST, preferred_element_type=F32)
        bias = jnp.dot(pe_ref[...], w1_ref[...], precision=HIGHEST, preferred_element_type=F32)
        return jax.nn.gelu(y1 + pltpu.roll(y2, nrow - 1, 0) + bias[0:1])

    hk = hidden(k16_ref, pek_ref, w1k_ref)
    kc = jnp.dot(hk, w2k_ref[...], precision=HIGHEST, preferred_element_type=F32)
    kc_rot = jnp.dot(hk, w2kr_ref[...], precision=HIGHEST, preferred_element_type=F32)
    ang = pos_ref[0].astype(F32) * inv_ref[...]
    kc = kc * jnp.cos(ang) + kc_rot * jnp.sin(ang)
    hi, lo = _split_bf16(kc)
    kct_ref[0, 0] = jnp.concatenate([hi.astype(F32), lo.astype(F32)], axis=1).T.astype(kct_ref.dtype)
    hv = hidden(v16_ref, pev_ref, w1v_ref)
    vc = jnp.dot(hv, w2v_ref[...], precision=HIGHEST, preferred_element_type=F32)
    vc_ref[0, 0] = jnp.concatenate([vc, jnp.zeros_like(vc)], axis=1).astype(vc_ref.dtype)


def _nsa_compress(k16, v16, pek, pev, w1k, w2k, w2kr, w1v, w2v, pos_cmp, inv64):
    G, b, nrow, wide = k16.shape
    x_spec = pl.BlockSpec((1, 1, nrow, wide), lambda bi, g: (g, bi, 0, 0))
    return pl.pallas_call(
        _nsa_compress_kernel,
        grid=(b, G),
        in_specs=[x_spec, x_spec,
                  _const_spec(pek.shape), _const_spec(pev.shape),
                  _const_spec(w1k.shape), _const_spec(w2k.shape), _const_spec(w2kr.shape),
                  _const_spec(w1v.shape), _const_spec(w2v.shape),
                  pl.BlockSpec((1, nrow, 1), lambda bi, g: (bi, 0, 0)),
                  _const_spec((1, HEAD_DIM))],
        out_specs=[pl.BlockSpec((1, 1, LANES, nrow), lambda bi, g: (bi, g, 0, 0)),
                   pl.BlockSpec((1, 1, nrow, LANES), lambda bi, g: (bi, g, 0, 0))],
        out_shape=[jax.ShapeDtypeStruct((b, G, LANES, nrow), BF16),
                   jax.ShapeDtypeStruct((b, G, nrow, LANES), BF16)],
        compiler_params=_params(("arbitrary", "arbitrary")),
        name="nsa_compress",
    )(k16, v16, pek, pev, w1k, w2k, w2kr, w1v, w2v, pos_cmp, inv64)


def _nsa_attn_kernel(q_ref, kct_ref, vc_ref, ov_ref, ks_ref, vs_ref, kw_ref, vw_ref, gt_ref,
                     o_ref, qa_ref, oc_ref, psum_ref, m_ref, acc_ref, alpha_ref, p_ref, pw_ref,
                     *, tq, kb, n_slc, wg, rc):
    i = pl.program_id(2)
    q0 = pl.multiple_of(i * tq, tq)
    n_rows = NSA_HPG * tq
    heads = range(NSA_HPG)
    hrows = lambda h: slice(h * tq, (h + 1) * tq)
    qrow = q0 + lax.broadcasted_iota(jnp.int32, (tq, 1), 0)
    q_heads = [q_ref[0, :, h * HEAD_DIM:(h + 1) * HEAD_DIM] for h in heads]
    for h in heads:
        qa_ref[hrows(h), :] = jnp.concatenate([q_heads[h], q_heads[h]], axis=1)

    n_cmp_rows = kct_ref.shape[3]
    s_all = jnp.dot(qa_ref[...], kct_ref[0, 0], preferred_element_type=F32)
    cmp_end = CMP_STRIDE * lax.broadcasted_iota(jnp.int32, (1, n_cmp_rows), 1) + (CMP_LEN - 1)
    c_bias = jnp.where(cmp_end <= qrow, 0.0, NEG)
    row_valid = jnp.where(qrow >= CMP_LEN - 1, 1.0, 0.0)
    for r0 in range(0, n_rows, rc):
        rows = slice(r0, r0 + rc)
        local = slice(r0 % tq, r0 % tq + rc)
        s = s_all[rows] + c_bias[local]
        e = jnp.exp2(s - jnp.max(s, axis=1, keepdims=True))
        p = e * (row_valid[local] / jnp.sum(e, axis=1, keepdims=True))
        if r0 < tq:
            psum_ref[local, :] = p
        else:
            psum_ref[local, :] += p
        p_ref[rows, 0:n_cmp_rows] = p.astype(p_ref.dtype)
    oc_ref[...] = jnp.dot(p_ref[:, 0:n_cmp_rows], vc_ref[0, 0], preferred_element_type=F32)
    imp = jnp.dot(jnp.concatenate(_split_bf16(psum_ref[...]), axis=1), ov_ref[...],
                  preferred_element_type=F32)
    imp_t = imp.T[:n_slc]

    qpos = q0 + lax.broadcasted_iota(jnp.int32, (1, tq), 1)
    blk = lax.broadcasted_iota(jnp.int32, (n_slc, 1), 0)
    cur = qpos >> SLC_SHIFT
    forced = (blk == 0) | (blk == cur) | (blk == cur - 1)
    causal_blk = blk * SLC_LEN <= qpos
    score = jnp.where(causal_blk, imp_t + FORCE_BONUS * forced.astype(F32), NEG)
    sub = lax.broadcasted_iota(jnp.int32, (8, 1), 0)
    groups = [score[8 * v:8 * v + 8] for v in range(n_slc // 8)]
    counts = [jnp.zeros((8, tq), F32) for _ in groups]
    for m in range(n_slc):
        row = score[m:m + 1]
        for v, sv in enumerate(groups):
            if v < m // 8:
                ahead = row > sv
            elif v > m // 8:
                ahead = row >= sv
            else:
                ahead = (row > sv) | ((row == sv) & (sub > m % 8))
            counts[v] = counts[v] + jnp.where(ahead, 1.0, 0.0)
    rank = jnp.concatenate(counts, axis=0)
    sel_bias_t = jnp.where(rank < SLC_TOP_N, 0.0, NEG)
    sel_bias = jnp.concatenate([sel_bias_t, jnp.zeros((LANES - n_slc, tq), F32)], axis=0).T
    sel_bias = sel_bias[:, :HEAD_DIM].astype(qa_ref.dtype)
    for h in heads:
        qa_ref[hrows(h), :] = jnp.concatenate([q_heads[h], sel_bias], axis=1)

    tok_bias = jnp.where(lax.broadcasted_iota(jnp.int32, (1, kb), 1)
                         <= lax.broadcasted_iota(jnp.int32, (tq, 1), 0), 0.0, NEG)

    def chunk(start, first):
        k_t = ks_ref[0, 0, pl.ds(start, kb), :]
        v_t = vs_ref[0, 0, pl.ds(start, kb), :]
        s_all = lax.dot_general(qa_ref[...], k_t, NT_DIMS, preferred_element_type=F32)
        for r0 in range(0, n_rows, rc):
            rows = slice(r0, r0 + rc)
            s = s_all[rows]
            if first:
                s = s + tok_bias[r0 % tq:r0 % tq + rc]
                m_new = jnp.broadcast_to(jnp.max(s, axis=1, keepdims=True), (rc, LANES))
            else:
                m_old = m_ref[rows, :]
                m_new = jnp.maximum(m_old, jnp.max(s, axis=1, keepdims=True))
                alpha_ref[rows, :] = jnp.exp2(m_old - m_new)
            m_ref[rows, :] = m_new
            m_wide = jnp.concatenate([m_new] * (kb // LANES), axis=1)
            p_ref[rows, :] = jnp.exp2(s - m_wide).astype(p_ref.dtype)
        pv = jnp.dot(p_ref[...], v_t, preferred_element_type=F32)
        if first:
            acc_ref[...] = pv
        else:
            for r0 in range(0, n_rows, rc):
                rows = slice(r0, r0 + rc)
                acc_ref[rows, :] = alpha_ref[rows, :] * acc_ref[rows, :] + pv[rows]

    chunk(q0, True)

    def chunk_body(c, carry):
        chunk(pl.multiple_of(c * kb, kb), False)
        return carry

    lax.fori_loop(0, q0 // kb, chunk_body, 0)

    w_keys = WINDOW + tq
    w_start = pl.multiple_of(jnp.maximum(q0 - WINDOW, 0), tq)
    w_diff = qrow - (w_start + lax.broadcasted_iota(jnp.int32, (1, w_keys), 1))
    w_bias = jnp.where((w_diff >= 0) & (w_diff < WINDOW), 0.0, NEG)
    k_w = kw_ref[0, 0, pl.ds(w_start, w_keys), :]
    v_w = vw_ref[0, 0, pl.ds(w_start, w_keys), :]
    gates = gt_ref[0]

    def normalised(o_aug):
        return o_aug[:, :HEAD_DIM] * (1.0 / o_aug[:, HEAD_DIM:HEAD_DIM + 1])

    for h0 in range(0, NSA_HPG, wg):
        rows = slice(h0 * tq, (h0 + wg) * tq)
        s_all = lax.dot_general(qa_ref[rows, :], k_w, NT_DIMS, preferred_element_type=F32)
        for r0 in range(0, wg * tq, rc):
            local = slice(r0 % tq, r0 % tq + rc)
            s = s_all[r0:r0 + rc] + w_bias[local]
            pw_ref[r0:r0 + rc, :] = jnp.exp2(s - jnp.max(s, axis=1, keepdims=True)).astype(pw_ref.dtype)
        o_w = jnp.dot(pw_ref[...], v_w, preferred_element_type=F32)
        outs = []
        for h in range(h0, h0 + wg):
            local = slice((h - h0) * tq, (h - h0 + 1) * tq)
            g_c, g_s, g_w = (gates[:, 3 * h + r:3 * h + r + 1] for r in range(3))
            outs.append(g_c * oc_ref[hrows(h), :HEAD_DIM] + g_s * normalised(acc_ref[hrows(h), :])
                        + g_w * normalised(o_w[local]))
        for hp in range(wg // 2):
            col0 = (h0 + 2 * hp) * HEAD_DIM
            pair = jnp.concatenate(outs[2 * hp:2 * hp + 2], axis=1)
            o_ref[0, :, col0:col0 + LANES] = pair.astype(o_ref.dtype)


def _nsa_attention(q3, kct, vc, ov, ksa, vsa, kwa, vwa, gt, tq=256, wg=4, rc=64):
    b, s, _ = q3.shape
    G = NSA_KV_GROUPS
    kb = tq
    n_slc = s // SLC_LEN
    nq = s // tq
    n_cmp_rows = kct.shape[3]
    gw = NSA_HPG * HEAD_DIM
    assert n_slc <= HEAD_DIM and tq % SLC_LEN == 0 and WINDOW % tq == 0 and s >= WINDOW + tq
    assert n_cmp_rows <= kb and n_slc % 8 == 0 and tq % rc == 0
    kv_spec = pl.BlockSpec((1, 1, s, LANES), lambda bi, g, i: (g, bi, 0, 0))
    return pl.pallas_call(
        functools.partial(_nsa_attn_kernel, tq=tq, kb=kb, n_slc=n_slc, wg=wg, rc=rc),
        grid=(b, G, nq),
        in_specs=[pl.BlockSpec((1, tq, gw), lambda bi, g, i: (bi, i, g)),
                  pl.BlockSpec((1, 1, LANES, n_cmp_rows), lambda bi, g, i: (bi, g, 0, 0)),
                  pl.BlockSpec((1, 1, n_cmp_rows, LANES), lambda bi, g, i: (bi, g, 0, 0)),
                  _const_spec(ov.shape),
                  kv_spec, kv_spec, kv_spec, kv_spec,
                  pl.BlockSpec((1, tq, LANES), lambda bi, g, i: (g, bi * nq + i, 0))],
        out_specs=pl.BlockSpec((1, tq, gw), lambda bi, g, i: (bi, i, g)),
        out_shape=jax.ShapeDtypeStruct((b, s, D_MODEL), BF16),
        scratch_shapes=[pltpu.VMEM((NSA_HPG * tq, LANES), BF16),
                        pltpu.VMEM((NSA_HPG * tq, LANES), F32),
                        pltpu.VMEM((tq, n_cmp_rows), F32),
                        pltpu.VMEM((NSA_HPG * tq, LANES), F32),
                        pltpu.VMEM((NSA_HPG * tq, LANES), F32),
                        pltpu.VMEM((NSA_HPG * tq, LANES), F32),
                        pltpu.VMEM((NSA_HPG * tq, kb), BF16),
                        pltpu.VMEM((wg * tq, WINDOW + tq), BF16)],
        compiler_params=_params(("arbitrary", "arbitrary", "arbitrary")),
        name="nsa_attention",
    )(q3, kct, vc, ov, ksa, vsa, kwa, vwa, gt)


def _nsa_layer_attention(hx2, norm_g, positions, w_in, pe_k, pe_v, w1k, w2k, w1v, w2v, b, s):
    t = b * s
    G = NSA_KV_GROUPS
    w_pad = jnp.pad(w_in, ((0, 0), (0, NSA_W_COLS - w_in.shape[1]))).astype(BF16)
    inv = ROPE_THETA ** (-jnp.arange(HALF, dtype=F32) / HALF)
    inv2 = jnp.tile(inv, LANES // HALF)[None, :]
    sgn2 = jnp.tile(jnp.concatenate([-jnp.ones(HALF, F32), jnp.ones(HALF, F32)]), LANES // HEAD_DIM)[None, :]
    q, kc_tok, vc_tok, ksa, vsa, kwa, vwa, gt = _nsa_in_proj(
        hx2, norm_g, w_pad, positions.reshape(t, 1), inv2, sgn2, s)

    nrow = s // CMP_STRIDE
    wide = CMP_STRIDE * HEAD_DIM
    k16 = kc_tok.reshape(G, b, nrow, wide)
    v16 = vc_tok.reshape(G, b, nrow, wide)
    pek = jnp.broadcast_to(pe_k.reshape(1, CMP_LEN * HEAD_DIM), (8, CMP_LEN * HEAD_DIM))
    pev = jnp.broadcast_to(pe_v.reshape(1, CMP_LEN * HEAD_DIM), (8, CMP_LEN * HEAD_DIM))
    w2k_rot = jnp.concatenate([-w2k[:, HALF:], w2k[:, :HALF]], axis=1)
    end_idx = jnp.minimum(jnp.arange(nrow) * CMP_STRIDE + CMP_LEN - 1, s - 1)
    pos_cmp = positions[:, end_idx][:, :, None]
    inv64 = jnp.tile(inv, 2)[None, :]
    kct, vc = _nsa_compress(k16, v16, pek, pev, w1k, w2k, w2k_rot, w1v, w2v, pos_cmp, inv64)

    n_slc = s // SLC_LEN
    c0 = np.arange(nrow)[:, None] * CMP_STRIDE
    s0 = np.arange(n_slc)[None, :] * SLC_LEN
    ov = np.clip(np.minimum(c0 + CMP_LEN, s0 + SLC_LEN) - np.maximum(c0, s0), 0, None) / CMP_LEN
    ov[(s - CMP_LEN) // CMP_STRIDE + 1:, :] = 0.0
    ov = np.pad(ov, ((0, 0), (0, LANES - n_slc)))
    ov = jnp.asarray(np.concatenate([ov, ov], axis=0), BF16)

    aug4 = lambda a: a.reshape(G, b, s, LANES)
    return _nsa_attention(q.reshape(b, s, D_MODEL), kct, vc, ov, aug4(ksa), aug4(vsa),
                          aug4(kwa), aug4(vwa), gt)


def kernel(x, positions, norm_mix, sba_w_in, sba_w_out, nsa_w_in, nsa_cmp_pos_k, nsa_cmp_pos_v,
           nsa_cmp_k_w1, nsa_cmp_k_w2, nsa_cmp_v_w1, nsa_cmp_v_w2, nsa_w_out, norm_ffn,
           ffn_w_up, ffn_conv_w, ffn_conv_b, ffn_w_down, norm_final):
    b, s, d = x.shape
    t = b * s
    depth = norm_mix.shape[0]
    x2 = x.reshape(t, d)
    g_final = norm_final.reshape(1, d)
    for layer in range(depth):
        j = layer // 2
        g_mix = norm_mix[layer].reshape(1, d)
        if layer % 2 == 0:
            qkv = _sba_in_proj(x2, g_mix, sba_w_in[j].astype(BF16))
            o = _sba_attention(qkv.reshape(b, s, 3 * d), b, s)
            w_out = sba_w_out[j]
        else:
            o = _nsa_layer_attention(x2, g_mix, positions, nsa_w_in[j], nsa_cmp_pos_k[j],
                                     nsa_cmp_pos_v[j], nsa_cmp_k_w1[j], nsa_cmp_k_w2[j],
                                     nsa_cmp_v_w1[j], nsa_cmp_v_w2[j], b, s)
            w_out = nsa_w_out[j]
        x2 = _out_proj_residual(x2, o.reshape(t, d), w_out.astype(BF16))
        x2 = _conv_ffn(x2, norm_ffn[layer].reshape(1, d), ffn_w_up[layer].astype(BF16),
                       ffn_conv_w[layer], ffn_conv_b[layer].reshape(1, -1),
                       ffn_w_down[layer].astype(BF16), g_final, s,
                       final_norm=(layer == depth - 1))
    return x2.reshape(b, s, d)
```

```python
import functools

import numpy as np
import jax
import jax.numpy as jnp
from jax import lax
from jax.experimental import pallas as pl
from jax.experimental.pallas import tpu as pltpu

D_MODEL = 1024
N_HEADS = 16
HEAD_DIM = 64
HALF = HEAD_DIM // 2
NSA_KV_GROUPS = 2
NSA_HPG = N_HEADS // NSA_KV_GROUPS
CMP_LEN = 32
CMP_STRIDE = 16
CMP_HIDDEN = 2 * HEAD_DIM
SLC_LEN = 64
SLC_SHIFT = SLC_LEN.bit_length() - 1
SLC_TOP_N = 16
WINDOW = 512
ROPE_THETA = 10000.0
D_FF = 2816
RMS_EPS = 1e-6
NEG = -1e30
FORCE_BONUS = 1e4
LOG2E = float(np.log2(np.e))
Q_SCALE = LOG2E * HEAD_DIM ** -0.5

LANES = 128
VMEM_LIMIT = 56 * 1024 * 1024

F32 = jnp.float32
BF16 = jnp.bfloat16
HIGHEST = lax.Precision.HIGHEST
NT_DIMS = (((1,), (1,)), ((), ()))


def _params(semantics):
    return pltpu.CompilerParams(dimension_semantics=semantics, vmem_limit_bytes=VMEM_LIMIT)


def _rmsnorm(x, g):
    return x * lax.rsqrt(jnp.mean(x * x, axis=-1, keepdims=True) + RMS_EPS) * g


def _const_spec(shape):
    return pl.BlockSpec(shape, lambda *_: (0,) * len(shape))


def _split_bf16(x):
    hi = x.astype(BF16)
    return hi, (x - hi.astype(F32)).astype(BF16)


def _sba_in_proj_kernel(x_ref, g_ref, w_ref, o_ref, *, n_chunk):
    hn = _rmsnorm(x_ref[...], g_ref[...]).astype(w_ref.dtype)
    n = w_ref.shape[1]
    for c in range(n // n_chunk):
        cols = slice(c * n_chunk, (c + 1) * n_chunk)
        y = jnp.dot(hn, w_ref[:, cols], preferred_element_type=F32)
        if c * n_chunk < D_MODEL:
            y = y * Q_SCALE
        o_ref[:, cols] = y.astype(o_ref.dtype)


def _sba_in_proj(x2, g, w, tm=512):
    t, d = x2.shape
    n = w.shape[1]
    return pl.pallas_call(
        functools.partial(_sba_in_proj_kernel, n_chunk=512),
        grid=(t // tm,),
        in_specs=[pl.BlockSpec((tm, d), lambda i: (i, 0)),
                  _const_spec((1, d)),
                  _const_spec((d, n))],
        out_specs=pl.BlockSpec((tm, n), lambda i: (i, 0)),
        out_shape=jax.ShapeDtypeStruct((t, n), BF16),
        compiler_params=_params(("arbitrary",)),
        name="sba_in_proj",
    )(x2, g, w)


def _sba_attn_kernel(q_ref, k_ref, v_ref, uu_ref, o_ref, acc_ref, car_ref, hl_ref, lb_ref, a_ref,
                     *, tq, kb, nsub, rc):
    i = pl.program_id(2)
    kt = kb * nsub
    q = q_ref[0]
    lane = lax.broadcasted_iota(jnp.int32, (tq, LANES), 1)
    klane = lax.broadcasted_iota(jnp.int32, (kb, LANES), 1)
    acc_ref[...] = jnp.zeros_like(acc_ref)
    car_ref[...] = jnp.zeros_like(car_ref)
    uu = uu_ref[...]
    rel = (lax.broadcasted_iota(jnp.int32, (rc, kb), 0)
           - lax.broadcasted_iota(jnp.int32, (rc, kb), 1))

    def super_tile(st, diag, slot, prev_slot):
        k_all = k_ref[0, pl.ds(pl.multiple_of(st * kt, kt), kt), :]

        def visibility(j, r0):
            if not diag:
                return 2, None
            if j * kb >= r0 + rc - 1:
                return 0, None
            if j * kb + kb - 1 < r0:
                return 2, None
            return 1, rel > (j * kb - r0)

        def scores(j):
            k_t = k_all[j * kb:(j + 1) * kb]
            k_bd = jnp.concatenate([jnp.where(klane < HEAD_DIM, k_t, jnp.zeros_like(k_t)),
                                    jnp.where(klane >= HEAD_DIM, k_t, jnp.zeros_like(k_t))], axis=0)
            return lax.dot_general(q, k_bd, NT_DIMS, preferred_element_type=F32)

        def log_terms(j, z):
            for h in range(2):
                for r0 in range(0, tq, rc):
                    rows = slice(h * tq + r0, h * tq + r0 + rc)
                    kind, strict = visibility(j, r0)
                    if kind == 0:
                        hl_ref[j, rows, :] = jnp.zeros((rc, 2 * kb), BF16)
                        continue
                    zc = z[r0:r0 + rc, h * kb:(h + 1) * kb]
                    nl = jnp.maximum(zc, 0.0) + jnp.log2(1.0 + jnp.exp2(-jnp.abs(zc)))
                    lb_ref[j, rows, :] = zc - nl
                    if kind == 1:
                        nl = jnp.where(strict, nl, 0.0)
                    hi, lo = _split_bf16(nl)
                    hl_ref[j, rows, 0:kb] = hi
                    hl_ref[j, rows, kb:2 * kb] = lo
            return jnp.dot(hl_ref[j], uu, preferred_element_type=F32)

        def weights(j, r):
            for h in range(2):
                car = car_ref[h]
                for r0 in range(0, tq, rc):
                    rows = slice(h * tq + r0, h * tq + r0 + rc)
                    kind, strict = visibility(j, r0)
                    if kind == 0:
                        a_ref[slot, h, r0:r0 + rc, j * kb:(j + 1) * kb] = jnp.zeros((rc, kb), BF16)
                        continue
                    a = jnp.exp2(lb_ref[j, rows, :] - r[rows, :kb] - car[r0:r0 + rc])
                    if kind == 1:
                        a = jnp.where(strict, a, 0.0)
                    a_ref[slot, h, r0:r0 + rc, j * kb:(j + 1) * kb] = a.astype(BF16)
                car_ref[h] = car + r[h * tq:(h + 1) * tq, kb:]

        z_next = scores(nsub - 1)
        pending = None
        for j in reversed(range(nsub)):
            z = z_next
            if j > 0:
                z_next = scores(j - 1)
            if j == nsub - 1 and prev_slot is not None:
                apply_weights(prev_slot, st + 1)
            r = log_terms(j, z)
            if pending is not None:
                weights(j + 1, pending)
            pending = r
        weights(0, pending)

    def apply_weights(slot, st):
        v_all = v_ref[0, pl.ds(pl.multiple_of(st * kt, kt), kt), :]
        for h in range(2):
            acc_ref[h] += jnp.dot(a_ref[slot, h], v_all, preferred_element_type=F32)

    super_tile(i, True, 0, None)

    def body(t, carry):
        super_tile(i - 1 - t, False, (t + 1) % 2, t % 2)
        return carry

    lax.fori_loop(0, i, body, 0)
    apply_weights(i % 2, 0)
    o_ref[0] = jnp.where(lane < HEAD_DIM, acc_ref[0], acc_ref[1]).astype(o_ref.dtype)


def _sba_attention(qkv, b, s, tq=512, kb=LANES, nsub=4, rc=64):
    assert tq == kb * nsub and s % tq == 0 and tq % rc == 0
    n_pairs = D_MODEL // LANES
    jj = np.arange(kb)
    u = np.concatenate([(jj[:, None] > jj[None, :]).astype(np.float32),
                        np.ones((kb, kb), np.float32)], axis=1)
    uu = jnp.asarray(np.concatenate([u, u], axis=0), BF16)
    return pl.pallas_call(
        functools.partial(_sba_attn_kernel, tq=tq, kb=kb, nsub=nsub, rc=rc),
        grid=(b, n_pairs, s // tq),
        in_specs=[pl.BlockSpec((1, tq, LANES), lambda bi, p, i: (bi, i, p)),
                  pl.BlockSpec((1, s, LANES), lambda bi, p, i: (bi, 0, n_pairs + p)),
                  pl.BlockSpec((1, s, LANES), lambda bi, p, i: (bi, 0, 2 * n_pairs + p)),
                  _const_spec((2 * kb, 2 * kb))],
        out_specs=pl.BlockSpec((1, tq, LANES), lambda bi, p, i: (bi, i, p)),
        out_shape=jax.ShapeDtypeStruct((b, s, D_MODEL), BF16),
        scratch_shapes=[pltpu.VMEM((2, tq, LANES), F32),
                        pltpu.VMEM((2, tq, LANES), F32),
                        pltpu.VMEM((nsub, 2 * tq, 2 * kb), BF16),
                        pltpu.VMEM((nsub, 2 * tq, kb), F32),
                        pltpu.VMEM((2, 2, tq, nsub * kb), BF16)],
        compiler_params=_params(("arbitrary", "arbitrary", "arbitrary")),
        name="sba_attention",
    )(qkv, qkv, qkv, uu)


def _out_proj_kernel(x_ref, o_ref, w_ref, y_ref):
    y_ref[...] = x_ref[...] + jnp.dot(o_ref[...], w_ref[...], preferred_element_type=F32)


def _out_proj_residual(x2, o2, w, tm=512):
    t, d = x2.shape
    return pl.pallas_call(
        _out_proj_kernel,
        grid=(t // tm,),
        in_specs=[pl.BlockSpec((tm, d), lambda i: (i, 0)),
                  pl.BlockSpec((tm, d), lambda i: (i, 0)),
                  _const_spec((d, d))],
        out_specs=pl.BlockSpec((tm, d), lambda i: (i, 0)),
        out_shape=jax.ShapeDtypeStruct((t, d), F32),
        compiler_params=_params(("arbitrary",)),
        name="out_proj_residual",
    )(x2, o2, w)


def _ffn_kernel(x_ref, g_ref, wup_ref, cw_ref, cb_ref, wdn_ref, gf_ref, y_ref,
                carry_ref, sg_ref, sv_ref, *, tm, fc, tiles_per_seq, final_norm):
    @pl.when(pl.program_id(0) % tiles_per_seq == 0)
    def _():
        carry_ref[...] = jnp.zeros_like(carry_ref)

    x = x_ref[...]
    hn = _rmsnorm(x, g_ref[...]).astype(wup_ref.dtype)

    def conv(col0, s_ref):
        cols = slice(col0, col0 + fc)
        u = jnp.dot(hn, wup_ref[:, cols], preferred_element_type=F32)
        s_ref[0:8, :] = carry_ref[:, cols]
        s_ref[8:tm + 8, :] = u
        carry_ref[:, cols] = u[tm - 8:tm, :]
        cw = cw_ref[:, cols]
        c = cb_ref[:, cols] + s_ref[6:tm + 6, :] * cw[0:1]
        c = c + s_ref[7:tm + 7, :] * cw[1:2]
        return c + u * cw[2:3]

    acc = jnp.zeros((tm, D_MODEL), F32)
    for c in range(D_FF // fc):
        gate = conv(c * fc, sg_ref)
        val = conv(D_FF + c * fc, sv_ref)
        act = (gate * jax.nn.sigmoid(gate) * val).astype(wdn_ref.dtype)
        acc = acc + jnp.dot(act, wdn_ref[c * fc:(c + 1) * fc, :], preferred_element_type=F32)
    y = x + acc
    if final_norm:
        y = _rmsnorm(y, gf_ref[...])
    y_ref[...] = y


def _conv_ffn(x2, g, w_up, conv_w, conv_b, w_down, g_final, s, final_norm, tm=256, fc=256):
    t, d = x2.shape
    f2 = w_up.shape[1]
    return pl.pallas_call(
        functools.partial(_ffn_kernel, tm=tm, fc=fc, tiles_per_seq=s // tm, final_norm=final_norm),
        grid=(t // tm,),
        in_specs=[pl.BlockSpec((tm, d), lambda i: (i, 0)),
                  _const_spec((1, d)),
                  _const_spec((d, f2)),
                  _const_spec((3, f2)),
                  _const_spec((1, f2)),
                  _const_spec((D_FF, d)),
                  _const_spec((1, d))],
        out_specs=pl.BlockSpec((tm, d), lambda i: (i, 0)),
        out_shape=jax.ShapeDtypeStruct((t, d), F32),
        scratch_shapes=[pltpu.VMEM((8, f2), F32),
                        pltpu.VMEM((tm + 8, fc), F32),
                        pltpu.VMEM((tm + 8, fc), F32)],
        compiler_params=_params(("arbitrary",)),
        name="conv_ffn",
    )(x2, g, w_up, conv_w, conv_b, w_down, g_final)


NSA_Q_COLS = D_MODEL
NSA_W_COLS = D_MODEL + 7 * LANES
NSA_GATES_PER_GROUP = 3 * NSA_HPG


def _swap_halves(y):
    lane = lax.broadcasted_iota(jnp.int32, y.shape, 1)
    first = (lane % HEAD_DIM) < HALF
    return jnp.where(first, pltpu.roll(y, LANES - HALF, 1), pltpu.roll(y, HALF, 1))


def _nsa_in_proj_kernel(x_ref, g_ref, w_ref, pos_ref, inv_ref, sgn_ref,
                        q_ref, kc_ref, vc_ref, ksa_ref, vsa_ref, kwa_ref, vwa_ref, gt_ref,
                        *, tm, seq):
    hn = _rmsnorm(x_ref[...], g_ref[...]).astype(w_ref.dtype)
    ang = pos_ref[...].astype(F32) * inv_ref[...]
    cos = jnp.cos(ang)
    sin = jnp.sin(ang) * sgn_ref[...]

    def rope(y):
        return y * cos + _swap_halves(y) * sin

    q_chunk = 4 * LANES
    for c in range(NSA_Q_COLS // q_chunk):
        y = jnp.dot(hn, w_ref[:, c * q_chunk:(c + 1) * q_chunk], preferred_element_type=F32)
        for l in range(q_chunk // LANES):
            yl = rope(y[:, l * LANES:(l + 1) * LANES]) * Q_SCALE
            q_ref[:, c * q_chunk + l * LANES:c * q_chunk + (l + 1) * LANES] = yl.astype(q_ref.dtype)
    y = jnp.dot(hn, w_ref[:, NSA_Q_COLS:], preferred_element_type=F32)
    part = lambda n: y[:, n * LANES:(n + 1) * LANES]

    tok = ((pl.program_id(0) % (seq // tm)) * tm
           + lax.broadcasted_iota(jnp.int32, (tm, HEAD_DIM), 0))
    lane = lax.broadcasted_iota(jnp.int32, (tm, HEAD_DIM), 1)
    blk_onehot = jnp.where((tok >> SLC_SHIFT) == lane, 1.0, 0.0)
    ones_col = jnp.where(lane == 0, 1.0, 0.0)
    zeros = jnp.zeros((tm, HEAD_DIM), F32)
    ks, vs, kw, vw = rope(part(2)), part(3), rope(part(4)), part(5)
    for g in range(NSA_KV_GROUPS):
        cols = slice(g * HEAD_DIM, (g + 1) * HEAD_DIM)
        kc_ref[g] = part(0)[:, cols]
        vc_ref[g] = part(1)[:, cols]
        ksa_ref[g] = jnp.concatenate([ks[:, cols], blk_onehot], axis=1).astype(ksa_ref.dtype)
        vsa_ref[g] = jnp.concatenate([vs[:, cols], ones_col], axis=1).astype(vsa_ref.dtype)
        kwa_ref[g] = jnp.concatenate([kw[:, cols], zeros], axis=1).astype(kwa_ref.dtype)
        vwa_ref[g] = jnp.concatenate([vw[:, cols], ones_col], axis=1).astype(vwa_ref.dtype)
    gates = jax.nn.sigmoid(part(6))
    gt_ref[0] = gates
    gt_ref[1] = pltpu.roll(gates, LANES - NSA_GATES_PER_GROUP, 1)


def _nsa_in_proj(x2, g, w, pos2, inv2, sgn2, seq, tm=512):
    t, d = x2.shape
    G = NSA_KV_GROUPS
    tok_g = lambda dt: jax.ShapeDtypeStruct((G, t, HEAD_DIM), dt)
    aug_g = lambda dt: jax.ShapeDtypeStruct((G, t, LANES), dt)
    tok_spec = pl.BlockSpec((G, tm, HEAD_DIM), lambda i: (0, i, 0))
    aug_spec = pl.BlockSpec((G, tm, LANES), lambda i: (0, i, 0))
    return pl.pallas_call(
        functools.partial(_nsa_in_proj_kernel, tm=tm, seq=seq),
        grid=(t // tm,),
        in_specs=[pl.BlockSpec((tm, d), lambda i: (i, 0)),
                  _const_spec((1, d)),
                  _const_spec((d, NSA_W_COLS)),
                  pl.BlockSpec((tm, 1), lambda i: (i, 0)),
                  _const_spec((1, LANES)),
                  _const_spec((1, LANES))],
        out_specs=[pl.BlockSpec((tm, NSA_Q_COLS), lambda i: (i, 0)),
                   tok_spec, tok_spec, aug_spec, aug_spec, aug_spec, aug_spec, aug_spec],
        out_shape=[jax.ShapeDtypeStruct((t, NSA_Q_COLS), BF16),
                   tok_g(F32), tok_g(F32), aug_g(BF16), aug_g(BF16), aug_g(BF16), aug_g(BF16),
                   aug_g(F32)],
        compiler_params=_params(("arbitrary",)),
        name="nsa_in_proj",
    )(x2, g, w, pos2, inv2, sgn2)


def _nsa_compress_kernel(k16_ref, v16_ref, pek_ref, pev_ref, w1k_ref, w2k_ref, w2kr_ref,
                         w1v_ref, w2v_ref, pos_ref, inv_ref, kct_ref, vc_ref):
    half_w = CMP_STRIDE * HEAD_DIM
    nrow = k16_ref.shape[2]

    def hidden(x16_ref, pe_ref, w1_ref):
        x = x16_ref[0, 0]
        y1 = jnp.dot(x, w1_ref[:half_w, :], precision=HIGHEST, preferred_element_type=F32)
        y2 = jnp.dot(x, w1_ref[half_w:, :], precision=HIGHEST, preferred_element_type=F32)
        bias = jnp.dot(pe_ref[...], w1_ref[...], precision=HIGHEST, preferred_element_type=F32)
        return jax.nn.gelu(y1 + pltpu.roll(y2, nrow - 1, 0) + bias[0:1])

    hk = hidden(k16_ref, pek_ref, w1k_ref)
    kc = jnp.dot(hk, w2k_ref[...], precision=HIGHEST, preferred_element_type=F32)
    kc_rot = jnp.dot(hk, w2kr_ref[...], precision=HIGHEST, preferred_element_type=F32)
    ang = pos_ref[0].astype(F32) * inv_ref[...]
    kc = kc * jnp.cos(ang) + kc_rot * jnp.sin(ang)
    hi, lo = _split_bf16(kc)
    kct_ref[0, 0] = jnp.concatenate([hi.astype(F32), lo.astype(F32)], axis=1).T.astype(kct_ref.dtype)
    hv = hidden(v16_ref, pev_ref, w1v_ref)
    vc = jnp.dot(hv, w2v_ref[...], precision=HIGHEST, preferred_element_type=F32)
    vc_ref[0, 0] = jnp.concatenate([vc, jnp.zeros_like(vc)], axis=1).astype(vc_ref.dtype)


def _nsa_compress(k16, v16, pek, pev, w1k, w2k, w2kr, w1v, w2v, pos_cmp, inv64):
    G, b, nrow, wide = k16.shape
    x_spec = pl.BlockSpec((1, 1, nrow, wide), lambda bi, g: (g, bi, 0, 0))
    return pl.pallas_call(
        _nsa_compress_kernel,
        grid=(b, G),
        in_specs=[x_spec, x_spec,
                  _const_spec(pek.shape), _const_spec(pev.shape),
                  _const_spec(w1k.shape), _const_spec(w2k.shape), _const_spec(w2kr.shape),
                  _const_spec(w1v.shape), _const_spec(w2v.shape),
                  pl.BlockSpec((1, nrow, 1), lambda bi, g: (bi, 0, 0)),
                  _const_spec((1, HEAD_DIM))],
        out_specs=[pl.BlockSpec((1, 1, LANES, nrow), lambda bi, g: (bi, g, 0, 0)),
                   pl.BlockSpec((1, 1, nrow, LANES), lambda bi, g: (bi, g, 0, 0))],
        out_shape=[jax.ShapeDtypeStruct((b, G, LANES, nrow), BF16),
                   jax.ShapeDtypeStruct((b, G, nrow, LANES), BF16)],
        compiler_params=_params(("arbitrary", "arbitrary")),
        name="nsa_compress",
    )(k16, v16, pek, pev, w1k, w2k, w2kr, w1v, w2v, pos_cmp, inv64)


def _nsa_attn_kernel(q_ref, kct_ref, vc_ref, ov_ref, ks_ref, vs_ref, kw_ref, vw_ref, gt_ref,
                     o_ref, qa_ref, oc_ref, psum_ref, m_ref, acc_ref, alpha_ref, p_ref, pw_ref,
                     *, tq, kb, n_slc, wg, rc):
    i = pl.program_id(2)
    q0 = pl.multiple_of(i * tq, tq)
    n_rows = NSA_HPG * tq
    heads = range(NSA_HPG)
    hrows = lambda h: slice(h * tq, (h + 1) * tq)
    qrow = q0 + lax.broadcasted_iota(jnp.int32, (tq, 1), 0)
    q_heads = [q_ref[0, :, h * HEAD_DIM:(h + 1) * HEAD_DIM] for h in heads]
    for h in heads:
        qa_ref[hrows(h), :] = jnp.concatenate([q_heads[h], q_heads[h]], axis=1)

    n_cmp_rows = kct_ref.shape[3]
    s_all = jnp.dot(qa_ref[...], kct_ref[0, 0], preferred_element_type=F32)
    cmp_end = CMP_STRIDE * lax.broadcasted_iota(jnp.int32, (1, n_cmp_rows), 1) + (CMP_LEN - 1)
    c_bias = jnp.where(cmp_end <= qrow, 0.0, NEG)
    row_valid = jnp.where(qrow >= CMP_LEN - 1, 1.0, 0.0)
    for r0 in range(0, n_rows, rc):
        rows = slice(r0, r0 + rc)
        local = slice(r0 % tq, r0 % tq + rc)
        s = s_all[rows] + c_bias[local]
        e = jnp.exp2(s - jnp.max(s, axis=1, keepdims=True))
        p = e * (row_valid[local] / jnp.sum(e, axis=1, keepdims=True))
        if r0 < tq:
            psum_ref[local, :] = p
        else:
            psum_ref[local, :] += p
        p_ref[rows, 0:n_cmp_rows] = p.astype(p_ref.dtype)
    oc_ref[...] = jnp.dot(p_ref[:, 0:n_cmp_rows], vc_ref[0, 0], preferred_element_type=F32)
    imp = jnp.dot(jnp.concatenate(_split_bf16(psum_ref[...]), axis=1), ov_ref[...],
                  preferred_element_type=F32)
    imp_t = imp.T[:n_slc]

    qpos = q0 + lax.broadcasted_iota(jnp.int32, (1, tq), 1)
    blk = lax.broadcasted_iota(jnp.int32, (n_slc, 1), 0)
    cur = qpos >> SLC_SHIFT
    forced = (blk == 0) | (blk == cur) | (blk == cur - 1)
    causal_blk = blk * SLC_LEN <= qpos
    score = jnp.where(causal_blk, imp_t + FORCE_BONUS * forced.astype(F32), NEG)
    sub = lax.broadcasted_iota(jnp.int32, (8, 1), 0)
    groups = [score[8 * v:8 * v + 8] for v in range(n_slc // 8)]
    counts = [jnp.zeros((8, tq), F32) for _ in groups]
    for m in range(n_slc):
        row = score[m:m + 1]
        for v, sv in enumerate(groups):
            if v < m // 8:
                ahead = row > sv
            elif v > m // 8:
                ahead = row >= sv
            else:
                ahead = (row > sv) | ((row == sv) & (sub > m % 8))
            counts[v] = counts[v] + jnp.where(ahead, 1.0, 0.0)
    rank = jnp.concatenate(counts, axis=0)
    sel_bias_t = jnp.where(rank < SLC_TOP_N, 0.0, NEG)
    sel_bias = jnp.concatenate([sel_bias_t, jnp.zeros((LANES - n_slc, tq), F32)], axis=0).T
    sel_bias = sel_bias[:, :HEAD_DIM].astype(qa_ref.dtype)
    for h in heads:
        qa_ref[hrows(h), :] = jnp.concatenate([q_heads[h], sel_bias], axis=1)

    tok_bias = jnp.where(lax.broadcasted_iota(jnp.int32, (1, kb), 1)
                         <= lax.broadcasted_iota(jnp.int32, (tq, 1), 0), 0.0, NEG)

    def chunk(start, first):
        k_t = ks_ref[0, 0, pl.ds(start, kb), :]
        v_t = vs_ref[0, 0, pl.ds(start, kb), :]
        s_all = lax.dot_general(qa_ref[...], k_t, NT_DIMS, preferred_element_type=F32)
        for r0 in range(0, n_rows, rc):
            rows = slice(r0, r0 + rc)
            s = s_all[rows]
            if first:
                s = s + tok_bias[r0 % tq:r0 % tq + rc]
                m_new = jnp.broadcast_to(jnp.max(s, axis=1, keepdims=True), (rc, LANES))
            else:
                m_old = m_ref[rows, :]
                m_new = jnp.maximum(m_old, jnp.max(s, axis=1, keepdims=True))
                alpha_ref[rows, :] = jnp.exp2(m_old - m_new)
            m_ref[rows, :] = m_new
            m_wide = jnp.concatenate([m_new] * (kb // LANES), axis=1)
            p_ref[rows, :] = jnp.exp2(s - m_wide).astype(p_ref.dtype)
        pv = jnp.dot(p_ref[...], v_t, preferred_element_type=F32)
        if first:
            acc_ref[...] = pv
        else:
            for r0 in range(0, n_rows, rc):
                rows = slice(r0, r0 + rc)
                acc_ref[rows, :] = alpha_ref[rows, :] * acc_ref[rows, :] + pv[rows]

    chunk(q0, True)

    def chunk_body(c, carry):
        chunk(pl.multiple_of(c * kb, kb), False)
        return carry

    lax.fori_loop(0, q0 // kb, chunk_body, 0)

    w_keys = WINDOW + tq
    w_start = pl.multiple_of(jnp.maximum(q0 - WINDOW, 0), tq)
    w_diff = qrow - (w_start + lax.broadcasted_iota(jnp.int32, (1, w_keys), 1))
    w_bias = jnp.where((w_diff >= 0) & (w_diff < WINDOW), 0.0, NEG)
    k_w = kw_ref[0, 0, pl.ds(w_start, w_keys), :]
    v_w = vw_ref[0, 0, pl.ds(w_start, w_keys), :]
    gates = gt_ref[0]

    def normalised(o_aug):
        return o_aug[:, :HEAD_DIM] * (1.0 / o_aug[:, HEAD_DIM:HEAD_DIM + 1])

    for h0 in range(0, NSA_HPG, wg):
        rows = slice(h0 * tq, (h0 + wg) * tq)
        s_all = lax.dot_general(qa_ref[rows, :], k_w, NT_DIMS, preferred_element_type=F32)
        for r0 in range(0, wg * tq, rc):
            local = slice(r0 % tq, r0 % tq + rc)
            s = s_all[r0:r0 + rc] + w_bias[local]
            pw_ref[r0:r0 + rc, :] = jnp.exp2(s - jnp.max(s, axis=1, keepdims=True)).astype(pw_ref.dtype)
        o_w = jnp.dot(pw_ref[...], v_w, preferred_element_type=F32)
        outs = []
        for h in range(h0, h0 + wg):
            local = slice((h - h0) * tq, (h - h0 + 1) * tq)
            g_c, g_s, g_w = (gates[:, 3 * h + r:3 * h + r + 1] for r in range(3))
            outs.append(g_c * oc_ref[hrows(h), :HEAD_DIM] + g_s * normalised(acc_ref[hrows(h), :])
                        + g_w * normalised(o_w[local]))
        for hp in range(wg // 2):
            col0 = (h0 + 2 * hp) * HEAD_DIM
            pair = jnp.concatenate(outs[2 * hp:2 * hp + 2], axis=1)
            o_ref[0, :, col0:col0 + LANES] = pair.astype(o_ref.dtype)


def _nsa_attention(q3, kct, vc, ov, ksa, vsa, kwa, vwa, gt, tq=256, wg=4, rc=64):
    b, s, _ = q3.shape
    G = NSA_KV_GROUPS
    kb = tq
    n_slc = s // SLC_LEN
    nq = s // tq
    n_cmp_rows = kct.shape[3]
    gw = NSA_HPG * HEAD_DIM
    assert n_slc <= HEAD_DIM and tq % SLC_LEN == 0 and WINDOW % tq == 0 and s >= WINDOW + tq
    assert n_cmp_rows <= kb and n_slc % 8 == 0 and tq % rc == 0
    kv_spec = pl.BlockSpec((1, 1, s, LANES), lambda bi, g, i: (g, bi, 0, 0))
    return pl.pallas_call(
        functools.partial(_nsa_attn_kernel, tq=tq, kb=kb, n_slc=n_slc, wg=wg, rc=rc),
        grid=(b, G, nq),
        in_specs=[pl.BlockSpec((1, tq, gw), lambda bi, g, i: (bi, i, g)),
                  pl.BlockSpec((1, 1, LANES, n_cmp_rows), lambda bi, g, i: (bi, g, 0, 0)),
                  pl.BlockSpec((1, 1, n_cmp_rows, LANES), lambda bi, g, i: (bi, g, 0, 0)),
                  _const_spec(ov.shape),
                  kv_spec, kv_spec, kv_spec, kv_spec,
                  pl.BlockSpec((1, tq, LANES), lambda bi, g, i: (g, bi * nq + i, 0))],
        out_specs=pl.BlockSpec((1, tq, gw), lambda bi, g, i: (bi, i, g)),
        out_shape=jax.ShapeDtypeStruct((b, s, D_MODEL), BF16),
        scratch_shapes=[pltpu.VMEM((NSA_HPG * tq, LANES), BF16),
                        pltpu.VMEM((NSA_HPG * tq, LANES), F32),
                        pltpu.VMEM((tq, n_cmp_rows), F32),
                        pltpu.VMEM((NSA_HPG * tq, LANES), F32),
                        pltpu.VMEM((NSA_HPG * tq, LANES), F32),
                        pltpu.VMEM((NSA_HPG * tq, LANES), F32),
                        pltpu.VMEM((NSA_HPG * tq, kb), BF16),
                        pltpu.VMEM((wg * tq, WINDOW + tq), BF16)],
        compiler_params=_params(("arbitrary", "arbitrary", "arbitrary")),
        name="nsa_attention",
    )(q3, kct, vc, ov, ksa, vsa, kwa, vwa, gt)


def _nsa_layer_attention(hx2, norm_g, positions, w_in, pe_k, pe_v, w1k, w2k, w1v, w2v, b, s):
    t = b * s
    G = NSA_KV_GROUPS
    w_pad = jnp.pad(w_in, ((0, 0), (0, NSA_W_COLS - w_in.shape[1]))).astype(BF16)
    inv = ROPE_THETA ** (-jnp.arange(HALF, dtype=F32) / HALF)
    inv2 = jnp.tile(inv, LANES // HALF)[None, :]
    sgn2 = jnp.tile(jnp.concatenate([-jnp.ones(HALF, F32), jnp.ones(HALF, F32)]), LANES // HEAD_DIM)[None, :]
    q, kc_tok, vc_tok, ksa, vsa, kwa, vwa, gt = _nsa_in_proj(
        hx2, norm_g, w_pad, positions.reshape(t, 1), inv2, sgn2, s)

    nrow = s // CMP_STRIDE
    wide = CMP_STRIDE * HEAD_DIM
    k16 = kc_tok.reshape(G, b, nrow, wide)
    v16 = vc_tok.reshape(G, b, nrow, wide)
    pek = jnp.broadcast_to(pe_k.reshape(1, CMP_LEN * HEAD_DIM), (8, CMP_LEN * HEAD_DIM))
    pev = jnp.broadcast_to(pe_v.reshape(1, CMP_LEN * HEAD_DIM), (8, CMP_LEN * HEAD_DIM))
    w2k_rot = jnp.concatenate([-w2k[:, HALF:], w2k[:, :HALF]], axis=1)
    end_idx = jnp.minimum(jnp.arange(nrow) * CMP_STRIDE + CMP_LEN - 1, s - 1)
    pos_cmp = positions[:, end_idx][:, :, None]
    inv64 = jnp.tile(inv, 2)[None, :]
    kct, vc = _nsa_compress(k16, v16, pek, pev, w1k, w2k, w2k_rot, w1v, w2v, pos_cmp, inv64)

    n_slc = s // SLC_LEN
    c0 = np.arange(nrow)[:, None] * CMP_STRIDE
    s0 = np.arange(n_slc)[None, :] * SLC_LEN
    ov = np.clip(np.minimum(c0 + CMP_LEN, s0 + SLC_LEN) - np.maximum(c0, s0), 0, None) / CMP_LEN
    ov[(s - CMP_LEN) // CMP_STRIDE + 1:, :] = 0.0
    ov = np.pad(ov, ((0, 0), (0, LANES - n_slc)))
    ov = jnp.asarray(np.concatenate([ov, ov], axis=0), BF16)

    aug4 = lambda a: a.reshape(G, b, s, LANES)
    return _nsa_attention(q.reshape(b, s, D_MODEL), kct, vc, ov, aug4(ksa), aug4(vsa),
                          aug4(kwa), aug4(vwa), gt)


def kernel(x, positions, norm_mix, sba_w_in, sba_w_out, nsa_w_in, nsa_cmp_pos_k, nsa_cmp_pos_v,
           nsa_cmp_k_w1, nsa_cmp_k_w2, nsa_cmp_v_w1, nsa_cmp_v_w2, nsa_w_out, norm_ffn,
           ffn_w_up, ffn_conv_w, ffn_conv_b, ffn_w_down, norm_final):
    b, s, d = x.shape
    t = b * s
    depth = norm_mix.shape[0]
    x2 = x.reshape(t, d)
    g_final = norm_final.reshape(1, d)
    for layer in range(depth):
        j = layer // 2
        g_mix = norm_mix[layer].reshape(1, d)
        if layer % 2 == 0:
            qkv = _sba_in_proj(x2, g_mix, sba_w_in[j].astype(BF16))
            o = _sba_attention(qkv.reshape(b, s, 3 * d), b, s)
            w_out = sba_w_out[j]
        else:
            o = _nsa_layer_attention(x2, g_mix, positions, nsa_w_in[j], nsa_cmp_pos_k[j],
                                     nsa_cmp_pos_v[j], nsa_cmp_k_w1[j], nsa_cmp_k_w2[j],
                                     nsa_cmp_v_w1[j], nsa_cmp_v_w2[j], b, s)
            w_out = nsa_w_out[j]
        x2 = _out_proj_residual(x2, o.reshape(t, d), w_out.astype(BF16))
        x2 = _conv_ffn(x2, norm_ffn[layer].reshape(1, d), ffn_w_up[layer].astype(BF16),
                       ffn_conv_w[layer], ffn_conv_b[layer].reshape(1, -1),
                       ffn_w_down[layer].astype(BF16), g_final, s,
                       final_norm=(layer == depth - 1))
    return x2.reshape(b, s, d)
```

```python
import functools

import numpy as np
import jax
import jax.numpy as jnp
from jax import lax
from jax.experimental import pallas as pl
from jax.experimental.pallas import tpu as pltpu

D_MODEL = 1024
N_HEADS = 16
HEAD_DIM = 64
HALF = HEAD_DIM // 2
NSA_KV_GROUPS = 2
NSA_HPG = N_HEADS // NSA_KV_GROUPS
CMP_LEN = 32
CMP_STRIDE = 16
CMP_HIDDEN = 2 * HEAD_DIM
SLC_LEN = 64
SLC_SHIFT = SLC_LEN.bit_length() - 1
SLC_TOP_N = 16
WINDOW = 512
ROPE_THETA = 10000.0
D_FF = 2816
RMS_EPS = 1e-6
NEG = -1e30
FORCE_BONUS = 1e4
LOG2E = float(np.log2(np.e))
Q_SCALE = LOG2E * HEAD_DIM ** -0.5

LANES = 128
VMEM_LIMIT = 56 * 1024 * 1024

F32 = jnp.float32
BF16 = jnp.bfloat16
HIGHEST = lax.Precision.HIGHEST
NT_DIMS = (((1,), (1,)), ((), ()))


def _params(semantics):
    return pltpu.CompilerParams(dimension_semantics=semantics, vmem_limit_bytes=VMEM_LIMIT)


def _rmsnorm(x, g):
    return x * lax.rsqrt(jnp.mean(x * x, axis=-1, keepdims=True) + RMS_EPS) * g


def _const_spec(shape):
    return pl.BlockSpec(shape, lambda *_: (0,) * len(shape), pipeline_mode=pl.Buffered(1))


def _split_bf16(x):
    hi = x.astype(BF16)
    return hi, (x - hi.astype(F32)).astype(BF16)


def _sba_in_proj_kernel(x_ref, g_ref, w_ref, o_ref, *, n_chunk):
    hn = _rmsnorm(x_ref[...], g_ref[...]).astype(w_ref.dtype)
    n = w_ref.shape[1]
    for c in range(n // n_chunk):
        cols = slice(c * n_chunk, (c + 1) * n_chunk)
        y = jnp.dot(hn, w_ref[:, cols], preferred_element_type=F32)
        if c * n_chunk < D_MODEL:
            y = y * Q_SCALE
        o_ref[:, cols] = y.astype(o_ref.dtype)


def _sba_in_proj(x2, g, w, tm=512):
    t, d = x2.shape
    n = w.shape[1]
    return pl.pallas_call(
        functools.partial(_sba_in_proj_kernel, n_chunk=512),
        grid=(t // tm,),
        in_specs=[pl.BlockSpec((tm, d), lambda i: (i, 0)),
                  _const_spec((1, d)),
                  _const_spec((d, n))],
        out_specs=pl.BlockSpec((tm, n), lambda i: (i, 0)),
        out_shape=jax.ShapeDtypeStruct((t, n), BF16),
        compiler_params=_params(("arbitrary",)),
        name="sba_in_proj",
    )(x2, g, w)


def _sba_attn_kernel(q_ref, k_ref, v_ref, uu_ref, o_ref, acc_ref, car_ref, hl_ref, lb_ref, a_ref,
                     *, tq, kb, nsub, rc):
    i = pl.program_id(2)
    kt = kb * nsub
    q = q_ref[0]
    lane = lax.broadcasted_iota(jnp.int32, (tq, LANES), 1)
    klane = lax.broadcasted_iota(jnp.int32, (kb, LANES), 1)
    acc_ref[...] = jnp.zeros_like(acc_ref)
    car_ref[...] = jnp.zeros_like(car_ref)
    uu = uu_ref[...]
    rel = (lax.broadcasted_iota(jnp.int32, (rc, kb), 0)
           - lax.broadcasted_iota(jnp.int32, (rc, kb), 1))

    def super_tile(st, diag, slot, prev_slot):
        k_all = k_ref[0, pl.ds(pl.multiple_of(st * kt, kt), kt), :]

        def visibility(j, r0):
            if not diag:
                return 2, None
            if j * kb >= r0 + rc - 1:
                return 0, None
            if j * kb + kb - 1 < r0:
                return 2, None
            return 1, rel > (j * kb - r0)

        def scores(j):
            k_t = k_all[j * kb:(j + 1) * kb]
            k_bd = jnp.concatenate([jnp.where(klane < HEAD_DIM, k_t, jnp.zeros_like(k_t)),
                                    jnp.where(klane >= HEAD_DIM, k_t, jnp.zeros_like(k_t))], axis=0)
            return lax.dot_general(q, k_bd, NT_DIMS, preferred_element_type=F32)

        def log_terms(j, z):
            for h in range(2):
                for r0 in range(0, tq, rc):
                    rows = slice(h * tq + r0, h * tq + r0 + rc)
                    kind, strict = visibility(j, r0)
                    if kind == 0:
                        hl_ref[j, rows, :] = jnp.zeros((rc, 2 * kb), BF16)
                        continue
                    zc = z[r0:r0 + rc, h * kb:(h + 1) * kb]
                    nl = jnp.maximum(zc, 0.0) + jnp.log2(1.0 + jnp.exp2(-jnp.abs(zc)))
                    lb_ref[j, rows, :] = zc - nl
                    if kind == 1:
                        nl = jnp.where(strict, nl, 0.0)
                    hi, lo = _split_bf16(nl)
                    hl_ref[j, rows, 0:kb] = hi
                    hl_ref[j, rows, kb:2 * kb] = lo
            return jnp.dot(hl_ref[j], uu, preferred_element_type=F32)

        def weights(j, r):
            for h in range(2):
                car = car_ref[h]
                for r0 in range(0, tq, rc):
                    rows = slice(h * tq + r0, h * tq + r0 + rc)
                    kind, strict = visibility(j, r0)
                    if kind == 0:
                        a_ref[slot, h, r0:r0 + rc, j * kb:(j + 1) * kb] = jnp.zeros((rc, kb), BF16)
                        continue
                    a = jnp.exp2(lb_ref[j, rows, :] - r[rows, :kb] - car[r0:r0 + rc])
                    if kind == 1:
                        a = jnp.where(strict, a, 0.0)
                    a_ref[slot, h, r0:r0 + rc, j * kb:(j + 1) * kb] = a.astype(BF16)
                car_ref[h] = car + r[h * tq:(h + 1) * tq, kb:]

        z_next = scores(nsub - 1)
        pending = None
        for j in reversed(range(nsub)):
            z = z_next
            if j > 0:
                z_next = scores(j - 1)
            if j == nsub - 1 and prev_slot is not None:
                apply_weights(prev_slot, st + 1)
            r = log_terms(j, z)
            if pending is not None:
                weights(j + 1, pending)
            pending = r
        weights(0, pending)

    def apply_weights(slot, st):
        v_all = v_ref[0, pl.ds(pl.multiple_of(st * kt, kt), kt), :]
        for h in range(2):
            acc_ref[h] += jnp.dot(a_ref[slot, h], v_all, preferred_element_type=F32)

    super_tile(i, True, 0, None)

    def body(t, carry):
        super_tile(i - 1 - t, False, (t + 1) % 2, t % 2)
        return carry

    lax.fori_loop(0, i, body, 0)
    apply_weights(i % 2, 0)
    o_ref[0] = jnp.where(lane < HEAD_DIM, acc_ref[0], acc_ref[1]).astype(o_ref.dtype)


def _sba_attention(qkv, b, s, tq=512, kb=LANES, nsub=4, rc=64):
    assert tq == kb * nsub and s % tq == 0 and tq % rc == 0
    n_pairs = D_MODEL // LANES
    jj = np.arange(kb)
    u = np.concatenate([(jj[:, None] > jj[None, :]).astype(np.float32),
                        np.ones((kb, kb), np.float32)], axis=1)
    uu = jnp.asarray(np.concatenate([u, u], axis=0), BF16)
    return pl.pallas_call(
        functools.partial(_sba_attn_kernel, tq=tq, kb=kb, nsub=nsub, rc=rc),
        grid=(b, n_pairs, s // tq),
        in_specs=[pl.BlockSpec((1, tq, LANES), lambda bi, p, i: (bi, i, p)),
                  pl.BlockSpec((1, s, LANES), lambda bi, p, i: (bi, 0, n_pairs + p)),
                  pl.BlockSpec((1, s, LANES), lambda bi, p, i: (bi, 0, 2 * n_pairs + p)),
                  _const_spec((2 * kb, 2 * kb))],
        out_specs=pl.BlockSpec((1, tq, LANES), lambda bi, p, i: (bi, i, p)),
        out_shape=jax.ShapeDtypeStruct((b, s, D_MODEL), BF16),
        scratch_shapes=[pltpu.VMEM((2, tq, LANES), F32),
                        pltpu.VMEM((2, tq, LANES), F32),
                        pltpu.VMEM((nsub, 2 * tq, 2 * kb), BF16),
                        pltpu.VMEM((nsub, 2 * tq, kb), F32),
                        pltpu.VMEM((2, 2, tq, nsub * kb), BF16)],
        compiler_params=_params(("arbitrary", "arbitrary", "arbitrary")),
        name="sba_attention",
    )(qkv, qkv, qkv, uu)


def _out_proj_kernel(x_ref, o_ref, w_ref, y_ref):
    y_ref[...] = x_ref[...] + jnp.dot(o_ref[...], w_ref[...], preferred_element_type=F32)


def _out_proj_residual(x2, o2, w, tm=512):
    t, d = x2.shape
    return pl.pallas_call(
        _out_proj_kernel,
        grid=(t // tm,),
        in_specs=[pl.BlockSpec((tm, d), lambda i: (i, 0)),
                  pl.BlockSpec((tm, d), lambda i: (i, 0)),
                  _const_spec((d, d))],
        out_specs=pl.BlockSpec((tm, d), lambda i: (i, 0)),
        out_shape=jax.ShapeDtypeStruct((t, d), F32),
        compiler_params=_params(("arbitrary",)),
        name="out_proj_residual",
    )(x2, o2, w)


def _ffn_kernel(x_ref, g_ref, wup_ref, cw_ref, cb_ref, wdn_ref, gf_ref, y_ref,
                carry_ref, sg_ref, sv_ref, act_ref, *, tm, fc, tiles_per_seq, final_norm):
    @pl.when(pl.program_id(0) % tiles_per_seq == 0)
    def _():
        carry_ref[...] = jnp.zeros_like(carry_ref)

    x = x_ref[...]
    hn = _rmsnorm(x, g_ref[...]).astype(wup_ref.dtype)

    def up(col0):
        return jnp.dot(hn, wup_ref[:, col0:col0 + fc], preferred_element_type=F32)

    def conv(u, col0, s_ref):
        cols = slice(col0, col0 + fc)
        s_ref[0:8, :] = carry_ref[:, cols]
        s_ref[8:tm + 8, :] = u
        carry_ref[:, cols] = u[tm - 8:tm, :]
        cw = cw_ref[:, cols]
        c = cb_ref[:, cols] + s_ref[6:tm + 6, :] * cw[0:1]
        c = c + s_ref[7:tm + 7, :] * cw[1:2]
        return c + u * cw[2:3]

    n_chunks = D_FF // fc
    u_next = (up(0), up(D_FF))
    for c in range(n_chunks):
        u_gate, u_val = u_next
        if c + 1 < n_chunks:
            u_next = (up((c + 1) * fc), up(D_FF + (c + 1) * fc))
        gate = conv(u_gate, c * fc, sg_ref)
        val = conv(u_val, D_FF + c * fc, sv_ref)
        act_ref[:, c * fc:(c + 1) * fc] = (gate * jax.nn.sigmoid(gate) * val).astype(act_ref.dtype)
    y = x + jnp.dot(act_ref[...], wdn_ref[...], preferred_element_type=F32)
    if final_norm:
        y = _rmsnorm(y, gf_ref[...])
    y_ref[...] = y


def _conv_ffn(x2, g, w_up, conv_w, conv_b, w_down, g_final, s, final_norm, tm=256, fc=256):
    t, d = x2.shape
    f2 = w_up.shape[1]
    return pl.pallas_call(
        functools.partial(_ffn_kernel, tm=tm, fc=fc, tiles_per_seq=s // tm, final_norm=final_norm),
        grid=(t // tm,),
        in_specs=[pl.BlockSpec((tm, d), lambda i: (i, 0)),
                  _const_spec((1, d)),
                  _const_spec((d, f2)),
                  _const_spec((3, f2)),
                  _const_spec((1, f2)),
                  _const_spec((D_FF, d)),
                  _const_spec((1, d))],
        out_specs=pl.BlockSpec((tm, d), lambda i: (i, 0)),
        out_shape=jax.ShapeDtypeStruct((t, d), F32),
        scratch_shapes=[pltpu.VMEM((8, f2), F32),
                        pltpu.VMEM((tm + 8, fc), F32),
                        pltpu.VMEM((tm + 8, fc), F32),
                        pltpu.VMEM((tm, D_FF), BF16)],
        compiler_params=_params(("arbitrary",)),
        name="conv_ffn",
    )(x2, g, w_up, conv_w, conv_b, w_down, g_final)


NSA_Q_COLS = D_MODEL
NSA_W_COLS = D_MODEL + 7 * LANES
NSA_GATES_PER_GROUP = 3 * NSA_HPG


def _swap_halves(y):
    lane = lax.broadcasted_iota(jnp.int32, y.shape, 1)
    first = (lane % HEAD_DIM) < HALF
    return jnp.where(first, pltpu.roll(y, LANES - HALF, 1), pltpu.roll(y, HALF, 1))


def _nsa_in_proj_kernel(x_ref, g_ref, w_ref, pos_ref, inv_ref, sgn_ref,
                        q_ref, kc_ref, vc_ref, ksa_ref, vsa_ref, kwa_ref, vwa_ref, gt_ref,
                        *, tm, seq):
    hn = _rmsnorm(x_ref[...], g_ref[...]).astype(w_ref.dtype)
    ang = pos_ref[...].astype(F32) * inv_ref[...]
    cos = jnp.cos(ang)
    sin = jnp.sin(ang) * sgn_ref[...]

    def rope(y):
        return y * cos + _swap_halves(y) * sin

    q_chunk = 4 * LANES
    for c in range(NSA_Q_COLS // q_chunk):
        y = jnp.dot(hn, w_ref[:, c * q_chunk:(c + 1) * q_chunk], preferred_element_type=F32)
        for l in range(q_chunk // LANES):
            yl = rope(y[:, l * LANES:(l + 1) * LANES]) * Q_SCALE
            q_ref[:, c * q_chunk + l * LANES:c * q_chunk + (l + 1) * LANES] = yl.astype(q_ref.dtype)
    y = jnp.dot(hn, w_ref[:, NSA_Q_COLS:], preferred_element_type=F32)
    part = lambda n: y[:, n * LANES:(n + 1) * LANES]

    tok = ((pl.program_id(0) % (seq // tm)) * tm
           + lax.broadcasted_iota(jnp.int32, (tm, HEAD_DIM), 0))
    lane = lax.broadcasted_iota(jnp.int32, (tm, HEAD_DIM), 1)
    blk_onehot = jnp.where((tok >> SLC_SHIFT) == lane, 1.0, 0.0)
    ones_col = jnp.where(lane == 0, 1.0, 0.0)
    zeros = jnp.zeros((tm, HEAD_DIM), F32)
    ks, vs, kw, vw = rope(part(2)), part(3), rope(part(4)), part(5)
    for g in range(NSA_KV_GROUPS):
        cols = slice(g * HEAD_DIM, (g + 1) * HEAD_DIM)
        kc_ref[g] = part(0)[:, cols]
        vc_ref[g] = part(1)[:, cols]
        ksa_ref[g] = jnp.concatenate([ks[:, cols], blk_onehot], axis=1).astype(ksa_ref.dtype)
        vsa_ref[g] = jnp.concatenate([vs[:, cols], ones_col], axis=1).astype(vsa_ref.dtype)
        kwa_ref[g] = jnp.concatenate([kw[:, cols], zeros], axis=1).astype(kwa_ref.dtype)
        vwa_ref[g] = jnp.concatenate([vw[:, cols], ones_col], axis=1).astype(vwa_ref.dtype)
    gates = jax.nn.sigmoid(part(6))
    gt_ref[0] = gates
    gt_ref[1] = pltpu.roll(gates, LANES - NSA_GATES_PER_GROUP, 1)


def _nsa_in_proj(x2, g, w, pos2, inv2, sgn2, seq, tm=512):
    t, d = x2.shape
    G = NSA_KV_GROUPS
    tok_g = lambda dt: jax.ShapeDtypeStruct((G, t, HEAD_DIM), dt)
    aug_g = lambda dt: jax.ShapeDtypeStruct((G, t, LANES), dt)
    tok_spec = pl.BlockSpec((G, tm, HEAD_DIM), lambda i: (0, i, 0))
    aug_spec = pl.BlockSpec((G, tm, LANES), lambda i: (0, i, 0))
    return pl.pallas_call(
        functools.partial(_nsa_in_proj_kernel, tm=tm, seq=seq),
        grid=(t // tm,),
        in_specs=[pl.BlockSpec((tm, d), lambda i: (i, 0)),
                  _const_spec((1, d)),
                  _const_spec((d, NSA_W_COLS)),
                  pl.BlockSpec((tm, 1), lambda i: (i, 0)),
                  _const_spec((1, LANES)),
                  _const_spec((1, LANES))],
        out_specs=[pl.BlockSpec((tm, NSA_Q_COLS), lambda i: (i, 0)),
                   tok_spec, tok_spec, aug_spec, aug_spec, aug_spec, aug_spec, aug_spec],
        out_shape=[jax.ShapeDtypeStruct((t, NSA_Q_COLS), BF16),
                   tok_g(F32), tok_g(F32), aug_g(BF16), aug_g(BF16), aug_g(BF16), aug_g(BF16),
                   aug_g(F32)],
        compiler_params=_params(("arbitrary",)),
        name="nsa_in_proj",
    )(x2, g, w, pos2, inv2, sgn2)


def _nsa_compress_kernel(k16_ref, v16_ref, pek_ref, pev_ref, w1k_ref, w2k_ref, w2kr_ref,
                         w1v_ref, w2v_ref, pos_ref, inv_ref, kct_ref, vc_ref):
    half_w = CMP_STRIDE * HEAD_DIM
    nrow = k16_ref.shape[2]

    def hidden(x16_ref, pe_ref, w1_ref):
        x = x16_ref[0, 0]
        y1 = jnp.dot(x, w1_ref[:half_w, :], precision=HIGHEST, preferred_element_type=F32)
        y2 = jnp.dot(x, w1_ref[half_w:, :], precision=HIGHEST, preferred_element_type=F32)
        bias = jnp.dot(pe_ref[...], w1_ref[...], precision=HIGHEST, preferred_element_type=F32)
        return jax.nn.gelu(y1 + pltpu.roll(y2, nrow - 1, 0) + bias[0:1])

    hk = hidden(k16_ref, pek_ref, w1k_ref)
    kc = jnp.dot(hk, w2k_ref[...], precision=HIGHEST, preferred_element_type=F32)
    kc_rot = jnp.dot(hk, w2kr_ref[...], precision=HIGHEST, preferred_element_type=F32)
    ang = pos_ref[0].astype(F32) * inv_ref[...]
    kc = kc * jnp.cos(ang) + kc_rot * jnp.sin(ang)
    hi, lo = _split_bf16(kc)
    kct_ref[0, 0] = jnp.concatenate([hi.astype(F32), lo.astype(F32)], axis=1).T.astype(kct_ref.dtype)
    hv = hidden(v16_ref, pev_ref, w1v_ref)
    vc = jnp.dot(hv, w2v_ref[...], precision=HIGHEST, preferred_element_type=F32)
    vc_ref[0, 0] = jnp.concatenate([vc, jnp.zeros_like(vc)], axis=1).astype(vc_ref.dtype)


def _nsa_compress(k16, v16, pek, pev, w1k, w2k, w2kr, w1v, w2v, pos_cmp, inv64):
    G, b, nrow, wide = k16.shape
    x_spec = pl.BlockSpec((1, 1, nrow, wide), lambda bi, g: (g, bi, 0, 0))
    return pl.pallas_call(
        _nsa_compress_kernel,
        grid=(b, G),
        in_specs=[x_spec, x_spec,
                  _const_spec(pek.shape), _const_spec(pev.shape),
                  _const_spec(w1k.shape), _const_spec(w2k.shape), _const_spec(w2kr.shape),
                  _const_spec(w1v.shape), _const_spec(w2v.shape),
                  pl.BlockSpec((1, nrow, 1), lambda bi, g: (bi, 0, 0)),
                  _const_spec((1, HEAD_DIM))],
        out_specs=[pl.BlockSpec((1, 1, LANES, nrow), lambda bi, g: (bi, g, 0, 0)),
                   pl.BlockSpec((1, 1, nrow, LANES), lambda bi, g: (bi, g, 0, 0))],
        out_shape=[jax.ShapeDtypeStruct((b, G, LANES, nrow), BF16),
                   jax.ShapeDtypeStruct((b, G, nrow, LANES), BF16)],
        compiler_params=_params(("arbitrary", "arbitrary")),
        name="nsa_compress",
    )(k16, v16, pek, pev, w1k, w2k, w2kr, w1v, w2v, pos_cmp, inv64)


def _nsa_attn_kernel(q_ref, kct_ref, vc_ref, ov_ref, ks_ref, vs_ref, kw_ref, vw_ref, gt_ref,
                     o_ref, qa_ref, oc_ref, psum_ref, m_ref, acc_ref, alpha_ref, p_ref, pw_ref,
                     ow_ref, s_ref,
                     *, tq, kb, n_slc, wg, rc):
    i = pl.program_id(2)
    q0 = pl.multiple_of(i * tq, tq)
    n_rows = NSA_HPG * tq
    heads = range(NSA_HPG)
    hrows = lambda h: slice(h * tq, (h + 1) * tq)
    qrow = q0 + lax.broadcasted_iota(jnp.int32, (tq, 1), 0)
    q_heads = [q_ref[0, :, h * HEAD_DIM:(h + 1) * HEAD_DIM] for h in heads]
    for h in heads:
        qa_ref[hrows(h), :] = jnp.concatenate([q_heads[h], q_heads[h]], axis=1)

    n_cmp_rows = kct_ref.shape[3]
    s_all = jnp.dot(qa_ref[...], kct_ref[0, 0], preferred_element_type=F32)
    w_keys = WINDOW + tq
    w_start = pl.multiple_of(jnp.maximum(q0 - WINDOW, 0), tq)
    k_w = kw_ref[0, 0, pl.ds(w_start, w_keys), :]
    s_win = [lax.dot_general(qa_ref[h0 * tq:(h0 + wg) * tq, :], k_w, NT_DIMS,
                             preferred_element_type=F32) for h0 in range(0, NSA_HPG, wg)]

    cmp_end = CMP_STRIDE * lax.broadcasted_iota(jnp.int32, (1, n_cmp_rows), 1) + (CMP_LEN - 1)
    c_bias = jnp.where(cmp_end <= qrow, 0.0, NEG)
    row_valid = jnp.where(qrow >= CMP_LEN - 1, 1.0, 0.0)
    for r0 in range(0, n_rows, rc):
        rows = slice(r0, r0 + rc)
        local = slice(r0 % tq, r0 % tq + rc)
        s = s_all[rows] + c_bias[local]
        e = jnp.exp2(s - jnp.max(s, axis=1, keepdims=True))
        p = e * (row_valid[local] / jnp.sum(e, axis=1, keepdims=True))
        if r0 < tq:
            psum_ref[local, :] = p
        else:
            psum_ref[local, :] += p
        p_ref[rows, 0:n_cmp_rows] = p.astype(p_ref.dtype)
    oc_ref[...] = jnp.dot(p_ref[:, 0:n_cmp_rows], vc_ref[0, 0], preferred_element_type=F32)
    imp = jnp.dot(jnp.concatenate(_split_bf16(psum_ref[...]), axis=1), ov_ref[...],
                  preferred_element_type=F32)
    imp_t = imp.T[:n_slc]

    w_diff = qrow - (w_start + lax.broadcasted_iota(jnp.int32, (1, w_keys), 1))
    w_bias = jnp.where((w_diff >= 0) & (w_diff < WINDOW), 0.0, NEG)
    for r0 in range(0, n_rows, rc):
        s = s_win[r0 // (wg * tq)][r0 % (wg * tq):r0 % (wg * tq) + rc] + w_bias[r0 % tq:r0 % tq + rc]
        pw_ref[r0:r0 + rc, :] = jnp.exp2(s - jnp.max(s, axis=1, keepdims=True)).astype(pw_ref.dtype)
    ow_ref[...] = jnp.dot(pw_ref[...], vw_ref[0, 0, pl.ds(w_start, w_keys), :],
                          preferred_element_type=F32)

    qpos = q0 + lax.broadcasted_iota(jnp.int32, (1, tq), 1)
    blk = lax.broadcasted_iota(jnp.int32, (n_slc, 1), 0)
    cur = qpos >> SLC_SHIFT
    forced = (blk == 0) | (blk == cur) | (blk == cur - 1)
    causal_blk = blk * SLC_LEN <= qpos
    score = jnp.where(causal_blk, imp_t + FORCE_BONUS * forced.astype(F32), NEG)
    sub = lax.broadcasted_iota(jnp.int32, (8, 1), 0)
    groups = [score[8 * v:8 * v + 8] for v in range(n_slc // 8)]
    counts = [jnp.zeros((8, tq), F32) for _ in groups]
    for m in range(n_slc):
        row = score[m:m + 1]
        for v, sv in enumerate(groups):
            if v < m // 8:
                ahead = row > sv
            elif v > m // 8:
                ahead = row >= sv
            else:
                ahead = (row > sv) | ((row == sv) & (sub > m % 8))
            counts[v] = counts[v] + jnp.where(ahead, 1.0, 0.0)
    rank = jnp.concatenate(counts, axis=0)
    sel_bias_t = jnp.where(rank < SLC_TOP_N, 0.0, NEG)
    sel_bias = jnp.concatenate([sel_bias_t, jnp.zeros((LANES - n_slc, tq), F32)], axis=0).T
    sel_bias = sel_bias[:, :HEAD_DIM].astype(qa_ref.dtype)
    for h in heads:
        qa_ref[hrows(h), :] = jnp.concatenate([q_heads[h], sel_bias], axis=1)

    tok_bias = jnp.where(lax.broadcasted_iota(jnp.int32, (1, kb), 1)
                         <= lax.broadcasted_iota(jnp.int32, (tq, 1), 0), 0.0, NEG)

    def chunk_scores(c, slot):
        k_t = ks_ref[0, 0, pl.ds(pl.multiple_of(c * kb, kb), kb), :]
        s_ref[slot] = lax.dot_general(qa_ref[...], k_t, NT_DIMS, preferred_element_type=F32)

    def chunk_update(c, slot, diag):
        for r0 in range(0, n_rows, rc):
            rows = slice(r0, r0 + rc)
            s = s_ref[slot, rows, :]
            if diag:
                s = s + tok_bias[r0 % tq:r0 % tq + rc]
            m_old = m_ref[rows, :]
            m_new = jnp.maximum(m_old, jnp.max(s, axis=1, keepdims=True))
            alpha_ref[rows, :] = jnp.exp2(m_old - m_new)
            m_ref[rows, :] = m_new
            m_wide = jnp.concatenate([m_new] * (kb // LANES), axis=1)
            p_ref[rows, :] = jnp.exp2(s - m_wide).astype(p_ref.dtype)
        v_t = vs_ref[0, 0, pl.ds(pl.multiple_of(c * kb, kb), kb), :]
        pv = jnp.dot(p_ref[...], v_t, preferred_element_type=F32)
        for r0 in range(0, n_rows, rc):
            rows = slice(r0, r0 + rc)
            acc_ref[rows, :] = alpha_ref[rows, :] * acc_ref[rows, :] + pv[rows]

    m_ref[...] = jnp.full(m_ref.shape, NEG, F32)
    acc_ref[...] = jnp.zeros_like(acc_ref)
    chunk_scores(0, 0)

    def pair_body(t, carry):
        c = 2 * t
        chunk_scores(c + 1, 1)
        chunk_update(c, 0, False)
        chunk_scores(c + 2, 0)
        chunk_update(c + 1, 1, False)
        return carry

    lax.fori_loop(0, i // 2, pair_body, 0)

    @pl.when(i % 2 == 0)
    def _():
        chunk_update(i, 0, True)

    @pl.when(i % 2 == 1)
    def _():
        chunk_scores(i, 1)
        chunk_update(i - 1, 0, False)
        chunk_update(i, 1, True)

    gates = gt_ref[0]

    def normalised(o_aug):
        return o_aug[:, :HEAD_DIM] * (1.0 / o_aug[:, HEAD_DIM:HEAD_DIM + 1])

    outs = []
    for h in heads:
        g_c, g_s, g_w = (gates[:, 3 * h + r:3 * h + r + 1] for r in range(3))
        outs.append(g_c * oc_ref[hrows(h), :HEAD_DIM] + g_s * normalised(acc_ref[hrows(h), :])
                    + g_w * normalised(ow_ref[hrows(h), :]))
    for hp in range(NSA_HPG // 2):
        pair = jnp.concatenate(outs[2 * hp:2 * hp + 2], axis=1)
        o_ref[0, :, hp * LANES:(hp + 1) * LANES] = pair.astype(o_ref.dtype)


def _nsa_attention(q3, kct, vc, ov, ksa, vsa, kwa, vwa, gt, tq=256, wg=4, rc=64):
    b, s, _ = q3.shape
    G = NSA_KV_GROUPS
    kb = tq
    n_slc = s // SLC_LEN
    nq = s // tq
    n_cmp_rows = kct.shape[3]
    gw = NSA_HPG * HEAD_DIM
    assert n_slc <= HEAD_DIM and tq % SLC_LEN == 0 and WINDOW % tq == 0 and s >= WINDOW + tq
    assert n_cmp_rows <= kb and n_slc % 8 == 0 and tq % rc == 0
    kv_spec = pl.BlockSpec((1, 1, s, LANES), lambda bi, g, i: (g, bi, 0, 0))
    return pl.pallas_call(
        functools.partial(_nsa_attn_kernel, tq=tq, kb=kb, n_slc=n_slc, wg=wg, rc=rc),
        grid=(b, G, nq),
        in_specs=[pl.BlockSpec((1, tq, gw), lambda bi, g, i: (bi, i, g)),
                  pl.BlockSpec((1, 1, LANES, n_cmp_rows), lambda bi, g, i: (bi, g, 0, 0)),
                  pl.BlockSpec((1, 1, n_cmp_rows, LANES), lambda bi, g, i: (bi, g, 0, 0)),
                  _const_spec(ov.shape),
                  kv_spec, kv_spec, kv_spec, kv_spec,
                  pl.BlockSpec((1, tq, LANES), lambda bi, g, i: (g, bi * nq + i, 0))],
        out_specs=pl.BlockSpec((1, tq, gw), lambda bi, g, i: (bi, i, g)),
        out_shape=jax.ShapeDtypeStruct((b, s, D_MODEL), BF16),
        scratch_shapes=[pltpu.VMEM((NSA_HPG * tq, LANES), BF16),
                        pltpu.VMEM((NSA_HPG * tq, LANES), F32),
                        pltpu.VMEM((tq, n_cmp_rows), F32),
                        pltpu.VMEM((NSA_HPG * tq, LANES), F32),
                        pltpu.VMEM((NSA_HPG * tq, LANES), F32),
                        pltpu.VMEM((NSA_HPG * tq, LANES), F32),
                        pltpu.VMEM((NSA_HPG * tq, kb), BF16),
                        pltpu.VMEM((NSA_HPG * tq, WINDOW + tq), BF16),
                        pltpu.VMEM((NSA_HPG * tq, LANES), F32),
                        pltpu.VMEM((2, NSA_HPG * tq, kb), F32)],
        compiler_params=_params(("arbitrary", "arbitrary", "arbitrary")),
        name="nsa_attention",
    )(q3, kct, vc, ov, ksa, vsa, kwa, vwa, gt)


def _nsa_layer_attention(hx2, norm_g, positions, w_in, pe_k, pe_v, w1k, w2k, w1v, w2v, b, s):
    t = b * s
    G = NSA_KV_GROUPS
    w_pad = jnp.pad(w_in, ((0, 0), (0, NSA_W_COLS - w_in.shape[1]))).astype(BF16)
    inv = ROPE_THETA ** (-jnp.arange(HALF, dtype=F32) / HALF)
    inv2 = jnp.tile(inv, LANES // HALF)[None, :]
    sgn2 = jnp.tile(jnp.concatenate([-jnp.ones(HALF, F32), jnp.ones(HALF, F32)]), LANES // HEAD_DIM)[None, :]
    q, kc_tok, vc_tok, ksa, vsa, kwa, vwa, gt = _nsa_in_proj(
        hx2, norm_g, w_pad, positions.reshape(t, 1), inv2, sgn2, s)

    nrow = s // CMP_STRIDE
    wide = CMP_STRIDE * HEAD_DIM
    k16 = kc_tok.reshape(G, b, nrow, wide)
    v16 = vc_tok.reshape(G, b, nrow, wide)
    pek = jnp.broadcast_to(pe_k.reshape(1, CMP_LEN * HEAD_DIM), (8, CMP_LEN * HEAD_DIM))
    pev = jnp.broadcast_to(pe_v.reshape(1, CMP_LEN * HEAD_DIM), (8, CMP_LEN * HEAD_DIM))
    w2k_rot = jnp.concatenate([-w2k[:, HALF:], w2k[:, :HALF]], axis=1)
    end_idx = jnp.minimum(jnp.arange(nrow) * CMP_STRIDE + CMP_LEN - 1, s - 1)
    pos_cmp = positions[:, end_idx][:, :, None]
    inv64 = jnp.tile(inv, 2)[None, :]
    kct, vc = _nsa_compress(k16, v16, pek, pev, w1k, w2k, w2k_rot, w1v, w2v, pos_cmp, inv64)

    n_slc = s // SLC_LEN
    c0 = np.arange(nrow)[:, None] * CMP_STRIDE
    s0 = np.arange(n_slc)[None, :] * SLC_LEN
    ov = np.clip(np.minimum(c0 + CMP_LEN, s0 + SLC_LEN) - np.maximum(c0, s0), 0, None) / CMP_LEN
    ov[(s - CMP_LEN) // CMP_STRIDE + 1:, :] = 0.0
    ov = np.pad(ov, ((0, 0), (0, LANES - n_slc)))
    ov = jnp.asarray(np.concatenate([ov, ov], axis=0), BF16)

    aug4 = lambda a: a.reshape(G, b, s, LANES)
    return _nsa_attention(q.reshape(b, s, D_MODEL), kct, vc, ov, aug4(ksa), aug4(vsa),
                          aug4(kwa), aug4(vwa), gt)


def kernel(x, positions, norm_mix, sba_w_in, sba_w_out, nsa_w_in, nsa_cmp_pos_k, nsa_cmp_pos_v,
           nsa_cmp_k_w1, nsa_cmp_k_w2, nsa_cmp_v_w1, nsa_cmp_v_w2, nsa_w_out, norm_ffn,
           ffn_w_up, ffn_conv_w, ffn_conv_b, ffn_w_down, norm_final):
    b, s, d = x.shape
    t = b * s
    depth = norm_mix.shape[0]
    x2 = x.reshape(t, d)
    g_final = norm_final.reshape(1, d)
    for layer in range(depth):
        j = layer // 2
        g_mix = norm_mix[layer].reshape(1, d)
        if layer % 2 == 0:
            qkv = _sba_in_proj(x2, g_mix, sba_w_in[j].astype(BF16))
            o = _sba_attention(qkv.reshape(b, s, 3 * d), b, s)
            w_out = sba_w_out[j]
        else:
            o = _nsa_layer_attention(x2, g_mix, positions, nsa_w_in[j], nsa_cmp_pos_k[j],
                                     nsa_cmp_pos_v[j], nsa_cmp_k_w1[j], nsa_cmp_k_w2[j],
                                     nsa_cmp_v_w1[j], nsa_cmp_v_w2[j], b, s)
            w_out = nsa_w_out[j]
        x2 = _out_proj_residual(x2, o.reshape(t, d), w_out.astype(BF16))
        x2 = _conv_ffn(x2, norm_ffn[layer].reshape(1, d), ffn_w_up[layer].astype(BF16),
                       ffn_conv_w[layer], ffn_conv_b[layer].reshape(1, -1),
                       ffn_w_down[layer].astype(BF16), g_final, s,
                       final_norm=(layer == depth - 1))
    return x2.reshape(b, s, d)
```

```python
import functools

import numpy as np
import jax
import jax.numpy as jnp
from jax import lax
from jax.experimental import pallas as pl
from jax.experimental.pallas import tpu as pltpu

D_MODEL = 1024
N_HEADS = 16
HEAD_DIM = 64
HALF = HEAD_DIM // 2
NSA_KV_GROUPS = 2
NSA_HPG = N_HEADS // NSA_KV_GROUPS
CMP_LEN = 32
CMP_STRIDE = 16
CMP_HIDDEN = 2 * HEAD_DIM
SLC_LEN = 64
SLC_SHIFT = SLC_LEN.bit_length() - 1
SLC_TOP_N = 16
WINDOW = 512
ROPE_THETA = 10000.0
D_FF = 2816
RMS_EPS = 1e-6
NEG = -1e30
FORCE_BONUS = 1e4
LOG2E = float(np.log2(np.e))
Q_SCALE = LOG2E * HEAD_DIM ** -0.5

LANES = 128
VMEM_LIMIT = 56 * 1024 * 1024

F32 = jnp.float32
BF16 = jnp.bfloat16
HIGHEST = lax.Precision.HIGHEST
NT_DIMS = (((1,), (1,)), ((), ()))


def _params(semantics):
    return pltpu.CompilerParams(dimension_semantics=semantics, vmem_limit_bytes=VMEM_LIMIT)


def _rmsnorm(x, g):
    return x * lax.rsqrt(jnp.mean(x * x, axis=-1, keepdims=True) + RMS_EPS) * g


def _const_spec(shape):
    return pl.BlockSpec(shape, lambda *_: (0,) * len(shape), pipeline_mode=pl.Buffered(1))


def _split_bf16(x):
    hi = x.astype(BF16)
    return hi, (x - hi.astype(F32)).astype(BF16)


def _sba_in_proj_kernel(x_ref, g_ref, w_ref, o_ref, *, n_chunk):
    hn = _rmsnorm(x_ref[...], g_ref[...]).astype(w_ref.dtype)
    n = w_ref.shape[1]
    for c in range(n // n_chunk):
        cols = slice(c * n_chunk, (c + 1) * n_chunk)
        y = jnp.dot(hn, w_ref[:, cols], preferred_element_type=F32)
        if c * n_chunk < D_MODEL:
            y = y * Q_SCALE
        o_ref[:, cols] = y.astype(o_ref.dtype)


def _sba_in_proj(x2, g, w, tm=512):
    t, d = x2.shape
    n = w.shape[1]
    return pl.pallas_call(
        functools.partial(_sba_in_proj_kernel, n_chunk=512),
        grid=(t // tm,),
        in_specs=[pl.BlockSpec((tm, d), lambda i: (i, 0)),
                  _const_spec((1, d)),
                  _const_spec((d, n))],
        out_specs=pl.BlockSpec((tm, n), lambda i: (i, 0)),
        out_shape=jax.ShapeDtypeStruct((t, n), BF16),
        compiler_params=_params(("arbitrary",)),
        name="sba_in_proj",
    )(x2, g, w)


def _sba_attn_kernel(q_ref, k_ref, v_ref, uu_ref, o_ref, acc_ref, car_ref, hl_ref, lb_ref, a_ref,
                     *, tq, kb, nsub, rc):
    i = pl.program_id(2)
    kt = kb * nsub
    q = q_ref[0]
    lane = lax.broadcasted_iota(jnp.int32, (tq, LANES), 1)
    klane = lax.broadcasted_iota(jnp.int32, (kb, LANES), 1)
    acc_ref[...] = jnp.zeros_like(acc_ref)
    car_ref[...] = jnp.zeros_like(car_ref)
    ud = uu_ref[...]
    rel = (lax.broadcasted_iota(jnp.int32, (rc, 2 * kb), 0)
           - (lax.broadcasted_iota(jnp.int32, (rc, 2 * kb), 1) & (kb - 1)))

    def super_tile(st, diag, slot, prev_slot):
        k_all = k_ref[0, pl.ds(pl.multiple_of(st * kt, kt), kt), :]

        def visibility(j, r0):
            if not diag:
                return 2, None
            if j * kb >= r0 + rc - 1:
                return 0, None
            if j * kb + kb - 1 < r0:
                return 2, None
            return 1, rel > (j * kb - r0)

        def scores(j):
            k_t = k_all[j * kb:(j + 1) * kb]
            k_bd = jnp.concatenate([jnp.where(klane < HEAD_DIM, k_t, jnp.zeros_like(k_t)),
                                    jnp.where(klane >= HEAD_DIM, k_t, jnp.zeros_like(k_t))], axis=0)
            return lax.dot_general(q, k_bd, NT_DIMS, preferred_element_type=F32)

        def log_terms(j, z):
            totals = []
            for r0 in range(0, tq, rc):
                rows = slice(r0, r0 + rc)
                kind, strict = visibility(j, r0)
                if kind == 0:
                    hl_ref[j, rows, :] = jnp.zeros((rc, 2 * kb), BF16)
                    totals.append(None)
                    continue
                zc = z[rows]
                nl = jnp.maximum(zc, 0.0) + jnp.log2(1.0 + jnp.exp2(-jnp.abs(zc)))
                lb_ref[j, rows, :] = zc - nl
                if kind == 1:
                    nl = jnp.where(strict, nl, 0.0)
                hl_ref[j, rows, :] = nl.astype(BF16)
                totals.append([jnp.sum(nl[:, h * kb:(h + 1) * kb], axis=1, keepdims=True)
                               for h in range(2)])
            return jnp.dot(hl_ref[j], ud, preferred_element_type=F32), totals

        def weights(j, sums):
            cum, totals = sums
            for n, r0 in enumerate(range(0, tq, rc)):
                rows = slice(r0, r0 + rc)
                cols = slice(j * kb, (j + 1) * kb)
                kind, strict = visibility(j, r0)
                if kind == 0:
                    for h in range(2):
                        a_ref[slot, h, rows, cols] = jnp.zeros((rc, kb), BF16)
                    continue
                car = [car_ref[h, rows, :] for h in range(2)]
                a = jnp.exp2(lb_ref[j, rows, :] - cum[rows] - jnp.concatenate(car, axis=1))
                if kind == 1:
                    a = jnp.where(strict, a, 0.0)
                for h in range(2):
                    a_ref[slot, h, rows, cols] = a[:, h * kb:(h + 1) * kb].astype(BF16)
                    car_ref[h, rows, :] = car[h] + totals[n][h]

        z_next = scores(nsub - 1)
        pending = None
        for j in reversed(range(nsub)):
            z = z_next
            if j > 0:
                z_next = scores(j - 1)
            if j == nsub - 1 and prev_slot is not None:
                apply_weights(prev_slot, st + 1)
            r = log_terms(j, z)
            if pending is not None:
                weights(j + 1, pending)
            pending = r
        weights(0, pending)

    def apply_weights(slot, st):
        v_all = v_ref[0, pl.ds(pl.multiple_of(st * kt, kt), kt), :]
        for h in range(2):
            acc_ref[h] += jnp.dot(a_ref[slot, h], v_all, preferred_element_type=F32)

    super_tile(i, True, 0, None)

    def body(t, carry):
        super_tile(i - 1 - t, False, (t + 1) % 2, t % 2)
        return carry

    lax.fori_loop(0, i, body, 0)
    apply_weights(i % 2, 0)
    o_ref[0] = jnp.where(lane < HEAD_DIM, acc_ref[0], acc_ref[1]).astype(o_ref.dtype)


def _sba_attention(qkv, b, s, tq=512, kb=LANES, nsub=4, rc=64):
    assert tq == kb * nsub and s % tq == 0 and tq % rc == 0
    n_pairs = D_MODEL // LANES
    assert kb & (kb - 1) == 0
    jj = np.arange(kb)
    uu = jnp.asarray(np.kron(np.eye(2), (jj[:, None] > jj[None, :]).astype(np.float32)), BF16)
    return pl.pallas_call(
        functools.partial(_sba_attn_kernel, tq=tq, kb=kb, nsub=nsub, rc=rc),
        grid=(b, n_pairs, s // tq),
        in_specs=[pl.BlockSpec((1, tq, LANES), lambda bi, p, i: (bi, i, p)),
                  pl.BlockSpec((1, s, LANES), lambda bi, p, i: (bi, 0, n_pairs + p)),
                  pl.BlockSpec((1, s, LANES), lambda bi, p, i: (bi, 0, 2 * n_pairs + p)),
                  _const_spec((2 * kb, 2 * kb))],
        out_specs=pl.BlockSpec((1, tq, LANES), lambda bi, p, i: (bi, i, p)),
        out_shape=jax.ShapeDtypeStruct((b, s, D_MODEL), BF16),
        scratch_shapes=[pltpu.VMEM((2, tq, LANES), F32),
                        pltpu.VMEM((2, tq, LANES), F32),
                        pltpu.VMEM((nsub, tq, 2 * kb), BF16),
                        pltpu.VMEM((nsub, tq, 2 * kb), F32),
                        pltpu.VMEM((2, 2, tq, nsub * kb), BF16)],
        compiler_params=_params(("arbitrary", "arbitrary", "arbitrary")),
        name="sba_attention",
    )(qkv, qkv, qkv, uu)


def _out_proj_kernel(x_ref, o_ref, w_ref, y_ref):
    y_ref[...] = x_ref[...] + jnp.dot(o_ref[...], w_ref[...], preferred_element_type=F32)


def _out_proj_residual(x2, o2, w, tm=512):
    t, d = x2.shape
    return pl.pallas_call(
        _out_proj_kernel,
        grid=(t // tm,),
        in_specs=[pl.BlockSpec((tm, d), lambda i: (i, 0)),
                  pl.BlockSpec((tm, d), lambda i: (i, 0)),
                  _const_spec((d, d))],
        out_specs=pl.BlockSpec((tm, d), lambda i: (i, 0)),
        out_shape=jax.ShapeDtypeStruct((t, d), F32),
        compiler_params=_params(("arbitrary",)),
        name="out_proj_residual",
    )(x2, o2, w)


def _ffn_kernel(x_ref, g_ref, wup_ref, cw_ref, cb_ref, wdn_ref, gf_ref, y_ref,
                carry_ref, sg_ref, sv_ref, act_ref, *, tm, fc, tiles_per_seq, final_norm):
    @pl.when(pl.program_id(0) % tiles_per_seq == 0)
    def _():
        carry_ref[...] = jnp.zeros_like(carry_ref)

    x = x_ref[...]
    hn = _rmsnorm(x, g_ref[...]).astype(wup_ref.dtype)

    def up(col0):
        return jnp.dot(hn, wup_ref[:, col0:col0 + fc], preferred_element_type=F32)

    def conv(u, col0, s_ref):
        cols = slice(col0, col0 + fc)
        s_ref[0:8, :] = carry_ref[:, cols]
        s_ref[8:tm + 8, :] = u
        carry_ref[:, cols] = u[tm - 8:tm, :]
        cw = cw_ref[:, cols]
        c = cb_ref[:, cols] + s_ref[6:tm + 6, :] * cw[0:1]
        c = c + s_ref[7:tm + 7, :] * cw[1:2]
        return c + u * cw[2:3]

    n_chunks = D_FF // fc
    u_next = (up(0), up(D_FF))
    for c in range(n_chunks):
        u_gate, u_val = u_next
        if c + 1 < n_chunks:
            u_next = (up((c + 1) * fc), up(D_FF + (c + 1) * fc))
        gate = conv(u_gate, c * fc, sg_ref)
        val = conv(u_val, D_FF + c * fc, sv_ref)
        act_ref[:, c * fc:(c + 1) * fc] = (gate * jax.nn.sigmoid(gate) * val).astype(act_ref.dtype)
    y = x + jnp.dot(act_ref[...], wdn_ref[...], preferred_element_type=F32)
    if final_norm:
        y = _rmsnorm(y, gf_ref[...])
    y_ref[...] = y


def _conv_ffn(x2, g, w_up, conv_w, conv_b, w_down, g_final, s, final_norm, tm=256, fc=256):
    t, d = x2.shape
    f2 = w_up.shape[1]
    return pl.pallas_call(
        functools.partial(_ffn_kernel, tm=tm, fc=fc, tiles_per_seq=s // tm, final_norm=final_norm),
        grid=(t // tm,),
        in_specs=[pl.BlockSpec((tm, d), lambda i: (i, 0)),
                  _const_spec((1, d)),
                  _const_spec((d, f2)),
                  _const_spec((3, f2)),
                  _const_spec((1, f2)),
                  _const_spec((D_FF, d)),
                  _const_spec((1, d))],
        out_specs=pl.BlockSpec((tm, d), lambda i: (i, 0)),
        out_shape=jax.ShapeDtypeStruct((t, d), F32),
        scratch_shapes=[pltpu.VMEM((8, f2), F32),
                        pltpu.VMEM((tm + 8, fc), F32),
                        pltpu.VMEM((tm + 8, fc), F32),
                        pltpu.VMEM((tm, D_FF), BF16)],
        compiler_params=_params(("arbitrary",)),
        name="conv_ffn",
    )(x2, g, w_up, conv_w, conv_b, w_down, g_final)


NSA_Q_COLS = D_MODEL
NSA_W_COLS = D_MODEL + 7 * LANES
NSA_GATES_PER_GROUP = 3 * NSA_HPG


def _swap_halves(y):
    lane = lax.broadcasted_iota(jnp.int32, y.shape, 1)
    first = (lane % HEAD_DIM) < HALF
    return jnp.where(first, pltpu.roll(y, LANES - HALF, 1), pltpu.roll(y, HALF, 1))


def _nsa_in_proj_kernel(x_ref, g_ref, w_ref, pos_ref, inv_ref, sgn_ref,
                        q_ref, kc_ref, vc_ref, ksa_ref, vsa_ref, kwa_ref, vwa_ref, gt_ref,
                        *, tm, seq):
    hn = _rmsnorm(x_ref[...], g_ref[...]).astype(w_ref.dtype)
    ang = pos_ref[...].astype(F32) * inv_ref[...]
    cos = jnp.cos(ang)
    sin = jnp.sin(ang) * sgn_ref[...]

    def rope(y):
        return y * cos + _swap_halves(y) * sin

    q_chunk = 4 * LANES
    for c in range(NSA_Q_COLS // q_chunk):
        y = jnp.dot(hn, w_ref[:, c * q_chunk:(c + 1) * q_chunk], preferred_element_type=F32)
        for l in range(q_chunk // LANES):
            yl = rope(y[:, l * LANES:(l + 1) * LANES]) * Q_SCALE
            q_ref[:, c * q_chunk + l * LANES:c * q_chunk + (l + 1) * LANES] = yl.astype(q_ref.dtype)
    y = jnp.dot(hn, w_ref[:, NSA_Q_COLS:], preferred_element_type=F32)
    part = lambda n: y[:, n * LANES:(n + 1) * LANES]

    tok = ((pl.program_id(0) % (seq // tm)) * tm
           + lax.broadcasted_iota(jnp.int32, (tm, HEAD_DIM), 0))
    lane = lax.broadcasted_iota(jnp.int32, (tm, HEAD_DIM), 1)
    blk_onehot = jnp.where((tok >> SLC_SHIFT) == lane, 1.0, 0.0)
    ones_col = jnp.where(lane == 0, 1.0, 0.0)
    zeros = jnp.zeros((tm, HEAD_DIM), F32)
    ks, vs, kw, vw = rope(part(2)), part(3), rope(part(4)), part(5)
    for g in range(NSA_KV_GROUPS):
        cols = slice(g * HEAD_DIM, (g + 1) * HEAD_DIM)
        kc_ref[g] = part(0)[:, cols]
        vc_ref[g] = part(1)[:, cols]
        ksa_ref[g] = jnp.concatenate([ks[:, cols], blk_onehot], axis=1).astype(ksa_ref.dtype)
        vsa_ref[g] = jnp.concatenate([vs[:, cols], ones_col], axis=1).astype(vsa_ref.dtype)
        kwa_ref[g] = jnp.concatenate([kw[:, cols], zeros], axis=1).astype(kwa_ref.dtype)
        vwa_ref[g] = jnp.concatenate([vw[:, cols], ones_col], axis=1).astype(vwa_ref.dtype)
    gates = jax.nn.sigmoid(part(6))
    gt_ref[0] = gates
    gt_ref[1] = pltpu.roll(gates, LANES - NSA_GATES_PER_GROUP, 1)


def _nsa_in_proj(x2, g, w, pos2, inv2, sgn2, seq, tm=512):
    t, d = x2.shape
    G = NSA_KV_GROUPS
    tok_g = lambda dt: jax.ShapeDtypeStruct((G, t, HEAD_DIM), dt)
    aug_g = lambda dt: jax.ShapeDtypeStruct((G, t, LANES), dt)
    tok_spec = pl.BlockSpec((G, tm, HEAD_DIM), lambda i: (0, i, 0))
    aug_spec = pl.BlockSpec((G, tm, LANES), lambda i: (0, i, 0))
    return pl.pallas_call(
        functools.partial(_nsa_in_proj_kernel, tm=tm, seq=seq),
        grid=(t // tm,),
        in_specs=[pl.BlockSpec((tm, d), lambda i: (i, 0)),
                  _const_spec((1, d)),
                  _const_spec((d, NSA_W_COLS)),
                  pl.BlockSpec((tm, 1), lambda i: (i, 0)),
                  _const_spec((1, LANES)),
                  _const_spec((1, LANES))],
        out_specs=[pl.BlockSpec((tm, NSA_Q_COLS), lambda i: (i, 0)),
                   tok_spec, tok_spec, aug_spec, aug_spec, aug_spec, aug_spec, aug_spec],
        out_shape=[jax.ShapeDtypeStruct((t, NSA_Q_COLS), BF16),
                   tok_g(F32), tok_g(F32), aug_g(BF16), aug_g(BF16), aug_g(BF16), aug_g(BF16),
                   aug_g(F32)],
        compiler_params=_params(("arbitrary",)),
        name="nsa_in_proj",
    )(x2, g, w, pos2, inv2, sgn2)


def _nsa_compress_kernel(k16_ref, v16_ref, pek_ref, pev_ref, w1k_ref, w2k_ref, w2kr_ref,
                         w1v_ref, w2v_ref, pos_ref, inv_ref, kct_ref, vc_ref):
    half_w = CMP_STRIDE * HEAD_DIM
    nrow = k16_ref.shape[2]

    def hidden(x16_ref, pe_ref, w1_ref):
        x = x16_ref[0, 0]
        y1 = jnp.dot(x, w1_ref[:half_w, :], precision=HIGHEST, preferred_element_type=F32)
        y2 = jnp.dot(x, w1_ref[half_w:, :], precision=HIGHEST, preferred_element_type=F32)
        bias = jnp.dot(pe_ref[...], w1_ref[...], precision=HIGHEST, preferred_element_type=F32)
        return jax.nn.gelu(y1 + pltpu.roll(y2, nrow - 1, 0) + bias[0:1])

    hk = hidden(k16_ref, pek_ref, w1k_ref)
    kc = jnp.dot(hk, w2k_ref[...], precision=HIGHEST, preferred_element_type=F32)
    kc_rot = jnp.dot(hk, w2kr_ref[...], precision=HIGHEST, preferred_element_type=F32)
    ang = pos_ref[0].astype(F32) * inv_ref[...]
    kc = kc * jnp.cos(ang) + kc_rot * jnp.sin(ang)
    hi, lo = _split_bf16(kc)
    kct_ref[0, 0] = jnp.concatenate([hi.astype(F32), lo.astype(F32)], axis=1).T.astype(kct_ref.dtype)
    hv = hidden(v16_ref, pev_ref, w1v_ref)
    vc = jnp.dot(hv, w2v_ref[...], precision=HIGHEST, preferred_element_type=F32)
    vc_ref[0, 0] = jnp.concatenate([vc, jnp.zeros_like(vc)], axis=1).astype(vc_ref.dtype)


def _nsa_compress(k16, v16, pek, pev, w1k, w2k, w2kr, w1v, w2v, pos_cmp, inv64):
    G, b, nrow, wide = k16.shape
    x_spec = pl.BlockSpec((1, 1, nrow, wide), lambda bi, g: (g, bi, 0, 0))
    return pl.pallas_call(
        _nsa_compress_kernel,
        grid=(b, G),
        in_specs=[x_spec, x_spec,
                  _const_spec(pek.shape), _const_spec(pev.shape),
                  _const_spec(w1k.shape), _const_spec(w2k.shape), _const_spec(w2kr.shape),
                  _const_spec(w1v.shape), _const_spec(w2v.shape),
                  pl.BlockSpec((1, nrow, 1), lambda bi, g: (bi, 0, 0)),
                  _const_spec((1, HEAD_DIM))],
        out_specs=[pl.BlockSpec((1, 1, LANES, nrow), lambda bi, g: (bi, g, 0, 0)),
                   pl.BlockSpec((1, 1, nrow, LANES), lambda bi, g: (bi, g, 0, 0))],
        out_shape=[jax.ShapeDtypeStruct((b, G, LANES, nrow), BF16),
                   jax.ShapeDtypeStruct((b, G, nrow, LANES), BF16)],
        compiler_params=_params(("arbitrary", "arbitrary")),
        name="nsa_compress",
    )(k16, v16, pek, pev, w1k, w2k, w2kr, w1v, w2v, pos_cmp, inv64)


def _nsa_attn_kernel(q_ref, kct_ref, vc_ref, ov_ref, ks_ref, vs_ref, kw_ref, vw_ref, gt_ref,
                     o_ref, qa_ref, oc_ref, psum_ref, m_ref, acc_ref, alpha_ref, p_ref, pw_ref,
                     ow_ref, s_ref,
                     *, tq, kb, n_slc, wg, rc):
    i = pl.program_id(2)
    q0 = pl.multiple_of(i * tq, tq)
    n_rows = NSA_HPG * tq
    heads = range(NSA_HPG)
    hrows = lambda h: slice(h * tq, (h + 1) * tq)
    qrow = q0 + lax.broadcasted_iota(jnp.int32, (tq, 1), 0)
    q_heads = [q_ref[0, :, h * HEAD_DIM:(h + 1) * HEAD_DIM] for h in heads]
    for h in heads:
        qa_ref[hrows(h), :] = jnp.concatenate([q_heads[h], q_heads[h]], axis=1)

    n_cmp_rows = kct_ref.shape[3]
    s_all = jnp.dot(qa_ref[...], kct_ref[0, 0], preferred_element_type=F32)
    w_keys = WINDOW + tq
    w_start = pl.multiple_of(jnp.maximum(q0 - WINDOW, 0), tq)
    k_w = kw_ref[0, 0, pl.ds(w_start, w_keys), :]
    s_win = [lax.dot_general(qa_ref[h0 * tq:(h0 + wg) * tq, :], k_w, NT_DIMS,
                             preferred_element_type=F32) for h0 in range(0, NSA_HPG, wg)]

    cmp_end = CMP_STRIDE * lax.broadcasted_iota(jnp.int32, (1, n_cmp_rows), 1) + (CMP_LEN - 1)
    c_bias = jnp.where(cmp_end <= qrow, 0.0, NEG)
    row_valid = jnp.where(qrow >= CMP_LEN - 1, 1.0, 0.0)
    for r0 in range(0, n_rows, rc):
        rows = slice(r0, r0 + rc)
        local = slice(r0 % tq, r0 % tq + rc)
        s = s_all[rows] + c_bias[local]
        e = jnp.exp2(s - jnp.max(s, axis=1, keepdims=True))
        p = e * (row_valid[local] / jnp.sum(e, axis=1, keepdims=True))
        if r0 < tq:
            psum_ref[local, :] = p
        else:
            psum_ref[local, :] += p
        p_ref[rows, 0:n_cmp_rows] = p.astype(p_ref.dtype)
    oc_ref[...] = jnp.dot(p_ref[:, 0:n_cmp_rows], vc_ref[0, 0], preferred_element_type=F32)
    imp = jnp.dot(jnp.concatenate(_split_bf16(psum_ref[...]), axis=1), ov_ref[...],
                  preferred_element_type=F32)
    imp_t = imp.T[:n_slc]

    w_diff = qrow - (w_start + lax.broadcasted_iota(jnp.int32, (1, w_keys), 1))
    w_bias = jnp.where((w_diff >= 0) & (w_diff < WINDOW), 0.0, NEG)
    for r0 in range(0, n_rows, rc):
        s = s_win[r0 // (wg * tq)][r0 % (wg * tq):r0 % (wg * tq) + rc] + w_bias[r0 % tq:r0 % tq + rc]
        pw_ref[r0:r0 + rc, :] = jnp.exp2(s - jnp.max(s, axis=1, keepdims=True)).astype(pw_ref.dtype)
    ow_ref[...] = jnp.dot(pw_ref[...], vw_ref[0, 0, pl.ds(w_start, w_keys), :],
                          preferred_element_type=F32)

    qpos = q0 + lax.broadcasted_iota(jnp.int32, (1, tq), 1)
    blk = lax.broadcasted_iota(jnp.int32, (n_slc, 1), 0)
    cur = qpos >> SLC_SHIFT
    forced = (blk == 0) | (blk == cur) | (blk == cur - 1)
    causal_blk = blk * SLC_LEN <= qpos
    score = jnp.where(causal_blk, imp_t + FORCE_BONUS * forced.astype(F32), NEG)
    sub = lax.broadcasted_iota(jnp.int32, (8, 1), 0)
    groups = [score[8 * v:8 * v + 8] for v in range(n_slc // 8)]
    counts = [jnp.zeros((8, tq), F32) for _ in groups]
    for m in range(n_slc):
        row = score[m:m + 1]
        for v, sv in enumerate(groups):
            if v < m // 8:
                ahead = row > sv
            elif v > m // 8:
                ahead = row >= sv
            else:
                ahead = (row > sv) | ((row == sv) & (sub > m % 8))
            counts[v] = counts[v] + jnp.where(ahead, 1.0, 0.0)
    rank = jnp.concatenate(counts, axis=0)
    sel_bias_t = jnp.where(rank < SLC_TOP_N, 0.0, NEG)
    sel_bias = jnp.concatenate([sel_bias_t, jnp.zeros((LANES - n_slc, tq), F32)], axis=0).T
    sel_bias = sel_bias[:, :HEAD_DIM].astype(qa_ref.dtype)
    for h in heads:
        qa_ref[hrows(h), :] = jnp.concatenate([q_heads[h], sel_bias], axis=1)

    tok_bias = jnp.where(lax.broadcasted_iota(jnp.int32, (1, kb), 1)
                         <= lax.broadcasted_iota(jnp.int32, (tq, 1), 0), 0.0, NEG)

    def chunk_scores(c, slot):
        k_t = ks_ref[0, 0, pl.ds(pl.multiple_of(c * kb, kb), kb), :]
        s_ref[slot] = lax.dot_general(qa_ref[...], k_t, NT_DIMS, preferred_element_type=F32)

    def chunk_update(c, slot, diag):
        for r0 in range(0, n_rows, rc):
            rows = slice(r0, r0 + rc)
            s = s_ref[slot, rows, :]
            if diag:
                s = s + tok_bias[r0 % tq:r0 % tq + rc]
            m_old = m_ref[rows, :]
            m_new = jnp.maximum(m_old, jnp.max(s, axis=1, keepdims=True))
            alpha_ref[rows, :] = jnp.exp2(m_old - m_new)
            m_ref[rows, :] = m_new
            m_wide = jnp.concatenate([m_new] * (kb // LANES), axis=1)
            p_ref[rows, :] = jnp.exp2(s - m_wide).astype(p_ref.dtype)
        v_t = vs_ref[0, 0, pl.ds(pl.multiple_of(c * kb, kb), kb), :]
        pv = jnp.dot(p_ref[...], v_t, preferred_element_type=F32)
        for r0 in range(0, n_rows, rc):
            rows = slice(r0, r0 + rc)
            acc_ref[rows, :] = alpha_ref[rows, :] * acc_ref[rows, :] + pv[rows]

    m_ref[...] = jnp.full(m_ref.shape, NEG, F32)
    acc_ref[...] = jnp.zeros_like(acc_ref)
    chunk_scores(0, 0)

    def pair_body(t, carry):
        c = 2 * t
        chunk_scores(c + 1, 1)
        chunk_update(c, 0, False)
        chunk_scores(c + 2, 0)
        chunk_update(c + 1, 1, False)
        return carry

    lax.fori_loop(0, i // 2, pair_body, 0)

    @pl.when(i % 2 == 0)
    def _():
        chunk_update(i, 0, True)

    @pl.when(i % 2 == 1)
    def _():
        chunk_scores(i, 1)
        chunk_update(i - 1, 0, False)
        chunk_update(i, 1, True)

    gates = gt_ref[0]

    def normalised(o_aug):
        return o_aug[:, :HEAD_DIM] * (1.0 / o_aug[:, HEAD_DIM:HEAD_DIM + 1])

    outs = []
    for h in heads:
        g_c, g_s, g_w = (gates[:, 3 * h + r:3 * h + r + 1] for r in range(3))
        outs.append(g_c * oc_ref[hrows(h), :HEAD_DIM] + g_s * normalised(acc_ref[hrows(h), :])
                    + g_w * normalised(ow_ref[hrows(h), :]))
    for hp in range(NSA_HPG // 2):
        pair = jnp.concatenate(outs[2 * hp:2 * hp + 2], axis=1)
        o_ref[0, :, hp * LANES:(hp + 1) * LANES] = pair.astype(o_ref.dtype)


def _nsa_attention(q3, kct, vc, ov, ksa, vsa, kwa, vwa, gt, tq=256, wg=4, rc=64):
    b, s, _ = q3.shape
    G = NSA_KV_GROUPS
    kb = tq
    n_slc = s // SLC_LEN
    nq = s // tq
    n_cmp_rows = kct.shape[3]
    gw = NSA_HPG * HEAD_DIM
    assert n_slc <= HEAD_DIM and tq % SLC_LEN == 0 and WINDOW % tq == 0 and s >= WINDOW + tq
    assert n_cmp_rows <= kb and n_slc % 8 == 0 and tq % rc == 0
    kv_spec = pl.BlockSpec((1, 1, s, LANES), lambda bi, g, i: (g, bi, 0, 0))
    return pl.pallas_call(
        functools.partial(_nsa_attn_kernel, tq=tq, kb=kb, n_slc=n_slc, wg=wg, rc=rc),
        grid=(b, G, nq),
        in_specs=[pl.BlockSpec((1, tq, gw), lambda bi, g, i: (bi, i, g)),
                  pl.BlockSpec((1, 1, LANES, n_cmp_rows), lambda bi, g, i: (bi, g, 0, 0)),
                  pl.BlockSpec((1, 1, n_cmp_rows, LANES), lambda bi, g, i: (bi, g, 0, 0)),
                  _const_spec(ov.shape),
                  kv_spec, kv_spec, kv_spec, kv_spec,
                  pl.BlockSpec((1, tq, LANES), lambda bi, g, i: (g, bi * nq + i, 0))],
        out_specs=pl.BlockSpec((1, tq, gw), lambda bi, g, i: (bi, i, g)),
        out_shape=jax.ShapeDtypeStruct((b, s, D_MODEL), BF16),
        scratch_shapes=[pltpu.VMEM((NSA_HPG * tq, LANES), BF16),
                        pltpu.VMEM((NSA_HPG * tq, LANES), F32),
                        pltpu.VMEM((tq, n_cmp_rows), F32),
                        pltpu.VMEM((NSA_HPG * tq, LANES), F32),
                        pltpu.VMEM((NSA_HPG * tq, LANES), F32),
                        pltpu.VMEM((NSA_HPG * tq, LANES), F32),
                        pltpu.VMEM((NSA_HPG * tq, kb), BF16),
                        pltpu.VMEM((NSA_HPG * tq, WINDOW + tq), BF16),
                        pltpu.VMEM((NSA_HPG * tq, LANES), F32),
                        pltpu.VMEM((2, NSA_HPG * tq, kb), F32)],
        compiler_params=_params(("arbitrary", "arbitrary", "arbitrary")),
        name="nsa_attention",
    )(q3, kct, vc, ov, ksa, vsa, kwa, vwa, gt)


def _nsa_layer_attention(hx2, norm_g, positions, w_in, pe_k, pe_v, w1k, w2k, w1v, w2v, b, s):
    t = b * s
    G = NSA_KV_GROUPS
    w_pad = jnp.pad(w_in, ((0, 0), (0, NSA_W_COLS - w_in.shape[1]))).astype(BF16)
    inv = ROPE_THETA ** (-jnp.arange(HALF, dtype=F32) / HALF)
    inv2 = jnp.tile(inv, LANES // HALF)[None, :]
    sgn2 = jnp.tile(jnp.concatenate([-jnp.ones(HALF, F32), jnp.ones(HALF, F32)]), LANES // HEAD_DIM)[None, :]
    q, kc_tok, vc_tok, ksa, vsa, kwa, vwa, gt = _nsa_in_proj(
        hx2, norm_g, w_pad, positions.reshape(t, 1), inv2, sgn2, s)

    nrow = s // CMP_STRIDE
    wide = CMP_STRIDE * HEAD_DIM
    k16 = kc_tok.reshape(G, b, nrow, wide)
    v16 = vc_tok.reshape(G, b, nrow, wide)
    pek = jnp.broadcast_to(pe_k.reshape(1, CMP_LEN * HEAD_DIM), (8, CMP_LEN * HEAD_DIM))
    pev = jnp.broadcast_to(pe_v.reshape(1, CMP_LEN * HEAD_DIM), (8, CMP_LEN * HEAD_DIM))
    w2k_rot = jnp.concatenate([-w2k[:, HALF:], w2k[:, :HALF]], axis=1)
    end_idx = jnp.minimum(jnp.arange(nrow) * CMP_STRIDE + CMP_LEN - 1, s - 1)
    pos_cmp = positions[:, end_idx][:, :, None]
    inv64 = jnp.tile(inv, 2)[None, :]
    kct, vc = _nsa_compress(k16, v16, pek, pev, w1k, w2k, w2k_rot, w1v, w2v, pos_cmp, inv64)

    n_slc = s // SLC_LEN
    c0 = np.arange(nrow)[:, None] * CMP_STRIDE
    s0 = np.arange(n_slc)[None, :] * SLC_LEN
    ov = np.clip(np.minimum(c0 + CMP_LEN, s0 + SLC_LEN) - np.maximum(c0, s0), 0, None) / CMP_LEN
    ov[(s - CMP_LEN) // CMP_STRIDE + 1:, :] = 0.0
    ov = np.pad(ov, ((0, 0), (0, LANES - n_slc)))
    ov = jnp.asarray(np.concatenate([ov, ov], axis=0), BF16)

    aug4 = lambda a: a.reshape(G, b, s, LANES)
    return _nsa_attention(q.reshape(b, s, D_MODEL), kct, vc, ov, aug4(ksa), aug4(vsa),
                          aug4(kwa), aug4(vwa), gt)


def kernel(x, positions, norm_mix, sba_w_in, sba_w_out, nsa_w_in, nsa_cmp_pos_k, nsa_cmp_pos_v,
           nsa_cmp_k_w1, nsa_cmp_k_w2, nsa_cmp_v_w1, nsa_cmp_v_w2, nsa_w_out, norm_ffn,
           ffn_w_up, ffn_conv_w, ffn_conv_b, ffn_w_down, norm_final):
    b, s, d = x.shape
    t = b * s
    depth = norm_mix.shape[0]
    x2 = x.reshape(t, d)
    g_final = norm_final.reshape(1, d)
    for layer in range(depth):
        j = layer // 2
        g_mix = norm_mix[layer].reshape(1, d)
        if layer % 2 == 0:
            qkv = _sba_in_proj(x2, g_mix, sba_w_in[j].astype(BF16))
            o = _sba_attention(qkv.reshape(b, s, 3 * d), b, s)
            w_out = sba_w_out[j]
        else:
            o = _nsa_layer_attention(x2, g_mix, positions, nsa_w_in[j], nsa_cmp_pos_k[j],
                                     nsa_cmp_pos_v[j], nsa_cmp_k_w1[j], nsa_cmp_k_w2[j],
                                     nsa_cmp_v_w1[j], nsa_cmp_v_w2[j], b, s)
            w_out = nsa_w_out[j]
        x2 = _out_proj_residual(x2, o.reshape(t, d), w_out.astype(BF16))
        x2 = _conv_ffn(x2, norm_ffn[layer].reshape(1, d), ffn_w_up[layer].astype(BF16),
                       ffn_conv_w[layer], ffn_conv_b[layer].reshape(1, -1),
                       ffn_w_down[layer].astype(BF16), g_final, s,
                       final_norm=(layer == depth - 1))
    return x2.reshape(b, s, d)
```

```python
import functools

import numpy as np
import jax
import jax.numpy as jnp
from jax import lax
from jax.experimental import pallas as pl
from jax.experimental.pallas import tpu as pltpu

D_MODEL = 1024
N_HEADS = 16
HEAD_DIM = 64
HALF = HEAD_DIM // 2
NSA_KV_GROUPS = 2
NSA_HPG = N_HEADS // NSA_KV_GROUPS
CMP_LEN = 32
CMP_STRIDE = 16
CMP_HIDDEN = 2 * HEAD_DIM
SLC_LEN = 64
SLC_SHIFT = SLC_LEN.bit_length() - 1
SLC_TOP_N = 16
WINDOW = 512
ROPE_THETA = 10000.0
D_FF = 2816
RMS_EPS = 1e-6
NEG = -1e30
FORCE_BONUS = 1e4
LOG2E = float(np.log2(np.e))
Q_SCALE = LOG2E * HEAD_DIM ** -0.5

LANES = 128
VMEM_LIMIT = 56 * 1024 * 1024

F32 = jnp.float32
BF16 = jnp.bfloat16
HIGHEST = lax.Precision.HIGHEST
NT_DIMS = (((1,), (1,)), ((), ()))


def _params(semantics):
    return pltpu.CompilerParams(dimension_semantics=semantics, vmem_limit_bytes=VMEM_LIMIT)


def _rmsnorm(x, g):
    return x * lax.rsqrt(jnp.mean(x * x, axis=-1, keepdims=True) + RMS_EPS) * g


def _const_spec(shape):
    return pl.BlockSpec(shape, lambda *_: (0,) * len(shape), pipeline_mode=pl.Buffered(1))


def _split_bf16(x):
    hi = x.astype(BF16)
    return hi, (x - hi.astype(F32)).astype(BF16)


def _sba_in_proj_kernel(x_ref, g_ref, w_ref, o_ref, *, n_chunk):
    hn = _rmsnorm(x_ref[...], g_ref[...]).astype(w_ref.dtype)
    n = w_ref.shape[1]
    for c in range(n // n_chunk):
        cols = slice(c * n_chunk, (c + 1) * n_chunk)
        y = jnp.dot(hn, w_ref[:, cols], preferred_element_type=F32)
        if c * n_chunk < D_MODEL:
            y = y * Q_SCALE
        o_ref[:, cols] = y.astype(o_ref.dtype)


def _sba_in_proj(x2, g, w, tm=512):
    t, d = x2.shape
    n = w.shape[1]
    return pl.pallas_call(
        functools.partial(_sba_in_proj_kernel, n_chunk=512),
        grid=(t // tm,),
        in_specs=[pl.BlockSpec((tm, d), lambda i: (i, 0)),
                  _const_spec((1, d)),
                  _const_spec((d, n))],
        out_specs=pl.BlockSpec((tm, n), lambda i: (i, 0)),
        out_shape=jax.ShapeDtypeStruct((t, n), BF16),
        compiler_params=_params(("arbitrary",)),
        name="sba_in_proj",
    )(x2, g, w)


def _sba_attn_kernel(q_ref, k_ref, v_ref, uu_ref, o_ref, acc_ref, car_ref, hl_ref, lb_ref, a_ref,
                     *, tq, kb, nsub, rc):
    i = pl.program_id(2)
    kt = kb * nsub
    q = q_ref[0]
    lane = lax.broadcasted_iota(jnp.int32, (tq, LANES), 1)
    klane = lax.broadcasted_iota(jnp.int32, (kb, LANES), 1)
    acc_ref[...] = jnp.zeros_like(acc_ref)
    car_ref[...] = jnp.zeros_like(car_ref)
    ud = uu_ref[...]
    rel = (lax.broadcasted_iota(jnp.int32, (rc, 2 * kb), 0)
           - (lax.broadcasted_iota(jnp.int32, (rc, 2 * kb), 1) & (kb - 1)))

    def super_tile(st, diag, slot, prev_slot):
        k_all = k_ref[0, pl.ds(pl.multiple_of(st * kt, kt), kt), :]

        def visibility(j, r0):
            if not diag:
                return 2, None
            if j * kb >= r0 + rc - 1:
                return 0, None
            if j * kb + kb - 1 < r0:
                return 2, None
            return 1, rel > (j * kb - r0)

        def scores(j):
            k_t = k_all[j * kb:(j + 1) * kb]
            k_bd = jnp.concatenate([jnp.where(klane < HEAD_DIM, k_t, jnp.zeros_like(k_t)),
                                    jnp.where(klane >= HEAD_DIM, k_t, jnp.zeros_like(k_t))], axis=0)
            return lax.dot_general(q, k_bd, NT_DIMS, preferred_element_type=F32)

        def log_terms(j, z):
            totals = []
            for r0 in range(0, tq, rc):
                rows = slice(r0, r0 + rc)
                kind, strict = visibility(j, r0)
                if kind == 0:
                    hl_ref[j, rows, :] = jnp.zeros((rc, 2 * kb), BF16)
                    totals.append(None)
                    continue
                zc = z[rows]
                nl = jnp.maximum(zc, 0.0) + jnp.log2(1.0 + jnp.exp2(-jnp.abs(zc)))
                lb_ref[j, rows, :] = zc - nl
                if kind == 1:
                    nl = jnp.where(strict, nl, 0.0)
                hl_ref[j, rows, :] = nl.astype(BF16)
                totals.append([jnp.sum(nl[:, h * kb:(h + 1) * kb], axis=1, keepdims=True)
                               for h in range(2)])
            return jnp.dot(hl_ref[j], ud, preferred_element_type=F32), totals

        def weights(j, sums):
            cum, totals = sums
            for n, r0 in enumerate(range(0, tq, rc)):
                rows = slice(r0, r0 + rc)
                cols = slice(j * kb, (j + 1) * kb)
                kind, strict = visibility(j, r0)
                if kind == 0:
                    for h in range(2):
                        a_ref[slot, h, rows, cols] = jnp.zeros((rc, kb), BF16)
                    continue
                car = [car_ref[h, rows, :] for h in range(2)]
                a = jnp.exp2(lb_ref[j, rows, :] - cum[rows] - jnp.concatenate(car, axis=1))
                if kind == 1:
                    a = jnp.where(strict, a, 0.0)
                for h in range(2):
                    a_ref[slot, h, rows, cols] = a[:, h * kb:(h + 1) * kb].astype(BF16)
                    car_ref[h, rows, :] = car[h] + totals[n][h]

        z_next = scores(nsub - 1)
        pending = None
        for j in reversed(range(nsub)):
            z = z_next
            if j > 0:
                z_next = scores(j - 1)
            if j == nsub - 1 and prev_slot is not None:
                apply_weights(prev_slot, st + 1)
            r = log_terms(j, z)
            if pending is not None:
                weights(j + 1, pending)
            pending = r
        weights(0, pending)

    def apply_weights(slot, st):
        v_all = v_ref[0, pl.ds(pl.multiple_of(st * kt, kt), kt), :]
        for h in range(2):
            acc_ref[h] += jnp.dot(a_ref[slot, h], v_all, preferred_element_type=F32)

    super_tile(i, True, 0, None)

    def body(t, carry):
        super_tile(i - 1 - t, False, (t + 1) % 2, t % 2)
        return carry

    lax.fori_loop(0, i, body, 0)
    apply_weights(i % 2, 0)
    o_ref[0] = jnp.where(lane < HEAD_DIM, acc_ref[0], acc_ref[1]).astype(o_ref.dtype)


def _sba_attention(qkv, b, s, tq=512, kb=LANES, nsub=4, rc=64):
    assert tq == kb * nsub and s % tq == 0 and tq % rc == 0
    n_pairs = D_MODEL // LANES
    assert kb & (kb - 1) == 0
    jj = np.arange(kb)
    uu = jnp.asarray(np.kron(np.eye(2), (jj[:, None] > jj[None, :]).astype(np.float32)), BF16)
    return pl.pallas_call(
        functools.partial(_sba_attn_kernel, tq=tq, kb=kb, nsub=nsub, rc=rc),
        grid=(b, n_pairs, s // tq),
        in_specs=[pl.BlockSpec((1, tq, LANES), lambda bi, p, i: (bi, i, p)),
                  pl.BlockSpec((1, s, LANES), lambda bi, p, i: (bi, 0, n_pairs + p)),
                  pl.BlockSpec((1, s, LANES), lambda bi, p, i: (bi, 0, 2 * n_pairs + p)),
                  _const_spec((2 * kb, 2 * kb))],
        out_specs=pl.BlockSpec((1, tq, LANES), lambda bi, p, i: (bi, i, p)),
        out_shape=jax.ShapeDtypeStruct((b, s, D_MODEL), BF16),
        scratch_shapes=[pltpu.VMEM((2, tq, LANES), F32),
                        pltpu.VMEM((2, tq, LANES), F32),
                        pltpu.VMEM((nsub, tq, 2 * kb), BF16),
                        pltpu.VMEM((nsub, tq, 2 * kb), F32),
                        pltpu.VMEM((2, 2, tq, nsub * kb), BF16)],
        compiler_params=_params(("arbitrary", "arbitrary", "arbitrary")),
        name="sba_attention",
    )(qkv, qkv, qkv, uu)


def _out_proj_kernel(x_ref, o_ref, w_ref, y_ref):
    y_ref[...] = x_ref[...] + jnp.dot(o_ref[...], w_ref[...], preferred_element_type=F32)


def _out_proj_residual(x2, o2, w, tm=512):
    t, d = x2.shape
    return pl.pallas_call(
        _out_proj_kernel,
        grid=(t // tm,),
        in_specs=[pl.BlockSpec((tm, d), lambda i: (i, 0)),
                  pl.BlockSpec((tm, d), lambda i: (i, 0)),
                  _const_spec((d, d))],
        out_specs=pl.BlockSpec((tm, d), lambda i: (i, 0)),
        out_shape=jax.ShapeDtypeStruct((t, d), F32),
        compiler_params=_params(("arbitrary",)),
        name="out_proj_residual",
    )(x2, o2, w)


def _ffn_kernel(x_ref, g_ref, wup_ref, cw_ref, cb_ref, wdn_ref, gf_ref, y_ref,
                carry_ref, sg_ref, sv_ref, act_ref, *, tm, fc, tiles_per_seq, final_norm):
    @pl.when(pl.program_id(0) % tiles_per_seq == 0)
    def _():
        carry_ref[...] = jnp.zeros_like(carry_ref)

    x = x_ref[...]
    hn = _rmsnorm(x, g_ref[...]).astype(wup_ref.dtype)

    def up(col0):
        return jnp.dot(hn, wup_ref[:, col0:col0 + fc], preferred_element_type=F32)

    def conv(u, col0, s_ref):
        cols = slice(col0, col0 + fc)
        s_ref[0:8, :] = carry_ref[:, cols]
        s_ref[8:tm + 8, :] = u
        carry_ref[:, cols] = u[tm - 8:tm, :]
        cw = cw_ref[:, cols]
        c = cb_ref[:, cols] + s_ref[6:tm + 6, :] * cw[0:1]
        c = c + s_ref[7:tm + 7, :] * cw[1:2]
        return c + u * cw[2:3]

    n_chunks = D_FF // fc
    u_next = (up(0), up(D_FF))
    for c in range(n_chunks):
        u_gate, u_val = u_next
        if c + 1 < n_chunks:
            u_next = (up((c + 1) * fc), up(D_FF + (c + 1) * fc))
        gate = conv(u_gate, c * fc, sg_ref)
        val = conv(u_val, D_FF + c * fc, sv_ref)
        act_ref[:, c * fc:(c + 1) * fc] = (gate * jax.nn.sigmoid(gate) * val).astype(act_ref.dtype)
    y = x + jnp.dot(act_ref[...], wdn_ref[...], preferred_element_type=F32)
    if final_norm:
        y = _rmsnorm(y, gf_ref[...])
    y_ref[...] = y


def _conv_ffn(x2, g, w_up, conv_w, conv_b, w_down, g_final, s, final_norm, tm=256, fc=256):
    t, d = x2.shape
    f2 = w_up.shape[1]
    return pl.pallas_call(
        functools.partial(_ffn_kernel, tm=tm, fc=fc, tiles_per_seq=s // tm, final_norm=final_norm),
        grid=(t // tm,),
        in_specs=[pl.BlockSpec((tm, d), lambda i: (i, 0)),
                  _const_spec((1, d)),
                  _const_spec((d, f2)),
                  _const_spec((3, f2)),
                  _const_spec((1, f2)),
                  _const_spec((D_FF, d)),
                  _const_spec((1, d))],
        out_specs=pl.BlockSpec((tm, d), lambda i: (i, 0)),
        out_shape=jax.ShapeDtypeStruct((t, d), F32),
        scratch_shapes=[pltpu.VMEM((8, f2), F32),
                        pltpu.VMEM((tm + 8, fc), F32),
                        pltpu.VMEM((tm + 8, fc), F32),
                        pltpu.VMEM((tm, D_FF), BF16)],
        compiler_params=_params(("arbitrary",)),
        name="conv_ffn",
    )(x2, g, w_up, conv_w, conv_b, w_down, g_final)


NSA_Q_COLS = D_MODEL
NSA_W_COLS = D_MODEL + 7 * LANES
NSA_GATES_PER_GROUP = 3 * NSA_HPG


def _swap_halves(y):
    lane = lax.broadcasted_iota(jnp.int32, y.shape, 1)
    first = (lane % HEAD_DIM) < HALF
    return jnp.where(first, pltpu.roll(y, LANES - HALF, 1), pltpu.roll(y, HALF, 1))


def _nsa_in_proj_kernel(x_ref, g_ref, w_ref, pos_ref, inv_ref, sgn_ref,
                        q_ref, kc_ref, vc_ref, ksa_ref, vsa_ref, kwa_ref, vwa_ref, gt_ref,
                        *, tm, seq):
    hn = _rmsnorm(x_ref[...], g_ref[...]).astype(w_ref.dtype)
    ang = pos_ref[...].astype(F32) * inv_ref[...]
    cos = jnp.cos(ang)
    sin = jnp.sin(ang) * sgn_ref[...]

    def rope(y):
        return y * cos + _swap_halves(y) * sin

    q_chunk = 4 * LANES
    for c in range(NSA_Q_COLS // q_chunk):
        y = jnp.dot(hn, w_ref[:, c * q_chunk:(c + 1) * q_chunk], preferred_element_type=F32)
        for l in range(q_chunk // LANES):
            yl = rope(y[:, l * LANES:(l + 1) * LANES]) * Q_SCALE
            q_ref[:, c * q_chunk + l * LANES:c * q_chunk + (l + 1) * LANES] = yl.astype(q_ref.dtype)
    y = jnp.dot(hn, w_ref[:, NSA_Q_COLS:], preferred_element_type=F32)
    part = lambda n: y[:, n * LANES:(n + 1) * LANES]

    tok = ((pl.program_id(0) % (seq // tm)) * tm
           + lax.broadcasted_iota(jnp.int32, (tm, HEAD_DIM), 0))
    lane = lax.broadcasted_iota(jnp.int32, (tm, HEAD_DIM), 1)
    blk_onehot = jnp.where((tok >> SLC_SHIFT) == lane, 1.0, 0.0)
    ones_col = jnp.where(lane == 0, 1.0, 0.0)
    zeros = jnp.zeros((tm, HEAD_DIM), F32)
    ks, vs, kw, vw = rope(part(2)), part(3), rope(part(4)), part(5)
    for g in range(NSA_KV_GROUPS):
        cols = slice(g * HEAD_DIM, (g + 1) * HEAD_DIM)
        kc_ref[g] = part(0)[:, cols]
        vc_ref[g] = part(1)[:, cols]
        ksa_ref[g] = jnp.concatenate([ks[:, cols], blk_onehot], axis=1).astype(ksa_ref.dtype)
        vsa_ref[g] = jnp.concatenate([vs[:, cols], ones_col], axis=1).astype(vsa_ref.dtype)
        kwa_ref[g] = jnp.concatenate([kw[:, cols], zeros], axis=1).astype(kwa_ref.dtype)
        vwa_ref[g] = jnp.concatenate([vw[:, cols], zeros, jnp.ones((tm, LANES), F32)],
                                     axis=1).astype(vwa_ref.dtype)
    gates = jax.nn.sigmoid(part(6))
    gt_ref[0] = gates
    gt_ref[1] = pltpu.roll(gates, LANES - NSA_GATES_PER_GROUP, 1)


def _nsa_in_proj(x2, g, w, pos2, inv2, sgn2, seq, tm=512):
    t, d = x2.shape
    G = NSA_KV_GROUPS
    tok_g = lambda dt: jax.ShapeDtypeStruct((G, t, HEAD_DIM), dt)
    aug_g = lambda dt: jax.ShapeDtypeStruct((G, t, LANES), dt)
    tok_spec = pl.BlockSpec((G, tm, HEAD_DIM), lambda i: (0, i, 0))
    aug_spec = pl.BlockSpec((G, tm, LANES), lambda i: (0, i, 0))
    return pl.pallas_call(
        functools.partial(_nsa_in_proj_kernel, tm=tm, seq=seq),
        grid=(t // tm,),
        in_specs=[pl.BlockSpec((tm, d), lambda i: (i, 0)),
                  _const_spec((1, d)),
                  _const_spec((d, NSA_W_COLS)),
                  pl.BlockSpec((tm, 1), lambda i: (i, 0)),
                  _const_spec((1, LANES)),
                  _const_spec((1, LANES))],
        out_specs=[pl.BlockSpec((tm, NSA_Q_COLS), lambda i: (i, 0)),
                   tok_spec, tok_spec, aug_spec, aug_spec, aug_spec,
                   pl.BlockSpec((G, tm, 2 * LANES), lambda i: (0, i, 0)), aug_spec],
        out_shape=[jax.ShapeDtypeStruct((t, NSA_Q_COLS), BF16),
                   tok_g(F32), tok_g(F32), aug_g(BF16), aug_g(BF16), aug_g(BF16),
                   jax.ShapeDtypeStruct((G, t, 2 * LANES), BF16), aug_g(F32)],
        compiler_params=_params(("arbitrary",)),
        name="nsa_in_proj",
    )(x2, g, w, pos2, inv2, sgn2)


def _nsa_compress_kernel(k16_ref, v16_ref, pek_ref, pev_ref, w1k_ref, w2k_ref, w2kr_ref,
                         w1v_ref, w2v_ref, pos_ref, inv_ref, kct_ref, vc_ref):
    half_w = CMP_STRIDE * HEAD_DIM
    nrow = k16_ref.shape[2]

    def hidden(x16_ref, pe_ref, w1_ref):
        x = x16_ref[0, 0]
        y1 = jnp.dot(x, w1_ref[:half_w, :], precision=HIGHEST, preferred_element_type=F32)
        y2 = jnp.dot(x, w1_ref[half_w:, :], precision=HIGHEST, preferred_element_type=F32)
        bias = jnp.dot(pe_ref[...], w1_ref[...], precision=HIGHEST, preferred_element_type=F32)
        return jax.nn.gelu(y1 + pltpu.roll(y2, nrow - 1, 0) + bias[0:1])

    hk = hidden(k16_ref, pek_ref, w1k_ref)
    kc = jnp.dot(hk, w2k_ref[...], precision=HIGHEST, preferred_element_type=F32)
    kc_rot = jnp.dot(hk, w2kr_ref[...], precision=HIGHEST, preferred_element_type=F32)
    ang = pos_ref[0].astype(F32) * inv_ref[...]
    kc = kc * jnp.cos(ang) + kc_rot * jnp.sin(ang)
    hi, lo = _split_bf16(kc)
    kct_ref[0, 0] = jnp.concatenate([hi.astype(F32), lo.astype(F32)], axis=1).T.astype(kct_ref.dtype)
    hv = hidden(v16_ref, pev_ref, w1v_ref)
    vc = jnp.dot(hv, w2v_ref[...], precision=HIGHEST, preferred_element_type=F32)
    vc_ref[0, 0] = jnp.concatenate([vc, jnp.zeros_like(vc)], axis=1).astype(vc_ref.dtype)


def _nsa_compress(k16, v16, pek, pev, w1k, w2k, w2kr, w1v, w2v, pos_cmp, inv64):
    G, b, nrow, wide = k16.shape
    x_spec = pl.BlockSpec((1, 1, nrow, wide), lambda bi, g: (g, bi, 0, 0))
    return pl.pallas_call(
        _nsa_compress_kernel,
        grid=(b, G),
        in_specs=[x_spec, x_spec,
                  _const_spec(pek.shape), _const_spec(pev.shape),
                  _const_spec(w1k.shape), _const_spec(w2k.shape), _const_spec(w2kr.shape),
                  _const_spec(w1v.shape), _const_spec(w2v.shape),
                  pl.BlockSpec((1, nrow, 1), lambda bi, g: (bi, 0, 0)),
                  _const_spec((1, HEAD_DIM))],
        out_specs=[pl.BlockSpec((1, 1, LANES, nrow), lambda bi, g: (bi, g, 0, 0)),
                   pl.BlockSpec((1, 1, nrow, LANES), lambda bi, g: (bi, g, 0, 0))],
        out_shape=[jax.ShapeDtypeStruct((b, G, LANES, nrow), BF16),
                   jax.ShapeDtypeStruct((b, G, nrow, LANES), BF16)],
        compiler_params=_params(("arbitrary", "arbitrary")),
        name="nsa_compress",
    )(k16, v16, pek, pev, w1k, w2k, w2kr, w1v, w2v, pos_cmp, inv64)


def _nsa_attn_kernel(q_ref, kct_ref, vc_ref, ov_ref, ks_ref, vs_ref, kw_ref, vw_ref, gt_ref,
                     gsel_ref, o_ref, qa_ref, oc_ref, psum_ref, m_ref, acc_ref, alpha_ref, p_ref, pw_ref,
                     ow_ref, gs_ref,
                     *, tq, kb, n_slc, wg, rc):
    i = pl.program_id(2)
    q0 = pl.multiple_of(i * tq, tq)
    n_rows = NSA_HPG * tq
    heads = range(NSA_HPG)
    hrows = lambda h: slice(h * tq, (h + 1) * tq)
    qrow = q0 + lax.broadcasted_iota(jnp.int32, (tq, 1), 0)
    q_heads = [q_ref[0, :, h * HEAD_DIM:(h + 1) * HEAD_DIM] for h in heads]
    for h in heads:
        qa_ref[hrows(h), :] = jnp.concatenate([q_heads[h], q_heads[h]], axis=1)

    n_cmp_rows = kct_ref.shape[3]
    s_all = jnp.dot(qa_ref[...], kct_ref[0, 0], preferred_element_type=F32)
    w_keys = WINDOW + tq
    w_start = pl.multiple_of(jnp.maximum(q0 - WINDOW, 0), tq)
    k_w = kw_ref[0, 0, pl.ds(w_start, w_keys), :]
    s_win = [lax.dot_general(qa_ref[h0 * tq:(h0 + wg) * tq, :], k_w, NT_DIMS,
                             preferred_element_type=F32) for h0 in range(0, NSA_HPG, wg)]

    cmp_end = CMP_STRIDE * lax.broadcasted_iota(jnp.int32, (1, n_cmp_rows), 1) + (CMP_LEN - 1)
    c_bias = jnp.where(cmp_end <= qrow, 0.0, NEG)
    row_valid = jnp.where(qrow >= CMP_LEN - 1, 1.0, 0.0)
    for r0 in range(0, n_rows, rc):
        rows = slice(r0, r0 + rc)
        local = slice(r0 % tq, r0 % tq + rc)
        s = s_all[rows] + c_bias[local]
        e = jnp.exp2(s - jnp.max(s, axis=1, keepdims=True))
        p = e * (row_valid[local] / jnp.sum(e, axis=1, keepdims=True))
        if r0 < tq:
            psum_ref[local, :] = p
        else:
            psum_ref[local, :] += p
        p_ref[0, rows, 0:n_cmp_rows] = p.astype(p_ref.dtype)
    g_wide = jnp.dot(jnp.concatenate(_split_bf16(gt_ref[0]), axis=1), gsel_ref[...],
                     preferred_element_type=F32)
    gate = lambda h, branch: g_wide[:, (3 * h + branch) * LANES:(3 * h + branch + 1) * LANES]
    o_cmp = jnp.dot(p_ref[0, :, 0:n_cmp_rows], vc_ref[0, 0], preferred_element_type=F32)
    for h in heads:
        oc_ref[hrows(h), :] = gate(h, 0) * o_cmp[hrows(h)]
        gs_ref[hrows(h), :] = gate(h, 1)
    imp = jnp.dot(jnp.concatenate(_split_bf16(psum_ref[...]), axis=1), ov_ref[...],
                  preferred_element_type=F32)
    imp_t = imp.T[:n_slc]

    w_diff = qrow - (w_start + lax.broadcasted_iota(jnp.int32, (1, w_keys), 1))
    w_bias = jnp.where((w_diff >= 0) & (w_diff < WINDOW), 0.0, NEG)
    for r0 in range(0, n_rows, rc):
        s = s_win[r0 // (wg * tq)][r0 % (wg * tq):r0 % (wg * tq) + rc] + w_bias[r0 % tq:r0 % tq + rc]
        pw_ref[r0:r0 + rc, :] = jnp.exp2(s - jnp.max(s, axis=1, keepdims=True)).astype(pw_ref.dtype)
    o_win = jnp.dot(pw_ref[...], vw_ref[0, 0, pl.ds(w_start, w_keys), :],
                    preferred_element_type=F32)
    for h in heads:
        o_h = o_win[hrows(h)]
        ow_ref[hrows(h), :] = (gate(h, 2) * o_h[:, :LANES]) * (1.0 / o_h[:, LANES:])

    qpos = q0 + lax.broadcasted_iota(jnp.int32, (1, tq), 1)
    blk = lax.broadcasted_iota(jnp.int32, (n_slc, 1), 0)
    cur = qpos >> SLC_SHIFT
    forced = (blk == 0) | (blk == cur) | (blk == cur - 1)
    causal_blk = blk * SLC_LEN <= qpos
    score = jnp.where(causal_blk, imp_t + FORCE_BONUS * forced.astype(F32), NEG)
    sub = lax.broadcasted_iota(jnp.int32, (8, 1), 0)
    groups = [score[8 * v:8 * v + 8] for v in range(n_slc // 8)]
    counts = [jnp.zeros((8, tq), F32) for _ in groups]
    for m in range(n_slc):
        row = score[m:m + 1]
        for v, sv in enumerate(groups):
            if v < m // 8:
                ahead = row > sv
            elif v > m // 8:
                ahead = row >= sv
            else:
                ahead = (row > sv) | ((row == sv) & (sub > m % 8))
            counts[v] = counts[v] + jnp.where(ahead, 1.0, 0.0)
    rank = jnp.concatenate(counts, axis=0)
    sel_bias_t = jnp.where(rank < SLC_TOP_N, 0.0, NEG)
    sel_bias = jnp.concatenate([sel_bias_t, jnp.zeros((LANES - n_slc, tq), F32)], axis=0).T
    sel_bias = sel_bias[:, :HEAD_DIM].astype(qa_ref.dtype)
    for h in heads:
        qa_ref[hrows(h), :] = jnp.concatenate([q_heads[h], sel_bias], axis=1)

    tok_bias = jnp.where(lax.broadcasted_iota(jnp.int32, (1, kb), 1)
                         <= lax.broadcasted_iota(jnp.int32, (tq, 1), 0), 0.0, NEG)

    def scores(start):
        k_t = ks_ref[0, 0, pl.ds(start, kb), :]
        return lax.dot_general(qa_ref[...], k_t, NT_DIMS, preferred_element_type=F32)

    def softmax(s_all, slot, own_keys):
        for r0 in range(0, n_rows, rc):
            rows = slice(r0, r0 + rc)
            s = s_all[rows]
            if own_keys:
                s = s + tok_bias[r0 % tq:r0 % tq + rc]
                m_new = jnp.broadcast_to(jnp.max(s, axis=1, keepdims=True), (rc, LANES))
            else:
                m_old = m_ref[rows, :]
                m_new = jnp.maximum(m_old, jnp.max(s, axis=1, keepdims=True))
                alpha_ref[slot, rows, :] = jnp.exp2(m_old - m_new)
            m_ref[rows, :] = m_new
            m_wide = jnp.concatenate([m_new] * (kb // LANES), axis=1)
            p_ref[slot, rows, :] = jnp.exp2(s - m_wide).astype(p_ref.dtype)

    def values(start, slot):
        v_t = vs_ref[0, 0, pl.ds(start, kb), :]
        pv = jnp.dot(p_ref[slot], v_t, preferred_element_type=F32)
        for r0 in range(0, n_rows, rc):
            rows = slice(r0, r0 + rc)
            acc_ref[rows, :] = alpha_ref[slot, rows, :] * acc_ref[rows, :] + pv[rows]

    chunk_start = lambda c: pl.multiple_of(c * kb, kb)
    owed_start = lambda c: pl.multiple_of(jnp.where(c == 0, q0, (c - 1) * kb), kb)

    acc_ref[...] = jnp.zeros_like(acc_ref)
    alpha_ref[1] = jnp.zeros(alpha_ref.shape[1:], F32)
    softmax(scores(q0), 1, True)

    def pair_body(t, carry):
        c = 2 * t
        s_a = scores(chunk_start(c))
        values(owed_start(c), 1)
        softmax(s_a, 0, False)
        s_b = scores(chunk_start(c + 1))
        values(chunk_start(c), 0)
        softmax(s_b, 1, False)
        return carry

    lax.fori_loop(0, i // 2, pair_body, 0)

    @pl.when(i % 2 == 1)
    def _():
        s_a = scores(chunk_start(i - 1))
        values(owed_start(i - 1), 1)
        softmax(s_a, 0, False)
        values(chunk_start(i - 1), 0)

    @pl.when(i % 2 == 0)
    def _():
        values(owed_start(i), 1)

    outs = []
    for h in heads:
        o_h = acc_ref[hrows(h), :]
        o_sel = (gs_ref[hrows(h), :] * o_h) * (1.0 / o_h[:, HEAD_DIM:HEAD_DIM + 1])
        outs.append(((oc_ref[hrows(h), :] + o_sel) + ow_ref[hrows(h), :])[:, :HEAD_DIM])
    for hp in range(NSA_HPG // 2):
        pair = jnp.concatenate(outs[2 * hp:2 * hp + 2], axis=1)
        o_ref[0, :, hp * LANES:(hp + 1) * LANES] = pair.astype(o_ref.dtype)


def _nsa_attention(q3, kct, vc, ov, ksa, vsa, kwa, vwa, gt, tq=256, wg=4, rc=64):
    b, s, _ = q3.shape
    G = NSA_KV_GROUPS
    kb = tq
    n_slc = s // SLC_LEN
    nq = s // tq
    n_cmp_rows = kct.shape[3]
    gw = NSA_HPG * HEAD_DIM
    assert n_slc <= HEAD_DIM and tq % SLC_LEN == 0 and WINDOW % tq == 0 and s >= WINDOW + tq
    assert n_cmp_rows <= kb and n_slc % 8 == 0 and tq % rc == 0
    kv_spec = pl.BlockSpec((1, 1, s, LANES), lambda bi, g, i: (g, bi, 0, 0))
    n_gates = NSA_GATES_PER_GROUP
    gsel = np.zeros((2 * LANES, n_gates * LANES), np.float32)
    for c in range(n_gates):
        gsel[[c, LANES + c], c * LANES:(c + 1) * LANES] = 1.0
    gsel = jnp.asarray(gsel, BF16)
    return pl.pallas_call(
        functools.partial(_nsa_attn_kernel, tq=tq, kb=kb, n_slc=n_slc, wg=wg, rc=rc),
        grid=(b, G, nq),
        in_specs=[pl.BlockSpec((1, tq, gw), lambda bi, g, i: (bi, i, g)),
                  pl.BlockSpec((1, 1, LANES, n_cmp_rows), lambda bi, g, i: (bi, g, 0, 0)),
                  pl.BlockSpec((1, 1, n_cmp_rows, LANES), lambda bi, g, i: (bi, g, 0, 0)),
                  _const_spec(ov.shape),
                  kv_spec, kv_spec, kv_spec,
                  pl.BlockSpec((1, 1, s, 2 * LANES), lambda bi, g, i: (g, bi, 0, 0)),
                  pl.BlockSpec((1, tq, LANES), lambda bi, g, i: (g, bi * nq + i, 0)),
                  _const_spec(gsel.shape)],
        out_specs=pl.BlockSpec((1, tq, gw), lambda bi, g, i: (bi, i, g)),
        out_shape=jax.ShapeDtypeStruct((b, s, D_MODEL), BF16),
        scratch_shapes=[pltpu.VMEM((NSA_HPG * tq, LANES), BF16),
                        pltpu.VMEM((NSA_HPG * tq, LANES), F32),
                        pltpu.VMEM((tq, n_cmp_rows), F32),
                        pltpu.VMEM((NSA_HPG * tq, LANES), F32),
                        pltpu.VMEM((NSA_HPG * tq, LANES), F32),
                        pltpu.VMEM((2, NSA_HPG * tq, LANES), F32),
                        pltpu.VMEM((2, NSA_HPG * tq, kb), BF16),
                        pltpu.VMEM((NSA_HPG * tq, WINDOW + tq), BF16),
                        pltpu.VMEM((NSA_HPG * tq, LANES), F32),
                        pltpu.VMEM((NSA_HPG * tq, LANES), F32)],
        compiler_params=_params(("arbitrary", "arbitrary", "arbitrary")),
        name="nsa_attention",
    )(q3, kct, vc, ov, ksa, vsa, kwa, vwa, gt, gsel)


def _nsa_layer_attention(hx2, norm_g, positions, w_in, pe_k, pe_v, w1k, w2k, w1v, w2v, b, s):
    t = b * s
    G = NSA_KV_GROUPS
    w_pad = jnp.pad(w_in, ((0, 0), (0, NSA_W_COLS - w_in.shape[1]))).astype(BF16)
    inv = ROPE_THETA ** (-jnp.arange(HALF, dtype=F32) / HALF)
    inv2 = jnp.tile(inv, LANES // HALF)[None, :]
    sgn2 = jnp.tile(jnp.concatenate([-jnp.ones(HALF, F32), jnp.ones(HALF, F32)]), LANES // HEAD_DIM)[None, :]
    q, kc_tok, vc_tok, ksa, vsa, kwa, vwa, gt = _nsa_in_proj(
        hx2, norm_g, w_pad, positions.reshape(t, 1), inv2, sgn2, s)

    nrow = s // CMP_STRIDE
    wide = CMP_STRIDE * HEAD_DIM
    k16 = kc_tok.reshape(G, b, nrow, wide)
    v16 = vc_tok.reshape(G, b, nrow, wide)
    pek = jnp.broadcast_to(pe_k.reshape(1, CMP_LEN * HEAD_DIM), (8, CMP_LEN * HEAD_DIM))
    pev = jnp.broadcast_to(pe_v.reshape(1, CMP_LEN * HEAD_DIM), (8, CMP_LEN * HEAD_DIM))
    w2k_rot = jnp.concatenate([-w2k[:, HALF:], w2k[:, :HALF]], axis=1)
    end_idx = jnp.minimum(jnp.arange(nrow) * CMP_STRIDE + CMP_LEN - 1, s - 1)
    pos_cmp = positions[:, end_idx][:, :, None]
    inv64 = jnp.tile(inv, 2)[None, :]
    kct, vc = _nsa_compress(k16, v16, pek, pev, w1k, w2k, w2k_rot, w1v, w2v, pos_cmp, inv64)

    n_slc = s // SLC_LEN
    c0 = np.arange(nrow)[:, None] * CMP_STRIDE
    s0 = np.arange(n_slc)[None, :] * SLC_LEN
    ov = np.clip(np.minimum(c0 + CMP_LEN, s0 + SLC_LEN) - np.maximum(c0, s0), 0, None) / CMP_LEN
    ov[(s - CMP_LEN) // CMP_STRIDE + 1:, :] = 0.0
    ov = np.pad(ov, ((0, 0), (0, LANES - n_slc)))
    ov = jnp.asarray(np.concatenate([ov, ov], axis=0), BF16)

    aug4 = lambda a: a.reshape(G, b, s, a.shape[-1])
    return _nsa_attention(q.reshape(b, s, D_MODEL), kct, vc, ov, aug4(ksa), aug4(vsa),
                          aug4(kwa), aug4(vwa), gt)


def kernel(x, positions, norm_mix, sba_w_in, sba_w_out, nsa_w_in, nsa_cmp_pos_k, nsa_cmp_pos_v,
           nsa_cmp_k_w1, nsa_cmp_k_w2, nsa_cmp_v_w1, nsa_cmp_v_w2, nsa_w_out, norm_ffn,
           ffn_w_up, ffn_conv_w, ffn_conv_b, ffn_w_down, norm_final):
    b, s, d = x.shape
    t = b * s
    depth = norm_mix.shape[0]
    x2 = x.reshape(t, d)
    g_final = norm_final.reshape(1, d)
    for layer in range(depth):
        j = layer // 2
        g_mix = norm_mix[layer].reshape(1, d)
        if layer % 2 == 0:
            qkv = _sba_in_proj(x2, g_mix, sba_w_in[j].astype(BF16))
            o = _sba_attention(qkv.reshape(b, s, 3 * d), b, s)
            w_out = sba_w_out[j]
        else:
            o = _nsa_layer_attention(x2, g_mix, positions, nsa_w_in[j], nsa_cmp_pos_k[j],
                                     nsa_cmp_pos_v[j], nsa_cmp_k_w1[j], nsa_cmp_k_w2[j],
                                     nsa_cmp_v_w1[j], nsa_cmp_v_w2[j], b, s)
            w_out = nsa_w_out[j]
        x2 = _out_proj_residual(x2, o.reshape(t, d), w_out.astype(BF16))
        x2 = _conv_ffn(x2, norm_ffn[layer].reshape(1, d), ffn_w_up[layer].astype(BF16),
                       ffn_conv_w[layer], ffn_conv_b[layer].reshape(1, -1),
                       ffn_w_down[layer].astype(BF16), g_final, s,
                       final_norm=(layer == depth - 1))
    return x2.reshape(b, s, d)
```

```python
import functools

import numpy as np
import jax
import jax.numpy as jnp
from jax import lax
from jax.experimental import pallas as pl
from jax.experimental.pallas import tpu as pltpu

D_MODEL = 1024
N_HEADS = 16
HEAD_DIM = 64
HALF = HEAD_DIM // 2
NSA_KV_GROUPS = 2
NSA_HPG = N_HEADS // NSA_KV_GROUPS
CMP_LEN = 32
CMP_STRIDE = 16
CMP_HIDDEN = 2 * HEAD_DIM
SLC_LEN = 64
SLC_SHIFT = SLC_LEN.bit_length() - 1
SLC_TOP_N = 16
WINDOW = 512
ROPE_THETA = 10000.0
D_FF = 2816
RMS_EPS = 1e-6
NEG = -1e30
FORCE_BONUS = 1e4
LOG2E = float(np.log2(np.e))
Q_SCALE = LOG2E * HEAD_DIM ** -0.5

LANES = 128
VMEM_LIMIT = 56 * 1024 * 1024

F32 = jnp.float32
BF16 = jnp.bfloat16
HIGHEST = lax.Precision.HIGHEST
NT_DIMS = (((1,), (1,)), ((), ()))


def _params(semantics):
    return pltpu.CompilerParams(dimension_semantics=semantics, vmem_limit_bytes=VMEM_LIMIT)


def _rmsnorm(x, g):
    return x * lax.rsqrt(jnp.mean(x * x, axis=-1, keepdims=True) + RMS_EPS) * g


def _const_spec(shape):
    return pl.BlockSpec(shape, lambda *_: (0,) * len(shape), pipeline_mode=pl.Buffered(1))


def _split_bf16(x):
    hi = x.astype(BF16)
    return hi, (x - hi.astype(F32)).astype(BF16)


def _sba_in_proj_kernel(x_ref, g_ref, w_ref, o_ref, *, n_chunk):
    hn = _rmsnorm(x_ref[...], g_ref[...]).astype(w_ref.dtype)
    n = w_ref.shape[1]
    for c in range(n // n_chunk):
        cols = slice(c * n_chunk, (c + 1) * n_chunk)
        y = jnp.dot(hn, w_ref[:, cols], preferred_element_type=F32)
        if c * n_chunk < D_MODEL:
            y = y * Q_SCALE
        o_ref[:, cols] = y.astype(o_ref.dtype)


def _sba_in_proj(x2, g, w, tm=512):
    t, d = x2.shape
    n = w.shape[1]
    return pl.pallas_call(
        functools.partial(_sba_in_proj_kernel, n_chunk=512),
        grid=(t // tm,),
        in_specs=[pl.BlockSpec((tm, d), lambda i: (i, 0)),
                  _const_spec((1, d)),
                  _const_spec((d, n))],
        out_specs=pl.BlockSpec((tm, n), lambda i: (i, 0)),
        out_shape=jax.ShapeDtypeStruct((t, n), BF16),
        compiler_params=_params(("arbitrary",)),
        name="sba_in_proj",
    )(x2, g, w)


def _sba_attn_kernel(q_ref, k_ref, v_ref, uu_ref, o_ref, acc_ref, car_ref, hl_ref, lb_ref, a_ref,
                     z_ref,
                     *, tq, kb, nsub, rc):
    i = pl.program_id(2)
    kt = kb * nsub
    q = q_ref[0]
    lane = lax.broadcasted_iota(jnp.int32, (tq, LANES), 1)
    klane = lax.broadcasted_iota(jnp.int32, (kb, LANES), 1)
    acc_ref[...] = jnp.zeros_like(acc_ref)
    car_ref[...] = jnp.zeros_like(car_ref)
    ud = uu_ref[...]
    rel = (lax.broadcasted_iota(jnp.int32, (rc, 2 * kb), 0)
           - (lax.broadcasted_iota(jnp.int32, (rc, 2 * kb), 1) & (kb - 1)))

    def super_tile(st, diag, slot, prev_slot, z_first):

        def visibility(j, r0):
            if not diag:
                return 2, None
            if j * kb >= r0 + rc - 1:
                return 0, None
            if j * kb + kb - 1 < r0:
                return 2, None
            return 1, rel > (j * kb - r0)

        def scores(j, st=st):
            k_t = k_ref[0, pl.ds(pl.multiple_of(st * kt + j * kb, kb), kb), :]
            k_bd = jnp.concatenate([jnp.where(klane < HEAD_DIM, k_t, jnp.zeros_like(k_t)),
                                    jnp.where(klane >= HEAD_DIM, k_t, jnp.zeros_like(k_t))], axis=0)
            return lax.dot_general(q, k_bd, NT_DIMS, preferred_element_type=F32)

        def log_terms(j, z):
            totals = []
            for r0 in range(0, tq, rc):
                rows = slice(r0, r0 + rc)
                kind, strict = visibility(j, r0)
                if kind == 0:
                    hl_ref[j, rows, :] = jnp.zeros((rc, 2 * kb), BF16)
                    totals.append(None)
                    continue
                zc = z[rows]
                nl = jnp.maximum(zc, 0.0) + jnp.log2(1.0 + jnp.exp2(-jnp.abs(zc)))
                lb_ref[j, rows, :] = zc - nl
                if kind == 1:
                    nl = jnp.where(strict, nl, 0.0)
                hl_ref[j, rows, :] = nl.astype(BF16)
                totals.append([jnp.sum(nl[:, h * kb:(h + 1) * kb], axis=1, keepdims=True)
                               for h in range(2)])
            return jnp.dot(hl_ref[j], ud, preferred_element_type=F32), totals

        def weights(j, sums):
            cum, totals = sums
            for n, r0 in enumerate(range(0, tq, rc)):
                rows = slice(r0, r0 + rc)
                cols = slice(j * kb, (j + 1) * kb)
                kind, strict = visibility(j, r0)
                if kind == 0:
                    for h in range(2):
                        a_ref[slot, h, rows, cols] = jnp.zeros((rc, kb), BF16)
                    continue
                car = [car_ref[h, rows, :] for h in range(2)]
                a = jnp.exp2(lb_ref[j, rows, :] - cum[rows] - jnp.concatenate(car, axis=1))
                if kind == 1:
                    a = jnp.where(strict, a, 0.0)
                for h in range(2):
                    a_ref[slot, h, rows, cols] = a[:, h * kb:(h + 1) * kb].astype(BF16)
                    car_ref[h, rows, :] = car[h] + totals[n][h]

        z_next = scores(nsub - 1) if z_first is None else z_first
        pending = None
        for j in reversed(range(nsub)):
            z = z_next
            if j > 0:
                z_next = scores(j - 1)
            if j == nsub - 1 and prev_slot is not None:
                apply_weights(prev_slot, st + 1)
            r = log_terms(j, z)
            if pending is not None:
                weights(j + 1, pending)
            pending = r
            if j == 0:
                z_ref[...] = scores(nsub - 1, jnp.maximum(st - 1, 0))
        weights(0, pending)

    def apply_weights(slot, st):
        v_all = v_ref[0, pl.ds(pl.multiple_of(st * kt, kt), kt), :]
        for h in range(2):
            acc_ref[h] += jnp.dot(a_ref[slot, h], v_all, preferred_element_type=F32)

    super_tile(i, True, 0, None, None)

    def body(t, carry):
        super_tile(i - 1 - t, False, (t + 1) % 2, t % 2, z_ref[...])
        return carry

    lax.fori_loop(0, i, body, 0)
    apply_weights(i % 2, 0)
    o_ref[0] = jnp.where(lane < HEAD_DIM, acc_ref[0], acc_ref[1]).astype(o_ref.dtype)


def _sba_attention(qkv, b, s, tq=512, kb=LANES, nsub=4, rc=64):
    assert tq == kb * nsub and s % tq == 0 and tq % rc == 0
    n_pairs = D_MODEL // LANES
    assert kb & (kb - 1) == 0
    jj = np.arange(kb)
    uu = jnp.asarray(np.kron(np.eye(2), (jj[:, None] > jj[None, :]).astype(np.float32)), BF16)
    return pl.pallas_call(
        functools.partial(_sba_attn_kernel, tq=tq, kb=kb, nsub=nsub, rc=rc),
        grid=(b, n_pairs, s // tq),
        in_specs=[pl.BlockSpec((1, tq, LANES), lambda bi, p, i: (bi, i, p)),
                  pl.BlockSpec((1, s, LANES), lambda bi, p, i: (bi, 0, n_pairs + p)),
                  pl.BlockSpec((1, s, LANES), lambda bi, p, i: (bi, 0, 2 * n_pairs + p)),
                  _const_spec((2 * kb, 2 * kb))],
        out_specs=pl.BlockSpec((1, tq, LANES), lambda bi, p, i: (bi, i, p)),
        out_shape=jax.ShapeDtypeStruct((b, s, D_MODEL), BF16),
        scratch_shapes=[pltpu.VMEM((2, tq, LANES), F32),
                        pltpu.VMEM((2, tq, LANES), F32),
                        pltpu.VMEM((nsub, tq, 2 * kb), BF16),
                        pltpu.VMEM((nsub, tq, 2 * kb), F32),
                        pltpu.VMEM((2, 2, tq, nsub * kb), BF16),
                        pltpu.VMEM((tq, 2 * kb), F32)],
        compiler_params=_params(("arbitrary", "arbitrary", "arbitrary")),
        name="sba_attention",
    )(qkv, qkv, qkv, uu)


def _ffn_kernel(x_ref, o_ref, wo_ref, g_ref, wup_ref, cw_ref, cb_ref, wdn_ref, gf_ref, y_ref,
                carry_ref, sg_ref, sv_ref, act_ref, *, tm, fc, tiles_per_seq, final_norm):
    @pl.when(pl.program_id(0) % tiles_per_seq == 0)
    def _():
        carry_ref[...] = jnp.zeros_like(carry_ref)

    x = x_ref[...] + jnp.dot(o_ref[...], wo_ref[...], preferred_element_type=F32)
    hn = _rmsnorm(x, g_ref[...]).astype(wup_ref.dtype)

    def up(col0):
        return jnp.dot(hn, wup_ref[:, col0:col0 + fc], preferred_element_type=F32)

    def conv(u, col0, s_ref):
        cols = slice(col0, col0 + fc)
        s_ref[0:8, :] = carry_ref[:, cols]
        s_ref[8:tm + 8, :] = u
        carry_ref[:, cols] = u[tm - 8:tm, :]
        cw = cw_ref[:, cols]
        c = cb_ref[:, cols] + s_ref[6:tm + 6, :] * cw[0:1]
        c = c + s_ref[7:tm + 7, :] * cw[1:2]
        return c + u * cw[2:3]

    n_chunks = D_FF // fc
    u_next = (up(0), up(D_FF))
    for c in range(n_chunks):
        u_gate, u_val = u_next
        if c + 1 < n_chunks:
            u_next = (up((c + 1) * fc), up(D_FF + (c + 1) * fc))
        gate = conv(u_gate, c * fc, sg_ref)
        val = conv(u_val, D_FF + c * fc, sv_ref)
        act_ref[:, c * fc:(c + 1) * fc] = (gate * jax.nn.sigmoid(gate) * val).astype(act_ref.dtype)
    y = x + jnp.dot(act_ref[...], wdn_ref[...], preferred_element_type=F32)
    if final_norm:
        y = _rmsnorm(y, gf_ref[...])
    y_ref[...] = y


def _conv_ffn(x2, o2, w_out, g, w_up, conv_w, conv_b, w_down, g_final, s, final_norm,
              tm=256, fc=256):
    t, d = x2.shape
    f2 = w_up.shape[1]
    return pl.pallas_call(
        functools.partial(_ffn_kernel, tm=tm, fc=fc, tiles_per_seq=s // tm, final_norm=final_norm),
        grid=(t // tm,),
        in_specs=[pl.BlockSpec((tm, d), lambda i: (i, 0)),
                  pl.BlockSpec((tm, d), lambda i: (i, 0)),
                  _const_spec((d, d)),
                  _const_spec((1, d)),
                  _const_spec((d, f2)),
                  _const_spec((3, f2)),
                  _const_spec((1, f2)),
                  _const_spec((D_FF, d)),
                  _const_spec((1, d))],
        out_specs=pl.BlockSpec((tm, d), lambda i: (i, 0)),
        out_shape=jax.ShapeDtypeStruct((t, d), F32),
        scratch_shapes=[pltpu.VMEM((8, f2), F32),
                        pltpu.VMEM((tm + 8, fc), F32),
                        pltpu.VMEM((tm + 8, fc), F32),
                        pltpu.VMEM((tm, D_FF), BF16)],
        compiler_params=_params(("arbitrary",)),
        name="conv_ffn",
    )(x2, o2, w_out, g, w_up, conv_w, conv_b, w_down, g_final)


NSA_Q_COLS = D_MODEL
NSA_W_COLS = D_MODEL + 7 * LANES
NSA_GATES_PER_GROUP = 3 * NSA_HPG


def _swap_halves(y):
    lane = lax.broadcasted_iota(jnp.int32, y.shape, 1)
    first = (lane % HEAD_DIM) < HALF
    return jnp.where(first, pltpu.roll(y, LANES - HALF, 1), pltpu.roll(y, HALF, 1))


def _nsa_in_proj_kernel(x_ref, g_ref, w_ref, pos_ref, inv_ref, sgn_ref,
                        q_ref, kc_ref, vc_ref, ksa_ref, vsa_ref, kwa_ref, vwa_ref, gt_ref,
                        *, tm, seq):
    hn = _rmsnorm(x_ref[...], g_ref[...]).astype(w_ref.dtype)
    ang = pos_ref[...].astype(F32) * inv_ref[...]
    cos = jnp.cos(ang)
    sin = jnp.sin(ang) * sgn_ref[...]

    def rope(y):
        return y * cos + _swap_halves(y) * sin

    q_chunk = 4 * LANES
    for c in range(NSA_Q_COLS // q_chunk):
        y = jnp.dot(hn, w_ref[:, c * q_chunk:(c + 1) * q_chunk], preferred_element_type=F32)
        for l in range(q_chunk // LANES):
            yl = rope(y[:, l * LANES:(l + 1) * LANES]) * Q_SCALE
            q_ref[:, c * q_chunk + l * LANES:c * q_chunk + (l + 1) * LANES] = yl.astype(q_ref.dtype)
    y = jnp.dot(hn, w_ref[:, NSA_Q_COLS:], preferred_element_type=F32)
    part = lambda n: y[:, n * LANES:(n + 1) * LANES]

    tok = ((pl.program_id(0) % (seq // tm)) * tm
           + lax.broadcasted_iota(jnp.int32, (tm, HEAD_DIM), 0))
    lane = lax.broadcasted_iota(jnp.int32, (tm, HEAD_DIM), 1)
    blk_onehot = jnp.where((tok >> SLC_SHIFT) == lane, 1.0, 0.0)
    ones_col = jnp.where(lane == 0, 1.0, 0.0)
    zeros = jnp.zeros((tm, HEAD_DIM), F32)
    ks, vs, kw, vw = rope(part(2)), part(3), rope(part(4)), part(5)
    for g in range(NSA_KV_GROUPS):
        cols = slice(g * HEAD_DIM, (g + 1) * HEAD_DIM)
        kc_ref[g] = part(0)[:, cols]
        vc_ref[g] = part(1)[:, cols]
        ksa_ref[g] = jnp.concatenate([ks[:, cols], blk_onehot], axis=1).astype(ksa_ref.dtype)
        vsa_ref[g] = jnp.concatenate([vs[:, cols], ones_col], axis=1).astype(vsa_ref.dtype)
        kwa_ref[g] = jnp.concatenate([kw[:, cols], zeros], axis=1).astype(kwa_ref.dtype)
        vwa_ref[g] = jnp.concatenate([vw[:, cols], zeros, jnp.ones((tm, LANES), F32)],
                                     axis=1).astype(vwa_ref.dtype)
    gates = jax.nn.sigmoid(part(6))
    gt_ref[0] = gates
    gt_ref[1] = pltpu.roll(gates, LANES - NSA_GATES_PER_GROUP, 1)


def _nsa_in_proj(x2, g, w, pos2, inv2, sgn2, seq, tm=512):
    t, d = x2.shape
    G = NSA_KV_GROUPS
    tok_g = lambda dt: jax.ShapeDtypeStruct((G, t, HEAD_DIM), dt)
    aug_g = lambda dt: jax.ShapeDtypeStruct((G, t, LANES), dt)
    tok_spec = pl.BlockSpec((G, tm, HEAD_DIM), lambda i: (0, i, 0))
    aug_spec = pl.BlockSpec((G, tm, LANES), lambda i: (0, i, 0))
    return pl.pallas_call(
        functools.partial(_nsa_in_proj_kernel, tm=tm, seq=seq),
        grid=(t // tm,),
        in_specs=[pl.BlockSpec((tm, d), lambda i: (i, 0)),
                  _const_spec((1, d)),
                  _const_spec((d, NSA_W_COLS)),
                  pl.BlockSpec((tm, 1), lambda i: (i, 0)),
                  _const_spec((1, LANES)),
                  _const_spec((1, LANES))],
        out_specs=[pl.BlockSpec((tm, NSA_Q_COLS), lambda i: (i, 0)),
                   tok_spec, tok_spec, aug_spec, aug_spec, aug_spec,
                   pl.BlockSpec((G, tm, 2 * LANES), lambda i: (0, i, 0)), aug_spec],
        out_shape=[jax.ShapeDtypeStruct((t, NSA_Q_COLS), BF16),
                   tok_g(F32), tok_g(F32), aug_g(BF16), aug_g(BF16), aug_g(BF16),
                   jax.ShapeDtypeStruct((G, t, 2 * LANES), BF16), aug_g(F32)],
        compiler_params=_params(("arbitrary",)),
        name="nsa_in_proj",
    )(x2, g, w, pos2, inv2, sgn2)


def _nsa_compress_kernel(k16_ref, v16_ref, pek_ref, pev_ref, w1k_ref, w2k_ref, w2kr_ref,
                         w1v_ref, w2v_ref, pos_ref, inv_ref, kct_ref, vc_ref):
    half_w = CMP_STRIDE * HEAD_DIM
    nrow = k16_ref.shape[2]

    def hidden(x16_ref, pe_ref, w1_ref):
        x = x16_ref[0, 0]
        y1 = jnp.dot(x, w1_ref[:half_w, :], precision=HIGHEST, preferred_element_type=F32)
        y2 = jnp.dot(x, w1_ref[half_w:, :], precision=HIGHEST, preferred_element_type=F32)
        bias = jnp.dot(pe_ref[...], w1_ref[...], precision=HIGHEST, preferred_element_type=F32)
        return jax.nn.gelu(y1 + pltpu.roll(y2, nrow - 1, 0) + bias[0:1])

    hk = hidden(k16_ref, pek_ref, w1k_ref)
    kc = jnp.dot(hk, w2k_ref[...], precision=HIGHEST, preferred_element_type=F32)
    kc_rot = jnp.dot(hk, w2kr_ref[...], precision=HIGHEST, preferred_element_type=F32)
    ang = pos_ref[0].astype(F32) * inv_ref[...]
    kc = kc * jnp.cos(ang) + kc_rot * jnp.sin(ang)
    hi, lo = _split_bf16(kc)
    kct_ref[0, 0] = jnp.concatenate([hi.astype(F32), lo.astype(F32)], axis=1).T.astype(kct_ref.dtype)
    hv = hidden(v16_ref, pev_ref, w1v_ref)
    vc = jnp.dot(hv, w2v_ref[...], precision=HIGHEST, preferred_element_type=F32)
    vc_ref[0, 0] = jnp.concatenate([vc, jnp.zeros_like(vc)], axis=1).astype(vc_ref.dtype)


def _nsa_compress(k16, v16, pek, pev, w1k, w2k, w2kr, w1v, w2v, pos_cmp, inv64):
    G, b, nrow, wide = k16.shape
    x_spec = pl.BlockSpec((1, 1, nrow, wide), lambda bi, g: (g, bi, 0, 0))
    return pl.pallas_call(
        _nsa_compress_kernel,
        grid=(b, G),
        in_specs=[x_spec, x_spec,
                  _const_spec(pek.shape), _const_spec(pev.shape),
                  _const_spec(w1k.shape), _const_spec(w2k.shape), _const_spec(w2kr.shape),
                  _const_spec(w1v.shape), _const_spec(w2v.shape),
                  pl.BlockSpec((1, nrow, 1), lambda bi, g: (bi, 0, 0)),
                  _const_spec((1, HEAD_DIM))],
        out_specs=[pl.BlockSpec((1, 1, LANES, nrow), lambda bi, g: (bi, g, 0, 0)),
                   pl.BlockSpec((1, 1, nrow, LANES), lambda bi, g: (bi, g, 0, 0))],
        out_shape=[jax.ShapeDtypeStruct((b, G, LANES, nrow), BF16),
                   jax.ShapeDtypeStruct((b, G, nrow, LANES), BF16)],
        compiler_params=_params(("arbitrary", "arbitrary")),
        name="nsa_compress",
    )(k16, v16, pek, pev, w1k, w2k, w2kr, w1v, w2v, pos_cmp, inv64)


def _nsa_attn_kernel(q_ref, kct_ref, vc_ref, ov_ref, ks_ref, vs_ref, kw_ref, vw_ref, gt_ref,
                     gsel_ref, o_ref, qa_ref, oc_ref, psum_ref, m_ref, acc_ref, alpha_ref, p_ref, pw_ref,
                     ow_ref, gs_ref,
                     *, tq, kb, n_slc, wg, rc):
    i = pl.program_id(2)
    q0 = pl.multiple_of(i * tq, tq)
    n_rows = NSA_HPG * tq
    heads = range(NSA_HPG)
    hrows = lambda h: slice(h * tq, (h + 1) * tq)
    qrow = q0 + lax.broadcasted_iota(jnp.int32, (tq, 1), 0)
    q_heads = [q_ref[0, :, h * HEAD_DIM:(h + 1) * HEAD_DIM] for h in heads]
    for h in heads:
        qa_ref[hrows(h), :] = jnp.concatenate([q_heads[h], q_heads[h]], axis=1)

    n_cmp_rows = kct_ref.shape[3]
    s_all = jnp.dot(qa_ref[...], kct_ref[0, 0], preferred_element_type=F32)
    w_keys = WINDOW + tq
    w_start = pl.multiple_of(jnp.maximum(q0 - WINDOW, 0), tq)
    k_w = kw_ref[0, 0, pl.ds(w_start, w_keys), :]
    s_win = [lax.dot_general(qa_ref[h0 * tq:(h0 + wg) * tq, :], k_w, NT_DIMS,
                             preferred_element_type=F32) for h0 in range(0, NSA_HPG, wg)]

    cmp_end = CMP_STRIDE * lax.broadcasted_iota(jnp.int32, (1, n_cmp_rows), 1) + (CMP_LEN - 1)
    c_bias = jnp.where(cmp_end <= qrow, 0.0, NEG)
    row_valid = jnp.where(qrow >= CMP_LEN - 1, 1.0, 0.0)
    for r0 in range(0, n_rows, rc):
        rows = slice(r0, r0 + rc)
        local = slice(r0 % tq, r0 % tq + rc)
        s = s_all[rows] + c_bias[local]
        e = jnp.exp2(s - jnp.max(s, axis=1, keepdims=True))
        p = e * (row_valid[local] / jnp.sum(e, axis=1, keepdims=True))
        if r0 < tq:
            psum_ref[local, :] = p
        else:
            psum_ref[local, :] += p
        p_ref[0, rows, 0:n_cmp_rows] = p.astype(p_ref.dtype)
    g_wide = jnp.dot(jnp.concatenate(_split_bf16(gt_ref[0]), axis=1), gsel_ref[...],
                     preferred_element_type=F32)
    gate = lambda h, branch: g_wide[:, (3 * h + branch) * LANES:(3 * h + branch + 1) * LANES]
    o_cmp = jnp.dot(p_ref[0, :, 0:n_cmp_rows], vc_ref[0, 0], preferred_element_type=F32)
    for h in heads:
        oc_ref[hrows(h), :] = gate(h, 0) * o_cmp[hrows(h)]
        gs_ref[hrows(h), :] = gate(h, 1)
    imp = jnp.dot(jnp.concatenate(_split_bf16(psum_ref[...]), axis=1), ov_ref[...],
                  preferred_element_type=F32)
    imp_t = imp.T[:n_slc]

    w_diff = qrow - (w_start + lax.broadcasted_iota(jnp.int32, (1, w_keys), 1))
    w_bias = jnp.where((w_diff >= 0) & (w_diff < WINDOW), 0.0, NEG)
    for r0 in range(0, n_rows, rc):
        s = s_win[r0 // (wg * tq)][r0 % (wg * tq):r0 % (wg * tq) + rc] + w_bias[r0 % tq:r0 % tq + rc]
        pw_ref[r0:r0 + rc, :] = jnp.exp2(s - jnp.max(s, axis=1, keepdims=True)).astype(pw_ref.dtype)
    o_win = jnp.dot(pw_ref[...], vw_ref[0, 0, pl.ds(w_start, w_keys), :],
                    preferred_element_type=F32)
    for h in heads:
        o_h = o_win[hrows(h)]
        ow_ref[hrows(h), :] = (gate(h, 2) * o_h[:, :LANES]) * (1.0 / o_h[:, LANES:])

    qpos = q0 + lax.broadcasted_iota(jnp.int32, (1, tq), 1)
    blk = lax.broadcasted_iota(jnp.int32, (n_slc, 1), 0)
    cur = qpos >> SLC_SHIFT
    forced = (blk == 0) | (blk == cur) | (blk == cur - 1)
    causal_blk = blk * SLC_LEN <= qpos
    score = jnp.where(causal_blk, imp_t + FORCE_BONUS * forced.astype(F32), NEG)
    sub = lax.broadcasted_iota(jnp.int32, (8, 1), 0)
    groups = [score[8 * v:8 * v + 8] for v in range(n_slc // 8)]
    counts = [jnp.zeros((8, tq), F32) for _ in groups]
    for m in range(n_slc):
        row = score[m:m + 1]
        for v, sv in enumerate(groups):
            if v < m // 8:
                ahead = row > sv
            elif v > m // 8:
                ahead = row >= sv
            else:
                ahead = (row > sv) | ((row == sv) & (sub > m % 8))
            counts[v] = counts[v] + jnp.where(ahead, 1.0, 0.0)
    rank = jnp.concatenate(counts, axis=0)
    sel_bias_t = jnp.where(rank < SLC_TOP_N, 0.0, NEG)
    sel_bias = jnp.concatenate([sel_bias_t, jnp.zeros((LANES - n_slc, tq), F32)], axis=0).T
    sel_bias = sel_bias[:, :HEAD_DIM].astype(qa_ref.dtype)
    for h in heads:
        qa_ref[hrows(h), :] = jnp.concatenate([q_heads[h], sel_bias], axis=1)

    tok_bias = jnp.where(lax.broadcasted_iota(jnp.int32, (1, kb), 1)
                         <= lax.broadcasted_iota(jnp.int32, (tq, 1), 0), 0.0, NEG)

    def scores(start):
        k_t = ks_ref[0, 0, pl.ds(start, kb), :]
        return lax.dot_general(qa_ref[...], k_t, NT_DIMS, preferred_element_type=F32)

    def softmax(s_all, slot, own_keys):
        for r0 in range(0, n_rows, rc):
            rows = slice(r0, r0 + rc)
            s = s_all[rows]
            if own_keys:
                s = s + tok_bias[r0 % tq:r0 % tq + rc]
                m_new = jnp.broadcast_to(jnp.max(s, axis=1, keepdims=True), (rc, LANES))
            else:
                m_old = m_ref[rows, :]
                m_new = jnp.maximum(m_old, jnp.max(s, axis=1, keepdims=True))
                alpha_ref[slot, rows, :] = jnp.exp2(m_old - m_new)
            m_ref[rows, :] = m_new
            m_wide = jnp.concatenate([m_new] * (kb // LANES), axis=1)
            p_ref[slot, rows, :] = jnp.exp2(s - m_wide).astype(p_ref.dtype)

    def values(start, slot):
        v_t = vs_ref[0, 0, pl.ds(start, kb), :]
        pv = jnp.dot(p_ref[slot], v_t, preferred_element_type=F32)
        for r0 in range(0, n_rows, rc):
            rows = slice(r0, r0 + rc)
            acc_ref[rows, :] = alpha_ref[slot, rows, :] * acc_ref[rows, :] + pv[rows]

    chunk_start = lambda c: pl.multiple_of(c * kb, kb)
    owed_start = lambda c: pl.multiple_of(jnp.where(c == 0, q0, (c - 1) * kb), kb)

    acc_ref[...] = jnp.zeros_like(acc_ref)
    alpha_ref[1] = jnp.zeros(alpha_ref.shape[1:], F32)
    softmax(scores(q0), 1, True)

    def pair_body(t, carry):
        c = 2 * t
        s_a = scores(chunk_start(c))
        values(owed_start(c), 1)
        softmax(s_a, 0, False)
        s_b = scores(chunk_start(c + 1))
        values(chunk_start(c), 0)
        softmax(s_b, 1, False)
        return carry

    lax.fori_loop(0, i // 2, pair_body, 0)

    @pl.when(i % 2 == 1)
    def _():
        s_a = scores(chunk_start(i - 1))
        values(owed_start(i - 1), 1)
        softmax(s_a, 0, False)
        values(chunk_start(i - 1), 0)

    @pl.when(i % 2 == 0)
    def _():
        values(owed_start(i), 1)

    outs = []
    for h in heads:
        o_h = acc_ref[hrows(h), :]
        o_sel = (gs_ref[hrows(h), :] * o_h) * (1.0 / o_h[:, HEAD_DIM:HEAD_DIM + 1])
        outs.append(((oc_ref[hrows(h), :] + o_sel) + ow_ref[hrows(h), :])[:, :HEAD_DIM])
    for hp in range(NSA_HPG // 2):
        pair = jnp.concatenate(outs[2 * hp:2 * hp + 2], axis=1)
        o_ref[0, :, hp * LANES:(hp + 1) * LANES] = pair.astype(o_ref.dtype)


def _nsa_attention(q3, kct, vc, ov, ksa, vsa, kwa, vwa, gt, tq=256, wg=4, rc=64):
    b, s, _ = q3.shape
    G = NSA_KV_GROUPS
    kb = tq
    n_slc = s // SLC_LEN
    nq = s // tq
    n_cmp_rows = kct.shape[3]
    gw = NSA_HPG * HEAD_DIM
    assert n_slc <= HEAD_DIM and tq % SLC_LEN == 0 and WINDOW % tq == 0 and s >= WINDOW + tq
    assert n_cmp_rows <= kb and n_slc % 8 == 0 and tq % rc == 0
    kv_spec = pl.BlockSpec((1, 1, s, LANES), lambda bi, g, i: (g, bi, 0, 0))
    n_gates = NSA_GATES_PER_GROUP
    gsel = np.zeros((2 * LANES, n_gates * LANES), np.float32)
    for c in range(n_gates):
        gsel[[c, LANES + c], c * LANES:(c + 1) * LANES] = 1.0
    gsel = jnp.asarray(gsel, BF16)
    return pl.pallas_call(
        functools.partial(_nsa_attn_kernel, tq=tq, kb=kb, n_slc=n_slc, wg=wg, rc=rc),
        grid=(b, G, nq),
        in_specs=[pl.BlockSpec((1, tq, gw), lambda bi, g, i: (bi, i, g)),
                  pl.BlockSpec((1, 1, LANES, n_cmp_rows), lambda bi, g, i: (bi, g, 0, 0)),
                  pl.BlockSpec((1, 1, n_cmp_rows, LANES), lambda bi, g, i: (bi, g, 0, 0)),
                  _const_spec(ov.shape),
                  kv_spec, kv_spec, kv_spec,
                  pl.BlockSpec((1, 1, s, 2 * LANES), lambda bi, g, i: (g, bi, 0, 0)),
                  pl.BlockSpec((1, tq, LANES), lambda bi, g, i: (g, bi * nq + i, 0)),
                  _const_spec(gsel.shape)],
        out_specs=pl.BlockSpec((1, tq, gw), lambda bi, g, i: (bi, i, g)),
        out_shape=jax.ShapeDtypeStruct((b, s, D_MODEL), BF16),
        scratch_shapes=[pltpu.VMEM((NSA_HPG * tq, LANES), BF16),
                        pltpu.VMEM((NSA_HPG * tq, LANES), F32),
                        pltpu.VMEM((tq, n_cmp_rows), F32),
                        pltpu.VMEM((NSA_HPG * tq, LANES), F32),
                        pltpu.VMEM((NSA_HPG * tq, LANES), F32),
                        pltpu.VMEM((2, NSA_HPG * tq, LANES), F32),
                        pltpu.VMEM((2, NSA_HPG * tq, kb), BF16),
                        pltpu.VMEM((NSA_HPG * tq, WINDOW + tq), BF16),
                        pltpu.VMEM((NSA_HPG * tq, LANES), F32),
                        pltpu.VMEM((NSA_HPG * tq, LANES), F32)],
        compiler_params=_params(("arbitrary", "arbitrary", "arbitrary")),
        name="nsa_attention",
    )(q3, kct, vc, ov, ksa, vsa, kwa, vwa, gt, gsel)


def _nsa_layer_attention(hx2, norm_g, positions, w_in, pe_k, pe_v, w1k, w2k, w1v, w2v, b, s):
    t = b * s
    G = NSA_KV_GROUPS
    w_pad = jnp.pad(w_in, ((0, 0), (0, NSA_W_COLS - w_in.shape[1]))).astype(BF16)
    inv = ROPE_THETA ** (-jnp.arange(HALF, dtype=F32) / HALF)
    inv2 = jnp.tile(inv, LANES // HALF)[None, :]
    sgn2 = jnp.tile(jnp.concatenate([-jnp.ones(HALF, F32), jnp.ones(HALF, F32)]), LANES // HEAD_DIM)[None, :]
    q, kc_tok, vc_tok, ksa, vsa, kwa, vwa, gt = _nsa_in_proj(
        hx2, norm_g, w_pad, positions.reshape(t, 1), inv2, sgn2, s)

    nrow = s // CMP_STRIDE
    wide = CMP_STRIDE * HEAD_DIM
    k16 = kc_tok.reshape(G, b, nrow, wide)
    v16 = vc_tok.reshape(G, b, nrow, wide)
    pek = jnp.broadcast_to(pe_k.reshape(1, CMP_LEN * HEAD_DIM), (8, CMP_LEN * HEAD_DIM))
    pev = jnp.broadcast_to(pe_v.reshape(1, CMP_LEN * HEAD_DIM), (8, CMP_LEN * HEAD_DIM))
    w2k_rot = jnp.concatenate([-w2k[:, HALF:], w2k[:, :HALF]], axis=1)
    end_idx = jnp.minimum(jnp.arange(nrow) * CMP_STRIDE + CMP_LEN - 1, s - 1)
    pos_cmp = positions[:, end_idx][:, :, None]
    inv64 = jnp.tile(inv, 2)[None, :]
    kct, vc = _nsa_compress(k16, v16, pek, pev, w1k, w2k, w2k_rot, w1v, w2v, pos_cmp, inv64)

    n_slc = s // SLC_LEN
    c0 = np.arange(nrow)[:, None] * CMP_STRIDE
    s0 = np.arange(n_slc)[None, :] * SLC_LEN
    ov = np.clip(np.minimum(c0 + CMP_LEN, s0 + SLC_LEN) - np.maximum(c0, s0), 0, None) / CMP_LEN
    ov[(s - CMP_LEN) // CMP_STRIDE + 1:, :] = 0.0
    ov = np.pad(ov, ((0, 0), (0, LANES - n_slc)))
    ov = jnp.asarray(np.concatenate([ov, ov], axis=0), BF16)

    aug4 = lambda a: a.reshape(G, b, s, a.shape[-1])
    return _nsa_attention(q.reshape(b, s, D_MODEL), kct, vc, ov, aug4(ksa), aug4(vsa),
                          aug4(kwa), aug4(vwa), gt)


def kernel(x, positions, norm_mix, sba_w_in, sba_w_out, nsa_w_in, nsa_cmp_pos_k, nsa_cmp_pos_v,
           nsa_cmp_k_w1, nsa_cmp_k_w2, nsa_cmp_v_w1, nsa_cmp_v_w2, nsa_w_out, norm_ffn,
           ffn_w_up, ffn_conv_w, ffn_conv_b, ffn_w_down, norm_final):
    b, s, d = x.shape
    t = b * s
    depth = norm_mix.shape[0]
    x2 = x.reshape(t, d)
    g_final = norm_final.reshape(1, d)
    for layer in range(depth):
        j = layer // 2
        g_mix = norm_mix[layer].reshape(1, d)
        if layer % 2 == 0:
            qkv = _sba_in_proj(x2, g_mix, sba_w_in[j].astype(BF16))
            o = _sba_attention(qkv.reshape(b, s, 3 * d), b, s)
            w_out = sba_w_out[j]
        else:
            o = _nsa_layer_attention(x2, g_mix, positions, nsa_w_in[j], nsa_cmp_pos_k[j],
                                     nsa_cmp_pos_v[j], nsa_cmp_k_w1[j], nsa_cmp_k_w2[j],
                                     nsa_cmp_v_w1[j], nsa_cmp_v_w2[j], b, s)
            w_out = nsa_w_out[j]
        x2 = _conv_ffn(x2, o.reshape(t, d), w_out.astype(BF16),
                       norm_ffn[layer].reshape(1, d), ffn_w_up[layer].astype(BF16),
                       ffn_conv_w[layer], ffn_conv_b[layer].reshape(1, -1),
                       ffn_w_down[layer].astype(BF16), g_final, s,
                       final_norm=(layer == depth - 1))
    return x2.reshape(b, s, d)
```

```python
import functools

import numpy as np
import jax
import jax.numpy as jnp
from jax import lax
from jax.experimental import pallas as pl
from jax.experimental.pallas import tpu as pltpu

D_MODEL = 1024
N_HEADS = 16
HEAD_DIM = 64
HALF = HEAD_DIM // 2
NSA_KV_GROUPS = 2
NSA_HPG = N_HEADS // NSA_KV_GROUPS
CMP_LEN = 32
CMP_STRIDE = 16
CMP_HIDDEN = 2 * HEAD_DIM
SLC_LEN = 64
SLC_SHIFT = SLC_LEN.bit_length() - 1
SLC_TOP_N = 16
WINDOW = 512
ROPE_THETA = 10000.0
D_FF = 2816
RMS_EPS = 1e-6
NEG = -1e30
FORCE_BONUS = 1e4
LOG2E = float(np.log2(np.e))
EXP2_CLAMP = 120.0
Q_SCALE = LOG2E * HEAD_DIM ** -0.5

LANES = 128
VMEM_LIMIT = 56 * 1024 * 1024

F32 = jnp.float32
BF16 = jnp.bfloat16
HIGHEST = lax.Precision.HIGHEST
NT_DIMS = (((1,), (1,)), ((), ()))


def _params(semantics):
    return pltpu.CompilerParams(dimension_semantics=semantics, vmem_limit_bytes=VMEM_LIMIT)


def _rmsnorm(x, g):
    return x * lax.rsqrt(jnp.mean(x * x, axis=-1, keepdims=True) + RMS_EPS) * g


def _const_spec(shape):
    return pl.BlockSpec(shape, lambda *_: (0,) * len(shape), pipeline_mode=pl.Buffered(1))


def _split_bf16(x):
    hi = x.astype(BF16)
    return hi, (x - hi.astype(F32)).astype(BF16)


def _sba_in_proj_kernel(x_ref, g_ref, w_ref, o_ref, *, n_chunk):
    hn = _rmsnorm(x_ref[...], g_ref[...]).astype(w_ref.dtype)
    n = w_ref.shape[1]
    for c in range(n // n_chunk):
        cols = slice(c * n_chunk, (c + 1) * n_chunk)
        y = jnp.dot(hn, w_ref[:, cols], preferred_element_type=F32)
        if c * n_chunk < D_MODEL:
            y = y * Q_SCALE
        o_ref[:, cols] = y.astype(o_ref.dtype)


def _sba_in_proj(x2, g, w, tm=512):
    t, d = x2.shape
    n = w.shape[1]
    return pl.pallas_call(
        functools.partial(_sba_in_proj_kernel, n_chunk=512),
        grid=(t // tm,),
        in_specs=[pl.BlockSpec((tm, d), lambda i: (i, 0)),
                  _const_spec((1, d)),
                  _const_spec((d, n))],
        out_specs=pl.BlockSpec((tm, n), lambda i: (i, 0)),
        out_shape=jax.ShapeDtypeStruct((t, n), BF16),
        compiler_params=_params(("arbitrary",)),
        name="sba_in_proj",
    )(x2, g, w)


def _sba_attn_kernel(q_ref, k_ref, v_ref, uu_ref, o_ref, acc_ref, car_ref, hl_ref, lb_ref, a_ref,
                     z_ref,
                     *, tq, kb, nsub, rc):
    i = pl.program_id(2)
    kt = kb * nsub
    q = q_ref[0]
    lane = lax.broadcasted_iota(jnp.int32, (tq, LANES), 1)
    klane = lax.broadcasted_iota(jnp.int32, (kb, LANES), 1)
    acc_ref[...] = jnp.zeros_like(acc_ref)
    car_ref[...] = jnp.zeros_like(car_ref)
    ud = uu_ref[...]
    rel = (lax.broadcasted_iota(jnp.int32, (rc, 2 * kb), 0)
           - (lax.broadcasted_iota(jnp.int32, (rc, 2 * kb), 1) & (kb - 1)))

    def super_tile(st, diag, slot, prev_slot, z_first):

        def visibility(j, r0):
            if not diag:
                return 2, None
            if j * kb >= r0 + rc - 1:
                return 0, None
            if j * kb + kb - 1 < r0:
                return 2, None
            return 1, rel > (j * kb - r0)

        def scores(j, st=st):
            k_t = k_ref[0, pl.ds(pl.multiple_of(st * kt + j * kb, kb), kb), :]
            k_bd = jnp.concatenate([jnp.where(klane < HEAD_DIM, k_t, jnp.zeros_like(k_t)),
                                    jnp.where(klane >= HEAD_DIM, k_t, jnp.zeros_like(k_t))], axis=0)
            return lax.dot_general(q, k_bd, NT_DIMS, preferred_element_type=F32)

        def log_terms(j, z):
            totals = []
            for r0 in range(0, tq, rc):
                rows = slice(r0, r0 + rc)
                kind, strict = visibility(j, r0)
                if kind == 0:
                    hl_ref[j, rows, :] = jnp.zeros((rc, 2 * kb), BF16)
                    totals.append(None)
                    continue
                zc = z[rows]
                nl = jnp.maximum(jnp.log2(1.0 + jnp.exp2(jnp.minimum(zc, EXP2_CLAMP))), zc)
                lb_ref[j, rows, :] = zc - nl
                if kind == 1:
                    nl = jnp.where(strict, nl, 0.0)
                hl_ref[j, rows, :] = nl.astype(BF16)
                totals.append([jnp.sum(nl[:, h * kb:(h + 1) * kb], axis=1, keepdims=True)
                               for h in range(2)])
            return jnp.dot(hl_ref[j], ud, preferred_element_type=F32), totals

        def weights(j, sums):
            cum, totals = sums
            for n, r0 in enumerate(range(0, tq, rc)):
                rows = slice(r0, r0 + rc)
                cols = slice(j * kb, (j + 1) * kb)
                kind, strict = visibility(j, r0)
                if kind == 0:
                    for h in range(2):
                        a_ref[slot, h, rows, cols] = jnp.zeros((rc, kb), BF16)
                    continue
                car = [car_ref[h, rows, :] for h in range(2)]
                a = jnp.exp2(lb_ref[j, rows, :] - cum[rows] - jnp.concatenate(car, axis=1))
                if kind == 1:
                    a = jnp.where(strict, a, 0.0)
                for h in range(2):
                    a_ref[slot, h, rows, cols] = a[:, h * kb:(h + 1) * kb].astype(BF16)
                    car_ref[h, rows, :] = car[h] + totals[n][h]

        z_next = scores(nsub - 1) if z_first is None else z_first
        pending = None
        for j in reversed(range(nsub)):
            z = z_next
            if j > 0:
                z_next = scores(j - 1)
            if j == nsub - 1 and prev_slot is not None:
                apply_weights(prev_slot, st + 1)
            r = log_terms(j, z)
            if pending is not None:
                weights(j + 1, pending)
            pending = r
            if j == 0:
                z_ref[...] = scores(nsub - 1, jnp.maximum(st - 1, 0))
        weights(0, pending)

    def apply_weights(slot, st):
        v_all = v_ref[0, pl.ds(pl.multiple_of(st * kt, kt), kt), :]
        for h in range(2):
            acc_ref[h] += jnp.dot(a_ref[slot, h], v_all, preferred_element_type=F32)

    super_tile(i, True, 0, None, None)

    def body(t, carry):
        super_tile(i - 1 - t, False, (t + 1) % 2, t % 2, z_ref[...])
        return carry

    lax.fori_loop(0, i, body, 0)
    apply_weights(i % 2, 0)
    o_ref[0] = jnp.where(lane < HEAD_DIM, acc_ref[0], acc_ref[1]).astype(o_ref.dtype)


def _sba_attention(qkv, b, s, tq=512, kb=LANES, nsub=4, rc=64):
    assert tq == kb * nsub and s % tq == 0 and tq % rc == 0
    n_pairs = D_MODEL // LANES
    assert kb & (kb - 1) == 0
    jj = np.arange(kb)
    uu = jnp.asarray(np.kron(np.eye(2), (jj[:, None] > jj[None, :]).astype(np.float32)), BF16)
    return pl.pallas_call(
        functools.partial(_sba_attn_kernel, tq=tq, kb=kb, nsub=nsub, rc=rc),
        grid=(b, n_pairs, s // tq),
        in_specs=[pl.BlockSpec((1, tq, LANES), lambda bi, p, i: (bi, i, p)),
                  pl.BlockSpec((1, s, LANES), lambda bi, p, i: (bi, 0, n_pairs + p)),
                  pl.BlockSpec((1, s, LANES), lambda bi, p, i: (bi, 0, 2 * n_pairs + p)),
                  _const_spec((2 * kb, 2 * kb))],
        out_specs=pl.BlockSpec((1, tq, LANES), lambda bi, p, i: (bi, i, p)),
        out_shape=jax.ShapeDtypeStruct((b, s, D_MODEL), BF16),
        scratch_shapes=[pltpu.VMEM((2, tq, LANES), F32),
                        pltpu.VMEM((2, tq, LANES), F32),
                        pltpu.VMEM((nsub, tq, 2 * kb), BF16),
                        pltpu.VMEM((nsub, tq, 2 * kb), F32),
                        pltpu.VMEM((2, 2, tq, nsub * kb), BF16),
                        pltpu.VMEM((tq, 2 * kb), F32)],
        compiler_params=_params(("arbitrary", "arbitrary", "arbitrary")),
        name="sba_attention",
    )(qkv, qkv, qkv, uu)


def _ffn_kernel(x_ref, o_ref, wo_ref, g_ref, wup_ref, cw_ref, cb_ref, wdn_ref, gf_ref, y_ref,
                carry_ref, sg_ref, sv_ref, act_ref, *, tm, fc, tiles_per_seq, final_norm):
    @pl.when(pl.program_id(0) % tiles_per_seq == 0)
    def _():
        carry_ref[...] = jnp.zeros_like(carry_ref)

    x = x_ref[...] + jnp.dot(o_ref[...], wo_ref[...], preferred_element_type=F32)
    hn = _rmsnorm(x, g_ref[...]).astype(wup_ref.dtype)

    def up(col0):
        return jnp.dot(hn, wup_ref[:, col0:col0 + fc], preferred_element_type=F32)

    def conv(u, col0, s_ref):
        cols = slice(col0, col0 + fc)
        s_ref[0:8, :] = carry_ref[:, cols]
        s_ref[8:tm + 8, :] = u
        carry_ref[:, cols] = u[tm - 8:tm, :]
        cw = cw_ref[:, cols]
        c = cb_ref[:, cols] + s_ref[6:tm + 6, :] * cw[0:1]
        c = c + s_ref[7:tm + 7, :] * cw[1:2]
        return c + u * cw[2:3]

    n_chunks = D_FF // fc
    u_next = (up(0), up(D_FF))
    for c in range(n_chunks):
        u_gate, u_val = u_next
        if c + 1 < n_chunks:
            u_next = (up((c + 1) * fc), up(D_FF + (c + 1) * fc))
        gate = conv(u_gate, c * fc, sg_ref)
        val = conv(u_val, D_FF + c * fc, sv_ref)
        act_ref[:, c * fc:(c + 1) * fc] = (gate * jax.nn.sigmoid(gate) * val).astype(act_ref.dtype)
    y = x + jnp.dot(act_ref[...], wdn_ref[...], preferred_element_type=F32)
    if final_norm:
        y = _rmsnorm(y, gf_ref[...])
    y_ref[...] = y


def _conv_ffn(x2, o2, w_out, g, w_up, conv_w, conv_b, w_down, g_final, s, final_norm,
              tm=256, fc=256):
    t, d = x2.shape
    f2 = w_up.shape[1]
    return pl.pallas_call(
        functools.partial(_ffn_kernel, tm=tm, fc=fc, tiles_per_seq=s // tm, final_norm=final_norm),
        grid=(t // tm,),
        in_specs=[pl.BlockSpec((tm, d), lambda i: (i, 0)),
                  pl.BlockSpec((tm, d), lambda i: (i, 0)),
                  _const_spec((d, d)),
                  _const_spec((1, d)),
                  _const_spec((d, f2)),
                  _const_spec((3, f2)),
                  _const_spec((1, f2)),
                  _const_spec((D_FF, d)),
                  _const_spec((1, d))],
        out_specs=pl.BlockSpec((tm, d), lambda i: (i, 0)),
        out_shape=jax.ShapeDtypeStruct((t, d), F32),
        scratch_shapes=[pltpu.VMEM((8, f2), F32),
                        pltpu.VMEM((tm + 8, fc), F32),
                        pltpu.VMEM((tm + 8, fc), F32),
                        pltpu.VMEM((tm, D_FF), BF16)],
        compiler_params=_params(("arbitrary",)),
        name="conv_ffn",
    )(x2, o2, w_out, g, w_up, conv_w, conv_b, w_down, g_final)


NSA_Q_COLS = D_MODEL
NSA_W_COLS = D_MODEL + 7 * LANES
NSA_GATES_PER_GROUP = 3 * NSA_HPG


def _swap_halves(y):
    lane = lax.broadcasted_iota(jnp.int32, y.shape, 1)
    first = (lane % HEAD_DIM) < HALF
    return jnp.where(first, pltpu.roll(y, LANES - HALF, 1), pltpu.roll(y, HALF, 1))


def _nsa_in_proj_kernel(x_ref, g_ref, w_ref, pos_ref, inv_ref, sgn_ref,
                        q_ref, kc_ref, vc_ref, ksa_ref, vsa_ref, kwa_ref, vwa_ref, gt_ref,
                        *, tm, seq):
    hn = _rmsnorm(x_ref[...], g_ref[...]).astype(w_ref.dtype)
    ang = pos_ref[...].astype(F32) * inv_ref[...]
    cos = jnp.cos(ang)
    sin = jnp.sin(ang) * sgn_ref[...]

    def rope(y):
        return y * cos + _swap_halves(y) * sin

    q_chunk = 4 * LANES
    for c in range(NSA_Q_COLS // q_chunk):
        y = jnp.dot(hn, w_ref[:, c * q_chunk:(c + 1) * q_chunk], preferred_element_type=F32)
        for l in range(q_chunk // LANES):
            yl = rope(y[:, l * LANES:(l + 1) * LANES]) * Q_SCALE
            q_ref[:, c * q_chunk + l * LANES:c * q_chunk + (l + 1) * LANES] = yl.astype(q_ref.dtype)
    y = jnp.dot(hn, w_ref[:, NSA_Q_COLS:], preferred_element_type=F32)
    part = lambda n: y[:, n * LANES:(n + 1) * LANES]

    tok = ((pl.program_id(0) % (seq // tm)) * tm
           + lax.broadcasted_iota(jnp.int32, (tm, HEAD_DIM), 0))
    lane = lax.broadcasted_iota(jnp.int32, (tm, HEAD_DIM), 1)
    blk_onehot = jnp.where((tok >> SLC_SHIFT) == lane, 1.0, 0.0)
    ones_col = jnp.where(lane == 0, 1.0, 0.0)
    zeros = jnp.zeros((tm, HEAD_DIM), F32)
    ks, vs, kw, vw = rope(part(2)), part(3), rope(part(4)), part(5)
    for g in range(NSA_KV_GROUPS):
        cols = slice(g * HEAD_DIM, (g + 1) * HEAD_DIM)
        kc_ref[g] = part(0)[:, cols]
        vc_ref[g] = part(1)[:, cols]
        ksa_ref[g] = jnp.concatenate([ks[:, cols], blk_onehot], axis=1).astype(ksa_ref.dtype)
        vsa_ref[g] = jnp.concatenate([vs[:, cols], ones_col], axis=1).astype(vsa_ref.dtype)
        kwa_ref[g] = jnp.concatenate([kw[:, cols], zeros], axis=1).astype(kwa_ref.dtype)
        vwa_ref[g] = jnp.concatenate([vw[:, cols], zeros, jnp.ones((tm, LANES), F32)],
                                     axis=1).astype(vwa_ref.dtype)
    gates = jax.nn.sigmoid(part(6))
    gt_ref[0] = gates
    gt_ref[1] = pltpu.roll(gates, LANES - NSA_GATES_PER_GROUP, 1)


def _nsa_in_proj(x2, g, w, pos2, inv2, sgn2, seq, tm=512):
    t, d = x2.shape
    G = NSA_KV_GROUPS
    tok_g = lambda dt: jax.ShapeDtypeStruct((G, t, HEAD_DIM), dt)
    aug_g = lambda dt: jax.ShapeDtypeStruct((G, t, LANES), dt)
    tok_spec = pl.BlockSpec((G, tm, HEAD_DIM), lambda i: (0, i, 0))
    aug_spec = pl.BlockSpec((G, tm, LANES), lambda i: (0, i, 0))
    return pl.pallas_call(
        functools.partial(_nsa_in_proj_kernel, tm=tm, seq=seq),
        grid=(t // tm,),
        in_specs=[pl.BlockSpec((tm, d), lambda i: (i, 0)),
                  _const_spec((1, d)),
                  _const_spec((d, NSA_W_COLS)),
                  pl.BlockSpec((tm, 1), lambda i: (i, 0)),
                  _const_spec((1, LANES)),
                  _const_spec((1, LANES))],
        out_specs=[pl.BlockSpec((tm, NSA_Q_COLS), lambda i: (i, 0)),
                   tok_spec, tok_spec, aug_spec, aug_spec, aug_spec,
                   pl.BlockSpec((G, tm, 2 * LANES), lambda i: (0, i, 0)), aug_spec],
        out_shape=[jax.ShapeDtypeStruct((t, NSA_Q_COLS), BF16),
                   tok_g(F32), tok_g(F32), aug_g(BF16), aug_g(BF16), aug_g(BF16),
                   jax.ShapeDtypeStruct((G, t, 2 * LANES), BF16), aug_g(F32)],
        compiler_params=_params(("arbitrary",)),
        name="nsa_in_proj",
    )(x2, g, w, pos2, inv2, sgn2)


def _nsa_compress_kernel(k16_ref, v16_ref, pek_ref, pev_ref, w1k_ref, w2k_ref, w2kr_ref,
                         w1v_ref, w2v_ref, pos_ref, inv_ref, kct_ref, vc_ref):
    half_w = CMP_STRIDE * HEAD_DIM
    nrow = k16_ref.shape[2]

    def hidden(x16_ref, pe_ref, w1_ref):
        x = x16_ref[0, 0]
        y1 = jnp.dot(x, w1_ref[:half_w, :], precision=HIGHEST, preferred_element_type=F32)
        y2 = jnp.dot(x, w1_ref[half_w:, :], precision=HIGHEST, preferred_element_type=F32)
        bias = jnp.dot(pe_ref[...], w1_ref[...], precision=HIGHEST, preferred_element_type=F32)
        return jax.nn.gelu(y1 + pltpu.roll(y2, nrow - 1, 0) + bias[0:1])

    hk = hidden(k16_ref, pek_ref, w1k_ref)
    kc = jnp.dot(hk, w2k_ref[...], precision=HIGHEST, preferred_element_type=F32)
    kc_rot = jnp.dot(hk, w2kr_ref[...], precision=HIGHEST, preferred_element_type=F32)
    ang = pos_ref[0].astype(F32) * inv_ref[...]
    kc = kc * jnp.cos(ang) + kc_rot * jnp.sin(ang)
    hi, lo = _split_bf16(kc)
    kct_ref[0, 0] = jnp.concatenate([hi.astype(F32), lo.astype(F32)], axis=1).T.astype(kct_ref.dtype)
    hv = hidden(v16_ref, pev_ref, w1v_ref)
    vc = jnp.dot(hv, w2v_ref[...], precision=HIGHEST, preferred_element_type=F32)
    vc_ref[0, 0] = jnp.concatenate([vc, jnp.zeros_like(vc)], axis=1).astype(vc_ref.dtype)


def _nsa_compress(k16, v16, pek, pev, w1k, w2k, w2kr, w1v, w2v, pos_cmp, inv64):
    G, b, nrow, wide = k16.shape
    x_spec = pl.BlockSpec((1, 1, nrow, wide), lambda bi, g: (g, bi, 0, 0))
    return pl.pallas_call(
        _nsa_compress_kernel,
        grid=(b, G),
        in_specs=[x_spec, x_spec,
                  _const_spec(pek.shape), _const_spec(pev.shape),
                  _const_spec(w1k.shape), _const_spec(w2k.shape), _const_spec(w2kr.shape),
                  _const_spec(w1v.shape), _const_spec(w2v.shape),
                  pl.BlockSpec((1, nrow, 1), lambda bi, g: (bi, 0, 0)),
                  _const_spec((1, HEAD_DIM))],
        out_specs=[pl.BlockSpec((1, 1, LANES, nrow), lambda bi, g: (bi, g, 0, 0)),
                   pl.BlockSpec((1, 1, nrow, LANES), lambda bi, g: (bi, g, 0, 0))],
        out_shape=[jax.ShapeDtypeStruct((b, G, LANES, nrow), BF16),
                   jax.ShapeDtypeStruct((b, G, nrow, LANES), BF16)],
        compiler_params=_params(("arbitrary", "arbitrary")),
        name="nsa_compress",
    )(k16, v16, pek, pev, w1k, w2k, w2kr, w1v, w2v, pos_cmp, inv64)


def _nsa_attn_kernel(q_ref, kct_ref, vc_ref, ov_ref, ks_ref, vs_ref, kw_ref, vw_ref, gt_ref,
                     gsel_ref, o_ref, qa_ref, oc_ref, psum_ref, m_ref, acc_ref, alpha_ref, p_ref, pw_ref,
                     ow_ref, gs_ref,
                     *, tq, kb, n_slc, wg, rc):
    i = pl.program_id(2)
    q0 = pl.multiple_of(i * tq, tq)
    n_rows = NSA_HPG * tq
    heads = range(NSA_HPG)
    hrows = lambda h: slice(h * tq, (h + 1) * tq)
    qrow = q0 + lax.broadcasted_iota(jnp.int32, (tq, 1), 0)
    q_heads = [q_ref[0, :, h * HEAD_DIM:(h + 1) * HEAD_DIM] for h in heads]
    for h in heads:
        qa_ref[hrows(h), :] = jnp.concatenate([q_heads[h], q_heads[h]], axis=1)

    n_cmp_rows = kct_ref.shape[3]
    s_all = jnp.dot(qa_ref[...], kct_ref[0, 0], preferred_element_type=F32)
    w_keys = WINDOW + tq
    w_start = pl.multiple_of(jnp.maximum(q0 - WINDOW, 0), tq)
    k_w = kw_ref[0, 0, pl.ds(w_start, w_keys), :]
    s_win = [lax.dot_general(qa_ref[h0 * tq:(h0 + wg) * tq, :], k_w, NT_DIMS,
                             preferred_element_type=F32) for h0 in range(0, NSA_HPG, wg)]

    cmp_end = CMP_STRIDE * lax.broadcasted_iota(jnp.int32, (1, n_cmp_rows), 1) + (CMP_LEN - 1)
    c_bias = jnp.where(cmp_end <= qrow, 0.0, NEG)
    row_valid = jnp.where(qrow >= CMP_LEN - 1, 1.0, 0.0)
    for r0 in range(0, n_rows, rc):
        rows = slice(r0, r0 + rc)
        local = slice(r0 % tq, r0 % tq + rc)
        s = s_all[rows] + c_bias[local]
        e = jnp.exp2(s - jnp.max(s, axis=1, keepdims=True))
        p = e * (row_valid[local] / jnp.sum(e, axis=1, keepdims=True))
        if r0 < tq:
            psum_ref[local, :] = p
        else:
            psum_ref[local, :] += p
        p_ref[0, rows, 0:n_cmp_rows] = p.astype(p_ref.dtype)
    g_wide = jnp.dot(jnp.concatenate(_split_bf16(gt_ref[0]), axis=1), gsel_ref[...],
                     preferred_element_type=F32)
    gate = lambda h, branch: g_wide[:, (3 * h + branch) * LANES:(3 * h + branch + 1) * LANES]
    o_cmp = jnp.dot(p_ref[0, :, 0:n_cmp_rows], vc_ref[0, 0], preferred_element_type=F32)
    for h in heads:
        oc_ref[hrows(h), :] = gate(h, 0) * o_cmp[hrows(h)]
        gs_ref[hrows(h), :] = gate(h, 1)
    imp = jnp.dot(jnp.concatenate(_split_bf16(psum_ref[...]), axis=1), ov_ref[...],
                  preferred_element_type=F32)
    imp_t = imp.T[:n_slc]

    w_diff = qrow - (w_start + lax.broadcasted_iota(jnp.int32, (1, w_keys), 1))
    w_bias = jnp.where((w_diff >= 0) & (w_diff < WINDOW), 0.0, NEG)
    for r0 in range(0, n_rows, rc):
        s = s_win[r0 // (wg * tq)][r0 % (wg * tq):r0 % (wg * tq) + rc] + w_bias[r0 % tq:r0 % tq + rc]
        pw_ref[r0:r0 + rc, :] = jnp.exp2(s - jnp.max(s, axis=1, keepdims=True)).astype(pw_ref.dtype)
    o_win = jnp.dot(pw_ref[...], vw_ref[0, 0, pl.ds(w_start, w_keys), :],
                    preferred_element_type=F32)
    for h in heads:
        o_h = o_win[hrows(h)]
        ow_ref[hrows(h), :] = (gate(h, 2) * o_h[:, :LANES]) * (1.0 / o_h[:, LANES:])

    qpos = q0 + lax.broadcasted_iota(jnp.int32, (1, tq), 1)
    blk = lax.broadcasted_iota(jnp.int32, (n_slc, 1), 0)
    cur = qpos >> SLC_SHIFT
    forced = (blk == 0) | (blk == cur) | (blk == cur - 1)
    causal_blk = blk * SLC_LEN <= qpos
    score = jnp.where(causal_blk, imp_t + FORCE_BONUS * forced.astype(F32), NEG)
    sub = lax.broadcasted_iota(jnp.int32, (8, 1), 0)
    groups = [score[8 * v:8 * v + 8] for v in range(n_slc // 8)]
    counts = [jnp.zeros((8, tq), F32) for _ in groups]
    for m in range(n_slc):
        row = score[m:m + 1]
        for v, sv in enumerate(groups):
            if v < m // 8:
                ahead = row > sv
            elif v > m // 8:
                ahead = row >= sv
            else:
                ahead = (row > sv) | ((row == sv) & (sub > m % 8))
            counts[v] = counts[v] + jnp.where(ahead, 1.0, 0.0)
    rank = jnp.concatenate(counts, axis=0)
    sel_bias_t = jnp.where(rank < SLC_TOP_N, 0.0, NEG)
    sel_bias = jnp.concatenate([sel_bias_t, jnp.zeros((LANES - n_slc, tq), F32)], axis=0).T
    sel_bias = sel_bias[:, :HEAD_DIM].astype(qa_ref.dtype)
    for h in heads:
        qa_ref[hrows(h), :] = jnp.concatenate([q_heads[h], sel_bias], axis=1)

    tok_bias = jnp.where(lax.broadcasted_iota(jnp.int32, (1, kb), 1)
                         <= lax.broadcasted_iota(jnp.int32, (tq, 1), 0), 0.0, NEG)

    def scores(start):
        k_t = ks_ref[0, 0, pl.ds(start, kb), :]
        return lax.dot_general(qa_ref[...], k_t, NT_DIMS, preferred_element_type=F32)

    def softmax(s_all, slot, own_keys):
        for r0 in range(0, n_rows, rc):
            rows = slice(r0, r0 + rc)
            s = s_all[rows]
            if own_keys:
                s = s + tok_bias[r0 % tq:r0 % tq + rc]
                m_new = jnp.broadcast_to(jnp.max(s, axis=1, keepdims=True), (rc, LANES))
            else:
                m_old = m_ref[rows, :]
                m_new = jnp.maximum(m_old, jnp.max(s, axis=1, keepdims=True))
                alpha_ref[slot, rows, :] = jnp.exp2(m_old - m_new)
            m_ref[rows, :] = m_new
            m_wide = jnp.concatenate([m_new] * (kb // LANES), axis=1)
            p_ref[slot, rows, :] = jnp.exp2(s - m_wide).astype(p_ref.dtype)

    def values(start, slot):
        v_t = vs_ref[0, 0, pl.ds(start, kb), :]
        pv = jnp.dot(p_ref[slot], v_t, preferred_element_type=F32)
        for r0 in range(0, n_rows, rc):
            rows = slice(r0, r0 + rc)
            acc_ref[rows, :] = alpha_ref[slot, rows, :] * acc_ref[rows, :] + pv[rows]

    chunk_start = lambda c: pl.multiple_of(c * kb, kb)
    owed_start = lambda c: pl.multiple_of(jnp.where(c == 0, q0, (c - 1) * kb), kb)

    acc_ref[...] = jnp.zeros_like(acc_ref)
    alpha_ref[1] = jnp.zeros(alpha_ref.shape[1:], F32)
    softmax(scores(q0), 1, True)

    def pair_body(t, carry):
        c = 2 * t
        s_a = scores(chunk_start(c))
        values(owed_start(c), 1)
        softmax(s_a, 0, False)
        s_b = scores(chunk_start(c + 1))
        values(chunk_start(c), 0)
        softmax(s_b, 1, False)
        return carry

    lax.fori_loop(0, i // 2, pair_body, 0)

    @pl.when(i % 2 == 1)
    def _():
        s_a = scores(chunk_start(i - 1))
        values(owed_start(i - 1), 1)
        softmax(s_a, 0, False)
        values(chunk_start(i - 1), 0)

    @pl.when(i % 2 == 0)
    def _():
        values(owed_start(i), 1)

    outs = []
    for h in heads:
        o_h = acc_ref[hrows(h), :]
        o_sel = (gs_ref[hrows(h), :] * o_h) * (1.0 / o_h[:, HEAD_DIM:HEAD_DIM + 1])
        outs.append(((oc_ref[hrows(h), :] + o_sel) + ow_ref[hrows(h), :])[:, :HEAD_DIM])
    for hp in range(NSA_HPG // 2):
        pair = jnp.concatenate(outs[2 * hp:2 * hp + 2], axis=1)
        o_ref[0, :, hp * LANES:(hp + 1) * LANES] = pair.astype(o_ref.dtype)


def _nsa_attention(q3, kct, vc, ov, ksa, vsa, kwa, vwa, gt, tq=256, wg=4, rc=64):
    b, s, _ = q3.shape
    G = NSA_KV_GROUPS
    kb = tq
    n_slc = s // SLC_LEN
    nq = s // tq
    n_cmp_rows = kct.shape[3]
    gw = NSA_HPG * HEAD_DIM
    assert n_slc <= HEAD_DIM and tq % SLC_LEN == 0 and WINDOW % tq == 0 and s >= WINDOW + tq
    assert n_cmp_rows <= kb and n_slc % 8 == 0 and tq % rc == 0
    kv_spec = pl.BlockSpec((1, 1, s, LANES), lambda bi, g, i: (g, bi, 0, 0))
    n_gates = NSA_GATES_PER_GROUP
    gsel = np.zeros((2 * LANES, n_gates * LANES), np.float32)
    for c in range(n_gates):
        gsel[[c, LANES + c], c * LANES:(c + 1) * LANES] = 1.0
    gsel = jnp.asarray(gsel, BF16)
    return pl.pallas_call(
        functools.partial(_nsa_attn_kernel, tq=tq, kb=kb, n_slc=n_slc, wg=wg, rc=rc),
        grid=(b, G, nq),
        in_specs=[pl.BlockSpec((1, tq, gw), lambda bi, g, i: (bi, i, g)),
                  pl.BlockSpec((1, 1, LANES, n_cmp_rows), lambda bi, g, i: (bi, g, 0, 0)),
                  pl.BlockSpec((1, 1, n_cmp_rows, LANES), lambda bi, g, i: (bi, g, 0, 0)),
                  _const_spec(ov.shape),
                  kv_spec, kv_spec, kv_spec,
                  pl.BlockSpec((1, 1, s, 2 * LANES), lambda bi, g, i: (g, bi, 0, 0)),
                  pl.BlockSpec((1, tq, LANES), lambda bi, g, i: (g, bi * nq + i, 0)),
                  _const_spec(gsel.shape)],
        out_specs=pl.BlockSpec((1, tq, gw), lambda bi, g, i: (bi, i, g)),
        out_shape=jax.ShapeDtypeStruct((b, s, D_MODEL), BF16),
        scratch_shapes=[pltpu.VMEM((NSA_HPG * tq, LANES), BF16),
                        pltpu.VMEM((NSA_HPG * tq, LANES), F32),
                        pltpu.VMEM((tq, n_cmp_rows), F32),
                        pltpu.VMEM((NSA_HPG * tq, LANES), F32),
                        pltpu.VMEM((NSA_HPG * tq, LANES), F32),
                        pltpu.VMEM((2, NSA_HPG * tq, LANES), F32),
                        pltpu.VMEM((2, NSA_HPG * tq, kb), BF16),
                        pltpu.VMEM((NSA_HPG * tq, WINDOW + tq), BF16),
                        pltpu.VMEM((NSA_HPG * tq, LANES), F32),
                        pltpu.VMEM((NSA_HPG * tq, LANES), F32)],
        compiler_params=_params(("arbitrary", "arbitrary", "arbitrary")),
        name="nsa_attention",
    )(q3, kct, vc, ov, ksa, vsa, kwa, vwa, gt, gsel)


def _nsa_layer_attention(hx2, norm_g, positions, w_in, pe_k, pe_v, w1k, w2k, w1v, w2v, b, s):
    t = b * s
    G = NSA_KV_GROUPS
    w_pad = jnp.pad(w_in, ((0, 0), (0, NSA_W_COLS - w_in.shape[1]))).astype(BF16)
    inv = ROPE_THETA ** (-jnp.arange(HALF, dtype=F32) / HALF)
    inv2 = jnp.tile(inv, LANES // HALF)[None, :]
    sgn2 = jnp.tile(jnp.concatenate([-jnp.ones(HALF, F32), jnp.ones(HALF, F32)]), LANES // HEAD_DIM)[None, :]
    q, kc_tok, vc_tok, ksa, vsa, kwa, vwa, gt = _nsa_in_proj(
        hx2, norm_g, w_pad, positions.reshape(t, 1), inv2, sgn2, s)

    nrow = s // CMP_STRIDE
    wide = CMP_STRIDE * HEAD_DIM
    k16 = kc_tok.reshape(G, b, nrow, wide)
    v16 = vc_tok.reshape(G, b, nrow, wide)
    pek = jnp.broadcast_to(pe_k.reshape(1, CMP_LEN * HEAD_DIM), (8, CMP_LEN * HEAD_DIM))
    pev = jnp.broadcast_to(pe_v.reshape(1, CMP_LEN * HEAD_DIM), (8, CMP_LEN * HEAD_DIM))
    w2k_rot = jnp.concatenate([-w2k[:, HALF:], w2k[:, :HALF]], axis=1)
    end_idx = jnp.minimum(jnp.arange(nrow) * CMP_STRIDE + CMP_LEN - 1, s - 1)
    pos_cmp = positions[:, end_idx][:, :, None]
    inv64 = jnp.tile(inv, 2)[None, :]
    kct, vc = _nsa_compress(k16, v16, pek, pev, w1k, w2k, w2k_rot, w1v, w2v, pos_cmp, inv64)

    n_slc = s // SLC_LEN
    c0 = np.arange(nrow)[:, None] * CMP_STRIDE
    s0 = np.arange(n_slc)[None, :] * SLC_LEN
    ov = np.clip(np.minimum(c0 + CMP_LEN, s0 + SLC_LEN) - np.maximum(c0, s0), 0, None) / CMP_LEN
    ov[(s - CMP_LEN) // CMP_STRIDE + 1:, :] = 0.0
    ov = np.pad(ov, ((0, 0), (0, LANES - n_slc)))
    ov = jnp.asarray(np.concatenate([ov, ov], axis=0), BF16)

    aug4 = lambda a: a.reshape(G, b, s, a.shape[-1])
    return _nsa_attention(q.reshape(b, s, D_MODEL), kct, vc, ov, aug4(ksa), aug4(vsa),
                          aug4(kwa), aug4(vwa), gt)


def kernel(x, positions, norm_mix, sba_w_in, sba_w_out, nsa_w_in, nsa_cmp_pos_k, nsa_cmp_pos_v,
           nsa_cmp_k_w1, nsa_cmp_k_w2, nsa_cmp_v_w1, nsa_cmp_v_w2, nsa_w_out, norm_ffn,
           ffn_w_up, ffn_conv_w, ffn_conv_b, ffn_w_down, norm_final):
    b, s, d = x.shape
    t = b * s
    depth = norm_mix.shape[0]
    x2 = x.reshape(t, d)
    g_final = norm_final.reshape(1, d)
    for layer in range(depth):
        j = layer // 2
        g_mix = norm_mix[layer].reshape(1, d)
        if layer % 2 == 0:
            qkv = _sba_in_proj(x2, g_mix, sba_w_in[j].astype(BF16))
            o = _sba_attention(qkv.reshape(b, s, 3 * d), b, s)
            w_out = sba_w_out[j]
        else:
            o = _nsa_layer_attention(x2, g_mix, positions, nsa_w_in[j], nsa_cmp_pos_k[j],
                                     nsa_cmp_pos_v[j], nsa_cmp_k_w1[j], nsa_cmp_k_w2[j],
                                     nsa_cmp_v_w1[j], nsa_cmp_v_w2[j], b, s)
            w_out = nsa_w_out[j]
        x2 = _conv_ffn(x2, o.reshape(t, d), w_out.astype(BF16),
                       norm_ffn[layer].reshape(1, d), ffn_w_up[layer].astype(BF16),
                       ffn_conv_w[layer], ffn_conv_b[layer].reshape(1, -1),
                       ffn_w_down[layer].astype(BF16), g_final, s,
                       final_norm=(layer == depth - 1))
    return x2.reshape(b, s, d)
```

```python
import functools

import numpy as np
import jax
import jax.numpy as jnp
from jax import lax
from jax.experimental import pallas as pl
from jax.experimental.pallas import tpu as pltpu

D_MODEL = 1024
N_HEADS = 16
HEAD_DIM = 64
HALF = HEAD_DIM // 2
NSA_KV_GROUPS = 2
NSA_HPG = N_HEADS // NSA_KV_GROUPS
CMP_LEN = 32
CMP_STRIDE = 16
CMP_HIDDEN = 2 * HEAD_DIM
SLC_LEN = 64
SLC_SHIFT = SLC_LEN.bit_length() - 1
SLC_TOP_N = 16
WINDOW = 512
ROPE_THETA = 10000.0
D_FF = 2816
RMS_EPS = 1e-6
NEG = -1e30
FORCE_BONUS = 1e4
LOG2E = float(np.log2(np.e))
EXP2_CLAMP = 120.0
EXP2_UNDERFLOW = 160.0
Q_SCALE = LOG2E * HEAD_DIM ** -0.5

LANES = 128
VMEM_LIMIT = 56 * 1024 * 1024

F32 = jnp.float32
BF16 = jnp.bfloat16
HIGHEST = lax.Precision.HIGHEST
NT_DIMS = (((1,), (1,)), ((), ()))


def _params(semantics):
    return pltpu.CompilerParams(dimension_semantics=semantics, vmem_limit_bytes=VMEM_LIMIT)


def _rmsnorm(x, g):
    return x * lax.rsqrt(jnp.mean(x * x, axis=-1, keepdims=True) + RMS_EPS) * g


def _const_spec(shape):
    return pl.BlockSpec(shape, lambda *_: (0,) * len(shape), pipeline_mode=pl.Buffered(1))


def _split_bf16(x):
    hi = x.astype(BF16)
    return hi, (x - hi.astype(F32)).astype(BF16)


def _sba_in_proj_kernel(x_ref, g_ref, w_ref, o_ref, *, n_chunk):
    hn = _rmsnorm(x_ref[...], g_ref[...]).astype(w_ref.dtype)
    n = w_ref.shape[1]
    for c in range(n // n_chunk):
        cols = slice(c * n_chunk, (c + 1) * n_chunk)
        y = jnp.dot(hn, w_ref[:, cols], preferred_element_type=F32)
        if c * n_chunk < D_MODEL:
            y = y * Q_SCALE
        o_ref[:, cols] = y.astype(o_ref.dtype)


def _sba_in_proj(x2, g, w, tm=512):
    t, d = x2.shape
    n = w.shape[1]
    return pl.pallas_call(
        functools.partial(_sba_in_proj_kernel, n_chunk=512),
        grid=(t // tm,),
        in_specs=[pl.BlockSpec((tm, d), lambda i: (i, 0)),
                  _const_spec((1, d)),
                  _const_spec((d, n))],
        out_specs=pl.BlockSpec((tm, n), lambda i: (i, 0)),
        out_shape=jax.ShapeDtypeStruct((t, n), BF16),
        compiler_params=_params(("arbitrary",)),
        name="sba_in_proj",
    )(x2, g, w)


def _sba_attn_kernel(q_ref, k_ref, v_ref, uu_ref, o_ref, acc_ref, car_ref, hl_ref, lb_ref, a_ref,
                     z_ref,
                     *, tq, kb, nsub, rc):
    i = pl.program_id(2)
    kt = kb * nsub
    q = q_ref[0]
    lane = lax.broadcasted_iota(jnp.int32, (tq, LANES), 1)
    klane = lax.broadcasted_iota(jnp.int32, (kb, LANES), 1)
    acc_ref[...] = jnp.zeros_like(acc_ref)
    car_ref[...] = jnp.zeros_like(car_ref)
    ud = uu_ref[...]
    rel = (lax.broadcasted_iota(jnp.int32, (rc, 2 * kb), 0)
           - (lax.broadcasted_iota(jnp.int32, (rc, 2 * kb), 1) & (kb - 1)))

    def super_tile(st, diag, slot, prev_slot, z_first):

        def visibility(j, r0):
            if not diag:
                return 2, None
            if j * kb >= r0 + rc - 1:
                return 0, None
            if j * kb + kb - 1 < r0:
                return 2, None
            return 1, rel > (j * kb - r0)

        def scores(j, st=st):
            k_t = k_ref[0, pl.ds(pl.multiple_of(st * kt + j * kb, kb), kb), :]
            k_bd = jnp.concatenate([jnp.where(klane < HEAD_DIM, k_t, jnp.zeros_like(k_t)),
                                    jnp.where(klane >= HEAD_DIM, k_t, jnp.zeros_like(k_t))], axis=0)
            return lax.dot_general(q, k_bd, NT_DIMS, preferred_element_type=F32)

        def log_terms(j, z):
            totals = []
            for r0 in range(0, tq, rc):
                rows = slice(r0, r0 + rc)
                kind, strict = visibility(j, r0)
                if kind == 0:
                    hl_ref[j, rows, :] = jnp.zeros((rc, 2 * kb), BF16)
                    totals.append(None)
                    continue
                zc = z[rows]
                nl = jnp.maximum(jnp.log2(1.0 + jnp.exp2(jnp.minimum(zc, EXP2_CLAMP))), zc)
                lb_ref[j, rows, :] = zc - nl
                if kind == 1:
                    nl = jnp.where(strict, nl, 0.0)
                hl_ref[j, rows, :] = nl.astype(BF16)
                totals.append([jnp.sum(nl[:, h * kb:(h + 1) * kb], axis=1, keepdims=True)
                               for h in range(2)])
            return jnp.dot(hl_ref[j], ud, preferred_element_type=F32), totals

        def weights(j, sums):
            cum, totals = sums
            for n, r0 in enumerate(range(0, tq, rc)):
                rows = slice(r0, r0 + rc)
                cols = slice(j * kb, (j + 1) * kb)
                kind, strict = visibility(j, r0)
                if kind == 0:
                    for h in range(2):
                        a_ref[slot, h, rows, cols] = jnp.zeros((rc, kb), BF16)
                    continue
                car = [car_ref[h, rows, :] for h in range(2)]
                a = jnp.exp2(lb_ref[j, rows, :] - cum[rows] - jnp.concatenate(car, axis=1))
                if kind == 1:
                    a = jnp.where(strict, a, 0.0)
                for h in range(2):
                    a_ref[slot, h, rows, cols] = a[:, h * kb:(h + 1) * kb].astype(BF16)
                    car_ref[h, rows, :] = car[h] + totals[n][h]

        z_next = scores(nsub - 1) if z_first is None else z_first
        pending = None
        for j in reversed(range(nsub)):
            z = z_next
            if j > 0:
                z_next = scores(j - 1)
            if j == nsub - 1 and prev_slot is not None:
                apply_weights(prev_slot, st + 1)
            r = log_terms(j, z)
            if pending is not None:
                weights(j + 1, pending)
            pending = r
            if j == 0:
                z_ref[...] = scores(nsub - 1, jnp.maximum(st - 1, 0))
        weights(0, pending)

    def apply_weights(slot, st):
        v_all = v_ref[0, pl.ds(pl.multiple_of(st * kt, kt), kt), :]
        for h in range(2):
            acc_ref[h] += jnp.dot(a_ref[slot, h], v_all, preferred_element_type=F32)

    super_tile(i, True, 0, None, None)

    def live():
        return (jnp.min(car_ref[...]) < EXP2_UNDERFLOW).astype(jnp.int32)

    def body(carry):
        t, _ = carry
        super_tile(i - 1 - t, False, (t + 1) % 2, t % 2, z_ref[...])
        return t + 1, live()

    t_end, _ = lax.while_loop(lambda c: (c[0] < i) & (c[1] > 0), body, (jnp.int32(0), live()))
    apply_weights(t_end % 2, i - t_end)
    o_ref[0] = jnp.where(lane < HEAD_DIM, acc_ref[0], acc_ref[1]).astype(o_ref.dtype)


def _sba_attention(qkv, b, s, tq=512, kb=LANES, nsub=4, rc=64):
    assert tq == kb * nsub and s % tq == 0 and tq % rc == 0
    n_pairs = D_MODEL // LANES
    assert kb & (kb - 1) == 0
    jj = np.arange(kb)
    uu = jnp.asarray(np.kron(np.eye(2), (jj[:, None] > jj[None, :]).astype(np.float32)), BF16)
    return pl.pallas_call(
        functools.partial(_sba_attn_kernel, tq=tq, kb=kb, nsub=nsub, rc=rc),
        grid=(b, n_pairs, s // tq),
        in_specs=[pl.BlockSpec((1, tq, LANES), lambda bi, p, i: (bi, i, p)),
                  pl.BlockSpec((1, s, LANES), lambda bi, p, i: (bi, 0, n_pairs + p)),
                  pl.BlockSpec((1, s, LANES), lambda bi, p, i: (bi, 0, 2 * n_pairs + p)),
                  _const_spec((2 * kb, 2 * kb))],
        out_specs=pl.BlockSpec((1, tq, LANES), lambda bi, p, i: (bi, i, p)),
        out_shape=jax.ShapeDtypeStruct((b, s, D_MODEL), BF16),
        scratch_shapes=[pltpu.VMEM((2, tq, LANES), F32),
                        pltpu.VMEM((2, tq, LANES), F32),
                        pltpu.VMEM((nsub, tq, 2 * kb), BF16),
                        pltpu.VMEM((nsub, tq, 2 * kb), F32),
                        pltpu.VMEM((2, 2, tq, nsub * kb), BF16),
                        pltpu.VMEM((tq, 2 * kb), F32)],
        compiler_params=_params(("arbitrary", "arbitrary", "arbitrary")),
        name="sba_attention",
    )(qkv, qkv, qkv, uu)


def _ffn_kernel(x_ref, o_ref, wo_ref, g_ref, wup_ref, cw_ref, cb_ref, wdn_ref, gf_ref, y_ref,
                carry_ref, sg_ref, sv_ref, act_ref, *, tm, fc, tiles_per_seq, final_norm):
    @pl.when(pl.program_id(0) % tiles_per_seq == 0)
    def _():
        carry_ref[...] = jnp.zeros_like(carry_ref)

    x = x_ref[...] + jnp.dot(o_ref[...], wo_ref[...], preferred_element_type=F32)
    hn = _rmsnorm(x, g_ref[...]).astype(wup_ref.dtype)

    def up(col0):
        return jnp.dot(hn, wup_ref[:, col0:col0 + fc], preferred_element_type=F32)

    def conv(u, col0, s_ref):
        cols = slice(col0, col0 + fc)
        s_ref[0:8, :] = carry_ref[:, cols]
        s_ref[8:tm + 8, :] = u
        carry_ref[:, cols] = u[tm - 8:tm, :]
        cw = cw_ref[:, cols]
        c = cb_ref[:, cols] + s_ref[6:tm + 6, :] * cw[0:1]
        c = c + s_ref[7:tm + 7, :] * cw[1:2]
        return c + u * cw[2:3]

    n_chunks = D_FF // fc
    u_next = (up(0), up(D_FF))
    for c in range(n_chunks):
        u_gate, u_val = u_next
        if c + 1 < n_chunks:
            u_next = (up((c + 1) * fc), up(D_FF + (c + 1) * fc))
        gate = conv(u_gate, c * fc, sg_ref)
        val = conv(u_val, D_FF + c * fc, sv_ref)
        act_ref[:, c * fc:(c + 1) * fc] = (gate * jax.nn.sigmoid(gate) * val).astype(act_ref.dtype)
    y = x + jnp.dot(act_ref[...], wdn_ref[...], preferred_element_type=F32)
    if final_norm:
        y = _rmsnorm(y, gf_ref[...])
    y_ref[...] = y


def _conv_ffn(x2, o2, w_out, g, w_up, conv_w, conv_b, w_down, g_final, s, final_norm,
              tm=256, fc=256):
    t, d = x2.shape
    f2 = w_up.shape[1]
    return pl.pallas_call(
        functools.partial(_ffn_kernel, tm=tm, fc=fc, tiles_per_seq=s // tm, final_norm=final_norm),
        grid=(t // tm,),
        in_specs=[pl.BlockSpec((tm, d), lambda i: (i, 0)),
                  pl.BlockSpec((tm, d), lambda i: (i, 0)),
                  _const_spec((d, d)),
                  _const_spec((1, d)),
                  _const_spec((d, f2)),
                  _const_spec((3, f2)),
                  _const_spec((1, f2)),
                  _const_spec((D_FF, d)),
                  _const_spec((1, d))],
        out_specs=pl.BlockSpec((tm, d), lambda i: (i, 0)),
        out_shape=jax.ShapeDtypeStruct((t, d), F32),
        scratch_shapes=[pltpu.VMEM((8, f2), F32),
                        pltpu.VMEM((tm + 8, fc), F32),
                        pltpu.VMEM((tm + 8, fc), F32),
                        pltpu.VMEM((tm, D_FF), BF16)],
        compiler_params=_params(("arbitrary",)),
        name="conv_ffn",
    )(x2, o2, w_out, g, w_up, conv_w, conv_b, w_down, g_final)


NSA_Q_COLS = D_MODEL
NSA_W_COLS = D_MODEL + 7 * LANES
NSA_GATES_PER_GROUP = 3 * NSA_HPG


def _swap_halves(y):
    lane = lax.broadcasted_iota(jnp.int32, y.shape, 1)
    first = (lane % HEAD_DIM) < HALF
    return jnp.where(first, pltpu.roll(y, LANES - HALF, 1), pltpu.roll(y, HALF, 1))


def _nsa_in_proj_kernel(x_ref, g_ref, w_ref, pos_ref, inv_ref, sgn_ref,
                        q_ref, kc_ref, vc_ref, ksa_ref, vsa_ref, kwa_ref, vwa_ref, gt_ref,
                        *, tm, seq):
    hn = _rmsnorm(x_ref[...], g_ref[...]).astype(w_ref.dtype)
    ang = pos_ref[...].astype(F32) * inv_ref[...]
    cos = jnp.cos(ang)
    sin = jnp.sin(ang) * sgn_ref[...]

    def rope(y):
        return y * cos + _swap_halves(y) * sin

    q_chunk = 4 * LANES
    for c in range(NSA_Q_COLS // q_chunk):
        y = jnp.dot(hn, w_ref[:, c * q_chunk:(c + 1) * q_chunk], preferred_element_type=F32)
        for l in range(q_chunk // LANES):
            yl = rope(y[:, l * LANES:(l + 1) * LANES]) * Q_SCALE
            q_ref[:, c * q_chunk + l * LANES:c * q_chunk + (l + 1) * LANES] = yl.astype(q_ref.dtype)
    y = jnp.dot(hn, w_ref[:, NSA_Q_COLS:], preferred_element_type=F32)
    part = lambda n: y[:, n * LANES:(n + 1) * LANES]

    tok = ((pl.program_id(0) % (seq // tm)) * tm
           + lax.broadcasted_iota(jnp.int32, (tm, HEAD_DIM), 0))
    lane = lax.broadcasted_iota(jnp.int32, (tm, HEAD_DIM), 1)
    blk_onehot = jnp.where((tok >> SLC_SHIFT) == lane, 1.0, 0.0)
    ones_col = jnp.where(lane == 0, 1.0, 0.0)
    zeros = jnp.zeros((tm, HEAD_DIM), F32)
    ks, vs, kw, vw = rope(part(2)), part(3), rope(part(4)), part(5)
    for g in range(NSA_KV_GROUPS):
        cols = slice(g * HEAD_DIM, (g + 1) * HEAD_DIM)
        kc_ref[g] = part(0)[:, cols]
        vc_ref[g] = part(1)[:, cols]
        ksa_ref[g] = jnp.concatenate([ks[:, cols], blk_onehot], axis=1).astype(ksa_ref.dtype)
        vsa_ref[g] = jnp.concatenate([vs[:, cols], ones_col], axis=1).astype(vsa_ref.dtype)
        kwa_ref[g] = jnp.concatenate([kw[:, cols], zeros], axis=1).astype(kwa_ref.dtype)
        vwa_ref[g] = jnp.concatenate([vw[:, cols], zeros, jnp.ones((tm, LANES), F32)],
                                     axis=1).astype(vwa_ref.dtype)
    gates = jax.nn.sigmoid(part(6))
    gt_ref[0] = gates
    gt_ref[1] = pltpu.roll(gates, LANES - NSA_GATES_PER_GROUP, 1)


def _nsa_in_proj(x2, g, w, pos2, inv2, sgn2, seq, tm=512):
    t, d = x2.shape
    G = NSA_KV_GROUPS
    tok_g = lambda dt: jax.ShapeDtypeStruct((G, t, HEAD_DIM), dt)
    aug_g = lambda dt: jax.ShapeDtypeStruct((G, t, LANES), dt)
    tok_spec = pl.BlockSpec((G, tm, HEAD_DIM), lambda i: (0, i, 0))
    aug_spec = pl.BlockSpec((G, tm, LANES), lambda i: (0, i, 0))
    return pl.pallas_call(
        functools.partial(_nsa_in_proj_kernel, tm=tm, seq=seq),
        grid=(t // tm,),
        in_specs=[pl.BlockSpec((tm, d), lambda i: (i, 0)),
                  _const_spec((1, d)),
                  _const_spec((d, NSA_W_COLS)),
                  pl.BlockSpec((tm, 1), lambda i: (i, 0)),
                  _const_spec((1, LANES)),
                  _const_spec((1, LANES))],
        out_specs=[pl.BlockSpec((tm, NSA_Q_COLS), lambda i: (i, 0)),
                   tok_spec, tok_spec, aug_spec, aug_spec, aug_spec,
                   pl.BlockSpec((G, tm, 2 * LANES), lambda i: (0, i, 0)), aug_spec],
        out_shape=[jax.ShapeDtypeStruct((t, NSA_Q_COLS), BF16),
                   tok_g(F32), tok_g(F32), aug_g(BF16), aug_g(BF16), aug_g(BF16),
                   jax.ShapeDtypeStruct((G, t, 2 * LANES), BF16), aug_g(F32)],
        compiler_params=_params(("arbitrary",)),
        name="nsa_in_proj",
    )(x2, g, w, pos2, inv2, sgn2)


def _nsa_compress_kernel(k16_ref, v16_ref, pek_ref, pev_ref, w1k_ref, w2k_ref, w2kr_ref,
                         w1v_ref, w2v_ref, pos_ref, inv_ref, kct_ref, vc_ref):
    half_w = CMP_STRIDE * HEAD_DIM
    nrow = k16_ref.shape[2]

    def hidden(x16_ref, pe_ref, w1_ref):
        x = x16_ref[0, 0]
        y1 = jnp.dot(x, w1_ref[:half_w, :], precision=HIGHEST, preferred_element_type=F32)
        y2 = jnp.dot(x, w1_ref[half_w:, :], precision=HIGHEST, preferred_element_type=F32)
        bias = jnp.dot(pe_ref[...], w1_ref[...], precision=HIGHEST, preferred_element_type=F32)
        return jax.nn.gelu(y1 + pltpu.roll(y2, nrow - 1, 0) + bias[0:1])

    hk = hidden(k16_ref, pek_ref, w1k_ref)
    kc = jnp.dot(hk, w2k_ref[...], precision=HIGHEST, preferred_element_type=F32)
    kc_rot = jnp.dot(hk, w2kr_ref[...], precision=HIGHEST, preferred_element_type=F32)
    ang = pos_ref[0].astype(F32) * inv_ref[...]
    kc = kc * jnp.cos(ang) + kc_rot * jnp.sin(ang)
    hi, lo = _split_bf16(kc)
    kct_ref[0, 0] = jnp.concatenate([hi.astype(F32), lo.astype(F32)], axis=1).T.astype(kct_ref.dtype)
    hv = hidden(v16_ref, pev_ref, w1v_ref)
    vc = jnp.dot(hv, w2v_ref[...], precision=HIGHEST, preferred_element_type=F32)
    vc_ref[0, 0] = jnp.concatenate([vc, jnp.zeros_like(vc)], axis=1).astype(vc_ref.dtype)


def _nsa_compress(k16, v16, pek, pev, w1k, w2k, w2kr, w1v, w2v, pos_cmp, inv64):
    G, b, nrow, wide = k16.shape
    x_spec = pl.BlockSpec((1, 1, nrow, wide), lambda bi, g: (g, bi, 0, 0))
    return pl.pallas_call(
        _nsa_compress_kernel,
        grid=(b, G),
        in_specs=[x_spec, x_spec,
                  _const_spec(pek.shape), _const_spec(pev.shape),
                  _const_spec(w1k.shape), _const_spec(w2k.shape), _const_spec(w2kr.shape),
                  _const_spec(w1v.shape), _const_spec(w2v.shape),
                  pl.BlockSpec((1, nrow, 1), lambda bi, g: (bi, 0, 0)),
                  _const_spec((1, HEAD_DIM))],
        out_specs=[pl.BlockSpec((1, 1, LANES, nrow), lambda bi, g: (bi, g, 0, 0)),
                   pl.BlockSpec((1, 1, nrow, LANES), lambda bi, g: (bi, g, 0, 0))],
        out_shape=[jax.ShapeDtypeStruct((b, G, LANES, nrow), BF16),
                   jax.ShapeDtypeStruct((b, G, nrow, LANES), BF16)],
        compiler_params=_params(("arbitrary", "arbitrary")),
        name="nsa_compress",
    )(k16, v16, pek, pev, w1k, w2k, w2kr, w1v, w2v, pos_cmp, inv64)


def _nsa_attn_kernel(q_ref, kct_ref, vc_ref, ov_ref, ks_ref, vs_ref, kw_ref, vw_ref, gt_ref,
                     gsel_ref, o_ref, qa_ref, oc_ref, psum_ref, m_ref, acc_ref, alpha_ref, p_ref, pw_ref,
                     ow_ref, gs_ref,
                     *, tq, kb, n_slc, wg, rc):
    i = pl.program_id(2)
    q0 = pl.multiple_of(i * tq, tq)
    n_rows = NSA_HPG * tq
    heads = range(NSA_HPG)
    hrows = lambda h: slice(h * tq, (h + 1) * tq)
    qrow = q0 + lax.broadcasted_iota(jnp.int32, (tq, 1), 0)
    q_heads = [q_ref[0, :, h * HEAD_DIM:(h + 1) * HEAD_DIM] for h in heads]
    for h in heads:
        qa_ref[hrows(h), :] = jnp.concatenate([q_heads[h], q_heads[h]], axis=1)

    n_cmp_rows = kct_ref.shape[3]
    s_all = jnp.dot(qa_ref[...], kct_ref[0, 0], preferred_element_type=F32)
    w_keys = WINDOW + tq
    w_start = pl.multiple_of(jnp.maximum(q0 - WINDOW, 0), tq)
    k_w = kw_ref[0, 0, pl.ds(w_start, w_keys), :]
    s_win = [lax.dot_general(qa_ref[h0 * tq:(h0 + wg) * tq, :], k_w, NT_DIMS,
                             preferred_element_type=F32) for h0 in range(0, NSA_HPG, wg)]

    cmp_end = CMP_STRIDE * lax.broadcasted_iota(jnp.int32, (1, n_cmp_rows), 1) + (CMP_LEN - 1)
    c_bias = jnp.where(cmp_end <= qrow, 0.0, NEG)
    row_valid = jnp.where(qrow >= CMP_LEN - 1, 1.0, 0.0)
    for r0 in range(0, n_rows, rc):
        rows = slice(r0, r0 + rc)
        local = slice(r0 % tq, r0 % tq + rc)
        s = s_all[rows] + c_bias[local]
        e = jnp.exp2(s - jnp.max(s, axis=1, keepdims=True))
        p = e * (row_valid[local] / jnp.sum(e, axis=1, keepdims=True))
        if r0 < tq:
            psum_ref[local, :] = p
        else:
            psum_ref[local, :] += p
        p_ref[0, rows, 0:n_cmp_rows] = p.astype(p_ref.dtype)
    g_wide = jnp.dot(jnp.concatenate(_split_bf16(gt_ref[0]), axis=1), gsel_ref[...],
                     preferred_element_type=F32)
    gate = lambda h, branch: g_wide[:, (3 * h + branch) * LANES:(3 * h + branch + 1) * LANES]
    o_cmp = jnp.dot(p_ref[0, :, 0:n_cmp_rows], vc_ref[0, 0], preferred_element_type=F32)
    for h in heads:
        oc_ref[hrows(h), :] = gate(h, 0) * o_cmp[hrows(h)]
        gs_ref[hrows(h), :] = gate(h, 1)
    imp = jnp.dot(jnp.concatenate(_split_bf16(psum_ref[...]), axis=1), ov_ref[...],
                  preferred_element_type=F32)
    imp_t = imp.T[:n_slc]

    w_diff = qrow - (w_start + lax.broadcasted_iota(jnp.int32, (1, w_keys), 1))
    w_bias = jnp.where((w_diff >= 0) & (w_diff < WINDOW), 0.0, NEG)
    for r0 in range(0, n_rows, rc):
        s = s_win[r0 // (wg * tq)][r0 % (wg * tq):r0 % (wg * tq) + rc] + w_bias[r0 % tq:r0 % tq + rc]
        pw_ref[r0:r0 + rc, :] = jnp.exp2(s - jnp.max(s, axis=1, keepdims=True)).astype(pw_ref.dtype)
    o_win = jnp.dot(pw_ref[...], vw_ref[0, 0, pl.ds(w_start, w_keys), :],
                    preferred_element_type=F32)
    for h in heads:
        o_h = o_win[hrows(h)]
        ow_ref[hrows(h), :] = (gate(h, 2) * o_h[:, :LANES]) * (1.0 / o_h[:, LANES:])

    qpos = q0 + lax.broadcasted_iota(jnp.int32, (1, tq), 1)
    blk = lax.broadcasted_iota(jnp.int32, (n_slc, 1), 0)
    cur = qpos >> SLC_SHIFT
    forced = (blk == 0) | (blk == cur) | (blk == cur - 1)
    causal_blk = blk * SLC_LEN <= qpos
    score = jnp.where(causal_blk, imp_t + FORCE_BONUS * forced.astype(F32), NEG)
    sub = lax.broadcasted_iota(jnp.int32, (8, 1), 0)
    groups = [score[8 * v:8 * v + 8] for v in range(n_slc // 8)]
    counts = [jnp.zeros((8, tq), F32) for _ in groups]
    for m in range(n_slc):
        row = score[m:m + 1]
        for v, sv in enumerate(groups):
            if v < m // 8:
                ahead = row > sv
            elif v > m // 8:
                ahead = row >= sv
            else:
                ahead = (row > sv) | ((row == sv) & (sub > m % 8))
            counts[v] = counts[v] + jnp.where(ahead, 1.0, 0.0)
    rank = jnp.concatenate(counts, axis=0)
    sel_bias_t = jnp.where(rank < SLC_TOP_N, 0.0, NEG)
    sel_bias = jnp.concatenate([sel_bias_t, jnp.zeros((LANES - n_slc, tq), F32)], axis=0).T
    sel_bias = sel_bias[:, :HEAD_DIM].astype(qa_ref.dtype)
    for h in heads:
        qa_ref[hrows(h), :] = jnp.concatenate([q_heads[h], sel_bias], axis=1)

    tok_bias = jnp.where(lax.broadcasted_iota(jnp.int32, (1, kb), 1)
                         <= lax.broadcasted_iota(jnp.int32, (tq, 1), 0), 0.0, NEG)

    def scores(start):
        k_t = ks_ref[0, 0, pl.ds(start, kb), :]
        return lax.dot_general(qa_ref[...], k_t, NT_DIMS, preferred_element_type=F32)

    def softmax(s_all, slot, own_keys):
        for r0 in range(0, n_rows, rc):
            rows = slice(r0, r0 + rc)
            s = s_all[rows]
            if own_keys:
                s = s + tok_bias[r0 % tq:r0 % tq + rc]
                m_new = jnp.broadcast_to(jnp.max(s, axis=1, keepdims=True), (rc, LANES))
            else:
                m_old = m_ref[rows, :]
                m_new = jnp.maximum(m_old, jnp.max(s, axis=1, keepdims=True))
                alpha_ref[slot, rows, :] = jnp.exp2(m_old - m_new)
            m_ref[rows, :] = m_new
            m_wide = jnp.concatenate([m_new] * (kb // LANES), axis=1)
            p_ref[slot, rows, :] = jnp.exp2(s - m_wide).astype(p_ref.dtype)

    def values(start, slot):
        v_t = vs_ref[0, 0, pl.ds(start, kb), :]
        pv = jnp.dot(p_ref[slot], v_t, preferred_element_type=F32)
        for r0 in range(0, n_rows, rc):
            rows = slice(r0, r0 + rc)
            acc_ref[rows, :] = alpha_ref[slot, rows, :] * acc_ref[rows, :] + pv[rows]

    chunk_start = lambda c: pl.multiple_of(c * kb, kb)
    owed_start = lambda c: pl.multiple_of(jnp.where(c == 0, q0, (c - 1) * kb), kb)

    acc_ref[...] = jnp.zeros_like(acc_ref)
    alpha_ref[1] = jnp.zeros(alpha_ref.shape[1:], F32)
    softmax(scores(q0), 1, True)

    def pair_body(t, carry):
        c = 2 * t
        s_a = scores(chunk_start(c))
        values(owed_start(c), 1)
        softmax(s_a, 0, False)
        s_b = scores(chunk_start(c + 1))
        values(chunk_start(c), 0)
        softmax(s_b, 1, False)
        return carry

    lax.fori_loop(0, i // 2, pair_body, 0)

    @pl.when(i % 2 == 1)
    def _():
        s_a = scores(chunk_start(i - 1))
        values(owed_start(i - 1), 1)
        softmax(s_a, 0, False)
        values(chunk_start(i - 1), 0)

    @pl.when(i % 2 == 0)
    def _():
        values(owed_start(i), 1)

    outs = []
    for h in heads:
        o_h = acc_ref[hrows(h), :]
        o_sel = (gs_ref[hrows(h), :] * o_h) * (1.0 / o_h[:, HEAD_DIM:HEAD_DIM + 1])
        outs.append(((oc_ref[hrows(h), :] + o_sel) + ow_ref[hrows(h), :])[:, :HEAD_DIM])
    for hp in range(NSA_HPG // 2):
        pair = jnp.concatenate(outs[2 * hp:2 * hp + 2], axis=1)
        o_ref[0, :, hp * LANES:(hp + 1) * LANES] = pair.astype(o_ref.dtype)


def _nsa_attention(q3, kct, vc, ov, ksa, vsa, kwa, vwa, gt, tq=256, wg=4, rc=64):
    b, s, _ = q3.shape
    G = NSA_KV_GROUPS
    kb = tq
    n_slc = s // SLC_LEN
    nq = s // tq
    n_cmp_rows = kct.shape[3]
    gw = NSA_HPG * HEAD_DIM
    assert n_slc <= HEAD_DIM and tq % SLC_LEN == 0 and WINDOW % tq == 0 and s >= WINDOW + tq
    assert n_cmp_rows <= kb and n_slc % 8 == 0 and tq % rc == 0
    kv_spec = pl.BlockSpec((1, 1, s, LANES), lambda bi, g, i: (g, bi, 0, 0))
    n_gates = NSA_GATES_PER_GROUP
    gsel = np.zeros((2 * LANES, n_gates * LANES), np.float32)
    for c in range(n_gates):
        gsel[[c, LANES + c], c * LANES:(c + 1) * LANES] = 1.0
    gsel = jnp.asarray(gsel, BF16)
    return pl.pallas_call(
        functools.partial(_nsa_attn_kernel, tq=tq, kb=kb, n_slc=n_slc, wg=wg, rc=rc),
        grid=(b, G, nq),
        in_specs=[pl.BlockSpec((1, tq, gw), lambda bi, g, i: (bi, i, g)),
                  pl.BlockSpec((1, 1, LANES, n_cmp_rows), lambda bi, g, i: (bi, g, 0, 0)),
                  pl.BlockSpec((1, 1, n_cmp_rows, LANES), lambda bi, g, i: (bi, g, 0, 0)),
                  _const_spec(ov.shape),
                  kv_spec, kv_spec, kv_spec,
                  pl.BlockSpec((1, 1, s, 2 * LANES), lambda bi, g, i: (g, bi, 0, 0)),
                  pl.BlockSpec((1, tq, LANES), lambda bi, g, i: (g, bi * nq + i, 0)),
                  _const_spec(gsel.shape)],
        out_specs=pl.BlockSpec((1, tq, gw), lambda bi, g, i: (bi, i, g)),
        out_shape=jax.ShapeDtypeStruct((b, s, D_MODEL), BF16),
        scratch_shapes=[pltpu.VMEM((NSA_HPG * tq, LANES), BF16),
                        pltpu.VMEM((NSA_HPG * tq, LANES), F32),
                        pltpu.VMEM((tq, n_cmp_rows), F32),
                        pltpu.VMEM((NSA_HPG * tq, LANES), F32),
                        pltpu.VMEM((NSA_HPG * tq, LANES), F32),
                        pltpu.VMEM((2, NSA_HPG * tq, LANES), F32),
                        pltpu.VMEM((2, NSA_HPG * tq, kb), BF16),
                        pltpu.VMEM((NSA_HPG * tq, WINDOW + tq), BF16),
                        pltpu.VMEM((NSA_HPG * tq, LANES), F32),
                        pltpu.VMEM((NSA_HPG * tq, LANES), F32)],
        compiler_params=_params(("arbitrary", "arbitrary", "arbitrary")),
        name="nsa_attention",
    )(q3, kct, vc, ov, ksa, vsa, kwa, vwa, gt, gsel)


def _nsa_layer_attention(hx2, norm_g, positions, w_in, pe_k, pe_v, w1k, w2k, w1v, w2v, b, s):
    t = b * s
    G = NSA_KV_GROUPS
    w_pad = jnp.pad(w_in, ((0, 0), (0, NSA_W_COLS - w_in.shape[1]))).astype(BF16)
    inv = ROPE_THETA ** (-jnp.arange(HALF, dtype=F32) / HALF)
    inv2 = jnp.tile(inv, LANES // HALF)[None, :]
    sgn2 = jnp.tile(jnp.concatenate([-jnp.ones(HALF, F32), jnp.ones(HALF, F32)]), LANES // HEAD_DIM)[None, :]
    q, kc_tok, vc_tok, ksa, vsa, kwa, vwa, gt = _nsa_in_proj(
        hx2, norm_g, w_pad, positions.reshape(t, 1), inv2, sgn2, s)

    nrow = s // CMP_STRIDE
    wide = CMP_STRIDE * HEAD_DIM
    k16 = kc_tok.reshape(G, b, nrow, wide)
    v16 = vc_tok.reshape(G, b, nrow, wide)
    pek = jnp.broadcast_to(pe_k.reshape(1, CMP_LEN * HEAD_DIM), (8, CMP_LEN * HEAD_DIM))
    pev = jnp.broadcast_to(pe_v.reshape(1, CMP_LEN * HEAD_DIM), (8, CMP_LEN * HEAD_DIM))
    w2k_rot = jnp.concatenate([-w2k[:, HALF:], w2k[:, :HALF]], axis=1)
    end_idx = jnp.minimum(jnp.arange(nrow) * CMP_STRIDE + CMP_LEN - 1, s - 1)
    pos_cmp = positions[:, end_idx][:, :, None]
    inv64 = jnp.tile(inv, 2)[None, :]
    kct, vc = _nsa_compress(k16, v16, pek, pev, w1k, w2k, w2k_rot, w1v, w2v, pos_cmp, inv64)

    n_slc = s // SLC_LEN
    c0 = np.arange(nrow)[:, None] * CMP_STRIDE
    s0 = np.arange(n_slc)[None, :] * SLC_LEN
    ov = np.clip(np.minimum(c0 + CMP_LEN, s0 + SLC_LEN) - np.maximum(c0, s0), 0, None) / CMP_LEN
    ov[(s - CMP_LEN) // CMP_STRIDE + 1:, :] = 0.0
    ov = np.pad(ov, ((0, 0), (0, LANES - n_slc)))
    ov = jnp.asarray(np.concatenate([ov, ov], axis=0), BF16)

    aug4 = lambda a: a.reshape(G, b, s, a.shape[-1])
    return _nsa_attention(q.reshape(b, s, D_MODEL), kct, vc, ov, aug4(ksa), aug4(vsa),
                          aug4(kwa), aug4(vwa), gt)


def kernel(x, positions, norm_mix, sba_w_in, sba_w_out, nsa_w_in, nsa_cmp_pos_k, nsa_cmp_pos_v,
           nsa_cmp_k_w1, nsa_cmp_k_w2, nsa_cmp_v_w1, nsa_cmp_v_w2, nsa_w_out, norm_ffn,
           ffn_w_up, ffn_conv_w, ffn_conv_b, ffn_w_down, norm_final):
    b, s, d = x.shape
    t = b * s
    depth = norm_mix.shape[0]
    x2 = x.reshape(t, d)
    g_final = norm_final.reshape(1, d)
    for layer in range(depth):
        j = layer // 2
        g_mix = norm_mix[layer].reshape(1, d)
        if layer % 2 == 0:
            qkv = _sba_in_proj(x2, g_mix, sba_w_in[j].astype(BF16))
            o = _sba_attention(qkv.reshape(b, s, 3 * d), b, s)
            w_out = sba_w_out[j]
        else:
            o = _nsa_layer_attention(x2, g_mix, positions, nsa_w_in[j], nsa_cmp_pos_k[j],
                                     nsa_cmp_pos_v[j], nsa_cmp_k_w1[j], nsa_cmp_k_w2[j],
                                     nsa_cmp_v_w1[j], nsa_cmp_v_w2[j], b, s)
            w_out = nsa_w_out[j]
        x2 = _conv_ffn(x2, o.reshape(t, d), w_out.astype(BF16),
                       norm_ffn[layer].reshape(1, d), ffn_w_up[layer].astype(BF16),
                       ffn_conv_w[layer], ffn_conv_b[layer].reshape(1, -1),
                       ffn_w_down[layer].astype(BF16), g_final, s,
                       final_norm=(layer == depth - 1))
    return x2.reshape(b, s, d)
```

```python
import functools

import numpy as np
import jax
import jax.numpy as jnp
from jax import lax
from jax.experimental import pallas as pl
from jax.experimental.pallas import tpu as pltpu

D_MODEL = 1024
N_HEADS = 16
HEAD_DIM = 64
HALF = HEAD_DIM // 2
NSA_KV_GROUPS = 2
NSA_HPG = N_HEADS // NSA_KV_GROUPS
CMP_LEN = 32
CMP_STRIDE = 16
CMP_HIDDEN = 2 * HEAD_DIM
SLC_LEN = 64
SLC_SHIFT = SLC_LEN.bit_length() - 1
SLC_TOP_N = 16
WINDOW = 512
ROPE_THETA = 10000.0
D_FF = 2816
RMS_EPS = 1e-6
NEG = -1e30
FORCE_BONUS = 1e4
LOG2E = float(np.log2(np.e))
EXP2_CLAMP = 120.0
EXP2_UNDERFLOW = 160.0
Q_SCALE = LOG2E * HEAD_DIM ** -0.5

LANES = 128
VMEM_LIMIT = 56 * 1024 * 1024

F32 = jnp.float32
BF16 = jnp.bfloat16
HIGHEST = lax.Precision.HIGHEST
NT_DIMS = (((1,), (1,)), ((), ()))


def _params(semantics):
    return pltpu.CompilerParams(dimension_semantics=semantics, vmem_limit_bytes=VMEM_LIMIT)


def _rmsnorm(x, g):
    return x * lax.rsqrt(jnp.mean(x * x, axis=-1, keepdims=True) + RMS_EPS) * g


def _const_spec(shape):
    return pl.BlockSpec(shape, lambda *_: (0,) * len(shape), pipeline_mode=pl.Buffered(1))


def _split_bf16(x):
    hi = x.astype(BF16)
    return hi, (x - hi.astype(F32)).astype(BF16)


def _sba_in_proj_kernel(x_ref, g_ref, w_ref, o_ref, *, n_chunk):
    hn = _rmsnorm(x_ref[...], g_ref[...]).astype(w_ref.dtype)
    n = w_ref.shape[1]
    for c in range(n // n_chunk):
        cols = slice(c * n_chunk, (c + 1) * n_chunk)
        y = jnp.dot(hn, w_ref[:, cols], preferred_element_type=F32)
        if c * n_chunk < D_MODEL:
            y = y * Q_SCALE
        o_ref[:, cols] = y.astype(o_ref.dtype)


def _sba_in_proj(x2, g, w, tm=512):
    t, d = x2.shape
    n = w.shape[1]
    return pl.pallas_call(
        functools.partial(_sba_in_proj_kernel, n_chunk=512),
        grid=(t // tm,),
        in_specs=[pl.BlockSpec((tm, d), lambda i: (i, 0)),
                  _const_spec((1, d)),
                  _const_spec((d, n))],
        out_specs=pl.BlockSpec((tm, n), lambda i: (i, 0)),
        out_shape=jax.ShapeDtypeStruct((t, n), BF16),
        compiler_params=_params(("arbitrary",)),
        name="sba_in_proj",
    )(x2, g, w)


def _sba_attn_kernel(q_ref, k_ref, v_ref, uu_ref, o_ref, acc_ref, car_ref, hl_ref, lb_ref, a_ref,
                     z_ref,
                     *, tq, kb, nsub, rc):
    i = pl.program_id(2)
    kt = kb * nsub
    streams = range(q_ref.shape[2] // LANES)
    scols = lambda s: slice(s * LANES, (s + 1) * LANES)
    q = q_ref[0]
    lane = lax.broadcasted_iota(jnp.int32, (tq, LANES), 1)
    klane = lax.broadcasted_iota(jnp.int32, (kb, LANES), 1)
    acc_ref[...] = jnp.zeros_like(acc_ref)
    car_ref[...] = jnp.zeros_like(car_ref)
    ud = uu_ref[...]
    rel = (lax.broadcasted_iota(jnp.int32, (rc, 2 * kb), 0)
           - (lax.broadcasted_iota(jnp.int32, (rc, 2 * kb), 1) & (kb - 1)))

    def super_tile(st, diag, slot, prev_slot, z_first):

        def visibility(j, r0):
            if not diag:
                return 2, None
            if j * kb >= r0 + rc - 1:
                return 0, None
            if j * kb + kb - 1 < r0:
                return 2, None
            return 1, rel > (j * kb - r0)

        def scores(s, j, st=st):
            k_t = k_ref[0, pl.ds(pl.multiple_of(st * kt + j * kb, kb), kb), scols(s)]
            k_bd = jnp.concatenate([jnp.where(klane < HEAD_DIM, k_t, jnp.zeros_like(k_t)),
                                    jnp.where(klane >= HEAD_DIM, k_t, jnp.zeros_like(k_t))], axis=0)
            return lax.dot_general(q[:, scols(s)], k_bd, NT_DIMS, preferred_element_type=F32)

        def log_terms(s, j, z):
            totals = []
            for r0 in range(0, tq, rc):
                rows = slice(r0, r0 + rc)
                kind, strict = visibility(j, r0)
                if kind == 0:
                    hl_ref[s, j, rows, :] = jnp.zeros((rc, 2 * kb), BF16)
                    totals.append(None)
                    continue
                zc = z[rows]
                nl = jnp.maximum(jnp.log2(1.0 + jnp.exp2(jnp.minimum(zc, EXP2_CLAMP))), zc)
                lb_ref[s, j, rows, :] = zc - nl
                if kind == 1:
                    nl = jnp.where(strict, nl, 0.0)
                hl_ref[s, j, rows, :] = nl.astype(BF16)
                totals.append([jnp.sum(nl[:, h * kb:(h + 1) * kb], axis=1, keepdims=True)
                               for h in range(2)])
            return jnp.dot(hl_ref[s, j], ud, preferred_element_type=F32), totals

        def weights(s, j, sums):
            cum, totals = sums
            for n, r0 in enumerate(range(0, tq, rc)):
                rows = slice(r0, r0 + rc)
                cols = slice(j * kb, (j + 1) * kb)
                kind, strict = visibility(j, r0)
                if kind == 0:
                    for h in range(2):
                        a_ref[s, slot, h, rows, cols] = jnp.zeros((rc, kb), BF16)
                    continue
                car = [car_ref[s, h, rows, :] for h in range(2)]
                a = jnp.exp2(lb_ref[s, j, rows, :] - cum[rows] - jnp.concatenate(car, axis=1))
                if kind == 1:
                    a = jnp.where(strict, a, 0.0)
                for h in range(2):
                    a_ref[s, slot, h, rows, cols] = a[:, h * kb:(h + 1) * kb].astype(BF16)
                    car_ref[s, h, rows, :] = car[h] + totals[n][h]

        z_next = [scores(s, nsub - 1) for s in streams] if z_first is None else z_first
        pending = None
        for j in reversed(range(nsub)):
            z = z_next
            if j > 0:
                z_next = [scores(s, j - 1) for s in streams]
            if j == nsub - 1 and prev_slot is not None:
                apply_weights(prev_slot, st + 1)
            sums = [log_terms(s, j, z[s]) for s in streams]
            if pending is not None:
                for s in streams:
                    weights(s, j + 1, pending[s])
            pending = sums
            if j == 0:
                for s in streams:
                    z_ref[s] = scores(s, nsub - 1, jnp.maximum(st - 1, 0))
        for s in streams:
            weights(s, 0, pending[s])

    def apply_weights(slot, st):
        for s in streams:
            v_all = v_ref[0, pl.ds(pl.multiple_of(st * kt, kt), kt), scols(s)]
            for h in range(2):
                acc_ref[s, h] += jnp.dot(a_ref[s, slot, h], v_all, preferred_element_type=F32)

    super_tile(i, True, 0, None, None)

    def live():
        return (jnp.min(car_ref[...]) < EXP2_UNDERFLOW).astype(jnp.int32)

    def body(carry):
        t, _ = carry
        super_tile(i - 1 - t, False, (t + 1) % 2, t % 2, [z_ref[s] for s in streams])
        return t + 1, live()

    t_end, _ = lax.while_loop(lambda c: (c[0] < i) & (c[1] > 0), body, (jnp.int32(0), live()))
    apply_weights(t_end % 2, i - t_end)
    for s in streams:
        o_ref[0, :, scols(s)] = jnp.where(lane < HEAD_DIM, acc_ref[s, 0],
                                          acc_ref[s, 1]).astype(o_ref.dtype)


def _sba_attention(qkv, b, s, tq=512, kb=LANES, nsub=4, rc=64, ns=2):
    assert tq == kb * nsub and s % tq == 0 and tq % rc == 0
    n_blocks = D_MODEL // (ns * LANES)
    wide = ns * LANES
    assert kb & (kb - 1) == 0 and D_MODEL % wide == 0
    jj = np.arange(kb)
    uu = jnp.asarray(np.kron(np.eye(2), (jj[:, None] > jj[None, :]).astype(np.float32)), BF16)
    return pl.pallas_call(
        functools.partial(_sba_attn_kernel, tq=tq, kb=kb, nsub=nsub, rc=rc),
        grid=(b, n_blocks, s // tq),
        in_specs=[pl.BlockSpec((1, tq, wide), lambda bi, p, i: (bi, i, p)),
                  pl.BlockSpec((1, s, wide), lambda bi, p, i: (bi, 0, n_blocks + p)),
                  pl.BlockSpec((1, s, wide), lambda bi, p, i: (bi, 0, 2 * n_blocks + p)),
                  _const_spec((2 * kb, 2 * kb))],
        out_specs=pl.BlockSpec((1, tq, wide), lambda bi, p, i: (bi, i, p)),
        out_shape=jax.ShapeDtypeStruct((b, s, D_MODEL), BF16),
        scratch_shapes=[pltpu.VMEM((ns, 2, tq, LANES), F32),
                        pltpu.VMEM((ns, 2, tq, LANES), F32),
                        pltpu.VMEM((ns, nsub, tq, 2 * kb), BF16),
                        pltpu.VMEM((ns, nsub, tq, 2 * kb), F32),
                        pltpu.VMEM((ns, 2, 2, tq, nsub * kb), BF16),
                        pltpu.VMEM((ns, tq, 2 * kb), F32)],
        compiler_params=_params(("arbitrary", "arbitrary", "arbitrary")),
        name="sba_attention",
    )(qkv, qkv, qkv, uu)


def _ffn_kernel(x_ref, o_ref, wo_ref, g_ref, wup_ref, cw_ref, cb_ref, wdn_ref, gf_ref, y_ref,
                carry_ref, sg_ref, sv_ref, act_ref, *, tm, fc, tiles_per_seq, final_norm):
    @pl.when(pl.program_id(0) % tiles_per_seq == 0)
    def _():
        carry_ref[...] = jnp.zeros_like(carry_ref)

    x = x_ref[...] + jnp.dot(o_ref[...], wo_ref[...], preferred_element_type=F32)
    hn = _rmsnorm(x, g_ref[...]).astype(wup_ref.dtype)

    def up(col0):
        return jnp.dot(hn, wup_ref[:, col0:col0 + fc], preferred_element_type=F32)

    def conv(u, col0, s_ref):
        cols = slice(col0, col0 + fc)
        s_ref[0:8, :] = carry_ref[:, cols]
        s_ref[8:tm + 8, :] = u
        carry_ref[:, cols] = u[tm - 8:tm, :]
        cw = cw_ref[:, cols]
        c = cb_ref[:, cols] + s_ref[6:tm + 6, :] * cw[0:1]
        c = c + s_ref[7:tm + 7, :] * cw[1:2]
        return c + u * cw[2:3]

    n_chunks = D_FF // fc
    u_next = (up(0), up(D_FF))
    for c in range(n_chunks):
        u_gate, u_val = u_next
        if c + 1 < n_chunks:
            u_next = (up((c + 1) * fc), up(D_FF + (c + 1) * fc))
        gate = conv(u_gate, c * fc, sg_ref)
        val = conv(u_val, D_FF + c * fc, sv_ref)
        act_ref[:, c * fc:(c + 1) * fc] = (gate * jax.nn.sigmoid(gate) * val).astype(act_ref.dtype)
    y = x + jnp.dot(act_ref[...], wdn_ref[...], preferred_element_type=F32)
    if final_norm:
        y = _rmsnorm(y, gf_ref[...])
    y_ref[...] = y


def _conv_ffn(x2, o2, w_out, g, w_up, conv_w, conv_b, w_down, g_final, s, final_norm,
              tm=256, fc=256):
    t, d = x2.shape
    f2 = w_up.shape[1]
    return pl.pallas_call(
        functools.partial(_ffn_kernel, tm=tm, fc=fc, tiles_per_seq=s // tm, final_norm=final_norm),
        grid=(t // tm,),
        in_specs=[pl.BlockSpec((tm, d), lambda i: (i, 0)),
                  pl.BlockSpec((tm, d), lambda i: (i, 0)),
                  _const_spec((d, d)),
                  _const_spec((1, d)),
                  _const_spec((d, f2)),
                  _const_spec((3, f2)),
                  _const_spec((1, f2)),
                  _const_spec((D_FF, d)),
                  _const_spec((1, d))],
        out_specs=pl.BlockSpec((tm, d), lambda i: (i, 0)),
        out_shape=jax.ShapeDtypeStruct((t, d), F32),
        scratch_shapes=[pltpu.VMEM((8, f2), F32),
                        pltpu.VMEM((tm + 8, fc), F32),
                        pltpu.VMEM((tm + 8, fc), F32),
                        pltpu.VMEM((tm, D_FF), BF16)],
        compiler_params=_params(("arbitrary",)),
        name="conv_ffn",
    )(x2, o2, w_out, g, w_up, conv_w, conv_b, w_down, g_final)


NSA_Q_COLS = D_MODEL
NSA_W_COLS = D_MODEL + 7 * LANES
NSA_GATES_PER_GROUP = 3 * NSA_HPG


def _swap_halves(y):
    lane = lax.broadcasted_iota(jnp.int32, y.shape, 1)
    first = (lane % HEAD_DIM) < HALF
    return jnp.where(first, pltpu.roll(y, LANES - HALF, 1), pltpu.roll(y, HALF, 1))


def _nsa_in_proj_kernel(x_ref, g_ref, w_ref, pos_ref, inv_ref, sgn_ref,
                        q_ref, kc_ref, vc_ref, ksa_ref, vsa_ref, kwa_ref, vwa_ref, gt_ref,
                        *, tm, seq):
    hn = _rmsnorm(x_ref[...], g_ref[...]).astype(w_ref.dtype)
    ang = pos_ref[...].astype(F32) * inv_ref[...]
    cos = jnp.cos(ang)
    sin = jnp.sin(ang) * sgn_ref[...]

    def rope(y):
        return y * cos + _swap_halves(y) * sin

    q_chunk = 4 * LANES
    for c in range(NSA_Q_COLS // q_chunk):
        y = jnp.dot(hn, w_ref[:, c * q_chunk:(c + 1) * q_chunk], preferred_element_type=F32)
        for l in range(q_chunk // LANES):
            yl = rope(y[:, l * LANES:(l + 1) * LANES]) * Q_SCALE
            q_ref[:, c * q_chunk + l * LANES:c * q_chunk + (l + 1) * LANES] = yl.astype(q_ref.dtype)
    y = jnp.dot(hn, w_ref[:, NSA_Q_COLS:], preferred_element_type=F32)
    part = lambda n: y[:, n * LANES:(n + 1) * LANES]

    tok = ((pl.program_id(0) % (seq // tm)) * tm
           + lax.broadcasted_iota(jnp.int32, (tm, HEAD_DIM), 0))
    lane = lax.broadcasted_iota(jnp.int32, (tm, HEAD_DIM), 1)
    blk_onehot = jnp.where((tok >> SLC_SHIFT) == lane, 1.0, 0.0)
    ones_col = jnp.where(lane == 0, 1.0, 0.0)
    zeros = jnp.zeros((tm, HEAD_DIM), F32)
    ks, vs, kw, vw = rope(part(2)), part(3), rope(part(4)), part(5)
    for g in range(NSA_KV_GROUPS):
        cols = slice(g * HEAD_DIM, (g + 1) * HEAD_DIM)
        kc_ref[g] = part(0)[:, cols]
        vc_ref[g] = part(1)[:, cols]
        ksa_ref[g] = jnp.concatenate([ks[:, cols], blk_onehot], axis=1).astype(ksa_ref.dtype)
        vsa_ref[g] = jnp.concatenate([vs[:, cols], ones_col], axis=1).astype(vsa_ref.dtype)
        kwa_ref[g] = jnp.concatenate([kw[:, cols], zeros], axis=1).astype(kwa_ref.dtype)
        vwa_ref[g] = jnp.concatenate([vw[:, cols], zeros, jnp.ones((tm, LANES), F32)],
                                     axis=1).astype(vwa_ref.dtype)
    gates = jax.nn.sigmoid(part(6))
    gt_ref[0] = gates
    gt_ref[1] = pltpu.roll(gates, LANES - NSA_GATES_PER_GROUP, 1)


def _nsa_in_proj(x2, g, w, pos2, inv2, sgn2, seq, tm=512):
    t, d = x2.shape
    G = NSA_KV_GROUPS
    tok_g = lambda dt: jax.ShapeDtypeStruct((G, t, HEAD_DIM), dt)
    aug_g = lambda dt: jax.ShapeDtypeStruct((G, t, LANES), dt)
    tok_spec = pl.BlockSpec((G, tm, HEAD_DIM), lambda i: (0, i, 0))
    aug_spec = pl.BlockSpec((G, tm, LANES), lambda i: (0, i, 0))
    return pl.pallas_call(
        functools.partial(_nsa_in_proj_kernel, tm=tm, seq=seq),
        grid=(t // tm,),
        in_specs=[pl.BlockSpec((tm, d), lambda i: (i, 0)),
                  _const_spec((1, d)),
                  _const_spec((d, NSA_W_COLS)),
                  pl.BlockSpec((tm, 1), lambda i: (i, 0)),
                  _const_spec((1, LANES)),
                  _const_spec((1, LANES))],
        out_specs=[pl.BlockSpec((tm, NSA_Q_COLS), lambda i: (i, 0)),
                   tok_spec, tok_spec, aug_spec, aug_spec, aug_spec,
                   pl.BlockSpec((G, tm, 2 * LANES), lambda i: (0, i, 0)), aug_spec],
        out_shape=[jax.ShapeDtypeStruct((t, NSA_Q_COLS), BF16),
                   tok_g(F32), tok_g(F32), aug_g(BF16), aug_g(BF16), aug_g(BF16),
                   jax.ShapeDtypeStruct((G, t, 2 * LANES), BF16), aug_g(F32)],
        compiler_params=_params(("arbitrary",)),
        name="nsa_in_proj",
    )(x2, g, w, pos2, inv2, sgn2)


def _nsa_compress_kernel(k16_ref, v16_ref, pek_ref, pev_ref, w1k_ref, w2k_ref, w2kr_ref,
                         w1v_ref, w2v_ref, pos_ref, inv_ref, kct_ref, vc_ref):
    half_w = CMP_STRIDE * HEAD_DIM
    nrow = k16_ref.shape[2]

    def hidden(x16_ref, pe_ref, w1_ref):
        x = x16_ref[0, 0]
        y1 = jnp.dot(x, w1_ref[:half_w, :], precision=HIGHEST, preferred_element_type=F32)
        y2 = jnp.dot(x, w1_ref[half_w:, :], precision=HIGHEST, preferred_element_type=F32)
        bias = jnp.dot(pe_ref[...], w1_ref[...], precision=HIGHEST, preferred_element_type=F32)
        return jax.nn.gelu(y1 + pltpu.roll(y2, nrow - 1, 0) + bias[0:1])

    hk = hidden(k16_ref, pek_ref, w1k_ref)
    kc = jnp.dot(hk, w2k_ref[...], precision=HIGHEST, preferred_element_type=F32)
    kc_rot = jnp.dot(hk, w2kr_ref[...], precision=HIGHEST, preferred_element_type=F32)
    ang = pos_ref[0].astype(F32) * inv_ref[...]
    kc = kc * jnp.cos(ang) + kc_rot * jnp.sin(ang)
    hi, lo = _split_bf16(kc)
    kct_ref[0, 0] = jnp.concatenate([hi.astype(F32), lo.astype(F32)], axis=1).T.astype(kct_ref.dtype)
    hv = hidden(v16_ref, pev_ref, w1v_ref)
    vc = jnp.dot(hv, w2v_ref[...], precision=HIGHEST, preferred_element_type=F32)
    vc_ref[0, 0] = jnp.concatenate([vc, jnp.zeros_like(vc)], axis=1).astype(vc_ref.dtype)


def _nsa_compress(k16, v16, pek, pev, w1k, w2k, w2kr, w1v, w2v, pos_cmp, inv64):
    G, b, nrow, wide = k16.shape
    x_spec = pl.BlockSpec((1, 1, nrow, wide), lambda bi, g: (g, bi, 0, 0))
    return pl.pallas_call(
        _nsa_compress_kernel,
        grid=(b, G),
        in_specs=[x_spec, x_spec,
                  _const_spec(pek.shape), _const_spec(pev.shape),
                  _const_spec(w1k.shape), _const_spec(w2k.shape), _const_spec(w2kr.shape),
                  _const_spec(w1v.shape), _const_spec(w2v.shape),
                  pl.BlockSpec((1, nrow, 1), lambda bi, g: (bi, 0, 0)),
                  _const_spec((1, HEAD_DIM))],
        out_specs=[pl.BlockSpec((1, 1, LANES, nrow), lambda bi, g: (bi, g, 0, 0)),
                   pl.BlockSpec((1, 1, nrow, LANES), lambda bi, g: (bi, g, 0, 0))],
        out_shape=[jax.ShapeDtypeStruct((b, G, LANES, nrow), BF16),
                   jax.ShapeDtypeStruct((b, G, nrow, LANES), BF16)],
        compiler_params=_params(("arbitrary", "arbitrary")),
        name="nsa_compress",
    )(k16, v16, pek, pev, w1k, w2k, w2kr, w1v, w2v, pos_cmp, inv64)


def _nsa_attn_kernel(q_ref, kct_ref, vc_ref, ov_ref, ks_ref, vs_ref, kw_ref, vw_ref, gt_ref,
                     gsel_ref, o_ref, qa_ref, oc_ref, psum_ref, m_ref, acc_ref, alpha_ref, p_ref, pw_ref,
                     ow_ref, gs_ref,
                     *, tq, kb, n_slc, wg, rc):
    i = pl.program_id(2)
    q0 = pl.multiple_of(i * tq, tq)
    n_rows = NSA_HPG * tq
    heads = range(NSA_HPG)
    hrows = lambda h: slice(h * tq, (h + 1) * tq)
    qrow = q0 + lax.broadcasted_iota(jnp.int32, (tq, 1), 0)
    q_heads = [q_ref[0, :, h * HEAD_DIM:(h + 1) * HEAD_DIM] for h in heads]
    for h in heads:
        qa_ref[hrows(h), :] = jnp.concatenate([q_heads[h], q_heads[h]], axis=1)

    n_cmp_rows = kct_ref.shape[3]
    s_all = jnp.dot(qa_ref[...], kct_ref[0, 0], preferred_element_type=F32)
    w_keys = WINDOW + tq
    w_start = pl.multiple_of(jnp.maximum(q0 - WINDOW, 0), tq)
    k_w = kw_ref[0, 0, pl.ds(w_start, w_keys), :]
    s_win = [lax.dot_general(qa_ref[h0 * tq:(h0 + wg) * tq, :], k_w, NT_DIMS,
                             preferred_element_type=F32) for h0 in range(0, NSA_HPG, wg)]

    cmp_end = CMP_STRIDE * lax.broadcasted_iota(jnp.int32, (1, n_cmp_rows), 1) + (CMP_LEN - 1)
    c_bias = jnp.where(cmp_end <= qrow, 0.0, NEG)
    row_valid = jnp.where(qrow >= CMP_LEN - 1, 1.0, 0.0)
    for r0 in range(0, n_rows, rc):
        rows = slice(r0, r0 + rc)
        local = slice(r0 % tq, r0 % tq + rc)
        s = s_all[rows] + c_bias[local]
        e = jnp.exp2(s - jnp.max(s, axis=1, keepdims=True))
        p = e * (row_valid[local] / jnp.sum(e, axis=1, keepdims=True))
        if r0 < tq:
            psum_ref[local, :] = p
        else:
            psum_ref[local, :] += p
        p_ref[0, rows, 0:n_cmp_rows] = p.astype(p_ref.dtype)
    g_wide = jnp.dot(jnp.concatenate(_split_bf16(gt_ref[0]), axis=1), gsel_ref[...],
                     preferred_element_type=F32)
    gate = lambda h, branch: g_wide[:, (3 * h + branch) * LANES:(3 * h + branch + 1) * LANES]
    o_cmp = jnp.dot(p_ref[0, :, 0:n_cmp_rows], vc_ref[0, 0], preferred_element_type=F32)
    for h in heads:
        oc_ref[hrows(h), :] = gate(h, 0) * o_cmp[hrows(h)]
        gs_ref[hrows(h), :] = gate(h, 1)
    imp = jnp.dot(jnp.concatenate(_split_bf16(psum_ref[...]), axis=1), ov_ref[...],
                  preferred_element_type=F32)
    imp_t = imp.T[:n_slc]

    w_diff = qrow - (w_start + lax.broadcasted_iota(jnp.int32, (1, w_keys), 1))
    w_bias = jnp.where((w_diff >= 0) & (w_diff < WINDOW), 0.0, NEG)
    for r0 in range(0, n_rows, rc):
        s = s_win[r0 // (wg * tq)][r0 % (wg * tq):r0 % (wg * tq) + rc] + w_bias[r0 % tq:r0 % tq + rc]
        pw_ref[r0:r0 + rc, :] = jnp.exp2(s - jnp.max(s, axis=1, keepdims=True)).astype(pw_ref.dtype)
    o_win = jnp.dot(pw_ref[...], vw_ref[0, 0, pl.ds(w_start, w_keys), :],
                    preferred_element_type=F32)
    for h in heads:
        o_h = o_win[hrows(h)]
        ow_ref[hrows(h), :] = (gate(h, 2) * o_h[:, :LANES]) * (1.0 / o_h[:, LANES:])

    qpos = q0 + lax.broadcasted_iota(jnp.int32, (1, tq), 1)
    blk = lax.broadcasted_iota(jnp.int32, (n_slc, 1), 0)
    cur = qpos >> SLC_SHIFT
    forced = (blk == 0) | (blk == cur) | (blk == cur - 1)
    causal_blk = blk * SLC_LEN <= qpos
    score = jnp.where(causal_blk, imp_t + FORCE_BONUS * forced.astype(F32), NEG)
    sub = lax.broadcasted_iota(jnp.int32, (8, 1), 0)
    groups = [score[8 * v:8 * v + 8] for v in range(n_slc // 8)]
    counts = [jnp.zeros((8, tq), F32) for _ in groups]
    for m in range(n_slc):
        row = score[m:m + 1]
        for v, sv in enumerate(groups):
            if v < m // 8:
                ahead = row > sv
            elif v > m // 8:
                ahead = row >= sv
            else:
                ahead = (row > sv) | ((row == sv) & (sub > m % 8))
            counts[v] = counts[v] + jnp.where(ahead, 1.0, 0.0)
    rank = jnp.concatenate(counts, axis=0)
    sel_bias_t = jnp.where(rank < SLC_TOP_N, 0.0, NEG)
    sel_bias = jnp.concatenate([sel_bias_t, jnp.zeros((LANES - n_slc, tq), F32)], axis=0).T
    sel_bias = sel_bias[:, :HEAD_DIM].astype(qa_ref.dtype)
    for h in heads:
        qa_ref[hrows(h), :] = jnp.concatenate([q_heads[h], sel_bias], axis=1)

    tok_bias = jnp.where(lax.broadcasted_iota(jnp.int32, (1, kb), 1)
                         <= lax.broadcasted_iota(jnp.int32, (tq, 1), 0), 0.0, NEG)

    def scores(start):
        k_t = ks_ref[0, 0, pl.ds(start, kb), :]
        return lax.dot_general(qa_ref[...], k_t, NT_DIMS, preferred_element_type=F32)

    def softmax(s_all, slot, own_keys):
        for r0 in range(0, n_rows, rc):
            rows = slice(r0, r0 + rc)
            s = s_all[rows]
            if own_keys:
                s = s + tok_bias[r0 % tq:r0 % tq + rc]
                m_new = jnp.broadcast_to(jnp.max(s, axis=1, keepdims=True), (rc, LANES))
            else:
                m_old = m_ref[rows, :]
                m_new = jnp.maximum(m_old, jnp.max(s, axis=1, keepdims=True))
                alpha_ref[slot, rows, :] = jnp.exp2(m_old - m_new)
            m_ref[rows, :] = m_new
            m_wide = jnp.concatenate([m_new] * (kb // LANES), axis=1)
            p_ref[slot, rows, :] = jnp.exp2(s - m_wide).astype(p_ref.dtype)

    def values(start, slot):
        v_t = vs_ref[0, 0, pl.ds(start, kb), :]
        pv = jnp.dot(p_ref[slot], v_t, preferred_element_type=F32)
        for r0 in range(0, n_rows, rc):
            rows = slice(r0, r0 + rc)
            acc_ref[rows, :] = alpha_ref[slot, rows, :] * acc_ref[rows, :] + pv[rows]

    chunk_start = lambda c: pl.multiple_of(c * kb, kb)
    owed_start = lambda c: pl.multiple_of(jnp.where(c == 0, q0, (c - 1) * kb), kb)

    acc_ref[...] = jnp.zeros_like(acc_ref)
    alpha_ref[1] = jnp.zeros(alpha_ref.shape[1:], F32)
    softmax(scores(q0), 1, True)

    def pair_body(t, carry):
        c = 2 * t
        s_a = scores(chunk_start(c))
        values(owed_start(c), 1)
        softmax(s_a, 0, False)
        s_b = scores(chunk_start(c + 1))
        values(chunk_start(c), 0)
        softmax(s_b, 1, False)
        return carry

    lax.fori_loop(0, i // 2, pair_body, 0)

    @pl.when(i % 2 == 1)
    def _():
        s_a = scores(chunk_start(i - 1))
        values(owed_start(i - 1), 1)
        softmax(s_a, 0, False)
        values(chunk_start(i - 1), 0)

    @pl.when(i % 2 == 0)
    def _():
        values(owed_start(i), 1)

    outs = []
    for h in heads:
        o_h = acc_ref[hrows(h), :]
        o_sel = (gs_ref[hrows(h), :] * o_h) * (1.0 / o_h[:, HEAD_DIM:HEAD_DIM + 1])
        outs.append(((oc_ref[hrows(h), :] + o_sel) + ow_ref[hrows(h), :])[:, :HEAD_DIM])
    for hp in range(NSA_HPG // 2):
        pair = jnp.concatenate(outs[2 * hp:2 * hp + 2], axis=1)
        o_ref[0, :, hp * LANES:(hp + 1) * LANES] = pair.astype(o_ref.dtype)


def _nsa_attention(q3, kct, vc, ov, ksa, vsa, kwa, vwa, gt, tq=256, wg=4, rc=64):
    b, s, _ = q3.shape
    G = NSA_KV_GROUPS
    kb = tq
    n_slc = s // SLC_LEN
    nq = s // tq
    n_cmp_rows = kct.shape[3]
    gw = NSA_HPG * HEAD_DIM
    assert n_slc <= HEAD_DIM and tq % SLC_LEN == 0 and WINDOW % tq == 0 and s >= WINDOW + tq
    assert n_cmp_rows <= kb and n_slc % 8 == 0 and tq % rc == 0
    kv_spec = pl.BlockSpec((1, 1, s, LANES), lambda bi, g, i: (g, bi, 0, 0))
    n_gates = NSA_GATES_PER_GROUP
    gsel = np.zeros((2 * LANES, n_gates * LANES), np.float32)
    for c in range(n_gates):
        gsel[[c, LANES + c], c * LANES:(c + 1) * LANES] = 1.0
    gsel = jnp.asarray(gsel, BF16)
    return pl.pallas_call(
        functools.partial(_nsa_attn_kernel, tq=tq, kb=kb, n_slc=n_slc, wg=wg, rc=rc),
        grid=(b, G, nq),
        in_specs=[pl.BlockSpec((1, tq, gw), lambda bi, g, i: (bi, i, g)),
                  pl.BlockSpec((1, 1, LANES, n_cmp_rows), lambda bi, g, i: (bi, g, 0, 0)),
                  pl.BlockSpec((1, 1, n_cmp_rows, LANES), lambda bi, g, i: (bi, g, 0, 0)),
                  _const_spec(ov.shape),
                  kv_spec, kv_spec, kv_spec,
                  pl.BlockSpec((1, 1, s, 2 * LANES), lambda bi, g, i: (g, bi, 0, 0)),
                  pl.BlockSpec((1, tq, LANES), lambda bi, g, i: (g, bi * nq + i, 0)),
                  _const_spec(gsel.shape)],
        out_specs=pl.BlockSpec((1, tq, gw), lambda bi, g, i: (bi, i, g)),
        out_shape=jax.ShapeDtypeStruct((b, s, D_MODEL), BF16),
        scratch_shapes=[pltpu.VMEM((NSA_HPG * tq, LANES), BF16),
                        pltpu.VMEM((NSA_HPG * tq, LANES), F32),
                        pltpu.VMEM((tq, n_cmp_rows), F32),
                        pltpu.VMEM((NSA_HPG * tq, LANES), F32),
                        pltpu.VMEM((NSA_HPG * tq, LANES), F32),
                        pltpu.VMEM((2, NSA_HPG * tq, LANES), F32),
                        pltpu.VMEM((2, NSA_HPG * tq, kb), BF16),
                        pltpu.VMEM((NSA_HPG * tq, WINDOW + tq), BF16),
                        pltpu.VMEM((NSA_HPG * tq, LANES), F32),
                        pltpu.VMEM((NSA_HPG * tq, LANES), F32)],
        compiler_params=_params(("arbitrary", "arbitrary", "arbitrary")),
        name="nsa_attention",
    )(q3, kct, vc, ov, ksa, vsa, kwa, vwa, gt, gsel)


def _nsa_layer_attention(hx2, norm_g, positions, w_in, pe_k, pe_v, w1k, w2k, w1v, w2v, b, s):
    t = b * s
    G = NSA_KV_GROUPS
    w_pad = jnp.pad(w_in, ((0, 0), (0, NSA_W_COLS - w_in.shape[1]))).astype(BF16)
    inv = ROPE_THETA ** (-jnp.arange(HALF, dtype=F32) / HALF)
    inv2 = jnp.tile(inv, LANES // HALF)[None, :]
    sgn2 = jnp.tile(jnp.concatenate([-jnp.ones(HALF, F32), jnp.ones(HALF, F32)]), LANES // HEAD_DIM)[None, :]
    q, kc_tok, vc_tok, ksa, vsa, kwa, vwa, gt = _nsa_in_proj(
        hx2, norm_g, w_pad, positions.reshape(t, 1), inv2, sgn2, s)

    nrow = s // CMP_STRIDE
    wide = CMP_STRIDE * HEAD_DIM
    k16 = kc_tok.reshape(G, b, nrow, wide)
    v16 = vc_tok.reshape(G, b, nrow, wide)
    pek = jnp.broadcast_to(pe_k.reshape(1, CMP_LEN * HEAD_DIM), (8, CMP_LEN * HEAD_DIM))
    pev = jnp.broadcast_to(pe_v.reshape(1, CMP_LEN * HEAD_DIM), (8, CMP_LEN * HEAD_DIM))
    w2k_rot = jnp.concatenate([-w2k[:, HALF:], w2k[:, :HALF]], axis=1)
    end_idx = jnp.minimum(jnp.arange(nrow) * CMP_STRIDE + CMP_LEN - 1, s - 1)
    pos_cmp = positions[:, end_idx][:, :, None]
    inv64 = jnp.tile(inv, 2)[None, :]
    kct, vc = _nsa_compress(k16, v16, pek, pev, w1k, w2k, w2k_rot, w1v, w2v, pos_cmp, inv64)

    n_slc = s // SLC_LEN
    c0 = np.arange(nrow)[:, None] * CMP_STRIDE
    s0 = np.arange(n_slc)[None, :] * SLC_LEN
    ov = np.clip(np.minimum(c0 + CMP_LEN, s0 + SLC_LEN) - np.maximum(c0, s0), 0, None) / CMP_LEN
    ov[(s - CMP_LEN) // CMP_STRIDE + 1:, :] = 0.0
    ov = np.pad(ov, ((0, 0), (0, LANES - n_slc)))
    ov = jnp.asarray(np.concatenate([ov, ov], axis=0), BF16)

    aug4 = lambda a: a.reshape(G, b, s, a.shape[-1])
    return _nsa_attention(q.reshape(b, s, D_MODEL), kct, vc, ov, aug4(ksa), aug4(vsa),
                          aug4(kwa), aug4(vwa), gt)


def kernel(x, positions, norm_mix, sba_w_in, sba_w_out, nsa_w_in, nsa_cmp_pos_k, nsa_cmp_pos_v,
           nsa_cmp_k_w1, nsa_cmp_k_w2, nsa_cmp_v_w1, nsa_cmp_v_w2, nsa_w_out, norm_ffn,
           ffn_w_up, ffn_conv_w, ffn_conv_b, ffn_w_down, norm_final):
    b, s, d = x.shape
    t = b * s
    depth = norm_mix.shape[0]
    x2 = x.reshape(t, d)
    g_final = norm_final.reshape(1, d)
    for layer in range(depth):
        j = layer // 2
        g_mix = norm_mix[layer].reshape(1, d)
        if layer % 2 == 0:
            qkv = _sba_in_proj(x2, g_mix, sba_w_in[j].astype(BF16))
            o = _sba_attention(qkv.reshape(b, s, 3 * d), b, s)
            w_out = sba_w_out[j]
        else:
            o = _nsa_layer_attention(x2, g_mix, positions, nsa_w_in[j], nsa_cmp_pos_k[j],
                                     nsa_cmp_pos_v[j], nsa_cmp_k_w1[j], nsa_cmp_k_w2[j],
                                     nsa_cmp_v_w1[j], nsa_cmp_v_w2[j], b, s)
            w_out = nsa_w_out[j]
        x2 = _conv_ffn(x2, o.reshape(t, d), w_out.astype(BF16),
                       norm_ffn[layer].reshape(1, d), ffn_w_up[layer].astype(BF16),
                       ffn_conv_w[layer], ffn_conv_b[layer].reshape(1, -1),
                       ffn_w_down[layer].astype(BF16), g_final, s,
                       final_norm=(layer == depth - 1))
    return x2.reshape(b, s, d)
```

```python
import functools

import numpy as np
import jax
import jax.numpy as jnp
from jax import lax
from jax.experimental import pallas as pl
from jax.experimental.pallas import tpu as pltpu

D_MODEL = 1024
N_HEADS = 16
HEAD_DIM = 64
HALF = HEAD_DIM // 2
NSA_KV_GROUPS = 2
NSA_HPG = N_HEADS // NSA_KV_GROUPS
CMP_LEN = 32
CMP_STRIDE = 16
CMP_HIDDEN = 2 * HEAD_DIM
SLC_LEN = 64
SLC_SHIFT = SLC_LEN.bit_length() - 1
SLC_TOP_N = 16
WINDOW = 512
ROPE_THETA = 10000.0
D_FF = 2816
RMS_EPS = 1e-6
NEG = -1e30
FORCE_BONUS = 1e4
LOG2E = float(np.log2(np.e))
EXP2_CLAMP = 120.0
EXP2_UNDERFLOW = 160.0
Q_SCALE = LOG2E * HEAD_DIM ** -0.5

LANES = 128
VMEM_LIMIT = 56 * 1024 * 1024

F32 = jnp.float32
BF16 = jnp.bfloat16
HIGHEST = lax.Precision.HIGHEST
NT_DIMS = (((1,), (1,)), ((), ()))


def _params(semantics):
    return pltpu.CompilerParams(dimension_semantics=semantics, vmem_limit_bytes=VMEM_LIMIT)


def _rmsnorm(x, g):
    return x * lax.rsqrt(jnp.mean(x * x, axis=-1, keepdims=True) + RMS_EPS) * g


def _const_spec(shape):
    return pl.BlockSpec(shape, lambda *_: (0,) * len(shape), pipeline_mode=pl.Buffered(1))


def _split_bf16(x):
    hi = x.astype(BF16)
    return hi, (x - hi.astype(F32)).astype(BF16)


def _sba_in_proj_kernel(x_ref, g_ref, w_ref, o_ref, *, n_chunk):
    hn = _rmsnorm(x_ref[...], g_ref[...]).astype(w_ref.dtype)
    n = w_ref.shape[1]
    for c in range(n // n_chunk):
        cols = slice(c * n_chunk, (c + 1) * n_chunk)
        y = jnp.dot(hn, w_ref[:, cols], preferred_element_type=F32)
        if c * n_chunk < D_MODEL:
            y = y * Q_SCALE
        o_ref[:, cols] = y.astype(o_ref.dtype)


def _sba_in_proj(x2, g, w, tm=512):
    t, d = x2.shape
    n = w.shape[1]
    return pl.pallas_call(
        functools.partial(_sba_in_proj_kernel, n_chunk=512),
        grid=(t // tm,),
        in_specs=[pl.BlockSpec((tm, d), lambda i: (i, 0)),
                  _const_spec((1, d)),
                  _const_spec((d, n))],
        out_specs=pl.BlockSpec((tm, n), lambda i: (i, 0)),
        out_shape=jax.ShapeDtypeStruct((t, n), BF16),
        compiler_params=_params(("arbitrary",)),
        name="sba_in_proj",
    )(x2, g, w)


def _sba_attn_kernel(q_ref, k_ref, v_ref, uu_ref, o_ref, acc_ref, car_ref, hl_ref, lb_ref, a_ref,
                     z_ref,
                     *, tq, kb, nsub, nwalk, rc):
    i = pl.program_id(2)
    kt = kb * nsub
    streams = range(q_ref.shape[2] // LANES)
    scols = lambda s: slice(s * LANES, (s + 1) * LANES)
    q = q_ref[0]
    lane = lax.broadcasted_iota(jnp.int32, (tq, LANES), 1)
    klane = lax.broadcasted_iota(jnp.int32, (kb, LANES), 1)
    acc_ref[...] = jnp.zeros_like(acc_ref)
    car_ref[...] = jnp.zeros_like(car_ref)
    ud = uu_ref[...]
    rel = (lax.broadcasted_iota(jnp.int32, (rc, 2 * kb), 0)
           - (lax.broadcasted_iota(jnp.int32, (rc, 2 * kb), 1) & (kb - 1)))

    def super_tile(base, n, diag, slot, owed, z_first):
        def visibility(j, r0):
            if not diag:
                return 2, None
            if j * kb >= r0 + rc - 1:
                return 0, None
            if j * kb + kb - 1 < r0:
                return 2, None
            return 1, rel > (j * kb - r0)

        def scores(s, start):
            k_t = k_ref[0, pl.ds(pl.multiple_of(start, kb), kb), scols(s)]
            k_bd = jnp.concatenate([jnp.where(klane < HEAD_DIM, k_t, jnp.zeros_like(k_t)),
                                    jnp.where(klane >= HEAD_DIM, k_t, jnp.zeros_like(k_t))], axis=0)
            return lax.dot_general(q[:, scols(s)], k_bd, NT_DIMS, preferred_element_type=F32)

        def log_terms(s, j, z):
            totals = []
            for r0 in range(0, tq, rc):
                rows = slice(r0, r0 + rc)
                kind, strict = visibility(j, r0)
                if kind == 0:
                    hl_ref[s, j, rows, :] = jnp.zeros((rc, 2 * kb), BF16)
                    totals.append(None)
                    continue
                zc = z[rows]
                nl = jnp.maximum(jnp.log2(1.0 + jnp.exp2(jnp.minimum(zc, EXP2_CLAMP))), zc)
                lb_ref[s, j, rows, :] = zc - nl
                if kind == 1:
                    nl = jnp.where(strict, nl, 0.0)
                hl_ref[s, j, rows, :] = nl.astype(BF16)
                totals.append([jnp.sum(nl[:, h * kb:(h + 1) * kb], axis=1, keepdims=True)
                               for h in range(2)])
            return jnp.dot(hl_ref[s, j], ud, preferred_element_type=F32), totals

        def weights(s, j, sums):
            cum, totals = sums
            for n, r0 in enumerate(range(0, tq, rc)):
                rows = slice(r0, r0 + rc)
                cols = slice(j * kb, (j + 1) * kb)
                kind, strict = visibility(j, r0)
                if kind == 0:
                    for h in range(2):
                        a_ref[s, slot, h, rows, cols] = jnp.zeros((rc, kb), BF16)
                    continue
                car = [car_ref[s, h, rows, :] for h in range(2)]
                a = jnp.exp2(lb_ref[s, j, rows, :] - cum[rows] - jnp.concatenate(car, axis=1))
                if kind == 1:
                    a = jnp.where(strict, a, 0.0)
                for h in range(2):
                    a_ref[s, slot, h, rows, cols] = a[:, h * kb:(h + 1) * kb].astype(BF16)
                    car_ref[s, h, rows, :] = car[h] + totals[n][h]

        z_next = ([scores(s, base + (n - 1) * kb) for s in streams] if z_first is None
                  else z_first)
        pending = None
        for j in reversed(range(n)):
            z = z_next
            if j > 0:
                z_next = [scores(s, base + (j - 1) * kb) for s in streams]
            if j == n - 1 and owed is not None:
                apply_weights(*owed, nwalk)
            sums = [log_terms(s, j, z[s]) for s in streams]
            if pending is not None:
                for s in streams:
                    weights(s, j + 1, pending[s])
            pending = sums
            if j == 0:
                for s in streams:
                    z_ref[s] = scores(s, jnp.maximum(base - kb, 0))
        for s in streams:
            weights(s, 0, pending[s])

    def apply_weights(slot, start, n):
        for s in streams:
            v_n = v_ref[0, pl.ds(pl.multiple_of(start, kb), n * kb), scols(s)]
            for h in range(2):
                acc_ref[s, h] += jnp.dot(a_ref[s, slot, h, :, 0:n * kb], v_n,
                                         preferred_element_type=F32)

    super_tile(i * kt, nsub, True, 0, None, None)
    apply_weights(0, i * kt, nsub)

    kw = nwalk * kb
    a_ref[:, 1, :, :, 0:kw] = jnp.zeros((len(streams), 2, tq, kw), BF16)

    def live():
        return (jnp.min(car_ref[...]) < EXP2_UNDERFLOW).astype(jnp.int32)

    def body(carry):
        t, _ = carry
        base = i * kt - (t + 1) * kw
        super_tile(base, nwalk, False, t % 2, ((t + 1) % 2, base + kw), [z_ref[s] for s in streams])
        return t + 1, live()

    n_walk = i * (kt // kw)
    t_end, _ = lax.while_loop(lambda c: (c[0] < n_walk) & (c[1] > 0), body, (jnp.int32(0), live()))
    apply_weights((t_end + 1) % 2, i * kt - t_end * kw, nwalk)
    for s in streams:
        o_ref[0, :, scols(s)] = jnp.where(lane < HEAD_DIM, acc_ref[s, 0],
                                          acc_ref[s, 1]).astype(o_ref.dtype)


def _sba_attention(qkv, b, s, tq=512, kb=LANES, nsub=4, nwalk=2, rc=64, ns=2):
    assert tq == kb * nsub and s % tq == 0 and tq % rc == 0 and nsub % nwalk == 0
    n_blocks = D_MODEL // (ns * LANES)
    wide = ns * LANES
    assert kb & (kb - 1) == 0 and D_MODEL % wide == 0
    jj = np.arange(kb)
    uu = jnp.asarray(np.kron(np.eye(2), (jj[:, None] > jj[None, :]).astype(np.float32)), BF16)
    return pl.pallas_call(
        functools.partial(_sba_attn_kernel, tq=tq, kb=kb, nsub=nsub, nwalk=nwalk, rc=rc),
        grid=(b, n_blocks, s // tq),
        in_specs=[pl.BlockSpec((1, tq, wide), lambda bi, p, i: (bi, i, p)),
                  pl.BlockSpec((1, s, wide), lambda bi, p, i: (bi, 0, n_blocks + p)),
                  pl.BlockSpec((1, s, wide), lambda bi, p, i: (bi, 0, 2 * n_blocks + p)),
                  _const_spec((2 * kb, 2 * kb))],
        out_specs=pl.BlockSpec((1, tq, wide), lambda bi, p, i: (bi, i, p)),
        out_shape=jax.ShapeDtypeStruct((b, s, D_MODEL), BF16),
        scratch_shapes=[pltpu.VMEM((ns, 2, tq, LANES), F32),
                        pltpu.VMEM((ns, 2, tq, LANES), F32),
                        pltpu.VMEM((ns, nsub, tq, 2 * kb), BF16),
                        pltpu.VMEM((ns, nsub, tq, 2 * kb), F32),
                        pltpu.VMEM((ns, 2, 2, tq, nsub * kb), BF16),
                        pltpu.VMEM((ns, tq, 2 * kb), F32)],
        compiler_params=_params(("arbitrary", "arbitrary", "arbitrary")),
        name="sba_attention",
    )(qkv, qkv, qkv, uu)


def _ffn_kernel(x_ref, o_ref, wo_ref, g_ref, wup_ref, cw_ref, cb_ref, wdn_ref, gf_ref, y_ref,
                carry_ref, sg_ref, sv_ref, act_ref, *, tm, fc, tiles_per_seq, final_norm):
    @pl.when(pl.program_id(0) % tiles_per_seq == 0)
    def _():
        carry_ref[...] = jnp.zeros_like(carry_ref)

    x = x_ref[...] + jnp.dot(o_ref[...], wo_ref[...], preferred_element_type=F32)
    hn = _rmsnorm(x, g_ref[...]).astype(wup_ref.dtype)

    def up(col0):
        return jnp.dot(hn, wup_ref[:, col0:col0 + fc], preferred_element_type=F32)

    def conv(u, col0, s_ref):
        cols = slice(col0, col0 + fc)
        s_ref[0:8, :] = carry_ref[:, cols]
        s_ref[8:tm + 8, :] = u
        carry_ref[:, cols] = u[tm - 8:tm, :]
        cw = cw_ref[:, cols]
        c = cb_ref[:, cols] + s_ref[6:tm + 6, :] * cw[0:1]
        c = c + s_ref[7:tm + 7, :] * cw[1:2]
        return c + u * cw[2:3]

    n_chunks = D_FF // fc
    u_next = (up(0), up(D_FF))
    for c in range(n_chunks):
        u_gate, u_val = u_next
        if c + 1 < n_chunks:
            u_next = (up((c + 1) * fc), up(D_FF + (c + 1) * fc))
        gate = conv(u_gate, c * fc, sg_ref)
        val = conv(u_val, D_FF + c * fc, sv_ref)
        act_ref[:, c * fc:(c + 1) * fc] = (gate * jax.nn.sigmoid(gate) * val).astype(act_ref.dtype)
    y = x + jnp.dot(act_ref[...], wdn_ref[...], preferred_element_type=F32)
    if final_norm:
        y = _rmsnorm(y, gf_ref[...])
    y_ref[...] = y


def _conv_ffn(x2, o2, w_out, g, w_up, conv_w, conv_b, w_down, g_final, s, final_norm,
              tm=256, fc=256):
    t, d = x2.shape
    f2 = w_up.shape[1]
    return pl.pallas_call(
        functools.partial(_ffn_kernel, tm=tm, fc=fc, tiles_per_seq=s // tm, final_norm=final_norm),
        grid=(t // tm,),
        in_specs=[pl.BlockSpec((tm, d), lambda i: (i, 0)),
                  pl.BlockSpec((tm, d), lambda i: (i, 0)),
                  _const_spec((d, d)),
                  _const_spec((1, d)),
                  _const_spec((d, f2)),
                  _const_spec((3, f2)),
                  _const_spec((1, f2)),
                  _const_spec((D_FF, d)),
                  _const_spec((1, d))],
        out_specs=pl.BlockSpec((tm, d), lambda i: (i, 0)),
        out_shape=jax.ShapeDtypeStruct((t, d), F32),
        scratch_shapes=[pltpu.VMEM((8, f2), F32),
                        pltpu.VMEM((tm + 8, fc), F32),
                        pltpu.VMEM((tm + 8, fc), F32),
                        pltpu.VMEM((tm, D_FF), BF16)],
        compiler_params=_params(("arbitrary",)),
        name="conv_ffn",
    )(x2, o2, w_out, g, w_up, conv_w, conv_b, w_down, g_final)


NSA_Q_COLS = D_MODEL
NSA_W_COLS = D_MODEL + 7 * LANES
NSA_GATES_PER_GROUP = 3 * NSA_HPG


def _swap_halves(y):
    lane = lax.broadcasted_iota(jnp.int32, y.shape, 1)
    first = (lane % HEAD_DIM) < HALF
    return jnp.where(first, pltpu.roll(y, LANES - HALF, 1), pltpu.roll(y, HALF, 1))


def _nsa_in_proj_kernel(x_ref, g_ref, w_ref, pos_ref, inv_ref, sgn_ref,
                        q_ref, kc_ref, vc_ref, ksa_ref, vsa_ref, kwa_ref, vwa_ref, gt_ref,
                        *, tm, seq):
    hn = _rmsnorm(x_ref[...], g_ref[...]).astype(w_ref.dtype)
    ang = pos_ref[...].astype(F32) * inv_ref[...]
    cos = jnp.cos(ang)
    sin = jnp.sin(ang) * sgn_ref[...]

    def rope(y):
        return y * cos + _swap_halves(y) * sin

    q_chunk = 4 * LANES
    for c in range(NSA_Q_COLS // q_chunk):
        y = jnp.dot(hn, w_ref[:, c * q_chunk:(c + 1) * q_chunk], preferred_element_type=F32)
        for l in range(q_chunk // LANES):
            yl = rope(y[:, l * LANES:(l + 1) * LANES]) * Q_SCALE
            q_ref[:, c * q_chunk + l * LANES:c * q_chunk + (l + 1) * LANES] = yl.astype(q_ref.dtype)
    y = jnp.dot(hn, w_ref[:, NSA_Q_COLS:], preferred_element_type=F32)
    part = lambda n: y[:, n * LANES:(n + 1) * LANES]

    tok = ((pl.program_id(0) % (seq // tm)) * tm
           + lax.broadcasted_iota(jnp.int32, (tm, HEAD_DIM), 0))
    lane = lax.broadcasted_iota(jnp.int32, (tm, HEAD_DIM), 1)
    blk_onehot = jnp.where((tok >> SLC_SHIFT) == lane, 1.0, 0.0)
    ones_col = jnp.where(lane == 0, 1.0, 0.0)
    zeros = jnp.zeros((tm, HEAD_DIM), F32)
    ks, vs, kw, vw = rope(part(2)), part(3), rope(part(4)), part(5)
    for g in range(NSA_KV_GROUPS):
        cols = slice(g * HEAD_DIM, (g + 1) * HEAD_DIM)
        kc_ref[g] = part(0)[:, cols]
        vc_ref[g] = part(1)[:, cols]
        ksa_ref[g] = jnp.concatenate([ks[:, cols], blk_onehot], axis=1).astype(ksa_ref.dtype)
        vsa_ref[g] = jnp.concatenate([vs[:, cols], ones_col], axis=1).astype(vsa_ref.dtype)
        kwa_ref[g] = jnp.concatenate([kw[:, cols], zeros], axis=1).astype(kwa_ref.dtype)
        vwa_ref[g] = jnp.concatenate([vw[:, cols], zeros, jnp.ones((tm, LANES), F32)],
                                     axis=1).astype(vwa_ref.dtype)
    gates = jax.nn.sigmoid(part(6))
    gt_ref[0] = gates
    gt_ref[1] = pltpu.roll(gates, LANES - NSA_GATES_PER_GROUP, 1)


def _nsa_in_proj(x2, g, w, pos2, inv2, sgn2, seq, tm=512):
    t, d = x2.shape
    G = NSA_KV_GROUPS
    tok_g = lambda dt: jax.ShapeDtypeStruct((G, t, HEAD_DIM), dt)
    aug_g = lambda dt: jax.ShapeDtypeStruct((G, t, LANES), dt)
    tok_spec = pl.BlockSpec((G, tm, HEAD_DIM), lambda i: (0, i, 0))
    aug_spec = pl.BlockSpec((G, tm, LANES), lambda i: (0, i, 0))
    return pl.pallas_call(
        functools.partial(_nsa_in_proj_kernel, tm=tm, seq=seq),
        grid=(t // tm,),
        in_specs=[pl.BlockSpec((tm, d), lambda i: (i, 0)),
                  _const_spec((1, d)),
                  _const_spec((d, NSA_W_COLS)),
                  pl.BlockSpec((tm, 1), lambda i: (i, 0)),
                  _const_spec((1, LANES)),
                  _const_spec((1, LANES))],
        out_specs=[pl.BlockSpec((tm, NSA_Q_COLS), lambda i: (i, 0)),
                   tok_spec, tok_spec, aug_spec, aug_spec, aug_spec,
                   pl.BlockSpec((G, tm, 2 * LANES), lambda i: (0, i, 0)), aug_spec],
        out_shape=[jax.ShapeDtypeStruct((t, NSA_Q_COLS), BF16),
                   tok_g(F32), tok_g(F32), aug_g(BF16), aug_g(BF16), aug_g(BF16),
                   jax.ShapeDtypeStruct((G, t, 2 * LANES), BF16), aug_g(F32)],
        compiler_params=_params(("arbitrary",)),
        name="nsa_in_proj",
    )(x2, g, w, pos2, inv2, sgn2)


def _nsa_compress_kernel(k16_ref, v16_ref, pek_ref, pev_ref, w1k_ref, w2k_ref, w2kr_ref,
                         w1v_ref, w2v_ref, pos_ref, inv_ref, kct_ref, vc_ref):
    half_w = CMP_STRIDE * HEAD_DIM
    nrow = k16_ref.shape[2]

    def hidden(x16_ref, pe_ref, w1_ref):
        x = x16_ref[0, 0]
        y1 = jnp.dot(x, w1_ref[:half_w, :], precision=HIGHEST, preferred_element_type=F32)
        y2 = jnp.dot(x, w1_ref[half_w:, :], precision=HIGHEST, preferred_element_type=F32)
        bias = jnp.dot(pe_ref[...], w1_ref[...], precision=HIGHEST, preferred_element_type=F32)
        return jax.nn.gelu(y1 + pltpu.roll(y2, nrow - 1, 0) + bias[0:1])

    hk = hidden(k16_ref, pek_ref, w1k_ref)
    kc = jnp.dot(hk, w2k_ref[...], precision=HIGHEST, preferred_element_type=F32)
    kc_rot = jnp.dot(hk, w2kr_ref[...], precision=HIGHEST, preferred_element_type=F32)
    ang = pos_ref[0].astype(F32) * inv_ref[...]
    kc = kc * jnp.cos(ang) + kc_rot * jnp.sin(ang)
    hi, lo = _split_bf16(kc)
    kct_ref[0, 0] = jnp.concatenate([hi.astype(F32), lo.astype(F32)], axis=1).T.astype(kct_ref.dtype)
    hv = hidden(v16_ref, pev_ref, w1v_ref)
    vc = jnp.dot(hv, w2v_ref[...], precision=HIGHEST, preferred_element_type=F32)
    vc_ref[0, 0] = jnp.concatenate([vc, jnp.zeros_like(vc)], axis=1).astype(vc_ref.dtype)


def _nsa_compress(k16, v16, pek, pev, w1k, w2k, w2kr, w1v, w2v, pos_cmp, inv64):
    G, b, nrow, wide = k16.shape
    x_spec = pl.BlockSpec((1, 1, nrow, wide), lambda bi, g: (g, bi, 0, 0))
    return pl.pallas_call(
        _nsa_compress_kernel,
        grid=(b, G),
        in_specs=[x_spec, x_spec,
                  _const_spec(pek.shape), _const_spec(pev.shape),
                  _const_spec(w1k.shape), _const_spec(w2k.shape), _const_spec(w2kr.shape),
                  _const_spec(w1v.shape), _const_spec(w2v.shape),
                  pl.BlockSpec((1, nrow, 1), lambda bi, g: (bi, 0, 0)),
                  _const_spec((1, HEAD_DIM))],
        out_specs=[pl.BlockSpec((1, 1, LANES, nrow), lambda bi, g: (bi, g, 0, 0)),
                   pl.BlockSpec((1, 1, nrow, LANES), lambda bi, g: (bi, g, 0, 0))],
        out_shape=[jax.ShapeDtypeStruct((b, G, LANES, nrow), BF16),
                   jax.ShapeDtypeStruct((b, G, nrow, LANES), BF16)],
        compiler_params=_params(("arbitrary", "arbitrary")),
        name="nsa_compress",
    )(k16, v16, pek, pev, w1k, w2k, w2kr, w1v, w2v, pos_cmp, inv64)


def _nsa_attn_kernel(q_ref, kct_ref, vc_ref, ov_ref, ks_ref, vs_ref, kw_ref, vw_ref, gt_ref,
                     gsel_ref, o_ref, qa_ref, oc_ref, psum_ref, m_ref, acc_ref, alpha_ref, p_ref, pw_ref,
                     ow_ref, gs_ref,
                     *, tq, kb, n_slc, wg, rc):
    i = pl.program_id(2)
    q0 = pl.multiple_of(i * tq, tq)
    n_rows = NSA_HPG * tq
    heads = range(NSA_HPG)
    hrows = lambda h: slice(h * tq, (h + 1) * tq)
    qrow = q0 + lax.broadcasted_iota(jnp.int32, (tq, 1), 0)
    q_heads = [q_ref[0, :, h * HEAD_DIM:(h + 1) * HEAD_DIM] for h in heads]
    for h in heads:
        qa_ref[hrows(h), :] = jnp.concatenate([q_heads[h], q_heads[h]], axis=1)

    n_cmp_rows = kct_ref.shape[3]
    s_all = jnp.dot(qa_ref[...], kct_ref[0, 0], preferred_element_type=F32)
    w_keys = WINDOW + tq
    w_start = pl.multiple_of(jnp.maximum(q0 - WINDOW, 0), tq)
    k_w = kw_ref[0, 0, pl.ds(w_start, w_keys), :]
    s_win = [lax.dot_general(qa_ref[h0 * tq:(h0 + wg) * tq, :], k_w, NT_DIMS,
                             preferred_element_type=F32) for h0 in range(0, NSA_HPG, wg)]

    cmp_end = CMP_STRIDE * lax.broadcasted_iota(jnp.int32, (1, n_cmp_rows), 1) + (CMP_LEN - 1)
    c_bias = jnp.where(cmp_end <= qrow, 0.0, NEG)
    row_valid = jnp.where(qrow >= CMP_LEN - 1, 1.0, 0.0)
    for r0 in range(0, n_rows, rc):
        rows = slice(r0, r0 + rc)
        local = slice(r0 % tq, r0 % tq + rc)
        s = s_all[rows] + c_bias[local]
        e = jnp.exp2(s - jnp.max(s, axis=1, keepdims=True))
        p = e * (row_valid[local] / jnp.sum(e, axis=1, keepdims=True))
        if r0 < tq:
            psum_ref[local, :] = p
        else:
            psum_ref[local, :] += p
        p_ref[0, rows, 0:n_cmp_rows] = p.astype(p_ref.dtype)
    g_wide = jnp.dot(jnp.concatenate(_split_bf16(gt_ref[0]), axis=1), gsel_ref[...],
                     preferred_element_type=F32)
    gate = lambda h, branch: g_wide[:, (3 * h + branch) * LANES:(3 * h + branch + 1) * LANES]
    o_cmp = jnp.dot(p_ref[0, :, 0:n_cmp_rows], vc_ref[0, 0], preferred_element_type=F32)
    for h in heads:
        oc_ref[hrows(h), :] = gate(h, 0) * o_cmp[hrows(h)]
        gs_ref[hrows(h), :] = gate(h, 1)
    imp = jnp.dot(jnp.concatenate(_split_bf16(psum_ref[...]), axis=1), ov_ref[...],
                  preferred_element_type=F32)
    imp_t = imp.T[:n_slc]

    w_diff = qrow - (w_start + lax.broadcasted_iota(jnp.int32, (1, w_keys), 1))
    w_bias = jnp.where((w_diff >= 0) & (w_diff < WINDOW), 0.0, NEG)
    for r0 in range(0, n_rows, rc):
        s = s_win[r0 // (wg * tq)][r0 % (wg * tq):r0 % (wg * tq) + rc] + w_bias[r0 % tq:r0 % tq + rc]
        pw_ref[r0:r0 + rc, :] = jnp.exp2(s - jnp.max(s, axis=1, keepdims=True)).astype(pw_ref.dtype)
    o_win = jnp.dot(pw_ref[...], vw_ref[0, 0, pl.ds(w_start, w_keys), :],
                    preferred_element_type=F32)
    for h in heads:
        o_h = o_win[hrows(h)]
        ow_ref[hrows(h), :] = (gate(h, 2) * o_h[:, :LANES]) * (1.0 / o_h[:, LANES:])

    qpos = q0 + lax.broadcasted_iota(jnp.int32, (1, tq), 1)
    blk = lax.broadcasted_iota(jnp.int32, (n_slc, 1), 0)
    cur = qpos >> SLC_SHIFT
    forced = (blk == 0) | (blk == cur) | (blk == cur - 1)
    causal_blk = blk * SLC_LEN <= qpos
    score = jnp.where(causal_blk, imp_t + FORCE_BONUS * forced.astype(F32), NEG)
    sub = lax.broadcasted_iota(jnp.int32, (8, 1), 0)
    groups = [score[8 * v:8 * v + 8] for v in range(n_slc // 8)]
    counts = [jnp.zeros((8, tq), F32) for _ in groups]
    for m in range(n_slc):
        row = score[m:m + 1]
        for v, sv in enumerate(groups):
            if v < m // 8:
                ahead = row > sv
            elif v > m // 8:
                ahead = row >= sv
            else:
                ahead = (row > sv) | ((row == sv) & (sub > m % 8))
            counts[v] = counts[v] + jnp.where(ahead, 1.0, 0.0)
    rank = jnp.concatenate(counts, axis=0)
    sel_bias_t = jnp.where(rank < SLC_TOP_N, 0.0, NEG)
    sel_bias = jnp.concatenate([sel_bias_t, jnp.zeros((LANES - n_slc, tq), F32)], axis=0).T
    sel_bias = sel_bias[:, :HEAD_DIM].astype(qa_ref.dtype)
    for h in heads:
        qa_ref[hrows(h), :] = jnp.concatenate([q_heads[h], sel_bias], axis=1)

    tok_bias = jnp.where(lax.broadcasted_iota(jnp.int32, (1, kb), 1)
                         <= lax.broadcasted_iota(jnp.int32, (tq, 1), 0), 0.0, NEG)

    def scores(start):
        k_t = ks_ref[0, 0, pl.ds(start, kb), :]
        return lax.dot_general(qa_ref[...], k_t, NT_DIMS, preferred_element_type=F32)

    def softmax(s_all, slot, own_keys):
        for r0 in range(0, n_rows, rc):
            rows = slice(r0, r0 + rc)
            s = s_all[rows]
            if own_keys:
                s = s + tok_bias[r0 % tq:r0 % tq + rc]
                m_new = jnp.broadcast_to(jnp.max(s, axis=1, keepdims=True), (rc, LANES))
            else:
                m_old = m_ref[rows, :]
                m_new = jnp.maximum(m_old, jnp.max(s, axis=1, keepdims=True))
                alpha_ref[slot, rows, :] = jnp.exp2(m_old - m_new)
            m_ref[rows, :] = m_new
            m_wide = jnp.concatenate([m_new] * (kb // LANES), axis=1)
            p_ref[slot, rows, :] = jnp.exp2(s - m_wide).astype(p_ref.dtype)

    def values(start, slot):
        v_t = vs_ref[0, 0, pl.ds(start, kb), :]
        pv = jnp.dot(p_ref[slot], v_t, preferred_element_type=F32)
        for r0 in range(0, n_rows, rc):
            rows = slice(r0, r0 + rc)
            acc_ref[rows, :] = alpha_ref[slot, rows, :] * acc_ref[rows, :] + pv[rows]

    chunk_start = lambda c: pl.multiple_of(c * kb, kb)
    owed_start = lambda c: pl.multiple_of(jnp.where(c == 0, q0, (c - 1) * kb), kb)

    acc_ref[...] = jnp.zeros_like(acc_ref)
    alpha_ref[1] = jnp.zeros(alpha_ref.shape[1:], F32)
    softmax(scores(q0), 1, True)

    def pair_body(t, carry):
        c = 2 * t
        s_a = scores(chunk_start(c))
        values(owed_start(c), 1)
        softmax(s_a, 0, False)
        s_b = scores(chunk_start(c + 1))
        values(chunk_start(c), 0)
        softmax(s_b, 1, False)
        return carry

    lax.fori_loop(0, i // 2, pair_body, 0)

    @pl.when(i % 2 == 1)
    def _():
        s_a = scores(chunk_start(i - 1))
        values(owed_start(i - 1), 1)
        softmax(s_a, 0, False)
        values(chunk_start(i - 1), 0)

    @pl.when(i % 2 == 0)
    def _():
        values(owed_start(i), 1)

    outs = []
    for h in heads:
        o_h = acc_ref[hrows(h), :]
        o_sel = (gs_ref[hrows(h), :] * o_h) * (1.0 / o_h[:, HEAD_DIM:HEAD_DIM + 1])
        outs.append(((oc_ref[hrows(h), :] + o_sel) + ow_ref[hrows(h), :])[:, :HEAD_DIM])
    for hp in range(NSA_HPG // 2):
        pair = jnp.concatenate(outs[2 * hp:2 * hp + 2], axis=1)
        o_ref[0, :, hp * LANES:(hp + 1) * LANES] = pair.astype(o_ref.dtype)


def _nsa_attention(q3, kct, vc, ov, ksa, vsa, kwa, vwa, gt, tq=256, wg=4, rc=64):
    b, s, _ = q3.shape
    G = NSA_KV_GROUPS
    kb = tq
    n_slc = s // SLC_LEN
    nq = s // tq
    n_cmp_rows = kct.shape[3]
    gw = NSA_HPG * HEAD_DIM
    assert n_slc <= HEAD_DIM and tq % SLC_LEN == 0 and WINDOW % tq == 0 and s >= WINDOW + tq
    assert n_cmp_rows <= kb and n_slc % 8 == 0 and tq % rc == 0
    kv_spec = pl.BlockSpec((1, 1, s, LANES), lambda bi, g, i: (g, bi, 0, 0))
    n_gates = NSA_GATES_PER_GROUP
    gsel = np.zeros((2 * LANES, n_gates * LANES), np.float32)
    for c in range(n_gates):
        gsel[[c, LANES + c], c * LANES:(c + 1) * LANES] = 1.0
    gsel = jnp.asarray(gsel, BF16)
    return pl.pallas_call(
        functools.partial(_nsa_attn_kernel, tq=tq, kb=kb, n_slc=n_slc, wg=wg, rc=rc),
        grid=(b, G, nq),
        in_specs=[pl.BlockSpec((1, tq, gw), lambda bi, g, i: (bi, i, g)),
                  pl.BlockSpec((1, 1, LANES, n_cmp_rows), lambda bi, g, i: (bi, g, 0, 0)),
                  pl.BlockSpec((1, 1, n_cmp_rows, LANES), lambda bi, g, i: (bi, g, 0, 0)),
                  _const_spec(ov.shape),
                  kv_spec, kv_spec, kv_spec,
                  pl.BlockSpec((1, 1, s, 2 * LANES), lambda bi, g, i: (g, bi, 0, 0)),
                  pl.BlockSpec((1, tq, LANES), lambda bi, g, i: (g, bi * nq + i, 0)),
                  _const_spec(gsel.shape)],
        out_specs=pl.BlockSpec((1, tq, gw), lambda bi, g, i: (bi, i, g)),
        out_shape=jax.ShapeDtypeStruct((b, s, D_MODEL), BF16),
        scratch_shapes=[pltpu.VMEM((NSA_HPG * tq, LANES), BF16),
                        pltpu.VMEM((NSA_HPG * tq, LANES), F32),
                        pltpu.VMEM((tq, n_cmp_rows), F32),
                        pltpu.VMEM((NSA_HPG * tq, LANES), F32),
                        pltpu.VMEM((NSA_HPG * tq, LANES), F32),
                        pltpu.VMEM((2, NSA_HPG * tq, LANES), F32),
                        pltpu.VMEM((2, NSA_HPG * tq, kb), BF16),
                        pltpu.VMEM((NSA_HPG * tq, WINDOW + tq), BF16),
                        pltpu.VMEM((NSA_HPG * tq, LANES), F32),
                        pltpu.VMEM((NSA_HPG * tq, LANES), F32)],
        compiler_params=_params(("arbitrary", "arbitrary", "arbitrary")),
        name="nsa_attention",
    )(q3, kct, vc, ov, ksa, vsa, kwa, vwa, gt, gsel)


def _nsa_layer_attention(hx2, norm_g, positions, w_in, pe_k, pe_v, w1k, w2k, w1v, w2v, b, s):
    t = b * s
    G = NSA_KV_GROUPS
    w_pad = jnp.pad(w_in, ((0, 0), (0, NSA_W_COLS - w_in.shape[1]))).astype(BF16)
    inv = ROPE_THETA ** (-jnp.arange(HALF, dtype=F32) / HALF)
    inv2 = jnp.tile(inv, LANES // HALF)[None, :]
    sgn2 = jnp.tile(jnp.concatenate([-jnp.ones(HALF, F32), jnp.ones(HALF, F32)]), LANES // HEAD_DIM)[None, :]
    q, kc_tok, vc_tok, ksa, vsa, kwa, vwa, gt = _nsa_in_proj(
        hx2, norm_g, w_pad, positions.reshape(t, 1), inv2, sgn2, s)

    nrow = s // CMP_STRIDE
    wide = CMP_STRIDE * HEAD_DIM
    k16 = kc_tok.reshape(G, b, nrow, wide)
    v16 = vc_tok.reshape(G, b, nrow, wide)
    pek = jnp.broadcast_to(pe_k.reshape(1, CMP_LEN * HEAD_DIM), (8, CMP_LEN * HEAD_DIM))
    pev = jnp.broadcast_to(pe_v.reshape(1, CMP_LEN * HEAD_DIM), (8, CMP_LEN * HEAD_DIM))
    w2k_rot = jnp.concatenate([-w2k[:, HALF:], w2k[:, :HALF]], axis=1)
    end_idx = jnp.minimum(jnp.arange(nrow) * CMP_STRIDE + CMP_LEN - 1, s - 1)
    pos_cmp = positions[:, end_idx][:, :, None]
    inv64 = jnp.tile(inv, 2)[None, :]
    kct, vc = _nsa_compress(k16, v16, pek, pev, w1k, w2k, w2k_rot, w1v, w2v, pos_cmp, inv64)

    n_slc = s // SLC_LEN
    c0 = np.arange(nrow)[:, None] * CMP_STRIDE
    s0 = np.arange(n_slc)[None, :] * SLC_LEN
    ov = np.clip(np.minimum(c0 + CMP_LEN, s0 + SLC_LEN) - np.maximum(c0, s0), 0, None) / CMP_LEN
    ov[(s - CMP_LEN) // CMP_STRIDE + 1:, :] = 0.0
    ov = np.pad(ov, ((0, 0), (0, LANES - n_slc)))
    ov = jnp.asarray(np.concatenate([ov, ov], axis=0), BF16)

    aug4 = lambda a: a.reshape(G, b, s, a.shape[-1])
    return _nsa_attention(q.reshape(b, s, D_MODEL), kct, vc, ov, aug4(ksa), aug4(vsa),
                          aug4(kwa), aug4(vwa), gt)


def kernel(x, positions, norm_mix, sba_w_in, sba_w_out, nsa_w_in, nsa_cmp_pos_k, nsa_cmp_pos_v,
           nsa_cmp_k_w1, nsa_cmp_k_w2, nsa_cmp_v_w1, nsa_cmp_v_w2, nsa_w_out, norm_ffn,
           ffn_w_up, ffn_conv_w, ffn_conv_b, ffn_w_down, norm_final):
    b, s, d = x.shape
    t = b * s
    depth = norm_mix.shape[0]
    x2 = x.reshape(t, d)
    g_final = norm_final.reshape(1, d)
    for layer in range(depth):
        j = layer // 2
        g_mix = norm_mix[layer].reshape(1, d)
        if layer % 2 == 0:
            qkv = _sba_in_proj(x2, g_mix, sba_w_in[j].astype(BF16))
            o = _sba_attention(qkv.reshape(b, s, 3 * d), b, s)
            w_out = sba_w_out[j]
        else:
            o = _nsa_layer_attention(x2, g_mix, positions, nsa_w_in[j], nsa_cmp_pos_k[j],
                                     nsa_cmp_pos_v[j], nsa_cmp_k_w1[j], nsa_cmp_k_w2[j],
                                     nsa_cmp_v_w1[j], nsa_cmp_v_w2[j], b, s)
            w_out = nsa_w_out[j]
        x2 = _conv_ffn(x2, o.reshape(t, d), w_out.astype(BF16),
                       norm_ffn[layer].reshape(1, d), ffn_w_up[layer].astype(BF16),
                       ffn_conv_w[layer], ffn_conv_b[layer].reshape(1, -1),
                       ffn_w_down[layer].astype(BF16), g_final, s,
                       final_norm=(layer == depth - 1))
    return x2.reshape(b, s, d)
```

```python
import functools

import numpy as np
import jax
import jax.numpy as jnp
from jax import lax
from jax.experimental import pallas as pl
from jax.experimental.pallas import tpu as pltpu

D_MODEL = 1024
N_HEADS = 16
HEAD_DIM = 64
HALF = HEAD_DIM // 2
NSA_KV_GROUPS = 2
NSA_HPG = N_HEADS // NSA_KV_GROUPS
CMP_LEN = 32
CMP_STRIDE = 16
SLC_LEN = 64
SLC_SHIFT = SLC_LEN.bit_length() - 1
SLC_TOP_N = 16
WINDOW = 512
ROPE_THETA = 10000.0
D_FF = 2816
RMS_EPS = 1e-6
NEG = -1e30
FORCE_BONUS = 1e4
LOG2E = float(np.log2(np.e))
EXP2_CLAMP = 120.0
EXP2_UNDERFLOW = 160.0
Q_SCALE = LOG2E * HEAD_DIM ** -0.5

LANES = 128
VMEM_LIMIT = 56 * 1024 * 1024

F32 = jnp.float32
BF16 = jnp.bfloat16
HIGHEST = lax.Precision.HIGHEST
NT_DIMS = (((1,), (1,)), ((), ()))


def _params(semantics):
    return pltpu.CompilerParams(dimension_semantics=semantics, vmem_limit_bytes=VMEM_LIMIT)


def _rmsnorm(x, g):
    return x * lax.rsqrt(jnp.mean(x * x, axis=-1, keepdims=True) + RMS_EPS) * g


def _const_spec(shape):
    return pl.BlockSpec(shape, lambda *_: (0,) * len(shape), pipeline_mode=pl.Buffered(1))


def _split_bf16(x):
    hi = x.astype(BF16)
    return hi, (x - hi.astype(F32)).astype(BF16)


def _sba_in_proj_kernel(x_ref, g_ref, w_ref, o_ref, *, n_chunk):
    hn = _rmsnorm(x_ref[...], g_ref[...]).astype(w_ref.dtype)
    n = w_ref.shape[1]
    for c in range(n // n_chunk):
        cols = slice(c * n_chunk, (c + 1) * n_chunk)
        y = jnp.dot(hn, w_ref[:, cols], preferred_element_type=F32)
        if c * n_chunk < D_MODEL:
            y = y * Q_SCALE
        o_ref[:, cols] = y.astype(o_ref.dtype)


def _sba_in_proj(x2, g, w, tm=512):
    t, d = x2.shape
    n = w.shape[1]
    return pl.pallas_call(
        functools.partial(_sba_in_proj_kernel, n_chunk=512),
        grid=(t // tm,),
        in_specs=[pl.BlockSpec((tm, d), lambda i: (i, 0)),
                  _const_spec((1, d)),
                  _const_spec((d, n))],
        out_specs=pl.BlockSpec((tm, n), lambda i: (i, 0)),
        out_shape=jax.ShapeDtypeStruct((t, n), BF16),
        compiler_params=_params(("arbitrary",)),
        name="sba_in_proj",
    )(x2, g, w)


def _sba_attn_kernel(q_ref, k_ref, v_ref, uu_ref, o_ref, acc_ref, car_ref, hl_ref, lb_ref, a_ref,
                     z_ref,
                     *, tq, kb, nsub, nwalk, rc):
    i = pl.program_id(2)
    kt = kb * nsub
    streams = range(q_ref.shape[2] // LANES)
    scols = lambda s: slice(s * LANES, (s + 1) * LANES)
    q = q_ref[0]
    lane = lax.broadcasted_iota(jnp.int32, (tq, LANES), 1)
    klane = lax.broadcasted_iota(jnp.int32, (kb, LANES), 1)
    acc_ref[...] = jnp.zeros_like(acc_ref)
    car_ref[...] = jnp.zeros_like(car_ref)
    ud = uu_ref[...]
    rel = (lax.broadcasted_iota(jnp.int32, (rc, 2 * kb), 0)
           - (lax.broadcasted_iota(jnp.int32, (rc, 2 * kb), 1) & (kb - 1)))

    def super_tile(base, n, diag, slot, owed, z_first):
        def visibility(j, r0):
            if not diag:
                return 2, None
            if j * kb >= r0 + rc - 1:
                return 0, None
            if j * kb + kb - 1 < r0:
                return 2, None
            return 1, rel > (j * kb - r0)

        def scores(s, start):
            k_t = k_ref[0, pl.ds(pl.multiple_of(start, kb), kb), scols(s)]
            k_bd = jnp.concatenate([jnp.where(klane < HEAD_DIM, k_t, jnp.zeros_like(k_t)),
                                    jnp.where(klane >= HEAD_DIM, k_t, jnp.zeros_like(k_t))], axis=0)
            return lax.dot_general(q[:, scols(s)], k_bd, NT_DIMS, preferred_element_type=F32)

        def log_terms(s, j, z):
            totals = []
            for r0 in range(0, tq, rc):
                rows = slice(r0, r0 + rc)
                kind, strict = visibility(j, r0)
                if kind == 0:
                    hl_ref[s, j, rows, :] = jnp.zeros((rc, 2 * kb), BF16)
                    totals.append(None)
                    continue
                zc = z[rows]
                nl = jnp.maximum(jnp.log2(1.0 + jnp.exp2(jnp.minimum(zc, EXP2_CLAMP))), zc)
                lb_ref[s, j, rows, :] = zc - nl
                if kind == 1:
                    nl = jnp.where(strict, nl, 0.0)
                hl_ref[s, j, rows, :] = nl.astype(BF16)
                totals.append([jnp.sum(nl[:, h * kb:(h + 1) * kb], axis=1, keepdims=True)
                               for h in range(2)])
            return jnp.dot(hl_ref[s, j], ud, preferred_element_type=F32), totals

        def weights(s, j, sums):
            cum, totals = sums
            for n, r0 in enumerate(range(0, tq, rc)):
                rows = slice(r0, r0 + rc)
                cols = slice(j * kb, (j + 1) * kb)
                kind, strict = visibility(j, r0)
                if kind == 0:
                    for h in range(2):
                        a_ref[s, slot, h, rows, cols] = jnp.zeros((rc, kb), BF16)
                    continue
                car = [car_ref[s, h, rows, :] for h in range(2)]
                a = jnp.exp2(lb_ref[s, j, rows, :] - cum[rows] - jnp.concatenate(car, axis=1))
                if kind == 1:
                    a = jnp.where(strict, a, 0.0)
                for h in range(2):
                    a_ref[s, slot, h, rows, cols] = a[:, h * kb:(h + 1) * kb].astype(BF16)
                    car_ref[s, h, rows, :] = car[h] + totals[n][h]

        z_next = ([scores(s, base + (n - 1) * kb) for s in streams] if z_first is None
                  else z_first)
        pending = None
        for j in reversed(range(n)):
            z = z_next
            if j > 0:
                z_next = [scores(s, base + (j - 1) * kb) for s in streams]
            if j == n - 1 and owed is not None:
                apply_weights(*owed, nwalk)
            sums = [log_terms(s, j, z[s]) for s in streams]
            if pending is not None:
                for s in streams:
                    weights(s, j + 1, pending[s])
            pending = sums
            if j == 0:
                for s in streams:
                    z_ref[s] = scores(s, jnp.maximum(base - kb, 0))
        for s in streams:
            weights(s, 0, pending[s])

    def apply_weights(slot, start, n):
        for s in streams:
            v_n = v_ref[0, pl.ds(pl.multiple_of(start, kb), n * kb), scols(s)]
            for h in range(2):
                acc_ref[s, h] += jnp.dot(a_ref[s, slot, h, :, 0:n * kb], v_n,
                                         preferred_element_type=F32)

    super_tile(i * kt, nsub, True, 0, None, None)
    apply_weights(0, i * kt, nsub)

    kw = nwalk * kb
    a_ref[:, 1, :, :, 0:kw] = jnp.zeros((len(streams), 2, tq, kw), BF16)

    def live():
        return (jnp.min(car_ref[...]) < EXP2_UNDERFLOW).astype(jnp.int32)

    def body(carry):
        t, _ = carry
        base = i * kt - (t + 1) * kw
        super_tile(base, nwalk, False, t % 2, ((t + 1) % 2, base + kw), [z_ref[s] for s in streams])
        return t + 1, live()

    n_walk = i * (kt // kw)
    t_end, _ = lax.while_loop(lambda c: (c[0] < n_walk) & (c[1] > 0), body, (jnp.int32(0), live()))
    apply_weights((t_end + 1) % 2, i * kt - t_end * kw, nwalk)
    for s in streams:
        o_ref[0, :, scols(s)] = jnp.where(lane < HEAD_DIM, acc_ref[s, 0],
                                          acc_ref[s, 1]).astype(o_ref.dtype)


def _sba_attention(qkv, b, s, tq=512, kb=LANES, nsub=4, nwalk=2, rc=64, ns=2):
    assert tq == kb * nsub and s % tq == 0 and tq % rc == 0 and nsub % nwalk == 0
    n_blocks = D_MODEL // (ns * LANES)
    wide = ns * LANES
    assert kb & (kb - 1) == 0 and D_MODEL % wide == 0
    jj = np.arange(kb)
    uu = jnp.asarray(np.kron(np.eye(2), (jj[:, None] > jj[None, :]).astype(np.float32)), BF16)
    return pl.pallas_call(
        functools.partial(_sba_attn_kernel, tq=tq, kb=kb, nsub=nsub, nwalk=nwalk, rc=rc),
        grid=(b, n_blocks, s // tq),
        in_specs=[pl.BlockSpec((1, tq, wide), lambda bi, p, i: (bi, i, p)),
                  pl.BlockSpec((1, s, wide), lambda bi, p, i: (bi, 0, n_blocks + p)),
                  pl.BlockSpec((1, s, wide), lambda bi, p, i: (bi, 0, 2 * n_blocks + p)),
                  _const_spec((2 * kb, 2 * kb))],
        out_specs=pl.BlockSpec((1, tq, wide), lambda bi, p, i: (bi, i, p)),
        out_shape=jax.ShapeDtypeStruct((b, s, D_MODEL), BF16),
        scratch_shapes=[pltpu.VMEM((ns, 2, tq, LANES), F32),
                        pltpu.VMEM((ns, 2, tq, LANES), F32),
                        pltpu.VMEM((ns, nsub, tq, 2 * kb), BF16),
                        pltpu.VMEM((ns, nsub, tq, 2 * kb), F32),
                        pltpu.VMEM((ns, 2, 2, tq, nsub * kb), BF16),
                        pltpu.VMEM((ns, tq, 2 * kb), F32)],
        compiler_params=_params(("arbitrary", "arbitrary", "arbitrary")),
        name="sba_attention",
    )(qkv, qkv, qkv, uu)


def _ffn_kernel(x_ref, o_ref, wo_ref, g_ref, wup_ref, cw_ref, cb_ref, wdn_ref, gf_ref, y_ref,
                carry_ref, sg_ref, sv_ref, act_ref, *, tm, fc, tiles_per_seq, final_norm):
    @pl.when(pl.program_id(0) % tiles_per_seq == 0)
    def _():
        carry_ref[...] = jnp.zeros_like(carry_ref)

    x = x_ref[...] + jnp.dot(o_ref[...], wo_ref[...], preferred_element_type=F32)
    hn = _rmsnorm(x, g_ref[...]).astype(wup_ref.dtype)

    def up(col0):
        return jnp.dot(hn, wup_ref[:, col0:col0 + fc], preferred_element_type=F32)

    def conv(u, col0, s_ref):
        cols = slice(col0, col0 + fc)
        s_ref[0:8, :] = carry_ref[:, cols]
        s_ref[8:tm + 8, :] = u
        carry_ref[:, cols] = u[tm - 8:tm, :]
        cw = cw_ref[:, cols]
        c = cb_ref[:, cols] + s_ref[6:tm + 6, :] * cw[0:1]
        c = c + s_ref[7:tm + 7, :] * cw[1:2]
        return c + u * cw[2:3]

    n_chunks = D_FF // fc
    u_next = (up(0), up(D_FF))
    for c in range(n_chunks):
        u_gate, u_val = u_next
        if c + 1 < n_chunks:
            u_next = (up((c + 1) * fc), up(D_FF + (c + 1) * fc))
        gate = conv(u_gate, c * fc, sg_ref)
        val = conv(u_val, D_FF + c * fc, sv_ref)
        act_ref[:, c * fc:(c + 1) * fc] = (gate * jax.nn.sigmoid(gate) * val).astype(act_ref.dtype)
    y = x + jnp.dot(act_ref[...], wdn_ref[...], preferred_element_type=F32)
    if final_norm:
        y = _rmsnorm(y, gf_ref[...])
    y_ref[...] = y


def _conv_ffn(x2, o2, w_out, g, w_up, conv_w, conv_b, w_down, g_final, s, final_norm,
              tm=256, fc=256):
    t, d = x2.shape
    f2 = w_up.shape[1]
    return pl.pallas_call(
        functools.partial(_ffn_kernel, tm=tm, fc=fc, tiles_per_seq=s // tm, final_norm=final_norm),
        grid=(t // tm,),
        in_specs=[pl.BlockSpec((tm, d), lambda i: (i, 0)),
                  pl.BlockSpec((tm, d), lambda i: (i, 0)),
                  _const_spec((d, d)),
                  _const_spec((1, d)),
                  _const_spec((d, f2)),
                  _const_spec((3, f2)),
                  _const_spec((1, f2)),
                  _const_spec((D_FF, d)),
                  _const_spec((1, d))],
        out_specs=pl.BlockSpec((tm, d), lambda i: (i, 0)),
        out_shape=jax.ShapeDtypeStruct((t, d), F32),
        scratch_shapes=[pltpu.VMEM((8, f2), F32),
                        pltpu.VMEM((tm + 8, fc), F32),
                        pltpu.VMEM((tm + 8, fc), F32),
                        pltpu.VMEM((tm, D_FF), BF16)],
        compiler_params=_params(("arbitrary",)),
        name="conv_ffn",
    )(x2, o2, w_out, g, w_up, conv_w, conv_b, w_down, g_final)


NSA_Q_COLS = D_MODEL
NSA_W_COLS = D_MODEL + 7 * LANES
NSA_GATES_PER_GROUP = 3 * NSA_HPG


def _swap_halves(y):
    lane = lax.broadcasted_iota(jnp.int32, y.shape, 1)
    first = (lane % HEAD_DIM) < HALF
    return jnp.where(first, pltpu.roll(y, LANES - HALF, 1), pltpu.roll(y, HALF, 1))


def _nsa_in_proj_kernel(x_ref, g_ref, w_ref, pos_ref, inv_ref, sgn_ref,
                        q_ref, kc_ref, vc_ref, ksa_ref, vsa_ref, kwa_ref, vwa_ref, gt_ref,
                        *, tm, seq):
    hn = _rmsnorm(x_ref[...], g_ref[...]).astype(w_ref.dtype)
    ang = pos_ref[...].astype(F32) * inv_ref[...]
    cos = jnp.cos(ang)
    sin = jnp.sin(ang) * sgn_ref[...]

    def rope(y):
        return y * cos + _swap_halves(y) * sin

    q_chunk = 4 * LANES
    for c in range(NSA_Q_COLS // q_chunk):
        y = jnp.dot(hn, w_ref[:, c * q_chunk:(c + 1) * q_chunk], preferred_element_type=F32)
        for l in range(q_chunk // LANES):
            yl = rope(y[:, l * LANES:(l + 1) * LANES]) * Q_SCALE
            q_ref[:, c * q_chunk + l * LANES:c * q_chunk + (l + 1) * LANES] = yl.astype(q_ref.dtype)
    y = jnp.dot(hn, w_ref[:, NSA_Q_COLS:], preferred_element_type=F32)
    part = lambda n: y[:, n * LANES:(n + 1) * LANES]

    tok = ((pl.program_id(0) % (seq // tm)) * tm
           + lax.broadcasted_iota(jnp.int32, (tm, HEAD_DIM), 0))
    lane = lax.broadcasted_iota(jnp.int32, (tm, HEAD_DIM), 1)
    blk_onehot = jnp.where((tok >> SLC_SHIFT) == lane, 1.0, 0.0)
    ones_col = jnp.where(lane == 0, 1.0, 0.0)
    zeros = jnp.zeros((tm, HEAD_DIM), F32)
    ks, vs, kw, vw = rope(part(2)), part(3), rope(part(4)), part(5)
    for g in range(NSA_KV_GROUPS):
        cols = slice(g * HEAD_DIM, (g + 1) * HEAD_DIM)
        kc_ref[g] = part(0)[:, cols]
        vc_ref[g] = part(1)[:, cols]
        ksa_ref[g] = jnp.concatenate([ks[:, cols], blk_onehot], axis=1).astype(ksa_ref.dtype)
        vsa_ref[g] = jnp.concatenate([vs[:, cols], ones_col], axis=1).astype(vsa_ref.dtype)
        kwa_ref[g] = jnp.concatenate([kw[:, cols], zeros], axis=1).astype(kwa_ref.dtype)
        vwa_ref[g] = jnp.concatenate([vw[:, cols], zeros, jnp.ones((tm, LANES), F32)],
                                     axis=1).astype(vwa_ref.dtype)
    gates = jax.nn.sigmoid(part(6))
    gt_ref[0] = gates
    gt_ref[1] = pltpu.roll(gates, LANES - NSA_GATES_PER_GROUP, 1)


def _nsa_in_proj(x2, g, w, pos2, inv2, sgn2, seq, tm=512):
    t, d = x2.shape
    G = NSA_KV_GROUPS
    tok_g = lambda dt: jax.ShapeDtypeStruct((G, t, HEAD_DIM), dt)
    aug_g = lambda dt: jax.ShapeDtypeStruct((G, t, LANES), dt)
    tok_spec = pl.BlockSpec((G, tm, HEAD_DIM), lambda i: (0, i, 0))
    aug_spec = pl.BlockSpec((G, tm, LANES), lambda i: (0, i, 0))
    return pl.pallas_call(
        functools.partial(_nsa_in_proj_kernel, tm=tm, seq=seq),
        grid=(t // tm,),
        in_specs=[pl.BlockSpec((tm, d), lambda i: (i, 0)),
                  _const_spec((1, d)),
                  _const_spec((d, NSA_W_COLS)),
                  pl.BlockSpec((tm, 1), lambda i: (i, 0)),
                  _const_spec((1, LANES)),
                  _const_spec((1, LANES))],
        out_specs=[pl.BlockSpec((tm, NSA_Q_COLS), lambda i: (i, 0)),
                   tok_spec, tok_spec, aug_spec, aug_spec, aug_spec,
                   pl.BlockSpec((G, tm, 2 * LANES), lambda i: (0, i, 0)), aug_spec],
        out_shape=[jax.ShapeDtypeStruct((t, NSA_Q_COLS), BF16),
                   tok_g(F32), tok_g(F32), aug_g(BF16), aug_g(BF16), aug_g(BF16),
                   jax.ShapeDtypeStruct((G, t, 2 * LANES), BF16), aug_g(F32)],
        compiler_params=_params(("arbitrary",)),
        name="nsa_in_proj",
    )(x2, g, w, pos2, inv2, sgn2)


def _nsa_compress_kernel(k16_ref, v16_ref, pek_ref, pev_ref, w1k_ref, w2k_ref, w2kr_ref,
                         w1v_ref, w2v_ref, pos_ref, inv_ref, kct_ref, vc_ref):
    half_w = CMP_STRIDE * HEAD_DIM
    nrow = k16_ref.shape[2]

    def hidden(x16_ref, pe_ref, w1_ref):
        x = x16_ref[0, 0]
        y1 = jnp.dot(x, w1_ref[:half_w, :], precision=HIGHEST, preferred_element_type=F32)
        y2 = jnp.dot(x, w1_ref[half_w:, :], precision=HIGHEST, preferred_element_type=F32)
        bias = jnp.dot(pe_ref[...], w1_ref[...], precision=HIGHEST, preferred_element_type=F32)
        return jax.nn.gelu(y1 + pltpu.roll(y2, nrow - 1, 0) + bias[0:1])

    hk = hidden(k16_ref, pek_ref, w1k_ref)
    kc = jnp.dot(hk, w2k_ref[...], precision=HIGHEST, preferred_element_type=F32)
    kc_rot = jnp.dot(hk, w2kr_ref[...], precision=HIGHEST, preferred_element_type=F32)
    ang = pos_ref[0].astype(F32) * inv_ref[...]
    kc = kc * jnp.cos(ang) + kc_rot * jnp.sin(ang)
    hi, lo = _split_bf16(kc)
    kct_ref[0, 0] = jnp.concatenate([hi.astype(F32), lo.astype(F32)], axis=1).T.astype(kct_ref.dtype)
    hv = hidden(v16_ref, pev_ref, w1v_ref)
    vc = jnp.dot(hv, w2v_ref[...], precision=HIGHEST, preferred_element_type=F32)
    vc_ref[0, 0] = jnp.concatenate([vc, jnp.zeros_like(vc)], axis=1).astype(vc_ref.dtype)


def _nsa_compress(k16, v16, pek, pev, w1k, w2k, w2kr, w1v, w2v, pos_cmp, inv64):
    G, b, nrow, wide = k16.shape
    x_spec = pl.BlockSpec((1, 1, nrow, wide), lambda bi, g: (g, bi, 0, 0))
    return pl.pallas_call(
        _nsa_compress_kernel,
        grid=(b, G),
        in_specs=[x_spec, x_spec,
                  _const_spec(pek.shape), _const_spec(pev.shape),
                  _const_spec(w1k.shape), _const_spec(w2k.shape), _const_spec(w2kr.shape),
                  _const_spec(w1v.shape), _const_spec(w2v.shape),
                  pl.BlockSpec((1, nrow, 1), lambda bi, g: (bi, 0, 0)),
                  _const_spec((1, HEAD_DIM))],
        out_specs=[pl.BlockSpec((1, 1, LANES, nrow), lambda bi, g: (bi, g, 0, 0)),
                   pl.BlockSpec((1, 1, nrow, LANES), lambda bi, g: (bi, g, 0, 0))],
        out_shape=[jax.ShapeDtypeStruct((b, G, LANES, nrow), BF16),
                   jax.ShapeDtypeStruct((b, G, nrow, LANES), BF16)],
        compiler_params=_params(("arbitrary", "arbitrary")),
        name="nsa_compress",
    )(k16, v16, pek, pev, w1k, w2k, w2kr, w1v, w2v, pos_cmp, inv64)


def _nsa_attn_kernel(q_ref, kct_ref, vc_ref, ov_ref, ks_ref, vs_ref, kw_ref, vw_ref, gt_ref,
                     gsel_ref, o_ref, qa_ref, oc_ref, psum_ref, m_ref, acc_ref, alpha_ref, p_ref, pw_ref,
                     ow_ref, gs_ref,
                     *, tq, kb, n_slc, wg, rc):
    i = pl.program_id(2)
    q0 = pl.multiple_of(i * tq, tq)
    n_rows = NSA_HPG * tq
    heads = range(NSA_HPG)
    hrows = lambda h: slice(h * tq, (h + 1) * tq)
    qrow = q0 + lax.broadcasted_iota(jnp.int32, (tq, 1), 0)
    q_heads = [q_ref[0, :, h * HEAD_DIM:(h + 1) * HEAD_DIM] for h in heads]
    for h in heads:
        qa_ref[hrows(h), :] = jnp.concatenate([q_heads[h], q_heads[h]], axis=1)

    n_cmp_rows = kct_ref.shape[3]
    s_all = jnp.dot(qa_ref[...], kct_ref[0, 0], preferred_element_type=F32)
    w_keys = WINDOW + tq
    w_start = pl.multiple_of(jnp.maximum(q0 - WINDOW, 0), tq)
    k_w = kw_ref[0, 0, pl.ds(w_start, w_keys), :]
    s_win = [lax.dot_general(qa_ref[h0 * tq:(h0 + wg) * tq, :], k_w, NT_DIMS,
                             preferred_element_type=F32) for h0 in range(0, NSA_HPG, wg)]

    cmp_end = CMP_STRIDE * lax.broadcasted_iota(jnp.int32, (1, n_cmp_rows), 1) + (CMP_LEN - 1)
    c_bias = jnp.where(cmp_end <= qrow, 0.0, NEG)
    row_valid = jnp.where(qrow >= CMP_LEN - 1, 1.0, 0.0)
    for r0 in range(0, n_rows, rc):
        rows = slice(r0, r0 + rc)
        local = slice(r0 % tq, r0 % tq + rc)
        s = s_all[rows] + c_bias[local]
        e = jnp.exp2(s - jnp.max(s, axis=1, keepdims=True))
        p = e * (row_valid[local] / jnp.sum(e, axis=1, keepdims=True))
        if r0 < tq:
            psum_ref[local, :] = p
        else:
            psum_ref[local, :] += p
        p_ref[0, rows, 0:n_cmp_rows] = p.astype(p_ref.dtype)
    g_wide = jnp.dot(jnp.concatenate(_split_bf16(gt_ref[0]), axis=1), gsel_ref[...],
                     preferred_element_type=F32)
    gate = lambda h, branch: g_wide[:, (3 * h + branch) * LANES:(3 * h + branch + 1) * LANES]
    o_cmp = jnp.dot(p_ref[0, :, 0:n_cmp_rows], vc_ref[0, 0], preferred_element_type=F32)
    for h in heads:
        oc_ref[hrows(h), :] = gate(h, 0) * o_cmp[hrows(h)]
        gs_ref[hrows(h), :] = gate(h, 1)
    imp = jnp.dot(jnp.concatenate(_split_bf16(psum_ref[...]), axis=1), ov_ref[...],
                  preferred_element_type=F32)
    imp_t = imp.T[:n_slc]

    w_diff = qrow - (w_start + lax.broadcasted_iota(jnp.int32, (1, w_keys), 1))
    w_bias = jnp.where((w_diff >= 0) & (w_diff < WINDOW), 0.0, NEG)
    for r0 in range(0, n_rows, rc):
        s = s_win[r0 // (wg * tq)][r0 % (wg * tq):r0 % (wg * tq) + rc] + w_bias[r0 % tq:r0 % tq + rc]
        pw_ref[r0:r0 + rc, :] = jnp.exp2(s - jnp.max(s, axis=1, keepdims=True)).astype(pw_ref.dtype)
    o_win = jnp.dot(pw_ref[...], vw_ref[0, 0, pl.ds(w_start, w_keys), :],
                    preferred_element_type=F32)
    for h in heads:
        o_h = o_win[hrows(h)]
        ow_ref[hrows(h), :] = (gate(h, 2) * o_h[:, :LANES]) * (1.0 / o_h[:, LANES:])

    qpos = q0 + lax.broadcasted_iota(jnp.int32, (1, tq), 1)
    blk = lax.broadcasted_iota(jnp.int32, (n_slc, 1), 0)
    cur = qpos >> SLC_SHIFT
    forced = (blk == 0) | (blk == cur) | (blk == cur - 1)
    causal_blk = blk * SLC_LEN <= qpos
    score = jnp.where(causal_blk, imp_t + FORCE_BONUS * forced.astype(F32), NEG)
    sub = lax.broadcasted_iota(jnp.int32, (8, 1), 0)
    groups = [score[8 * v:8 * v + 8] for v in range(n_slc // 8)]
    counts = [jnp.zeros((8, tq), F32) for _ in groups]
    for m in range(n_slc):
        row = score[m:m + 1]
        for v, sv in enumerate(groups):
            if v < m // 8:
                ahead = row > sv
            elif v > m // 8:
                ahead = row >= sv
            else:
                ahead = (row > sv) | ((row == sv) & (sub > m % 8))
            counts[v] = counts[v] + jnp.where(ahead, 1.0, 0.0)
    rank = jnp.concatenate(counts, axis=0)
    sel_bias_t = jnp.where(rank < SLC_TOP_N, 0.0, NEG)
    sel_bias = jnp.concatenate([sel_bias_t, jnp.zeros((LANES - n_slc, tq), F32)], axis=0).T
    sel_bias = sel_bias[:, :HEAD_DIM].astype(qa_ref.dtype)
    for h in heads:
        qa_ref[hrows(h), :] = jnp.concatenate([q_heads[h], sel_bias], axis=1)

    tok_bias = jnp.where(lax.broadcasted_iota(jnp.int32, (1, kb), 1)
                         <= lax.broadcasted_iota(jnp.int32, (tq, 1), 0), 0.0, NEG)

    def scores(start):
        k_t = ks_ref[0, 0, pl.ds(start, kb), :]
        return lax.dot_general(qa_ref[...], k_t, NT_DIMS, preferred_element_type=F32)

    def softmax(s_all, slot, own_keys):
        for r0 in range(0, n_rows, rc):
            rows = slice(r0, r0 + rc)
            s = s_all[rows]
            if own_keys:
                s = s + tok_bias[r0 % tq:r0 % tq + rc]
                m_new = jnp.broadcast_to(jnp.max(s, axis=1, keepdims=True), (rc, LANES))
            else:
                m_old = m_ref[rows, :]
                m_new = jnp.maximum(m_old, jnp.max(s, axis=1, keepdims=True))
                alpha_ref[slot, rows, :] = jnp.exp2(m_old - m_new)
            m_ref[rows, :] = m_new
            m_wide = jnp.concatenate([m_new] * (kb // LANES), axis=1)
            p_ref[slot, rows, :] = jnp.exp2(s - m_wide).astype(p_ref.dtype)

    def values(start, slot):
        v_t = vs_ref[0, 0, pl.ds(start, kb), :]
        pv = jnp.dot(p_ref[slot], v_t, preferred_element_type=F32)
        for r0 in range(0, n_rows, rc):
            rows = slice(r0, r0 + rc)
            acc_ref[rows, :] = alpha_ref[slot, rows, :] * acc_ref[rows, :] + pv[rows]

    chunk_start = lambda c: pl.multiple_of(c * kb, kb)
    owed_start = lambda c: pl.multiple_of(jnp.where(c == 0, q0, (c - 1) * kb), kb)

    acc_ref[...] = jnp.zeros_like(acc_ref)
    alpha_ref[1] = jnp.zeros(alpha_ref.shape[1:], F32)
    softmax(scores(q0), 1, True)

    def pair_body(t, carry):
        c = 2 * t
        s_a = scores(chunk_start(c))
        values(owed_start(c), 1)
        softmax(s_a, 0, False)
        s_b = scores(chunk_start(c + 1))
        values(chunk_start(c), 0)
        softmax(s_b, 1, False)
        return carry

    lax.fori_loop(0, i // 2, pair_body, 0)

    @pl.when(i % 2 == 1)
    def _():
        s_a = scores(chunk_start(i - 1))
        values(owed_start(i - 1), 1)
        softmax(s_a, 0, False)
        values(chunk_start(i - 1), 0)

    @pl.when(i % 2 == 0)
    def _():
        values(owed_start(i), 1)

    outs = []
    for h in heads:
        o_h = acc_ref[hrows(h), :]
        o_sel = (gs_ref[hrows(h), :] * o_h) * (1.0 / o_h[:, HEAD_DIM:HEAD_DIM + 1])
        outs.append(((oc_ref[hrows(h), :] + o_sel) + ow_ref[hrows(h), :])[:, :HEAD_DIM])
    for hp in range(NSA_HPG // 2):
        pair = jnp.concatenate(outs[2 * hp:2 * hp + 2], axis=1)
        o_ref[0, :, hp * LANES:(hp + 1) * LANES] = pair.astype(o_ref.dtype)


def _nsa_attention(q3, kct, vc, ov, ksa, vsa, kwa, vwa, gt, tq=256, wg=4, rc=64):
    b, s, _ = q3.shape
    G = NSA_KV_GROUPS
    kb = tq
    n_slc = s // SLC_LEN
    nq = s // tq
    n_cmp_rows = kct.shape[3]
    gw = NSA_HPG * HEAD_DIM
    assert n_slc <= HEAD_DIM and tq % SLC_LEN == 0 and WINDOW % tq == 0 and s >= WINDOW + tq
    assert n_cmp_rows <= kb and n_slc % 8 == 0 and tq % rc == 0
    kv_spec = pl.BlockSpec((1, 1, s, LANES), lambda bi, g, i: (g, bi, 0, 0))
    n_gates = NSA_GATES_PER_GROUP
    gsel = np.zeros((2 * LANES, n_gates * LANES), np.float32)
    for c in range(n_gates):
        gsel[[c, LANES + c], c * LANES:(c + 1) * LANES] = 1.0
    gsel = jnp.asarray(gsel, BF16)
    return pl.pallas_call(
        functools.partial(_nsa_attn_kernel, tq=tq, kb=kb, n_slc=n_slc, wg=wg, rc=rc),
        grid=(b, G, nq),
        in_specs=[pl.BlockSpec((1, tq, gw), lambda bi, g, i: (bi, i, g)),
                  pl.BlockSpec((1, 1, LANES, n_cmp_rows), lambda bi, g, i: (bi, g, 0, 0)),
                  pl.BlockSpec((1, 1, n_cmp_rows, LANES), lambda bi, g, i: (bi, g, 0, 0)),
                  _const_spec(ov.shape),
                  kv_spec, kv_spec, kv_spec,
                  pl.BlockSpec((1, 1, s, 2 * LANES), lambda bi, g, i: (g, bi, 0, 0)),
                  pl.BlockSpec((1, tq, LANES), lambda bi, g, i: (g, bi * nq + i, 0)),
                  _const_spec(gsel.shape)],
        out_specs=pl.BlockSpec((1, tq, gw), lambda bi, g, i: (bi, i, g)),
        out_shape=jax.ShapeDtypeStruct((b, s, D_MODEL), BF16),
        scratch_shapes=[pltpu.VMEM((NSA_HPG * tq, LANES), BF16),
                        pltpu.VMEM((NSA_HPG * tq, LANES), F32),
                        pltpu.VMEM((tq, n_cmp_rows), F32),
                        pltpu.VMEM((NSA_HPG * tq, LANES), F32),
                        pltpu.VMEM((NSA_HPG * tq, LANES), F32),
                        pltpu.VMEM((2, NSA_HPG * tq, LANES), F32),
                        pltpu.VMEM((2, NSA_HPG * tq, kb), BF16),
                        pltpu.VMEM((NSA_HPG * tq, WINDOW + tq), BF16),
                        pltpu.VMEM((NSA_HPG * tq, LANES), F32),
                        pltpu.VMEM((NSA_HPG * tq, LANES), F32)],
        compiler_params=_params(("arbitrary", "arbitrary", "arbitrary")),
        name="nsa_attention",
    )(q3, kct, vc, ov, ksa, vsa, kwa, vwa, gt, gsel)


def _nsa_layer_attention(hx2, norm_g, positions, w_in, pe_k, pe_v, w1k, w2k, w1v, w2v, b, s):
    t = b * s
    G = NSA_KV_GROUPS
    w_pad = jnp.pad(w_in, ((0, 0), (0, NSA_W_COLS - w_in.shape[1]))).astype(BF16)
    inv = ROPE_THETA ** (-jnp.arange(HALF, dtype=F32) / HALF)
    inv2 = jnp.tile(inv, LANES // HALF)[None, :]
    sgn2 = jnp.tile(jnp.concatenate([-jnp.ones(HALF, F32), jnp.ones(HALF, F32)]), LANES // HEAD_DIM)[None, :]
    q, kc_tok, vc_tok, ksa, vsa, kwa, vwa, gt = _nsa_in_proj(
        hx2, norm_g, w_pad, positions.reshape(t, 1), inv2, sgn2, s)

    nrow = s // CMP_STRIDE
    wide = CMP_STRIDE * HEAD_DIM
    k16 = kc_tok.reshape(G, b, nrow, wide)
    v16 = vc_tok.reshape(G, b, nrow, wide)
    pek = jnp.broadcast_to(pe_k.reshape(1, CMP_LEN * HEAD_DIM), (8, CMP_LEN * HEAD_DIM))
    pev = jnp.broadcast_to(pe_v.reshape(1, CMP_LEN * HEAD_DIM), (8, CMP_LEN * HEAD_DIM))
    w2k_rot = jnp.concatenate([-w2k[:, HALF:], w2k[:, :HALF]], axis=1)
    end_idx = jnp.minimum(jnp.arange(nrow) * CMP_STRIDE + CMP_LEN - 1, s - 1)
    pos_cmp = positions[:, end_idx][:, :, None]
    inv64 = jnp.tile(inv, 2)[None, :]
    kct, vc = _nsa_compress(k16, v16, pek, pev, w1k, w2k, w2k_rot, w1v, w2v, pos_cmp, inv64)

    n_slc = s // SLC_LEN
    c0 = np.arange(nrow)[:, None] * CMP_STRIDE
    s0 = np.arange(n_slc)[None, :] * SLC_LEN
    ov = np.clip(np.minimum(c0 + CMP_LEN, s0 + SLC_LEN) - np.maximum(c0, s0), 0, None) / CMP_LEN
    ov[(s - CMP_LEN) // CMP_STRIDE + 1:, :] = 0.0
    ov = np.pad(ov, ((0, 0), (0, LANES - n_slc)))
    ov = jnp.asarray(np.concatenate([ov, ov], axis=0), BF16)

    aug4 = lambda a: a.reshape(G, b, s, a.shape[-1])
    return _nsa_attention(q.reshape(b, s, D_MODEL), kct, vc, ov, aug4(ksa), aug4(vsa),
                          aug4(kwa), aug4(vwa), gt)


def kernel(x, positions, norm_mix, sba_w_in, sba_w_out, nsa_w_in, nsa_cmp_pos_k, nsa_cmp_pos_v,
           nsa_cmp_k_w1, nsa_cmp_k_w2, nsa_cmp_v_w1, nsa_cmp_v_w2, nsa_w_out, norm_ffn,
           ffn_w_up, ffn_conv_w, ffn_conv_b, ffn_w_down, norm_final):
    b, s, d = x.shape
    t = b * s
    depth = norm_mix.shape[0]
    x2 = x.reshape(t, d)
    g_final = norm_final.reshape(1, d)
    for layer in range(depth):
        j = layer // 2
        g_mix = norm_mix[layer].reshape(1, d)
        if layer % 2 == 0:
            qkv = _sba_in_proj(x2, g_mix, sba_w_in[j].astype(BF16))
            o = _sba_attention(qkv.reshape(b, s, 3 * d), b, s)
            w_out = sba_w_out[j]
        else:
            o = _nsa_layer_attention(x2, g_mix, positions, nsa_w_in[j], nsa_cmp_pos_k[j],
                                     nsa_cmp_pos_v[j], nsa_cmp_k_w1[j], nsa_cmp_k_w2[j],
                                     nsa_cmp_v_w1[j], nsa_cmp_v_w2[j], b, s)
            w_out = nsa_w_out[j]
        x2 = _conv_ffn(x2, o.reshape(t, d), w_out.astype(BF16),
                       norm_ffn[layer].reshape(1, d), ffn_w_up[layer].astype(BF16),
                       ffn_conv_w[layer], ffn_conv_b[layer].reshape(1, -1),
                       ffn_w_down[layer].astype(BF16), g_final, s,
                       final_norm=(layer == depth - 1))
    return x2.reshape(b, s, d)
```

```python
import functools

import numpy as np
import jax
import jax.numpy as jnp
from jax import lax
from jax.experimental import pallas as pl
from jax.experimental.pallas import tpu as pltpu

D_MODEL = 1024
N_HEADS = 16
HEAD_DIM = 64
HALF = HEAD_DIM // 2
NSA_KV_GROUPS = 2
NSA_HPG = N_HEADS // NSA_KV_GROUPS
CMP_LEN = 32
CMP_STRIDE = 16
SLC_LEN = 64
SLC_SHIFT = SLC_LEN.bit_length() - 1
SLC_TOP_N = 16
WINDOW = 512
ROPE_THETA = 10000.0
D_FF = 2816
RMS_EPS = 1e-6
NEG = -1e30
FORCE_BONUS = 1e4
LOG2E = float(np.log2(np.e))
EXP2_CLAMP = 120.0
EXP2_UNDERFLOW = 160.0
Q_SCALE = LOG2E * HEAD_DIM ** -0.5

LANES = 128
VMEM_LIMIT = 56 * 1024 * 1024

F32 = jnp.float32
BF16 = jnp.bfloat16
HIGHEST = lax.Precision.HIGHEST
NT_DIMS = (((1,), (1,)), ((), ()))


def _params(semantics):
    return pltpu.CompilerParams(dimension_semantics=semantics, vmem_limit_bytes=VMEM_LIMIT)


def _rmsnorm(x, g):
    return x * lax.rsqrt(jnp.mean(x * x, axis=-1, keepdims=True) + RMS_EPS) * g


def _const_spec(shape):
    return pl.BlockSpec(shape, lambda *_: (0,) * len(shape), pipeline_mode=pl.Buffered(1))


def _split_bf16(x):
    hi = x.astype(BF16)
    return hi, (x - hi.astype(F32)).astype(BF16)


def _sba_in_proj_kernel(x_ref, g_ref, w_ref, o_ref, *, n_chunk):
    hn = _rmsnorm(x_ref[...], g_ref[...]).astype(w_ref.dtype)
    n = w_ref.shape[1]
    for c in range(n // n_chunk):
        cols = slice(c * n_chunk, (c + 1) * n_chunk)
        y = jnp.dot(hn, w_ref[:, cols], preferred_element_type=F32)
        if c * n_chunk < D_MODEL:
            y = y * Q_SCALE
        o_ref[:, cols] = y.astype(o_ref.dtype)


def _sba_in_proj(x2, g, w, tm=512):
    t, d = x2.shape
    n = w.shape[1]
    return pl.pallas_call(
        functools.partial(_sba_in_proj_kernel, n_chunk=512),
        grid=(t // tm,),
        in_specs=[pl.BlockSpec((tm, d), lambda i: (i, 0)),
                  _const_spec((1, d)),
                  _const_spec((d, n))],
        out_specs=pl.BlockSpec((tm, n), lambda i: (i, 0)),
        out_shape=jax.ShapeDtypeStruct((t, n), BF16),
        compiler_params=_params(("arbitrary",)),
        name="sba_in_proj",
    )(x2, g, w)


def _sba_attn_kernel(q_ref, k_ref, v_ref, uu_ref, o_ref, acc_ref, car_ref, hl_ref, lb_ref, a_ref,
                     z_ref,
                     *, tq, kb, nsub, nwalk, rc):
    i = pl.program_id(2)
    kt = kb * nsub
    streams = range(q_ref.shape[2] // LANES)
    scols = lambda s: slice(s * LANES, (s + 1) * LANES)
    q = q_ref[0]
    lane = lax.broadcasted_iota(jnp.int32, (tq, LANES), 1)
    klane = lax.broadcasted_iota(jnp.int32, (kb, LANES), 1)
    acc_ref[...] = jnp.zeros_like(acc_ref)
    car_ref[...] = jnp.zeros_like(car_ref)
    ud = uu_ref[...]
    rel = (lax.broadcasted_iota(jnp.int32, (rc, 2 * kb), 0)
           - (lax.broadcasted_iota(jnp.int32, (rc, 2 * kb), 1) & (kb - 1)))

    def super_tile(base, n, diag, slot, owed, z_first):
        def visibility(j, r0):
            if not diag:
                return 2, None
            if j * kb >= r0 + rc - 1:
                return 0, None
            if j * kb + kb - 1 < r0:
                return 2, None
            return 1, rel > (j * kb - r0)

        def scores(s, start):
            k_t = k_ref[0, pl.ds(pl.multiple_of(start, kb), kb), scols(s)]
            k_bd = jnp.concatenate([jnp.where(klane < HEAD_DIM, k_t, jnp.zeros_like(k_t)),
                                    jnp.where(klane >= HEAD_DIM, k_t, jnp.zeros_like(k_t))], axis=0)
            return lax.dot_general(q[:, scols(s)], k_bd, NT_DIMS, preferred_element_type=F32)

        def log_terms(s, j, z):
            totals = []
            for r0 in range(0, tq, rc):
                rows = slice(r0, r0 + rc)
                kind, strict = visibility(j, r0)
                if kind == 0:
                    hl_ref[s, j, rows, :] = jnp.zeros((rc, 2 * kb), BF16)
                    totals.append(None)
                    continue
                zc = z[rows]
                nl = jnp.maximum(jnp.log2(1.0 + jnp.exp2(jnp.minimum(zc, EXP2_CLAMP))), zc)
                lb_ref[s, j, rows, :] = zc - nl
                if kind == 1:
                    nl = jnp.where(strict, nl, 0.0)
                hl_ref[s, j, rows, :] = nl.astype(BF16)
                totals.append([jnp.sum(nl[:, h * kb:(h + 1) * kb], axis=1, keepdims=True)
                               for h in range(2)])
            return jnp.dot(hl_ref[s, j], ud, preferred_element_type=F32), totals

        def weights(s, j, sums):
            cum, totals = sums
            a_slot = slot + j // nwalk
            cols = slice((j % nwalk) * kb, (j % nwalk + 1) * kb)
            for c, r0 in enumerate(range(0, tq, rc)):
                rows = slice(r0, r0 + rc)
                kind, strict = visibility(j, r0)
                if kind == 0:
                    for h in range(2):
                        a_ref[s, a_slot, h, rows, cols] = jnp.zeros((rc, kb), BF16)
                    continue
                car = [car_ref[s, h, rows, :] for h in range(2)]
                a = jnp.exp2(lb_ref[s, j, rows, :] - cum[rows] - jnp.concatenate(car, axis=1))
                if kind == 1:
                    a = jnp.where(strict, a, 0.0)
                for h in range(2):
                    a_ref[s, a_slot, h, rows, cols] = a[:, h * kb:(h + 1) * kb].astype(BF16)
                    car_ref[s, h, rows, :] = car[h] + totals[c][h]

        z_next = ([scores(s, base + (n - 1) * kb) for s in streams] if z_first is None
                  else z_first)
        pending = None
        for j in reversed(range(n)):
            z = z_next
            if j > 0:
                z_next = [scores(s, base + (j - 1) * kb) for s in streams]
            if j == n - 1 and owed is not None:
                apply_weights(*owed)
            sums = [log_terms(s, j, z[s]) for s in streams]
            if pending is not None:
                for s in streams:
                    weights(s, j + 1, pending[s])
                if (j + 1) % nwalk == 0:
                    apply_weights(slot + (j + 1) // nwalk, base + (j + 1) * kb)
            pending = sums
            if j == 0:
                for s in streams:
                    z_ref[s] = scores(s, jnp.maximum(base - kb, 0))
        for s in streams:
            weights(s, 0, pending[s])

    kw = nwalk * kb

    def apply_weights(slot, start):
        for s in streams:
            v_n = v_ref[0, pl.ds(pl.multiple_of(start, kb), kw), scols(s)]
            for h in range(2):
                acc_ref[s, h] += jnp.dot(a_ref[s, slot, h], v_n, preferred_element_type=F32)

    OWN = 2
    super_tile(i * kt, nsub, True, OWN, None, None)

    def owed_after(t):
        return (jnp.where(t == 0, OWN, (t + 1) % 2),
                jnp.where(t == 0, i * kt, i * kt - t * kw))

    def live():
        return (jnp.min(car_ref[...]) < EXP2_UNDERFLOW).astype(jnp.int32)

    def body(carry):
        t, _ = carry
        super_tile(i * kt - (t + 1) * kw, nwalk, False, t % 2, owed_after(t),
                   [z_ref[s] for s in streams])
        return t + 1, live()

    n_walk = i * (kt // kw)
    t_end, _ = lax.while_loop(lambda c: (c[0] < n_walk) & (c[1] > 0), body, (jnp.int32(0), live()))
    apply_weights(*owed_after(t_end))
    for s in streams:
        o_ref[0, :, scols(s)] = jnp.where(lane < HEAD_DIM, acc_ref[s, 0],
                                          acc_ref[s, 1]).astype(o_ref.dtype)


def _sba_attention(qkv, b, s, tq=512, kb=LANES, nsub=4, nwalk=2, rc=64, ns=4):
    assert tq == kb * nsub and s % tq == 0 and tq % rc == 0 and nsub % nwalk == 0
    n_blocks = D_MODEL // (ns * LANES)
    wide = ns * LANES
    assert kb & (kb - 1) == 0 and D_MODEL % wide == 0
    jj = np.arange(kb)
    uu = jnp.asarray(np.kron(np.eye(2), (jj[:, None] > jj[None, :]).astype(np.float32)), BF16)
    return pl.pallas_call(
        functools.partial(_sba_attn_kernel, tq=tq, kb=kb, nsub=nsub, nwalk=nwalk, rc=rc),
        grid=(b, n_blocks, s // tq),
        in_specs=[pl.BlockSpec((1, tq, wide), lambda bi, p, i: (bi, i, p)),
                  pl.BlockSpec((1, s, wide), lambda bi, p, i: (bi, 0, n_blocks + p)),
                  pl.BlockSpec((1, s, wide), lambda bi, p, i: (bi, 0, 2 * n_blocks + p)),
                  _const_spec((2 * kb, 2 * kb))],
        out_specs=pl.BlockSpec((1, tq, wide), lambda bi, p, i: (bi, i, p)),
        out_shape=jax.ShapeDtypeStruct((b, s, D_MODEL), BF16),
        scratch_shapes=[pltpu.VMEM((ns, 2, tq, LANES), F32),
                        pltpu.VMEM((ns, 2, tq, LANES), F32),
                        pltpu.VMEM((ns, nsub, tq, 2 * kb), BF16),
                        pltpu.VMEM((ns, nsub, tq, 2 * kb), F32),
                        pltpu.VMEM((ns, 2 + nsub // nwalk, 2, tq, nwalk * kb), BF16),
                        pltpu.VMEM((ns, tq, 2 * kb), F32)],
        compiler_params=_params(("arbitrary", "arbitrary", "arbitrary")),
        name="sba_attention",
    )(qkv, qkv, qkv, uu)


def _ffn_kernel(x_ref, o_ref, wo_ref, g_ref, wup_ref, cw_ref, cb_ref, wdn_ref, gf_ref, y_ref,
                carry_ref, sg_ref, sv_ref, act_ref, *, tm, fc, tiles_per_seq, final_norm):
    @pl.when(pl.program_id(0) % tiles_per_seq == 0)
    def _():
        carry_ref[...] = jnp.zeros_like(carry_ref)

    x = x_ref[...] + jnp.dot(o_ref[...], wo_ref[...], preferred_element_type=F32)
    hn = _rmsnorm(x, g_ref[...]).astype(wup_ref.dtype)

    def up(col0):
        return jnp.dot(hn, wup_ref[:, col0:col0 + fc], preferred_element_type=F32)

    def conv(u, col0, s_ref):
        cols = slice(col0, col0 + fc)
        s_ref[0:8, :] = carry_ref[:, cols]
        s_ref[8:tm + 8, :] = u
        carry_ref[:, cols] = u[tm - 8:tm, :]
        cw = cw_ref[:, cols]
        c = cb_ref[:, cols] + s_ref[6:tm + 6, :] * cw[0:1]
        c = c + s_ref[7:tm + 7, :] * cw[1:2]
        return c + u * cw[2:3]

    n_chunks = D_FF // fc
    u_next = (up(0), up(D_FF))
    for c in range(n_chunks):
        u_gate, u_val = u_next
        if c + 1 < n_chunks:
            u_next = (up((c + 1) * fc), up(D_FF + (c + 1) * fc))
        gate = conv(u_gate, c * fc, sg_ref)
        val = conv(u_val, D_FF + c * fc, sv_ref)
        act_ref[:, c * fc:(c + 1) * fc] = (gate * jax.nn.sigmoid(gate) * val).astype(act_ref.dtype)
    y = x + jnp.dot(act_ref[...], wdn_ref[...], preferred_element_type=F32)
    if final_norm:
        y = _rmsnorm(y, gf_ref[...])
    y_ref[...] = y


def _conv_ffn(x2, o2, w_out, g, w_up, conv_w, conv_b, w_down, g_final, s, final_norm,
              tm=256, fc=256):
    t, d = x2.shape
    f2 = w_up.shape[1]
    return pl.pallas_call(
        functools.partial(_ffn_kernel, tm=tm, fc=fc, tiles_per_seq=s // tm, final_norm=final_norm),
        grid=(t // tm,),
        in_specs=[pl.BlockSpec((tm, d), lambda i: (i, 0)),
                  pl.BlockSpec((tm, d), lambda i: (i, 0)),
                  _const_spec((d, d)),
                  _const_spec((1, d)),
                  _const_spec((d, f2)),
                  _const_spec((3, f2)),
                  _const_spec((1, f2)),
                  _const_spec((D_FF, d)),
                  _const_spec((1, d))],
        out_specs=pl.BlockSpec((tm, d), lambda i: (i, 0)),
        out_shape=jax.ShapeDtypeStruct((t, d), F32),
        scratch_shapes=[pltpu.VMEM((8, f2), F32),
                        pltpu.VMEM((tm + 8, fc), F32),
                        pltpu.VMEM((tm + 8, fc), F32),
                        pltpu.VMEM((tm, D_FF), BF16)],
        compiler_params=_params(("arbitrary",)),
        name="conv_ffn",
    )(x2, o2, w_out, g, w_up, conv_w, conv_b, w_down, g_final)


NSA_Q_COLS = D_MODEL
NSA_W_COLS = D_MODEL + 7 * LANES
NSA_GATES_PER_GROUP = 3 * NSA_HPG


def _swap_halves(y):
    lane = lax.broadcasted_iota(jnp.int32, y.shape, 1)
    first = (lane % HEAD_DIM) < HALF
    return jnp.where(first, pltpu.roll(y, LANES - HALF, 1), pltpu.roll(y, HALF, 1))


def _nsa_in_proj_kernel(x_ref, g_ref, w_ref, pos_ref, inv_ref, sgn_ref,
                        q_ref, kc_ref, vc_ref, ksa_ref, vsa_ref, kwa_ref, vwa_ref, gt_ref,
                        *, tm, seq):
    hn = _rmsnorm(x_ref[...], g_ref[...]).astype(w_ref.dtype)
    ang = pos_ref[...].astype(F32) * inv_ref[...]
    cos = jnp.cos(ang)
    sin = jnp.sin(ang) * sgn_ref[...]

    def rope(y):
        return y * cos + _swap_halves(y) * sin

    q_chunk = 4 * LANES
    for c in range(NSA_Q_COLS // q_chunk):
        y = jnp.dot(hn, w_ref[:, c * q_chunk:(c + 1) * q_chunk], preferred_element_type=F32)
        for l in range(q_chunk // LANES):
            yl = rope(y[:, l * LANES:(l + 1) * LANES]) * Q_SCALE
            q_ref[:, c * q_chunk + l * LANES:c * q_chunk + (l + 1) * LANES] = yl.astype(q_ref.dtype)
    y = jnp.dot(hn, w_ref[:, NSA_Q_COLS:], preferred_element_type=F32)
    part = lambda n: y[:, n * LANES:(n + 1) * LANES]

    tok = ((pl.program_id(0) % (seq // tm)) * tm
           + lax.broadcasted_iota(jnp.int32, (tm, HEAD_DIM), 0))
    lane = lax.broadcasted_iota(jnp.int32, (tm, HEAD_DIM), 1)
    blk_onehot = jnp.where((tok >> SLC_SHIFT) == lane, 1.0, 0.0)
    ones_col = jnp.where(lane == 0, 1.0, 0.0)
    zeros = jnp.zeros((tm, HEAD_DIM), F32)
    ks, vs, kw, vw = rope(part(2)), part(3), rope(part(4)), part(5)
    for g in range(NSA_KV_GROUPS):
        cols = slice(g * HEAD_DIM, (g + 1) * HEAD_DIM)
        kc_ref[g] = part(0)[:, cols]
        vc_ref[g] = part(1)[:, cols]
        ksa_ref[g] = jnp.concatenate([ks[:, cols], blk_onehot], axis=1).astype(ksa_ref.dtype)
        vsa_ref[g] = jnp.concatenate([vs[:, cols], ones_col], axis=1).astype(vsa_ref.dtype)
        kwa_ref[g] = jnp.concatenate([kw[:, cols], zeros], axis=1).astype(kwa_ref.dtype)
        vwa_ref[g] = jnp.concatenate([vw[:, cols], zeros, jnp.ones((tm, LANES), F32)],
                                     axis=1).astype(vwa_ref.dtype)
    gates = jax.nn.sigmoid(part(6))
    gt_ref[0] = gates
    gt_ref[1] = pltpu.roll(gates, LANES - NSA_GATES_PER_GROUP, 1)


def _nsa_in_proj(x2, g, w, pos2, inv2, sgn2, seq, tm=512):
    t, d = x2.shape
    G = NSA_KV_GROUPS
    tok_g = lambda dt: jax.ShapeDtypeStruct((G, t, HEAD_DIM), dt)
    aug_g = lambda dt: jax.ShapeDtypeStruct((G, t, LANES), dt)
    tok_spec = pl.BlockSpec((G, tm, HEAD_DIM), lambda i: (0, i, 0))
    aug_spec = pl.BlockSpec((G, tm, LANES), lambda i: (0, i, 0))
    return pl.pallas_call(
        functools.partial(_nsa_in_proj_kernel, tm=tm, seq=seq),
        grid=(t // tm,),
        in_specs=[pl.BlockSpec((tm, d), lambda i: (i, 0)),
                  _const_spec((1, d)),
                  _const_spec((d, NSA_W_COLS)),
                  pl.BlockSpec((tm, 1), lambda i: (i, 0)),
                  _const_spec((1, LANES)),
                  _const_spec((1, LANES))],
        out_specs=[pl.BlockSpec((tm, NSA_Q_COLS), lambda i: (i, 0)),
                   tok_spec, tok_spec, aug_spec, aug_spec, aug_spec,
                   pl.BlockSpec((G, tm, 2 * LANES), lambda i: (0, i, 0)), aug_spec],
        out_shape=[jax.ShapeDtypeStruct((t, NSA_Q_COLS), BF16),
                   tok_g(F32), tok_g(F32), aug_g(BF16), aug_g(BF16), aug_g(BF16),
                   jax.ShapeDtypeStruct((G, t, 2 * LANES), BF16), aug_g(F32)],
        compiler_params=_params(("arbitrary",)),
        name="nsa_in_proj",
    )(x2, g, w, pos2, inv2, sgn2)


def _nsa_compress_kernel(k16_ref, v16_ref, pek_ref, pev_ref, w1k_ref, w2k_ref, w2kr_ref,
                         w1v_ref, w2v_ref, pos_ref, inv_ref, kct_ref, vc_ref):
    half_w = CMP_STRIDE * HEAD_DIM
    nrow = k16_ref.shape[2]

    def hidden(x16_ref, pe_ref, w1_ref):
        x = x16_ref[0, 0]
        y1 = jnp.dot(x, w1_ref[:half_w, :], precision=HIGHEST, preferred_element_type=F32)
        y2 = jnp.dot(x, w1_ref[half_w:, :], precision=HIGHEST, preferred_element_type=F32)
        bias = jnp.dot(pe_ref[...], w1_ref[...], precision=HIGHEST, preferred_element_type=F32)
        return jax.nn.gelu(y1 + pltpu.roll(y2, nrow - 1, 0) + bias[0:1])

    hk = hidden(k16_ref, pek_ref, w1k_ref)
    kc = jnp.dot(hk, w2k_ref[...], precision=HIGHEST, preferred_element_type=F32)
    kc_rot = jnp.dot(hk, w2kr_ref[...], precision=HIGHEST, preferred_element_type=F32)
    ang = pos_ref[0].astype(F32) * inv_ref[...]
    kc = kc * jnp.cos(ang) + kc_rot * jnp.sin(ang)
    hi, lo = _split_bf16(kc)
    kct_ref[0, 0] = jnp.concatenate([hi.astype(F32), lo.astype(F32)], axis=1).T.astype(kct_ref.dtype)
    hv = hidden(v16_ref, pev_ref, w1v_ref)
    vc = jnp.dot(hv, w2v_ref[...], precision=HIGHEST, preferred_element_type=F32)
    vc_ref[0, 0] = jnp.concatenate([vc, jnp.zeros_like(vc)], axis=1).astype(vc_ref.dtype)


def _nsa_compress(k16, v16, pek, pev, w1k, w2k, w2kr, w1v, w2v, pos_cmp, inv64):
    G, b, nrow, wide = k16.shape
    x_spec = pl.BlockSpec((1, 1, nrow, wide), lambda bi, g: (g, bi, 0, 0))
    return pl.pallas_call(
        _nsa_compress_kernel,
        grid=(b, G),
        in_specs=[x_spec, x_spec,
                  _const_spec(pek.shape), _const_spec(pev.shape),
                  _const_spec(w1k.shape), _const_spec(w2k.shape), _const_spec(w2kr.shape),
                  _const_spec(w1v.shape), _const_spec(w2v.shape),
                  pl.BlockSpec((1, nrow, 1), lambda bi, g: (bi, 0, 0)),
                  _const_spec((1, HEAD_DIM))],
        out_specs=[pl.BlockSpec((1, 1, LANES, nrow), lambda bi, g: (bi, g, 0, 0)),
                   pl.BlockSpec((1, 1, nrow, LANES), lambda bi, g: (bi, g, 0, 0))],
        out_shape=[jax.ShapeDtypeStruct((b, G, LANES, nrow), BF16),
                   jax.ShapeDtypeStruct((b, G, nrow, LANES), BF16)],
        compiler_params=_params(("arbitrary", "arbitrary")),
        name="nsa_compress",
    )(k16, v16, pek, pev, w1k, w2k, w2kr, w1v, w2v, pos_cmp, inv64)


def _nsa_attn_kernel(q_ref, kct_ref, vc_ref, ov_ref, ks_ref, vs_ref, kw_ref, vw_ref, gt_ref,
                     gsel_ref, o_ref, qa_ref, oc_ref, psum_ref, m_ref, acc_ref, alpha_ref, p_ref, pw_ref,
                     ow_ref, gs_ref,
                     *, tq, kb, n_slc, wg, rc):
    i = pl.program_id(2)
    q0 = pl.multiple_of(i * tq, tq)
    n_rows = NSA_HPG * tq
    heads = range(NSA_HPG)
    hrows = lambda h: slice(h * tq, (h + 1) * tq)
    qrow = q0 + lax.broadcasted_iota(jnp.int32, (tq, 1), 0)
    q_heads = [q_ref[0, :, h * HEAD_DIM:(h + 1) * HEAD_DIM] for h in heads]
    for h in heads:
        qa_ref[hrows(h), :] = jnp.concatenate([q_heads[h], q_heads[h]], axis=1)

    n_cmp_rows = kct_ref.shape[3]
    s_all = jnp.dot(qa_ref[...], kct_ref[0, 0], preferred_element_type=F32)
    w_keys = WINDOW + tq
    w_start = pl.multiple_of(jnp.maximum(q0 - WINDOW, 0), tq)
    k_w = kw_ref[0, 0, pl.ds(w_start, w_keys), :]
    s_win = [lax.dot_general(qa_ref[h0 * tq:(h0 + wg) * tq, :], k_w, NT_DIMS,
                             preferred_element_type=F32) for h0 in range(0, NSA_HPG, wg)]

    cmp_end = CMP_STRIDE * lax.broadcasted_iota(jnp.int32, (1, n_cmp_rows), 1) + (CMP_LEN - 1)
    c_bias = jnp.where(cmp_end <= qrow, 0.0, NEG)
    row_valid = jnp.where(qrow >= CMP_LEN - 1, 1.0, 0.0)
    for r0 in range(0, n_rows, rc):
        rows = slice(r0, r0 + rc)
        local = slice(r0 % tq, r0 % tq + rc)
        s = s_all[rows] + c_bias[local]
        e = jnp.exp2(s - jnp.max(s, axis=1, keepdims=True))
        p = e * (row_valid[local] / jnp.sum(e, axis=1, keepdims=True))
        if r0 < tq:
            psum_ref[local, :] = p
        else:
            psum_ref[local, :] += p
        p_ref[0, rows, 0:n_cmp_rows] = p.astype(p_ref.dtype)
    g_wide = jnp.dot(jnp.concatenate(_split_bf16(gt_ref[0]), axis=1), gsel_ref[...],
                     preferred_element_type=F32)
    gate = lambda h, branch: g_wide[:, (3 * h + branch) * LANES:(3 * h + branch + 1) * LANES]
    o_cmp = jnp.dot(p_ref[0, :, 0:n_cmp_rows], vc_ref[0, 0], preferred_element_type=F32)
    for h in heads:
        oc_ref[hrows(h), :] = gate(h, 0) * o_cmp[hrows(h)]
        gs_ref[hrows(h), :] = gate(h, 1)
    imp = jnp.dot(jnp.concatenate(_split_bf16(psum_ref[...]), axis=1), ov_ref[...],
                  preferred_element_type=F32)
    imp_t = imp.T[:n_slc]

    w_diff = qrow - (w_start + lax.broadcasted_iota(jnp.int32, (1, w_keys), 1))
    w_bias = jnp.where((w_diff >= 0) & (w_diff < WINDOW), 0.0, NEG)
    for r0 in range(0, n_rows, rc):
        s = s_win[r0 // (wg * tq)][r0 % (wg * tq):r0 % (wg * tq) + rc] + w_bias[r0 % tq:r0 % tq + rc]
        pw_ref[r0:r0 + rc, :] = jnp.exp2(s - jnp.max(s, axis=1, keepdims=True)).astype(pw_ref.dtype)
    o_win = jnp.dot(pw_ref[...], vw_ref[0, 0, pl.ds(w_start, w_keys), :],
                    preferred_element_type=F32)
    for h in heads:
        o_h = o_win[hrows(h)]
        ow_ref[hrows(h), :] = (gate(h, 2) * o_h[:, :LANES]) * (1.0 / o_h[:, LANES:])

    qpos = q0 + lax.broadcasted_iota(jnp.int32, (1, tq), 1)
    blk = lax.broadcasted_iota(jnp.int32, (n_slc, 1), 0)
    cur = qpos >> SLC_SHIFT
    forced = (blk == 0) | (blk == cur) | (blk == cur - 1)
    causal_blk = blk * SLC_LEN <= qpos
    score = jnp.where(causal_blk, imp_t + FORCE_BONUS * forced.astype(F32), NEG)
    sub = lax.broadcasted_iota(jnp.int32, (8, 1), 0)
    groups = [score[8 * v:8 * v + 8] for v in range(n_slc // 8)]
    counts = [jnp.zeros((8, tq), F32) for _ in groups]
    for m in range(n_slc):
        row = score[m:m + 1]
        for v, sv in enumerate(groups):
            if v < m // 8:
                ahead = row > sv
            elif v > m // 8:
                ahead = row >= sv
            else:
                ahead = (row > sv) | ((row == sv) & (sub > m % 8))
            counts[v] = counts[v] + jnp.where(ahead, 1.0, 0.0)
    rank = jnp.concatenate(counts, axis=0)
    sel_bias_t = jnp.where(rank < SLC_TOP_N, 0.0, NEG)
    sel_bias = jnp.concatenate([sel_bias_t, jnp.zeros((LANES - n_slc, tq), F32)], axis=0).T
    sel_bias = sel_bias[:, :HEAD_DIM].astype(qa_ref.dtype)
    for h in heads:
        qa_ref[hrows(h), :] = jnp.concatenate([q_heads[h], sel_bias], axis=1)

    tok_bias = jnp.where(lax.broadcasted_iota(jnp.int32, (1, kb), 1)
                         <= lax.broadcasted_iota(jnp.int32, (tq, 1), 0), 0.0, NEG)

    def scores(start):
        k_t = ks_ref[0, 0, pl.ds(start, kb), :]
        return lax.dot_general(qa_ref[...], k_t, NT_DIMS, preferred_element_type=F32)

    def softmax(s_all, slot, own_keys):
        for r0 in range(0, n_rows, rc):
            rows = slice(r0, r0 + rc)
            s = s_all[rows]
            if own_keys:
                s = s + tok_bias[r0 % tq:r0 % tq + rc]
                m_new = jnp.broadcast_to(jnp.max(s, axis=1, keepdims=True), (rc, LANES))
            else:
                m_old = m_ref[rows, :]
                m_new = jnp.maximum(m_old, jnp.max(s, axis=1, keepdims=True))
                alpha_ref[slot, rows, :] = jnp.exp2(m_old - m_new)
            m_ref[rows, :] = m_new
            m_wide = jnp.concatenate([m_new] * (kb // LANES), axis=1)
            p_ref[slot, rows, :] = jnp.exp2(s - m_wide).astype(p_ref.dtype)

    def values(start, slot):
        v_t = vs_ref[0, 0, pl.ds(start, kb), :]
        pv = jnp.dot(p_ref[slot], v_t, preferred_element_type=F32)
        for r0 in range(0, n_rows, rc):
            rows = slice(r0, r0 + rc)
            acc_ref[rows, :] = alpha_ref[slot, rows, :] * acc_ref[rows, :] + pv[rows]

    chunk_start = lambda c: pl.multiple_of(c * kb, kb)
    owed_start = lambda c: pl.multiple_of(jnp.where(c == 0, q0, (c - 1) * kb), kb)

    acc_ref[...] = jnp.zeros_like(acc_ref)
    alpha_ref[1] = jnp.zeros(alpha_ref.shape[1:], F32)
    softmax(scores(q0), 1, True)

    def pair_body(t, carry):
        c = 2 * t
        s_a = scores(chunk_start(c))
        values(owed_start(c), 1)
        softmax(s_a, 0, False)
        s_b = scores(chunk_start(c + 1))
        values(chunk_start(c), 0)
        softmax(s_b, 1, False)
        return carry

    lax.fori_loop(0, i // 2, pair_body, 0)

    @pl.when(i % 2 == 1)
    def _():
        s_a = scores(chunk_start(i - 1))
        values(owed_start(i - 1), 1)
        softmax(s_a, 0, False)
        values(chunk_start(i - 1), 0)

    @pl.when(i % 2 == 0)
    def _():
        values(owed_start(i), 1)

    outs = []
    for h in heads:
        o_h = acc_ref[hrows(h), :]
        o_sel = (gs_ref[hrows(h), :] * o_h) * (1.0 / o_h[:, HEAD_DIM:HEAD_DIM + 1])
        outs.append(((oc_ref[hrows(h), :] + o_sel) + ow_ref[hrows(h), :])[:, :HEAD_DIM])
    for hp in range(NSA_HPG // 2):
        pair = jnp.concatenate(outs[2 * hp:2 * hp + 2], axis=1)
        o_ref[0, :, hp * LANES:(hp + 1) * LANES] = pair.astype(o_ref.dtype)


def _nsa_attention(q3, kct, vc, ov, ksa, vsa, kwa, vwa, gt, tq=256, wg=4, rc=64):
    b, s, _ = q3.shape
    G = NSA_KV_GROUPS
    kb = tq
    n_slc = s // SLC_LEN
    nq = s // tq
    n_cmp_rows = kct.shape[3]
    gw = NSA_HPG * HEAD_DIM
    assert n_slc <= HEAD_DIM and tq % SLC_LEN == 0 and WINDOW % tq == 0 and s >= WINDOW + tq
    assert n_cmp_rows <= kb and n_slc % 8 == 0 and tq % rc == 0
    kv_spec = pl.BlockSpec((1, 1, s, LANES), lambda bi, g, i: (g, bi, 0, 0))
    n_gates = NSA_GATES_PER_GROUP
    gsel = np.zeros((2 * LANES, n_gates * LANES), np.float32)
    for c in range(n_gates):
        gsel[[c, LANES + c], c * LANES:(c + 1) * LANES] = 1.0
    gsel = jnp.asarray(gsel, BF16)
    return pl.pallas_call(
        functools.partial(_nsa_attn_kernel, tq=tq, kb=kb, n_slc=n_slc, wg=wg, rc=rc),
        grid=(b, G, nq),
        in_specs=[pl.BlockSpec((1, tq, gw), lambda bi, g, i: (bi, i, g)),
                  pl.BlockSpec((1, 1, LANES, n_cmp_rows), lambda bi, g, i: (bi, g, 0, 0)),
                  pl.BlockSpec((1, 1, n_cmp_rows, LANES), lambda bi, g, i: (bi, g, 0, 0)),
                  _const_spec(ov.shape),
                  kv_spec, kv_spec, kv_spec,
                  pl.BlockSpec((1, 1, s, 2 * LANES), lambda bi, g, i: (g, bi, 0, 0)),
                  pl.BlockSpec((1, tq, LANES), lambda bi, g, i: (g, bi * nq + i, 0)),
                  _const_spec(gsel.shape)],
        out_specs=pl.BlockSpec((1, tq, gw), lambda bi, g, i: (bi, i, g)),
        out_shape=jax.ShapeDtypeStruct((b, s, D_MODEL), BF16),
        scratch_shapes=[pltpu.VMEM((NSA_HPG * tq, LANES), BF16),
                        pltpu.VMEM((NSA_HPG * tq, LANES), F32),
                        pltpu.VMEM((tq, n_cmp_rows), F32),
                        pltpu.VMEM((NSA_HPG * tq, LANES), F32),
                        pltpu.VMEM((NSA_HPG * tq, LANES), F32),
                        pltpu.VMEM((2, NSA_HPG * tq, LANES), F32),
                        pltpu.VMEM((2, NSA_HPG * tq, kb), BF16),
                        pltpu.VMEM((NSA_HPG * tq, WINDOW + tq), BF16),
                        pltpu.VMEM((NSA_HPG * tq, LANES), F32),
                        pltpu.VMEM((NSA_HPG * tq, LANES), F32)],
        compiler_params=_params(("arbitrary", "arbitrary", "arbitrary")),
        name="nsa_attention",
    )(q3, kct, vc, ov, ksa, vsa, kwa, vwa, gt, gsel)


def _nsa_layer_attention(hx2, norm_g, positions, w_in, pe_k, pe_v, w1k, w2k, w1v, w2v, b, s):
    t = b * s
    G = NSA_KV_GROUPS
    w_pad = jnp.pad(w_in, ((0, 0), (0, NSA_W_COLS - w_in.shape[1]))).astype(BF16)
    inv = ROPE_THETA ** (-jnp.arange(HALF, dtype=F32) / HALF)
    inv2 = jnp.tile(inv, LANES // HALF)[None, :]
    sgn2 = jnp.tile(jnp.concatenate([-jnp.ones(HALF, F32), jnp.ones(HALF, F32)]), LANES // HEAD_DIM)[None, :]
    q, kc_tok, vc_tok, ksa, vsa, kwa, vwa, gt = _nsa_in_proj(
        hx2, norm_g, w_pad, positions.reshape(t, 1), inv2, sgn2, s)

    nrow = s // CMP_STRIDE
    wide = CMP_STRIDE * HEAD_DIM
    k16 = kc_tok.reshape(G, b, nrow, wide)
    v16 = vc_tok.reshape(G, b, nrow, wide)
    pek = jnp.broadcast_to(pe_k.reshape(1, CMP_LEN * HEAD_DIM), (8, CMP_LEN * HEAD_DIM))
    pev = jnp.broadcast_to(pe_v.reshape(1, CMP_LEN * HEAD_DIM), (8, CMP_LEN * HEAD_DIM))
    w2k_rot = jnp.concatenate([-w2k[:, HALF:], w2k[:, :HALF]], axis=1)
    end_idx = jnp.minimum(jnp.arange(nrow) * CMP_STRIDE + CMP_LEN - 1, s - 1)
    pos_cmp = positions[:, end_idx][:, :, None]
    inv64 = jnp.tile(inv, 2)[None, :]
    kct, vc = _nsa_compress(k16, v16, pek, pev, w1k, w2k, w2k_rot, w1v, w2v, pos_cmp, inv64)

    n_slc = s // SLC_LEN
    c0 = np.arange(nrow)[:, None] * CMP_STRIDE
    s0 = np.arange(n_slc)[None, :] * SLC_LEN
    ov = np.clip(np.minimum(c0 + CMP_LEN, s0 + SLC_LEN) - np.maximum(c0, s0), 0, None) / CMP_LEN
    ov[(s - CMP_LEN) // CMP_STRIDE + 1:, :] = 0.0
    ov = np.pad(ov, ((0, 0), (0, LANES - n_slc)))
    ov = jnp.asarray(np.concatenate([ov, ov], axis=0), BF16)

    aug4 = lambda a: a.reshape(G, b, s, a.shape[-1])
    return _nsa_attention(q.reshape(b, s, D_MODEL), kct, vc, ov, aug4(ksa), aug4(vsa),
                          aug4(kwa), aug4(vwa), gt)


def kernel(x, positions, norm_mix, sba_w_in, sba_w_out, nsa_w_in, nsa_cmp_pos_k, nsa_cmp_pos_v,
           nsa_cmp_k_w1, nsa_cmp_k_w2, nsa_cmp_v_w1, nsa_cmp_v_w2, nsa_w_out, norm_ffn,
           ffn_w_up, ffn_conv_w, ffn_conv_b, ffn_w_down, norm_final):
    b, s, d = x.shape
    t = b * s
    depth = norm_mix.shape[0]
    x2 = x.reshape(t, d)
    g_final = norm_final.reshape(1, d)
    for layer in range(depth):
        j = layer // 2
        g_mix = norm_mix[layer].reshape(1, d)
        if layer % 2 == 0:
            qkv = _sba_in_proj(x2, g_mix, sba_w_in[j].astype(BF16))
            o = _sba_attention(qkv.reshape(b, s, 3 * d), b, s)
            w_out = sba_w_out[j]
        else:
            o = _nsa_layer_attention(x2, g_mix, positions, nsa_w_in[j], nsa_cmp_pos_k[j],
                                     nsa_cmp_pos_v[j], nsa_cmp_k_w1[j], nsa_cmp_k_w2[j],
                                     nsa_cmp_v_w1[j], nsa_cmp_v_w2[j], b, s)
            w_out = nsa_w_out[j]
        x2 = _conv_ffn(x2, o.reshape(t, d), w_out.astype(BF16),
                       norm_ffn[layer].reshape(1, d), ffn_w_up[layer].astype(BF16),
                       ffn_conv_w[layer], ffn_conv_b[layer].reshape(1, -1),
                       ffn_w_down[layer].astype(BF16), g_final, s,
                       final_norm=(layer == depth - 1))
    return x2.reshape(b, s, d)
```

```python
import functools

import numpy as np
import jax
import jax.numpy as jnp
from jax import lax
from jax.experimental import pallas as pl
from jax.experimental.pallas import tpu as pltpu

D_MODEL = 1024
N_HEADS = 16
HEAD_DIM = 64
HALF = HEAD_DIM // 2
NSA_KV_GROUPS = 2
NSA_HPG = N_HEADS // NSA_KV_GROUPS
CMP_LEN = 32
CMP_STRIDE = 16
SLC_LEN = 64
SLC_SHIFT = SLC_LEN.bit_length() - 1
SLC_TOP_N = 16
WINDOW = 512
ROPE_THETA = 10000.0
D_FF = 2816
RMS_EPS = 1e-6
NEG = -1e30
FORCE_BONUS = 1e4
LOG2E = float(np.log2(np.e))
EXP2_CLAMP = 120.0
EXP2_UNDERFLOW = 160.0
Q_SCALE = LOG2E * HEAD_DIM ** -0.5

LANES = 128
VMEM_LIMIT = 56 * 1024 * 1024

F32 = jnp.float32
BF16 = jnp.bfloat16
HIGHEST = lax.Precision.HIGHEST
NT_DIMS = (((1,), (1,)), ((), ()))


def _params(semantics):
    return pltpu.CompilerParams(dimension_semantics=semantics, vmem_limit_bytes=VMEM_LIMIT)


def _rmsnorm(x, g):
    return x * lax.rsqrt(jnp.mean(x * x, axis=-1, keepdims=True) + RMS_EPS) * g


def _const_spec(shape):
    return pl.BlockSpec(shape, lambda *_: (0,) * len(shape), pipeline_mode=pl.Buffered(1))


def _split_bf16(x):
    hi = x.astype(BF16)
    return hi, (x - hi.astype(F32)).astype(BF16)


def _sba_in_proj_kernel(x_ref, g_ref, w_ref, o_ref, *, n_chunk):
    hn = _rmsnorm(x_ref[...], g_ref[...]).astype(w_ref.dtype)
    n = w_ref.shape[1]
    for c in range(n // n_chunk):
        cols = slice(c * n_chunk, (c + 1) * n_chunk)
        y = jnp.dot(hn, w_ref[:, cols], preferred_element_type=F32)
        if c * n_chunk < D_MODEL:
            y = y * Q_SCALE
        o_ref[:, cols] = y.astype(o_ref.dtype)


def _sba_in_proj(x2, g, w, tm=512):
    t, d = x2.shape
    n = w.shape[1]
    return pl.pallas_call(
        functools.partial(_sba_in_proj_kernel, n_chunk=512),
        grid=(t // tm,),
        in_specs=[pl.BlockSpec((tm, d), lambda i: (i, 0)),
                  _const_spec((1, d)),
                  _const_spec((d, n))],
        out_specs=pl.BlockSpec((tm, n), lambda i: (i, 0)),
        out_shape=jax.ShapeDtypeStruct((t, n), BF16),
        compiler_params=_params(("arbitrary",)),
        name="sba_in_proj",
    )(x2, g, w)


def _sba_attn_kernel(q_ref, k_ref, v_ref, uu_ref, o_ref, acc_ref, car_ref, hl_ref, lb_ref, a_ref,
                     z_ref,
                     *, tq, kb, nsub, nwalk, rc):
    i = pl.program_id(2)
    kt = kb * nsub
    streams = range(q_ref.shape[2] // LANES)
    scols = lambda s: slice(s * LANES, (s + 1) * LANES)
    q = q_ref[0]
    lane = lax.broadcasted_iota(jnp.int32, (tq, LANES), 1)
    klane = lax.broadcasted_iota(jnp.int32, (kb, LANES), 1)
    acc_ref[...] = jnp.zeros_like(acc_ref)
    car_ref[...] = jnp.zeros_like(car_ref)
    ud = uu_ref[...]
    rel = (lax.broadcasted_iota(jnp.int32, (rc, 2 * kb), 0)
           - (lax.broadcasted_iota(jnp.int32, (rc, 2 * kb), 1) & (kb - 1)))

    def super_tile(base, n, diag, slot, owed, z_first):
        def visibility(j, r0):
            if not diag:
                return 2, None
            if j * kb >= r0 + rc - 1:
                return 0, None
            if j * kb + kb - 1 < r0:
                return 2, None
            return 1, rel > (j * kb - r0)

        def first_row(j):
            return j * kb if diag else 0

        def scores(s, start, row0=0):
            k_t = k_ref[0, pl.ds(pl.multiple_of(start, kb), kb), scols(s)]
            k_bd = jnp.concatenate([jnp.where(klane < HEAD_DIM, k_t, jnp.zeros_like(k_t)),
                                    jnp.where(klane >= HEAD_DIM, k_t, jnp.zeros_like(k_t))], axis=0)
            return lax.dot_general(q[row0:, scols(s)], k_bd, NT_DIMS,
                                   preferred_element_type=F32)

        def log_terms(s, j, z):
            totals = {}
            row0 = first_row(j)
            for r0 in range(row0, tq, rc):
                rows = slice(r0, r0 + rc)
                kind, strict = visibility(j, r0)
                if kind == 0:
                    hl_ref[s, j, rows, :] = jnp.zeros((rc, 2 * kb), BF16)
                    continue
                zc = z[r0 - row0:r0 - row0 + rc]
                nl = jnp.maximum(jnp.log2(1.0 + jnp.exp2(jnp.minimum(zc, EXP2_CLAMP))), zc)
                lb_ref[s, j, rows, :] = zc - nl
                if kind == 1:
                    nl = jnp.where(strict, nl, 0.0)
                hl_ref[s, j, rows, :] = nl.astype(BF16)
                totals[r0] = [jnp.sum(nl[:, h * kb:(h + 1) * kb], axis=1, keepdims=True)
                              for h in range(2)]
            return jnp.dot(hl_ref[s, j, row0:], ud, preferred_element_type=F32), totals

        def weights(s, j, sums):
            cum, totals = sums
            a_slot = slot + j // nwalk
            cols = slice((j % nwalk) * kb, (j % nwalk + 1) * kb)
            row0 = first_row(j)
            for r0 in range(0, tq, rc):
                rows = slice(r0, r0 + rc)
                kind, strict = visibility(j, r0)
                if kind == 0:
                    for h in range(2):
                        a_ref[s, a_slot, h, rows, cols] = jnp.zeros((rc, kb), BF16)
                    continue
                car = [car_ref[s, h, rows, :] for h in range(2)]
                a = jnp.exp2(lb_ref[s, j, rows, :] - cum[r0 - row0:r0 - row0 + rc]
                             - jnp.concatenate(car, axis=1))
                if kind == 1:
                    a = jnp.where(strict, a, 0.0)
                for h in range(2):
                    a_ref[s, a_slot, h, rows, cols] = a[:, h * kb:(h + 1) * kb].astype(BF16)
                    car_ref[s, h, rows, :] = car[h] + totals[r0][h]

        z_next = ([scores(s, base + (n - 1) * kb, first_row(n - 1)) for s in streams]
                  if z_first is None else z_first)
        pending = None
        for j in reversed(range(n)):
            z = z_next
            if j > 0:
                z_next = [scores(s, base + (j - 1) * kb, first_row(j - 1)) for s in streams]
            if j == n - 1 and owed is not None:
                apply_weights(*owed)
            sums = [log_terms(s, j, z[s]) for s in streams]
            if pending is not None:
                for s in streams:
                    weights(s, j + 1, pending[s])
                if (j + 1) % nwalk == 0:
                    apply_weights(slot + (j + 1) // nwalk, base + (j + 1) * kb,
                                  first_row(j + 1))
            pending = sums
            if j == 0:
                for s in streams:
                    z_ref[s] = scores(s, jnp.maximum(base - kb, 0))
        for s in streams:
            weights(s, 0, pending[s])

    kw = nwalk * kb

    def apply_weights(slot, start, row0=0):
        for s in streams:
            v_n = v_ref[0, pl.ds(pl.multiple_of(start, kb), kw), scols(s)]
            for h in range(2):
                acc_ref[s, h, row0:, :] += jnp.dot(a_ref[s, slot, h, row0:, :], v_n,
                                                   preferred_element_type=F32)

    OWN = 2
    super_tile(i * kt, nsub, True, OWN, None, None)

    def owed_after(t):
        return (jnp.where(t == 0, OWN, (t + 1) % 2),
                jnp.where(t == 0, i * kt, i * kt - t * kw))

    def live():
        return (jnp.min(car_ref[...]) < EXP2_UNDERFLOW).astype(jnp.int32)

    def body(carry):
        t, _ = carry
        super_tile(i * kt - (t + 1) * kw, nwalk, False, t % 2, owed_after(t),
                   [z_ref[s] for s in streams])
        return t + 1, live()

    n_walk = i * (kt // kw)
    t_end, _ = lax.while_loop(lambda c: (c[0] < n_walk) & (c[1] > 0), body, (jnp.int32(0), live()))
    apply_weights(*owed_after(t_end))
    for s in streams:
        o_ref[0, :, scols(s)] = jnp.where(lane < HEAD_DIM, acc_ref[s, 0],
                                          acc_ref[s, 1]).astype(o_ref.dtype)


def _sba_attention(qkv, b, s, tq=512, kb=LANES, nsub=4, nwalk=2, rc=64, ns=4):
    assert tq == kb * nsub and s % tq == 0 and tq % rc == 0 and nsub % nwalk == 0
    n_blocks = D_MODEL // (ns * LANES)
    wide = ns * LANES
    assert kb & (kb - 1) == 0 and D_MODEL % wide == 0
    jj = np.arange(kb)
    uu = jnp.asarray(np.kron(np.eye(2), (jj[:, None] > jj[None, :]).astype(np.float32)), BF16)
    return pl.pallas_call(
        functools.partial(_sba_attn_kernel, tq=tq, kb=kb, nsub=nsub, nwalk=nwalk, rc=rc),
        grid=(b, n_blocks, s // tq),
        in_specs=[pl.BlockSpec((1, tq, wide), lambda bi, p, i: (bi, i, p)),
                  pl.BlockSpec((1, s, wide), lambda bi, p, i: (bi, 0, n_blocks + p)),
                  pl.BlockSpec((1, s, wide), lambda bi, p, i: (bi, 0, 2 * n_blocks + p)),
                  _const_spec((2 * kb, 2 * kb))],
        out_specs=pl.BlockSpec((1, tq, wide), lambda bi, p, i: (bi, i, p)),
        out_shape=jax.ShapeDtypeStruct((b, s, D_MODEL), BF16),
        scratch_shapes=[pltpu.VMEM((ns, 2, tq, LANES), F32),
                        pltpu.VMEM((ns, 2, tq, LANES), F32),
                        pltpu.VMEM((ns, nsub, tq, 2 * kb), BF16),
                        pltpu.VMEM((ns, nsub, tq, 2 * kb), F32),
                        pltpu.VMEM((ns, 2 + nsub // nwalk, 2, tq, nwalk * kb), BF16),
                        pltpu.VMEM((ns, tq, 2 * kb), F32)],
        compiler_params=_params(("arbitrary", "arbitrary", "arbitrary")),
        name="sba_attention",
    )(qkv, qkv, qkv, uu)


def _ffn_kernel(x_ref, o_ref, wo_ref, g_ref, wup_ref, cw_ref, cb_ref, wdn_ref, gf_ref, y_ref,
                carry_ref, sg_ref, sv_ref, act_ref, *, tm, fc, tiles_per_seq, final_norm):
    @pl.when(pl.program_id(0) % tiles_per_seq == 0)
    def _():
        carry_ref[...] = jnp.zeros_like(carry_ref)

    x = x_ref[...] + jnp.dot(o_ref[...], wo_ref[...], preferred_element_type=F32)
    hn = _rmsnorm(x, g_ref[...]).astype(wup_ref.dtype)

    def up(col0):
        return jnp.dot(hn, wup_ref[:, col0:col0 + fc], preferred_element_type=F32)

    def conv(u, col0, s_ref):
        cols = slice(col0, col0 + fc)
        s_ref[0:8, :] = carry_ref[:, cols]
        s_ref[8:tm + 8, :] = u
        carry_ref[:, cols] = u[tm - 8:tm, :]
        cw = cw_ref[:, cols]
        c = cb_ref[:, cols] + s_ref[6:tm + 6, :] * cw[0:1]
        c = c + s_ref[7:tm + 7, :] * cw[1:2]
        return c + u * cw[2:3]

    n_chunks = D_FF // fc
    u_next = (up(0), up(D_FF))
    for c in range(n_chunks):
        u_gate, u_val = u_next
        if c + 1 < n_chunks:
            u_next = (up((c + 1) * fc), up(D_FF + (c + 1) * fc))
        gate = conv(u_gate, c * fc, sg_ref)
        val = conv(u_val, D_FF + c * fc, sv_ref)
        act_ref[:, c * fc:(c + 1) * fc] = (gate * jax.nn.sigmoid(gate) * val).astype(act_ref.dtype)
    y = x + jnp.dot(act_ref[...], wdn_ref[...], preferred_element_type=F32)
    if final_norm:
        y = _rmsnorm(y, gf_ref[...])
    y_ref[...] = y


def _conv_ffn(x2, o2, w_out, g, w_up, conv_w, conv_b, w_down, g_final, s, final_norm,
              tm=256, fc=256):
    t, d = x2.shape
    f2 = w_up.shape[1]
    return pl.pallas_call(
        functools.partial(_ffn_kernel, tm=tm, fc=fc, tiles_per_seq=s // tm, final_norm=final_norm),
        grid=(t // tm,),
        in_specs=[pl.BlockSpec((tm, d), lambda i: (i, 0)),
                  pl.BlockSpec((tm, d), lambda i: (i, 0)),
                  _const_spec((d, d)),
                  _const_spec((1, d)),
                  _const_spec((d, f2)),
                  _const_spec((3, f2)),
                  _const_spec((1, f2)),
                  _const_spec((D_FF, d)),
                  _const_spec((1, d))],
        out_specs=pl.BlockSpec((tm, d), lambda i: (i, 0)),
        out_shape=jax.ShapeDtypeStruct((t, d), F32),
        scratch_shapes=[pltpu.VMEM((8, f2), F32),
                        pltpu.VMEM((tm + 8, fc), F32),
                        pltpu.VMEM((tm + 8, fc), F32),
                        pltpu.VMEM((tm, D_FF), BF16)],
        compiler_params=_params(("arbitrary",)),
        name="conv_ffn",
    )(x2, o2, w_out, g, w_up, conv_w, conv_b, w_down, g_final)


NSA_Q_COLS = D_MODEL
NSA_W_COLS = D_MODEL + 7 * LANES
NSA_GATES_PER_GROUP = 3 * NSA_HPG


def _swap_halves(y):
    lane = lax.broadcasted_iota(jnp.int32, y.shape, 1)
    first = (lane % HEAD_DIM) < HALF
    return jnp.where(first, pltpu.roll(y, LANES - HALF, 1), pltpu.roll(y, HALF, 1))


def _nsa_in_proj_kernel(x_ref, g_ref, w_ref, pos_ref, inv_ref, sgn_ref,
                        q_ref, kc_ref, vc_ref, ksa_ref, vsa_ref, kwa_ref, vwa_ref, gt_ref,
                        *, tm, seq):
    hn = _rmsnorm(x_ref[...], g_ref[...]).astype(w_ref.dtype)
    ang = pos_ref[...].astype(F32) * inv_ref[...]
    cos = jnp.cos(ang)
    sin = jnp.sin(ang) * sgn_ref[...]

    def rope(y):
        return y * cos + _swap_halves(y) * sin

    q_chunk = 4 * LANES
    for c in range(NSA_Q_COLS // q_chunk):
        y = jnp.dot(hn, w_ref[:, c * q_chunk:(c + 1) * q_chunk], preferred_element_type=F32)
        for l in range(q_chunk // LANES):
            yl = rope(y[:, l * LANES:(l + 1) * LANES]) * Q_SCALE
            q_ref[:, c * q_chunk + l * LANES:c * q_chunk + (l + 1) * LANES] = yl.astype(q_ref.dtype)
    y = jnp.dot(hn, w_ref[:, NSA_Q_COLS:], preferred_element_type=F32)
    part = lambda n: y[:, n * LANES:(n + 1) * LANES]

    tok = ((pl.program_id(0) % (seq // tm)) * tm
           + lax.broadcasted_iota(jnp.int32, (tm, HEAD_DIM), 0))
    lane = lax.broadcasted_iota(jnp.int32, (tm, HEAD_DIM), 1)
    blk_onehot = jnp.where((tok >> SLC_SHIFT) == lane, 1.0, 0.0)
    ones_col = jnp.where(lane == 0, 1.0, 0.0)
    zeros = jnp.zeros((tm, HEAD_DIM), F32)
    ks, vs, kw, vw = rope(part(2)), part(3), rope(part(4)), part(5)
    for g in range(NSA_KV_GROUPS):
        cols = slice(g * HEAD_DIM, (g + 1) * HEAD_DIM)
        kc_ref[g] = part(0)[:, cols]
        vc_ref[g] = part(1)[:, cols]
        ksa_ref[g] = jnp.concatenate([ks[:, cols], blk_onehot], axis=1).astype(ksa_ref.dtype)
        vsa_ref[g] = jnp.concatenate([vs[:, cols], ones_col], axis=1).astype(vsa_ref.dtype)
        kwa_ref[g] = jnp.concatenate([kw[:, cols], zeros], axis=1).astype(kwa_ref.dtype)
        vwa_ref[g] = jnp.concatenate([vw[:, cols], zeros, jnp.ones((tm, LANES), F32)],
                                     axis=1).astype(vwa_ref.dtype)
    gates = jax.nn.sigmoid(part(6))
    gt_ref[0] = gates
    gt_ref[1] = pltpu.roll(gates, LANES - NSA_GATES_PER_GROUP, 1)


def _nsa_in_proj(x2, g, w, pos2, inv2, sgn2, seq, tm=512):
    t, d = x2.shape
    G = NSA_KV_GROUPS
    tok_g = lambda dt: jax.ShapeDtypeStruct((G, t, HEAD_DIM), dt)
    aug_g = lambda dt: jax.ShapeDtypeStruct((G, t, LANES), dt)
    tok_spec = pl.BlockSpec((G, tm, HEAD_DIM), lambda i: (0, i, 0))
    aug_spec = pl.BlockSpec((G, tm, LANES), lambda i: (0, i, 0))
    return pl.pallas_call(
        functools.partial(_nsa_in_proj_kernel, tm=tm, seq=seq),
        grid=(t // tm,),
        in_specs=[pl.BlockSpec((tm, d), lambda i: (i, 0)),
                  _const_spec((1, d)),
                  _const_spec((d, NSA_W_COLS)),
                  pl.BlockSpec((tm, 1), lambda i: (i, 0)),
                  _const_spec((1, LANES)),
                  _const_spec((1, LANES))],
        out_specs=[pl.BlockSpec((tm, NSA_Q_COLS), lambda i: (i, 0)),
                   tok_spec, tok_spec, aug_spec, aug_spec, aug_spec,
                   pl.BlockSpec((G, tm, 2 * LANES), lambda i: (0, i, 0)), aug_spec],
        out_shape=[jax.ShapeDtypeStruct((t, NSA_Q_COLS), BF16),
                   tok_g(F32), tok_g(F32), aug_g(BF16), aug_g(BF16), aug_g(BF16),
                   jax.ShapeDtypeStruct((G, t, 2 * LANES), BF16), aug_g(F32)],
        compiler_params=_params(("arbitrary",)),
        name="nsa_in_proj",
    )(x2, g, w, pos2, inv2, sgn2)


def _nsa_compress_kernel(k16_ref, v16_ref, pek_ref, pev_ref, w1k_ref, w2k_ref, w2kr_ref,
                         w1v_ref, w2v_ref, pos_ref, inv_ref, kct_ref, vc_ref):
    half_w = CMP_STRIDE * HEAD_DIM
    nrow = k16_ref.shape[2]

    def hidden(x16_ref, pe_ref, w1_ref):
        x = x16_ref[0, 0]
        y1 = jnp.dot(x, w1_ref[:half_w, :], precision=HIGHEST, preferred_element_type=F32)
        y2 = jnp.dot(x, w1_ref[half_w:, :], precision=HIGHEST, preferred_element_type=F32)
        bias = jnp.dot(pe_ref[...], w1_ref[...], precision=HIGHEST, preferred_element_type=F32)
        return jax.nn.gelu(y1 + pltpu.roll(y2, nrow - 1, 0) + bias[0:1])

    hk = hidden(k16_ref, pek_ref, w1k_ref)
    kc = jnp.dot(hk, w2k_ref[...], precision=HIGHEST, preferred_element_type=F32)
    kc_rot = jnp.dot(hk, w2kr_ref[...], precision=HIGHEST, preferred_element_type=F32)
    ang = pos_ref[0].astype(F32) * inv_ref[...]
    kc = kc * jnp.cos(ang) + kc_rot * jnp.sin(ang)
    hi, lo = _split_bf16(kc)
    kct_ref[0, 0] = jnp.concatenate([hi.astype(F32), lo.astype(F32)], axis=1).T.astype(kct_ref.dtype)
    hv = hidden(v16_ref, pev_ref, w1v_ref)
    vc = jnp.dot(hv, w2v_ref[...], precision=HIGHEST, preferred_element_type=F32)
    vc_ref[0, 0] = jnp.concatenate([vc, jnp.zeros_like(vc)], axis=1).astype(vc_ref.dtype)


def _nsa_compress(k16, v16, pek, pev, w1k, w2k, w2kr, w1v, w2v, pos_cmp, inv64):
    G, b, nrow, wide = k16.shape
    x_spec = pl.BlockSpec((1, 1, nrow, wide), lambda bi, g: (g, bi, 0, 0))
    return pl.pallas_call(
        _nsa_compress_kernel,
        grid=(b, G),
        in_specs=[x_spec, x_spec,
                  _const_spec(pek.shape), _const_spec(pev.shape),
                  _const_spec(w1k.shape), _const_spec(w2k.shape), _const_spec(w2kr.shape),
                  _const_spec(w1v.shape), _const_spec(w2v.shape),
                  pl.BlockSpec((1, nrow, 1), lambda bi, g: (bi, 0, 0)),
                  _const_spec((1, HEAD_DIM))],
        out_specs=[pl.BlockSpec((1, 1, LANES, nrow), lambda bi, g: (bi, g, 0, 0)),
                   pl.BlockSpec((1, 1, nrow, LANES), lambda bi, g: (bi, g, 0, 0))],
        out_shape=[jax.ShapeDtypeStruct((b, G, LANES, nrow), BF16),
                   jax.ShapeDtypeStruct((b, G, nrow, LANES), BF16)],
        compiler_params=_params(("arbitrary", "arbitrary")),
        name="nsa_compress",
    )(k16, v16, pek, pev, w1k, w2k, w2kr, w1v, w2v, pos_cmp, inv64)


def _nsa_attn_kernel(q_ref, kct_ref, vc_ref, ov_ref, ks_ref, vs_ref, kw_ref, vw_ref, gt_ref,
                     gsel_ref, o_ref, qa_ref, oc_ref, psum_ref, m_ref, acc_ref, alpha_ref, p_ref, pw_ref,
                     ow_ref, gs_ref,
                     *, tq, kb, n_slc, wg, rc):
    i = pl.program_id(2)
    q0 = pl.multiple_of(i * tq, tq)
    n_rows = NSA_HPG * tq
    heads = range(NSA_HPG)
    hrows = lambda h: slice(h * tq, (h + 1) * tq)
    qrow = q0 + lax.broadcasted_iota(jnp.int32, (tq, 1), 0)
    q_heads = [q_ref[0, :, h * HEAD_DIM:(h + 1) * HEAD_DIM] for h in heads]
    for h in heads:
        qa_ref[hrows(h), :] = jnp.concatenate([q_heads[h], q_heads[h]], axis=1)

    n_cmp_rows = kct_ref.shape[3]
    s_all = jnp.dot(qa_ref[...], kct_ref[0, 0], preferred_element_type=F32)
    w_keys = WINDOW + tq
    w_start = pl.multiple_of(jnp.maximum(q0 - WINDOW, 0), tq)
    k_w = kw_ref[0, 0, pl.ds(w_start, w_keys), :]
    s_win = [lax.dot_general(qa_ref[h0 * tq:(h0 + wg) * tq, :], k_w, NT_DIMS,
                             preferred_element_type=F32) for h0 in range(0, NSA_HPG, wg)]

    cmp_end = CMP_STRIDE * lax.broadcasted_iota(jnp.int32, (1, n_cmp_rows), 1) + (CMP_LEN - 1)
    c_bias = jnp.where(cmp_end <= qrow, 0.0, NEG)
    row_valid = jnp.where(qrow >= CMP_LEN - 1, 1.0, 0.0)
    for r0 in range(0, n_rows, rc):
        rows = slice(r0, r0 + rc)
        local = slice(r0 % tq, r0 % tq + rc)
        s = s_all[rows] + c_bias[local]
        e = jnp.exp2(s - jnp.max(s, axis=1, keepdims=True))
        p = e * (row_valid[local] / jnp.sum(e, axis=1, keepdims=True))
        if r0 < tq:
            psum_ref[local, :] = p
        else:
            psum_ref[local, :] += p
        p_ref[0, rows, 0:n_cmp_rows] = p.astype(p_ref.dtype)
    g_wide = jnp.dot(jnp.concatenate(_split_bf16(gt_ref[0]), axis=1), gsel_ref[...],
                     preferred_element_type=F32)
    gate = lambda h, branch: g_wide[:, (3 * h + branch) * LANES:(3 * h + branch + 1) * LANES]
    o_cmp = jnp.dot(p_ref[0, :, 0:n_cmp_rows], vc_ref[0, 0], preferred_element_type=F32)
    for h in heads:
        oc_ref[hrows(h), :] = gate(h, 0) * o_cmp[hrows(h)]
        gs_ref[hrows(h), :] = gate(h, 1)
    imp = jnp.dot(jnp.concatenate(_split_bf16(psum_ref[...]), axis=1), ov_ref[...],
                  preferred_element_type=F32)
    imp_t = imp.T[:n_slc]

    w_diff = qrow - (w_start + lax.broadcasted_iota(jnp.int32, (1, w_keys), 1))
    w_bias = jnp.where((w_diff >= 0) & (w_diff < WINDOW), 0.0, NEG)
    for r0 in range(0, n_rows, rc):
        s = s_win[r0 // (wg * tq)][r0 % (wg * tq):r0 % (wg * tq) + rc] + w_bias[r0 % tq:r0 % tq + rc]
        pw_ref[r0:r0 + rc, :] = jnp.exp2(s - jnp.max(s, axis=1, keepdims=True)).astype(pw_ref.dtype)
    o_win = jnp.dot(pw_ref[...], vw_ref[0, 0, pl.ds(w_start, w_keys), :],
                    preferred_element_type=F32)
    for h in heads:
        o_h = o_win[hrows(h)]
        ow_ref[hrows(h), :] = (gate(h, 2) * o_h[:, :LANES]) * (1.0 / o_h[:, LANES:])

    qpos = q0 + lax.broadcasted_iota(jnp.int32, (1, tq), 1)
    blk = lax.broadcasted_iota(jnp.int32, (n_slc, 1), 0)
    cur = qpos >> SLC_SHIFT
    forced = (blk == 0) | (blk == cur) | (blk == cur - 1)
    causal_blk = blk * SLC_LEN <= qpos
    score = jnp.where(causal_blk, imp_t + FORCE_BONUS * forced.astype(F32), NEG)
    sub = lax.broadcasted_iota(jnp.int32, (8, 1), 0)
    groups = [score[8 * v:8 * v + 8] for v in range(n_slc // 8)]
    counts = [jnp.zeros((8, tq), F32) for _ in groups]
    for m in range(n_slc):
        row = score[m:m + 1]
        for v, sv in enumerate(groups):
            if v < m // 8:
                ahead = row > sv
            elif v > m // 8:
                ahead = row >= sv
            else:
                ahead = (row > sv) | ((row == sv) & (sub > m % 8))
            counts[v] = counts[v] + jnp.where(ahead, 1.0, 0.0)
    rank = jnp.concatenate(counts, axis=0)
    sel_bias_t = jnp.where(rank < SLC_TOP_N, 0.0, NEG)
    sel_bias = jnp.concatenate([sel_bias_t, jnp.zeros((LANES - n_slc, tq), F32)], axis=0).T
    sel_bias = sel_bias[:, :HEAD_DIM].astype(qa_ref.dtype)
    for h in heads:
        qa_ref[hrows(h), :] = jnp.concatenate([q_heads[h], sel_bias], axis=1)

    tok_bias = jnp.where(lax.broadcasted_iota(jnp.int32, (1, kb), 1)
                         <= lax.broadcasted_iota(jnp.int32, (tq, 1), 0), 0.0, NEG)

    def scores(start):
        k_t = ks_ref[0, 0, pl.ds(start, kb), :]
        return lax.dot_general(qa_ref[...], k_t, NT_DIMS, preferred_element_type=F32)

    def softmax(s_all, slot, own_keys):
        for r0 in range(0, n_rows, rc):
            rows = slice(r0, r0 + rc)
            s = s_all[rows]
            if own_keys:
                s = s + tok_bias[r0 % tq:r0 % tq + rc]
                m_new = jnp.broadcast_to(jnp.max(s, axis=1, keepdims=True), (rc, LANES))
            else:
                m_old = m_ref[rows, :]
                m_new = jnp.maximum(m_old, jnp.max(s, axis=1, keepdims=True))
                alpha_ref[slot, rows, :] = jnp.exp2(m_old - m_new)
            m_ref[rows, :] = m_new
            m_wide = jnp.concatenate([m_new] * (kb // LANES), axis=1)
            p_ref[slot, rows, :] = jnp.exp2(s - m_wide).astype(p_ref.dtype)

    def values(start, slot):
        v_t = vs_ref[0, 0, pl.ds(start, kb), :]
        pv = jnp.dot(p_ref[slot], v_t, preferred_element_type=F32)
        for r0 in range(0, n_rows, rc):
            rows = slice(r0, r0 + rc)
            acc_ref[rows, :] = alpha_ref[slot, rows, :] * acc_ref[rows, :] + pv[rows]

    chunk_start = lambda c: pl.multiple_of(c * kb, kb)
    owed_start = lambda c: pl.multiple_of(jnp.where(c == 0, q0, (c - 1) * kb), kb)

    acc_ref[...] = jnp.zeros_like(acc_ref)
    alpha_ref[1] = jnp.zeros(alpha_ref.shape[1:], F32)
    softmax(scores(q0), 1, True)

    def pair_body(t, carry):
        c = 2 * t
        s_a = scores(chunk_start(c))
        values(owed_start(c), 1)
        softmax(s_a, 0, False)
        s_b = scores(chunk_start(c + 1))
        values(chunk_start(c), 0)
        softmax(s_b, 1, False)
        return carry

    lax.fori_loop(0, i // 2, pair_body, 0)

    @pl.when(i % 2 == 1)
    def _():
        s_a = scores(chunk_start(i - 1))
        values(owed_start(i - 1), 1)
        softmax(s_a, 0, False)
        values(chunk_start(i - 1), 0)

    @pl.when(i % 2 == 0)
    def _():
        values(owed_start(i), 1)

    outs = []
    for h in heads:
        o_h = acc_ref[hrows(h), :]
        o_sel = (gs_ref[hrows(h), :] * o_h) * (1.0 / o_h[:, HEAD_DIM:HEAD_DIM + 1])
        outs.append(((oc_ref[hrows(h), :] + o_sel) + ow_ref[hrows(h), :])[:, :HEAD_DIM])
    for hp in range(NSA_HPG // 2):
        pair = jnp.concatenate(outs[2 * hp:2 * hp + 2], axis=1)
        o_ref[0, :, hp * LANES:(hp + 1) * LANES] = pair.astype(o_ref.dtype)


def _nsa_attention(q3, kct, vc, ov, ksa, vsa, kwa, vwa, gt, tq=256, wg=4, rc=64):
    b, s, _ = q3.shape
    G = NSA_KV_GROUPS
    kb = tq
    n_slc = s // SLC_LEN
    nq = s // tq
    n_cmp_rows = kct.shape[3]
    gw = NSA_HPG * HEAD_DIM
    assert n_slc <= HEAD_DIM and tq % SLC_LEN == 0 and WINDOW % tq == 0 and s >= WINDOW + tq
    assert n_cmp_rows <= kb and n_slc % 8 == 0 and tq % rc == 0
    kv_spec = pl.BlockSpec((1, 1, s, LANES), lambda bi, g, i: (g, bi, 0, 0))
    n_gates = NSA_GATES_PER_GROUP
    gsel = np.zeros((2 * LANES, n_gates * LANES), np.float32)
    for c in range(n_gates):
        gsel[[c, LANES + c], c * LANES:(c + 1) * LANES] = 1.0
    gsel = jnp.asarray(gsel, BF16)
    return pl.pallas_call(
        functools.partial(_nsa_attn_kernel, tq=tq, kb=kb, n_slc=n_slc, wg=wg, rc=rc),
        grid=(b, G, nq),
        in_specs=[pl.BlockSpec((1, tq, gw), lambda bi, g, i: (bi, i, g)),
                  pl.BlockSpec((1, 1, LANES, n_cmp_rows), lambda bi, g, i: (bi, g, 0, 0)),
                  pl.BlockSpec((1, 1, n_cmp_rows, LANES), lambda bi, g, i: (bi, g, 0, 0)),
                  _const_spec(ov.shape),
                  kv_spec, kv_spec, kv_spec,
                  pl.BlockSpec((1, 1, s, 2 * LANES), lambda bi, g, i: (g, bi, 0, 0)),
                  pl.BlockSpec((1, tq, LANES), lambda bi, g, i: (g, bi * nq + i, 0)),
                  _const_spec(gsel.shape)],
        out_specs=pl.BlockSpec((1, tq, gw), lambda bi, g, i: (bi, i, g)),
        out_shape=jax.ShapeDtypeStruct((b, s, D_MODEL), BF16),
        scratch_shapes=[pltpu.VMEM((NSA_HPG * tq, LANES), BF16),
                        pltpu.VMEM((NSA_HPG * tq, LANES), F32),
                        pltpu.VMEM((tq, n_cmp_rows), F32),
                        pltpu.VMEM((NSA_HPG * tq, LANES), F32),
                        pltpu.VMEM((NSA_HPG * tq, LANES), F32),
                        pltpu.VMEM((2, NSA_HPG * tq, LANES), F32),
                        pltpu.VMEM((2, NSA_HPG * tq, kb), BF16),
                        pltpu.VMEM((NSA_HPG * tq, WINDOW + tq), BF16),
                        pltpu.VMEM((NSA_HPG * tq, LANES), F32),
                        pltpu.VMEM((NSA_HPG * tq, LANES), F32)],
        compiler_params=_params(("arbitrary", "arbitrary", "arbitrary")),
        name="nsa_attention",
    )(q3, kct, vc, ov, ksa, vsa, kwa, vwa, gt, gsel)


def _nsa_layer_attention(hx2, norm_g, positions, w_in, pe_k, pe_v, w1k, w2k, w1v, w2v, b, s):
    t = b * s
    G = NSA_KV_GROUPS
    w_pad = jnp.pad(w_in, ((0, 0), (0, NSA_W_COLS - w_in.shape[1]))).astype(BF16)
    inv = ROPE_THETA ** (-jnp.arange(HALF, dtype=F32) / HALF)
    inv2 = jnp.tile(inv, LANES // HALF)[None, :]
    sgn2 = jnp.tile(jnp.concatenate([-jnp.ones(HALF, F32), jnp.ones(HALF, F32)]), LANES // HEAD_DIM)[None, :]
    q, kc_tok, vc_tok, ksa, vsa, kwa, vwa, gt = _nsa_in_proj(
        hx2, norm_g, w_pad, positions.reshape(t, 1), inv2, sgn2, s)

    nrow = s // CMP_STRIDE
    wide = CMP_STRIDE * HEAD_DIM
    k16 = kc_tok.reshape(G, b, nrow, wide)
    v16 = vc_tok.reshape(G, b, nrow, wide)
    pek = jnp.broadcast_to(pe_k.reshape(1, CMP_LEN * HEAD_DIM), (8, CMP_LEN * HEAD_DIM))
    pev = jnp.broadcast_to(pe_v.reshape(1, CMP_LEN * HEAD_DIM), (8, CMP_LEN * HEAD_DIM))
    w2k_rot = jnp.concatenate([-w2k[:, HALF:], w2k[:, :HALF]], axis=1)
    end_idx = jnp.minimum(jnp.arange(nrow) * CMP_STRIDE + CMP_LEN - 1, s - 1)
    pos_cmp = positions[:, end_idx][:, :, None]
    inv64 = jnp.tile(inv, 2)[None, :]
    kct, vc = _nsa_compress(k16, v16, pek, pev, w1k, w2k, w2k_rot, w1v, w2v, pos_cmp, inv64)

    n_slc = s // SLC_LEN
    c0 = np.arange(nrow)[:, None] * CMP_STRIDE
    s0 = np.arange(n_slc)[None, :] * SLC_LEN
    ov = np.clip(np.minimum(c0 + CMP_LEN, s0 + SLC_LEN) - np.maximum(c0, s0), 0, None) / CMP_LEN
    ov[(s - CMP_LEN) // CMP_STRIDE + 1:, :] = 0.0
    ov = np.pad(ov, ((0, 0), (0, LANES - n_slc)))
    ov = jnp.asarray(np.concatenate([ov, ov], axis=0), BF16)

    aug4 = lambda a: a.reshape(G, b, s, a.shape[-1])
    return _nsa_attention(q.reshape(b, s, D_MODEL), kct, vc, ov, aug4(ksa), aug4(vsa),
                          aug4(kwa), aug4(vwa), gt)


def kernel(x, positions, norm_mix, sba_w_in, sba_w_out, nsa_w_in, nsa_cmp_pos_k, nsa_cmp_pos_v,
           nsa_cmp_k_w1, nsa_cmp_k_w2, nsa_cmp_v_w1, nsa_cmp_v_w2, nsa_w_out, norm_ffn,
           ffn_w_up, ffn_conv_w, ffn_conv_b, ffn_w_down, norm_final):
    b, s, d = x.shape
    t = b * s
    depth = norm_mix.shape[0]
    x2 = x.reshape(t, d)
    g_final = norm_final.reshape(1, d)
    for layer in range(depth):
        j = layer // 2
        g_mix = norm_mix[layer].reshape(1, d)
        if layer % 2 == 0:
            qkv = _sba_in_proj(x2, g_mix, sba_w_in[j].astype(BF16))
            o = _sba_attention(qkv.reshape(b, s, 3 * d), b, s)
            w_out = sba_w_out[j]
        else:
            o = _nsa_layer_attention(x2, g_mix, positions, nsa_w_in[j], nsa_cmp_pos_k[j],
                                     nsa_cmp_pos_v[j], nsa_cmp_k_w1[j], nsa_cmp_k_w2[j],
                                     nsa_cmp_v_w1[j], nsa_cmp_v_w2[j], b, s)
            w_out = nsa_w_out[j]
        x2 = _conv_ffn(x2, o.reshape(t, d), w_out.astype(BF16),
                       norm_ffn[layer].reshape(1, d), ffn_w_up[layer].astype(BF16),
                       ffn_conv_w[layer], ffn_conv_b[layer].reshape(1, -1),
                       ffn_w_down[layer].astype(BF16), g_final, s,
                       final_norm=(layer == depth - 1))
    return x2.reshape(b, s, d)
```

```python
import functools

import numpy as np
import jax
import jax.numpy as jnp
from jax import lax
from jax.experimental import pallas as pl
from jax.experimental.pallas import tpu as pltpu

D_MODEL = 1024
N_HEADS = 16
HEAD_DIM = 64
HALF = HEAD_DIM // 2
NSA_KV_GROUPS = 2
NSA_HPG = N_HEADS // NSA_KV_GROUPS
CMP_LEN = 32
CMP_STRIDE = 16
SLC_LEN = 64
SLC_SHIFT = SLC_LEN.bit_length() - 1
SLC_TOP_N = 16
WINDOW = 512
ROPE_THETA = 10000.0
D_FF = 2816
RMS_EPS = 1e-6
NEG = -1e30
FORCE_BONUS = 1e4
LOG2E = float(np.log2(np.e))
EXP2_CLAMP = 120.0
EXP2_UNDERFLOW = 160.0
Q_SCALE = LOG2E * HEAD_DIM ** -0.5

LANES = 128
VMEM_LIMIT = 56 * 1024 * 1024

F32 = jnp.float32
BF16 = jnp.bfloat16
HIGHEST = lax.Precision.HIGHEST
NT_DIMS = (((1,), (1,)), ((), ()))


def _params(semantics):
    return pltpu.CompilerParams(dimension_semantics=semantics, vmem_limit_bytes=VMEM_LIMIT)


def _rmsnorm(x, g):
    return x * lax.rsqrt(jnp.mean(x * x, axis=-1, keepdims=True) + RMS_EPS) * g


def _const_spec(shape):
    return pl.BlockSpec(shape, lambda *_: (0,) * len(shape), pipeline_mode=pl.Buffered(1))


def _split_bf16(x):
    hi = x.astype(BF16)
    return hi, (x - hi.astype(F32)).astype(BF16)


def _sba_in_proj_kernel(x_ref, g_ref, w_ref, o_ref, *, n_chunk):
    hn = _rmsnorm(x_ref[...], g_ref[...]).astype(w_ref.dtype)
    n = w_ref.shape[1]
    for c in range(n // n_chunk):
        cols = slice(c * n_chunk, (c + 1) * n_chunk)
        y = jnp.dot(hn, w_ref[:, cols], preferred_element_type=F32)
        if c * n_chunk < D_MODEL:
            y = y * Q_SCALE
        o_ref[:, cols] = y.astype(o_ref.dtype)


def _sba_in_proj(x2, g, w, tm=512):
    t, d = x2.shape
    n = w.shape[1]
    return pl.pallas_call(
        functools.partial(_sba_in_proj_kernel, n_chunk=512),
        grid=(t // tm,),
        in_specs=[pl.BlockSpec((tm, d), lambda i: (i, 0)),
                  _const_spec((1, d)),
                  _const_spec((d, n))],
        out_specs=pl.BlockSpec((tm, n), lambda i: (i, 0)),
        out_shape=jax.ShapeDtypeStruct((t, n), BF16),
        compiler_params=_params(("arbitrary",)),
        name="sba_in_proj",
    )(x2, g, w)


def _sba_attn_kernel(q_ref, k_ref, v_ref, uu_ref, o_ref, acc_ref, car_ref, hl_ref, lb_ref, a_ref,
                     z_ref,
                     *, tq, kb, nsub, nwalk, rc):
    i = pl.program_id(2)
    kt = kb * nsub
    streams = range(q_ref.shape[2] // LANES)
    scols = lambda s: slice(s * LANES, (s + 1) * LANES)
    q = q_ref[0]
    lane = lax.broadcasted_iota(jnp.int32, (tq, LANES), 1)
    klane = lax.broadcasted_iota(jnp.int32, (kb, LANES), 1)
    acc_ref[...] = jnp.zeros_like(acc_ref)
    car_ref[...] = jnp.zeros_like(car_ref)
    ud = uu_ref[...]
    rel = (lax.broadcasted_iota(jnp.int32, (rc, 2 * kb), 0)
           - (lax.broadcasted_iota(jnp.int32, (rc, 2 * kb), 1) & (kb - 1)))

    def super_tile(base, n, diag, slot, owed, z_first):
        def visibility(j, r0):
            if not diag:
                return 2, None
            if j * kb >= r0 + rc - 1:
                return 0, None
            if j * kb + kb - 1 < r0:
                return 2, None
            return 1, rel > (j * kb - r0)

        def first_row(j):
            return j * kb if diag else 0

        def scores(s, start, row0=0):
            k_t = k_ref[0, pl.ds(pl.multiple_of(start, kb), kb), scols(s)]
            k_bd = jnp.concatenate([jnp.where(klane < HEAD_DIM, k_t, jnp.zeros_like(k_t)),
                                    jnp.where(klane >= HEAD_DIM, k_t, jnp.zeros_like(k_t))], axis=0)
            return lax.dot_general(q[row0:, scols(s)], k_bd, NT_DIMS,
                                   preferred_element_type=F32)

        def log_terms(s, j, z):
            totals = {}
            row0 = first_row(j)
            for r0 in range(row0, tq, rc):
                rows = slice(r0, r0 + rc)
                kind, strict = visibility(j, r0)
                if kind == 0:
                    hl_ref[s, j, rows, :] = jnp.zeros((rc, 2 * kb), BF16)
                    continue
                zc = z[r0 - row0:r0 - row0 + rc]
                nl = jnp.maximum(jnp.log2(1.0 + jnp.exp2(jnp.minimum(zc, EXP2_CLAMP))), zc)
                lb_ref[s, j, rows, :] = zc - nl
                if kind == 1:
                    nl = jnp.where(strict, nl, 0.0)
                hl_ref[s, j, rows, :] = nl.astype(BF16)
                totals[r0] = [jnp.sum(nl[:, h * kb:(h + 1) * kb], axis=1, keepdims=True)
                              for h in range(2)]
            return jnp.dot(hl_ref[s, j, row0:], ud, preferred_element_type=F32), totals

        def weights(s, j, sums):
            cum, totals = sums
            a_slot = slot + j // nwalk
            cols = slice((j % nwalk) * kb, (j % nwalk + 1) * kb)
            row0 = first_row(j)
            for r0 in range(0, tq, rc):
                rows = slice(r0, r0 + rc)
                kind, strict = visibility(j, r0)
                if kind == 0:
                    for h in range(2):
                        a_ref[s, a_slot, h, rows, cols] = jnp.zeros((rc, kb), BF16)
                    continue
                car = [car_ref[s, h, rows, :] for h in range(2)]
                a = jnp.exp2(lb_ref[s, j, rows, :] - cum[r0 - row0:r0 - row0 + rc]
                             - jnp.concatenate(car, axis=1))
                if kind == 1:
                    a = jnp.where(strict, a, 0.0)
                for h in range(2):
                    a_ref[s, a_slot, h, rows, cols] = a[:, h * kb:(h + 1) * kb].astype(BF16)
                    car_ref[s, h, rows, :] = car[h] + totals[r0][h]

        z_next = ([scores(s, base + (n - 1) * kb, first_row(n - 1)) for s in streams]
                  if z_first is None else z_first)
        pending = None
        for j in reversed(range(n)):
            z = z_next
            if j > 0:
                z_next = [scores(s, base + (j - 1) * kb, first_row(j - 1)) for s in streams]
            if j == n - 1 and owed is not None:
                apply_weights(*owed)
            sums = [log_terms(s, j, z[s]) for s in streams]
            if pending is not None:
                for s in streams:
                    weights(s, j + 1, pending[s])
                if (j + 1) % nwalk == 0:
                    apply_weights(slot + (j + 1) // nwalk, base + (j + 1) * kb,
                                  first_row(j + 1))
            pending = sums
            if j == 0:
                for s in streams:
                    z_ref[s] = scores(s, jnp.maximum(base - kb, 0))
        for s in streams:
            weights(s, 0, pending[s])

    kw = nwalk * kb

    def apply_weights(slot, start, row0=0):
        for s in streams:
            v_n = v_ref[0, pl.ds(pl.multiple_of(start, kb), kw), scols(s)]
            for h in range(2):
                acc_ref[s, h, row0:, :] += jnp.dot(a_ref[s, slot, h, row0:, :], v_n,
                                                   preferred_element_type=F32)

    OWN = 2
    super_tile(i * kt, nsub, True, OWN, None, None)

    def owed_after(t):
        return (jnp.where(t == 0, OWN, (t + 1) % 2),
                jnp.where(t == 0, i * kt, i * kt - t * kw))

    def live():
        return (jnp.min(car_ref[...]) < EXP2_UNDERFLOW).astype(jnp.int32)

    def body(carry):
        t, _ = carry
        super_tile(i * kt - (t + 1) * kw, nwalk, False, t % 2, owed_after(t),
                   [z_ref[s] for s in streams])
        return t + 1, live()

    n_walk = i * (kt // kw)
    t_end, _ = lax.while_loop(lambda c: (c[0] < n_walk) & (c[1] > 0), body, (jnp.int32(0), live()))
    apply_weights(*owed_after(t_end))
    for s in streams:
        o_ref[0, :, scols(s)] = jnp.where(lane < HEAD_DIM, acc_ref[s, 0],
                                          acc_ref[s, 1]).astype(o_ref.dtype)


def _sba_attention(qkv, b, s, tq=512, kb=LANES, nsub=4, nwalk=2, rc=64, ns=4):
    assert tq == kb * nsub and s % tq == 0 and tq % rc == 0 and nsub % nwalk == 0
    n_blocks = D_MODEL // (ns * LANES)
    wide = ns * LANES
    assert kb & (kb - 1) == 0 and D_MODEL % wide == 0
    jj = np.arange(kb)
    uu = jnp.asarray(np.kron(np.eye(2), (jj[:, None] > jj[None, :]).astype(np.float32)), BF16)
    return pl.pallas_call(
        functools.partial(_sba_attn_kernel, tq=tq, kb=kb, nsub=nsub, nwalk=nwalk, rc=rc),
        grid=(b, n_blocks, s // tq),
        in_specs=[pl.BlockSpec((1, tq, wide), lambda bi, p, i: (bi, i, p)),
                  pl.BlockSpec((1, s, wide), lambda bi, p, i: (bi, 0, n_blocks + p)),
                  pl.BlockSpec((1, s, wide), lambda bi, p, i: (bi, 0, 2 * n_blocks + p)),
                  _const_spec((2 * kb, 2 * kb))],
        out_specs=pl.BlockSpec((1, tq, wide), lambda bi, p, i: (bi, i, p)),
        out_shape=jax.ShapeDtypeStruct((b, s, D_MODEL), BF16),
        scratch_shapes=[pltpu.VMEM((ns, 2, tq, LANES), F32),
                        pltpu.VMEM((ns, 2, tq, LANES), F32),
                        pltpu.VMEM((ns, nsub, tq, 2 * kb), BF16),
                        pltpu.VMEM((ns, nsub, tq, 2 * kb), F32),
                        pltpu.VMEM((ns, 2 + nsub // nwalk, 2, tq, nwalk * kb), BF16),
                        pltpu.VMEM((ns, tq, 2 * kb), F32)],
        compiler_params=_params(("arbitrary", "arbitrary", "arbitrary")),
        name="sba_attention",
    )(qkv, qkv, qkv, uu)


def _ffn_kernel(x_ref, o_ref, wo_ref, g_ref, wup_ref, cw_ref, cb_ref, wdn_ref, gf_ref, y_ref,
                carry_ref, sg_ref, sv_ref, act_ref, *, tm, fc, tiles_per_seq, final_norm):
    @pl.when(pl.program_id(0) % tiles_per_seq == 0)
    def _():
        carry_ref[...] = jnp.zeros_like(carry_ref)

    x = x_ref[...] + jnp.dot(o_ref[...], wo_ref[...], preferred_element_type=F32)
    hn = _rmsnorm(x, g_ref[...]).astype(wup_ref.dtype)

    def up(col0):
        return jnp.dot(hn, wup_ref[:, col0:col0 + fc], preferred_element_type=F32)

    def conv(u, col0, s_ref):
        cols = slice(col0, col0 + fc)
        s_ref[0:8, :] = carry_ref[:, cols]
        s_ref[8:tm + 8, :] = u
        carry_ref[:, cols] = u[tm - 8:tm, :]
        cw = cw_ref[:, cols]
        c = cb_ref[:, cols] + s_ref[6:tm + 6, :] * cw[0:1]
        c = c + s_ref[7:tm + 7, :] * cw[1:2]
        return c + u * cw[2:3]

    n_chunks = D_FF // fc
    u_next = (up(0), up(D_FF))
    for c in range(n_chunks):
        u_gate, u_val = u_next
        if c + 1 < n_chunks:
            u_next = (up((c + 1) * fc), up(D_FF + (c + 1) * fc))
        gate = conv(u_gate, c * fc, sg_ref)
        val = conv(u_val, D_FF + c * fc, sv_ref)
        act_ref[:, c * fc:(c + 1) * fc] = (gate * jax.nn.sigmoid(gate) * val).astype(act_ref.dtype)
    y = x + jnp.dot(act_ref[...], wdn_ref[...], preferred_element_type=F32)
    if final_norm:
        y = _rmsnorm(y, gf_ref[...])
    y_ref[...] = y


def _conv_ffn(x2, o2, w_out, g, w_up, conv_w, conv_b, w_down, g_final, s, final_norm,
              tm=256, fc=256):
    t, d = x2.shape
    f2 = w_up.shape[1]
    return pl.pallas_call(
        functools.partial(_ffn_kernel, tm=tm, fc=fc, tiles_per_seq=s // tm, final_norm=final_norm),
        grid=(t // tm,),
        in_specs=[pl.BlockSpec((tm, d), lambda i: (i, 0)),
                  pl.BlockSpec((tm, d), lambda i: (i, 0)),
                  _const_spec((d, d)),
                  _const_spec((1, d)),
                  _const_spec((d, f2)),
                  _const_spec((3, f2)),
                  _const_spec((1, f2)),
                  _const_spec((D_FF, d)),
                  _const_spec((1, d))],
        out_specs=pl.BlockSpec((tm, d), lambda i: (i, 0)),
        out_shape=jax.ShapeDtypeStruct((t, d), F32),
        scratch_shapes=[pltpu.VMEM((8, f2), F32),
                        pltpu.VMEM((tm + 8, fc), F32),
                        pltpu.VMEM((tm + 8, fc), F32),
                        pltpu.VMEM((tm, D_FF), BF16)],
        compiler_params=pltpu.CompilerParams(
            dimension_semantics=("arbitrary",), vmem_limit_bytes=VMEM_LIMIT,
            allow_input_fusion=[False, False, True, False, True, False, False, True, False]),
        name="conv_ffn",
    )(x2, o2, w_out, g, w_up, conv_w, conv_b, w_down, g_final)


NSA_Q_COLS = D_MODEL
NSA_W_COLS = D_MODEL + 7 * LANES
NSA_GATES_PER_GROUP = 3 * NSA_HPG


def _swap_halves(y):
    lane = lax.broadcasted_iota(jnp.int32, y.shape, 1)
    first = (lane % HEAD_DIM) < HALF
    return jnp.where(first, pltpu.roll(y, LANES - HALF, 1), pltpu.roll(y, HALF, 1))


def _nsa_in_proj_kernel(x_ref, g_ref, w_ref, pos_ref, inv_ref, sgn_ref,
                        q_ref, kc_ref, vc_ref, ksa_ref, vsa_ref, kwa_ref, vwa_ref, gt_ref,
                        *, tm, seq):
    hn = _rmsnorm(x_ref[...], g_ref[...]).astype(w_ref.dtype)
    ang = pos_ref[...].astype(F32) * inv_ref[...]
    cos = jnp.cos(ang)
    sin = jnp.sin(ang) * sgn_ref[...]

    def rope(y):
        return y * cos + _swap_halves(y) * sin

    q_chunk = 4 * LANES
    for c in range(NSA_Q_COLS // q_chunk):
        y = jnp.dot(hn, w_ref[:, c * q_chunk:(c + 1) * q_chunk], preferred_element_type=F32)
        for l in range(q_chunk // LANES):
            yl = rope(y[:, l * LANES:(l + 1) * LANES]) * Q_SCALE
            q_ref[:, c * q_chunk + l * LANES:c * q_chunk + (l + 1) * LANES] = yl.astype(q_ref.dtype)
    y = jnp.dot(hn, w_ref[:, NSA_Q_COLS:], preferred_element_type=F32)
    part = lambda n: y[:, n * LANES:(n + 1) * LANES]

    tok = ((pl.program_id(0) % (seq // tm)) * tm
           + lax.broadcasted_iota(jnp.int32, (tm, HEAD_DIM), 0))
    lane = lax.broadcasted_iota(jnp.int32, (tm, HEAD_DIM), 1)
    blk_onehot = jnp.where((tok >> SLC_SHIFT) == lane, 1.0, 0.0)
    ones_col = jnp.where(lane == 0, 1.0, 0.0)
    zeros = jnp.zeros((tm, HEAD_DIM), F32)
    ks, vs, kw, vw = rope(part(2)), part(3), rope(part(4)), part(5)
    for g in range(NSA_KV_GROUPS):
        cols = slice(g * HEAD_DIM, (g + 1) * HEAD_DIM)
        kc_ref[g] = part(0)[:, cols]
        vc_ref[g] = part(1)[:, cols]
        ksa_ref[g] = jnp.concatenate([ks[:, cols], blk_onehot], axis=1).astype(ksa_ref.dtype)
        vsa_ref[g] = jnp.concatenate([vs[:, cols], ones_col], axis=1).astype(vsa_ref.dtype)
        kwa_ref[g] = jnp.concatenate([kw[:, cols], zeros], axis=1).astype(kwa_ref.dtype)
        vwa_ref[g] = jnp.concatenate([vw[:, cols], zeros, jnp.ones((tm, LANES), F32)],
                                     axis=1).astype(vwa_ref.dtype)
    gates = jax.nn.sigmoid(part(6))
    gt_ref[0] = gates
    gt_ref[1] = pltpu.roll(gates, LANES - NSA_GATES_PER_GROUP, 1)


def _nsa_in_proj(x2, g, w, pos2, inv2, sgn2, seq, tm=512):
    t, d = x2.shape
    G = NSA_KV_GROUPS
    tok_g = lambda dt: jax.ShapeDtypeStruct((G, t, HEAD_DIM), dt)
    aug_g = lambda dt: jax.ShapeDtypeStruct((G, t, LANES), dt)
    tok_spec = pl.BlockSpec((G, tm, HEAD_DIM), lambda i: (0, i, 0))
    aug_spec = pl.BlockSpec((G, tm, LANES), lambda i: (0, i, 0))
    return pl.pallas_call(
        functools.partial(_nsa_in_proj_kernel, tm=tm, seq=seq),
        grid=(t // tm,),
        in_specs=[pl.BlockSpec((tm, d), lambda i: (i, 0)),
                  _const_spec((1, d)),
                  _const_spec((d, NSA_W_COLS)),
                  pl.BlockSpec((tm, 1), lambda i: (i, 0)),
                  _const_spec((1, LANES)),
                  _const_spec((1, LANES))],
        out_specs=[pl.BlockSpec((tm, NSA_Q_COLS), lambda i: (i, 0)),
                   tok_spec, tok_spec, aug_spec, aug_spec, aug_spec,
                   pl.BlockSpec((G, tm, 2 * LANES), lambda i: (0, i, 0)), aug_spec],
        out_shape=[jax.ShapeDtypeStruct((t, NSA_Q_COLS), BF16),
                   tok_g(F32), tok_g(F32), aug_g(BF16), aug_g(BF16), aug_g(BF16),
                   jax.ShapeDtypeStruct((G, t, 2 * LANES), BF16), aug_g(F32)],
        compiler_params=_params(("arbitrary",)),
        name="nsa_in_proj",
    )(x2, g, w, pos2, inv2, sgn2)


def _nsa_compress_kernel(k16_ref, v16_ref, pek_ref, pev_ref, w1k_ref, w2k_ref, w2kr_ref,
                         w1v_ref, w2v_ref, pos_ref, inv_ref, kct_ref, vc_ref):
    half_w = CMP_STRIDE * HEAD_DIM
    nrow = k16_ref.shape[2]

    def hidden(x16_ref, pe_ref, w1_ref):
        x = x16_ref[0, 0]
        y1 = jnp.dot(x, w1_ref[:half_w, :], precision=HIGHEST, preferred_element_type=F32)
        y2 = jnp.dot(x, w1_ref[half_w:, :], precision=HIGHEST, preferred_element_type=F32)
        bias = jnp.dot(pe_ref[...], w1_ref[...], precision=HIGHEST, preferred_element_type=F32)
        return jax.nn.gelu(y1 + pltpu.roll(y2, nrow - 1, 0) + bias[0:1])

    hk = hidden(k16_ref, pek_ref, w1k_ref)
    kc = jnp.dot(hk, w2k_ref[...], precision=HIGHEST, preferred_element_type=F32)
    kc_rot = jnp.dot(hk, w2kr_ref[...], precision=HIGHEST, preferred_element_type=F32)
    ang = pos_ref[0].astype(F32) * inv_ref[...]
    kc = kc * jnp.cos(ang) + kc_rot * jnp.sin(ang)
    hi, lo = _split_bf16(kc)
    kct_ref[0, 0] = jnp.concatenate([hi.astype(F32), lo.astype(F32)], axis=1).T.astype(kct_ref.dtype)
    hv = hidden(v16_ref, pev_ref, w1v_ref)
    vc = jnp.dot(hv, w2v_ref[...], precision=HIGHEST, preferred_element_type=F32)
    vc_ref[0, 0] = jnp.concatenate([vc, jnp.zeros_like(vc)], axis=1).astype(vc_ref.dtype)


def _nsa_compress(k16, v16, pek, pev, w1k, w2k, w2kr, w1v, w2v, pos_cmp, inv64):
    G, b, nrow, wide = k16.shape
    x_spec = pl.BlockSpec((1, 1, nrow, wide), lambda bi, g: (g, bi, 0, 0))
    return pl.pallas_call(
        _nsa_compress_kernel,
        grid=(b, G),
        in_specs=[x_spec, x_spec,
                  _const_spec(pek.shape), _const_spec(pev.shape),
                  _const_spec(w1k.shape), _const_spec(w2k.shape), _const_spec(w2kr.shape),
                  _const_spec(w1v.shape), _const_spec(w2v.shape),
                  pl.BlockSpec((1, nrow, 1), lambda bi, g: (bi, 0, 0)),
                  _const_spec((1, HEAD_DIM))],
        out_specs=[pl.BlockSpec((1, 1, LANES, nrow), lambda bi, g: (bi, g, 0, 0)),
                   pl.BlockSpec((1, 1, nrow, LANES), lambda bi, g: (bi, g, 0, 0))],
        out_shape=[jax.ShapeDtypeStruct((b, G, LANES, nrow), BF16),
                   jax.ShapeDtypeStruct((b, G, nrow, LANES), BF16)],
        compiler_params=_params(("arbitrary", "arbitrary")),
        name="nsa_compress",
    )(k16, v16, pek, pev, w1k, w2k, w2kr, w1v, w2v, pos_cmp, inv64)


def _nsa_attn_kernel(q_ref, kct_ref, vc_ref, ov_ref, ks_ref, vs_ref, kw_ref, vw_ref, gt_ref,
                     gsel_ref, o_ref, qa_ref, oc_ref, psum_ref, m_ref, acc_ref, alpha_ref, p_ref, pw_ref,
                     ow_ref, gs_ref,
                     *, tq, kb, n_slc, wg, rc):
    i = pl.program_id(2)
    q0 = pl.multiple_of(i * tq, tq)
    n_rows = NSA_HPG * tq
    heads = range(NSA_HPG)
    hrows = lambda h: slice(h * tq, (h + 1) * tq)
    qrow = q0 + lax.broadcasted_iota(jnp.int32, (tq, 1), 0)
    q_heads = [q_ref[0, :, h * HEAD_DIM:(h + 1) * HEAD_DIM] for h in heads]
    for h in heads:
        qa_ref[hrows(h), :] = jnp.concatenate([q_heads[h], q_heads[h]], axis=1)

    n_cmp_rows = kct_ref.shape[3]
    s_all = jnp.dot(qa_ref[...], kct_ref[0, 0], preferred_element_type=F32)
    w_keys = WINDOW + tq
    w_start = pl.multiple_of(jnp.maximum(q0 - WINDOW, 0), tq)
    k_w = kw_ref[0, 0, pl.ds(w_start, w_keys), :]
    s_win = [lax.dot_general(qa_ref[h0 * tq:(h0 + wg) * tq, :], k_w, NT_DIMS,
                             preferred_element_type=F32) for h0 in range(0, NSA_HPG, wg)]

    cmp_end = CMP_STRIDE * lax.broadcasted_iota(jnp.int32, (1, n_cmp_rows), 1) + (CMP_LEN - 1)
    c_bias = jnp.where(cmp_end <= qrow, 0.0, NEG)
    row_valid = jnp.where(qrow >= CMP_LEN - 1, 1.0, 0.0)
    for r0 in range(0, n_rows, rc):
        rows = slice(r0, r0 + rc)
        local = slice(r0 % tq, r0 % tq + rc)
        s = s_all[rows] + c_bias[local]
        e = jnp.exp2(s - jnp.max(s, axis=1, keepdims=True))
        p = e * (row_valid[local] / jnp.sum(e, axis=1, keepdims=True))
        if r0 < tq:
            psum_ref[local, :] = p
        else:
            psum_ref[local, :] += p
        p_ref[0, rows, 0:n_cmp_rows] = p.astype(p_ref.dtype)
    g_wide = jnp.dot(jnp.concatenate(_split_bf16(gt_ref[0]), axis=1), gsel_ref[...],
                     preferred_element_type=F32)
    gate = lambda h, branch: g_wide[:, (3 * h + branch) * LANES:(3 * h + branch + 1) * LANES]
    o_cmp = jnp.dot(p_ref[0, :, 0:n_cmp_rows], vc_ref[0, 0], preferred_element_type=F32)
    for h in heads:
        oc_ref[hrows(h), :] = gate(h, 0) * o_cmp[hrows(h)]
        gs_ref[hrows(h), :] = gate(h, 1)
    imp = jnp.dot(jnp.concatenate(_split_bf16(psum_ref[...]), axis=1), ov_ref[...],
                  preferred_element_type=F32)
    imp_t = imp.T[:n_slc]

    w_diff = qrow - (w_start + lax.broadcasted_iota(jnp.int32, (1, w_keys), 1))
    w_bias = jnp.where((w_diff >= 0) & (w_diff < WINDOW), 0.0, NEG)
    for r0 in range(0, n_rows, rc):
        s = s_win[r0 // (wg * tq)][r0 % (wg * tq):r0 % (wg * tq) + rc] + w_bias[r0 % tq:r0 % tq + rc]
        pw_ref[r0:r0 + rc, :] = jnp.exp2(s - jnp.max(s, axis=1, keepdims=True)).astype(pw_ref.dtype)
    o_win = jnp.dot(pw_ref[...], vw_ref[0, 0, pl.ds(w_start, w_keys), :],
                    preferred_element_type=F32)
    for h in heads:
        o_h = o_win[hrows(h)]
        ow_ref[hrows(h), :] = (gate(h, 2) * o_h[:, :LANES]) * (1.0 / o_h[:, LANES:])

    qpos = q0 + lax.broadcasted_iota(jnp.int32, (1, tq), 1)
    blk = lax.broadcasted_iota(jnp.int32, (n_slc, 1), 0)
    cur = qpos >> SLC_SHIFT
    forced = (blk == 0) | (blk == cur) | (blk == cur - 1)
    causal_blk = blk * SLC_LEN <= qpos
    score = jnp.where(causal_blk, imp_t + FORCE_BONUS * forced.astype(F32), NEG)
    sub = lax.broadcasted_iota(jnp.int32, (8, 1), 0)
    groups = [score[8 * v:8 * v + 8] for v in range(n_slc // 8)]
    counts = [jnp.zeros((8, tq), F32) for _ in groups]
    for m in range(n_slc):
        row = score[m:m + 1]
        for v, sv in enumerate(groups):
            if v < m // 8:
                ahead = row > sv
            elif v > m // 8:
                ahead = row >= sv
            else:
                ahead = (row > sv) | ((row == sv) & (sub > m % 8))
            counts[v] = counts[v] + jnp.where(ahead, 1.0, 0.0)
    rank = jnp.concatenate(counts, axis=0)
    sel_bias_t = jnp.where(rank < SLC_TOP_N, 0.0, NEG)
    sel_bias = jnp.concatenate([sel_bias_t, jnp.zeros((LANES - n_slc, tq), F32)], axis=0).T
    sel_bias = sel_bias[:, :HEAD_DIM].astype(qa_ref.dtype)
    for h in heads:
        qa_ref[hrows(h), :] = jnp.concatenate([q_heads[h], sel_bias], axis=1)

    tok_bias = jnp.where(lax.broadcasted_iota(jnp.int32, (1, kb), 1)
                         <= lax.broadcasted_iota(jnp.int32, (tq, 1), 0), 0.0, NEG)

    def scores(start):
        k_t = ks_ref[0, 0, pl.ds(start, kb), :]
        return lax.dot_general(qa_ref[...], k_t, NT_DIMS, preferred_element_type=F32)

    def softmax(s_all, slot, own_keys):
        for r0 in range(0, n_rows, rc):
            rows = slice(r0, r0 + rc)
            s = s_all[rows]
            if own_keys:
                s = s + tok_bias[r0 % tq:r0 % tq + rc]
                m_new = jnp.broadcast_to(jnp.max(s, axis=1, keepdims=True), (rc, LANES))
            else:
                m_old = m_ref[rows, :]
                m_new = jnp.maximum(m_old, jnp.max(s, axis=1, keepdims=True))
                alpha_ref[slot, rows, :] = jnp.exp2(m_old - m_new)
            m_ref[rows, :] = m_new
            m_wide = jnp.concatenate([m_new] * (kb // LANES), axis=1)
            p_ref[slot, rows, :] = jnp.exp2(s - m_wide).astype(p_ref.dtype)

    def values(start, slot):
        v_t = vs_ref[0, 0, pl.ds(start, kb), :]
        pv = jnp.dot(p_ref[slot], v_t, preferred_element_type=F32)
        for r0 in range(0, n_rows, rc):
            rows = slice(r0, r0 + rc)
            acc_ref[rows, :] = alpha_ref[slot, rows, :] * acc_ref[rows, :] + pv[rows]

    chunk_start = lambda c: pl.multiple_of(c * kb, kb)
    owed_start = lambda c: pl.multiple_of(jnp.where(c == 0, q0, (c - 1) * kb), kb)

    acc_ref[...] = jnp.zeros_like(acc_ref)
    alpha_ref[1] = jnp.zeros(alpha_ref.shape[1:], F32)
    softmax(scores(q0), 1, True)

    def pair_body(t, carry):
        c = 2 * t
        s_a = scores(chunk_start(c))
        values(owed_start(c), 1)
        softmax(s_a, 0, False)
        s_b = scores(chunk_start(c + 1))
        values(chunk_start(c), 0)
        softmax(s_b, 1, False)
        return carry

    lax.fori_loop(0, i // 2, pair_body, 0)

    @pl.when(i % 2 == 1)
    def _():
        s_a = scores(chunk_start(i - 1))
        values(owed_start(i - 1), 1)
        softmax(s_a, 0, False)
        values(chunk_start(i - 1), 0)

    @pl.when(i % 2 == 0)
    def _():
        values(owed_start(i), 1)

    outs = []
    for h in heads:
        o_h = acc_ref[hrows(h), :]
        o_sel = (gs_ref[hrows(h), :] * o_h) * (1.0 / o_h[:, HEAD_DIM:HEAD_DIM + 1])
        outs.append(((oc_ref[hrows(h), :] + o_sel) + ow_ref[hrows(h), :])[:, :HEAD_DIM])
    for hp in range(NSA_HPG // 2):
        pair = jnp.concatenate(outs[2 * hp:2 * hp + 2], axis=1)
        o_ref[0, :, hp * LANES:(hp + 1) * LANES] = pair.astype(o_ref.dtype)


def _nsa_attention(q3, kct, vc, ov, ksa, vsa, kwa, vwa, gt, tq=256, wg=4, rc=64):
    b, s, _ = q3.shape
    G = NSA_KV_GROUPS
    kb = tq
    n_slc = s // SLC_LEN
    nq = s // tq
    n_cmp_rows = kct.shape[3]
    gw = NSA_HPG * HEAD_DIM
    assert n_slc <= HEAD_DIM and tq % SLC_LEN == 0 and WINDOW % tq == 0 and s >= WINDOW + tq
    assert n_cmp_rows <= kb and n_slc % 8 == 0 and tq % rc == 0
    kv_spec = pl.BlockSpec((1, 1, s, LANES), lambda bi, g, i: (g, bi, 0, 0))
    n_gates = NSA_GATES_PER_GROUP
    gsel = np.zeros((2 * LANES, n_gates * LANES), np.float32)
    for c in range(n_gates):
        gsel[[c, LANES + c], c * LANES:(c + 1) * LANES] = 1.0
    gsel = jnp.asarray(gsel, BF16)
    return pl.pallas_call(
        functools.partial(_nsa_attn_kernel, tq=tq, kb=kb, n_slc=n_slc, wg=wg, rc=rc),
        grid=(b, G, nq),
        in_specs=[pl.BlockSpec((1, tq, gw), lambda bi, g, i: (bi, i, g)),
                  pl.BlockSpec((1, 1, LANES, n_cmp_rows), lambda bi, g, i: (bi, g, 0, 0)),
                  pl.BlockSpec((1, 1, n_cmp_rows, LANES), lambda bi, g, i: (bi, g, 0, 0)),
                  _const_spec(ov.shape),
                  kv_spec, kv_spec, kv_spec,
                  pl.BlockSpec((1, 1, s, 2 * LANES), lambda bi, g, i: (g, bi, 0, 0)),
                  pl.BlockSpec((1, tq, LANES), lambda bi, g, i: (g, bi * nq + i, 0)),
                  _const_spec(gsel.shape)],
        out_specs=pl.BlockSpec((1, tq, gw), lambda bi, g, i: (bi, i, g)),
        out_shape=jax.ShapeDtypeStruct((b, s, D_MODEL), BF16),
        scratch_shapes=[pltpu.VMEM((NSA_HPG * tq, LANES), BF16),
                        pltpu.VMEM((NSA_HPG * tq, LANES), F32),
                        pltpu.VMEM((tq, n_cmp_rows), F32),
                        pltpu.VMEM((NSA_HPG * tq, LANES), F32),
                        pltpu.VMEM((NSA_HPG * tq, LANES), F32),
                        pltpu.VMEM((2, NSA_HPG * tq, LANES), F32),
                        pltpu.VMEM((2, NSA_HPG * tq, kb), BF16),
                        pltpu.VMEM((NSA_HPG * tq, WINDOW + tq), BF16),
                        pltpu.VMEM((NSA_HPG * tq, LANES), F32),
                        pltpu.VMEM((NSA_HPG * tq, LANES), F32)],
        compiler_params=_params(("arbitrary", "arbitrary", "arbitrary")),
        name="nsa_attention",
    )(q3, kct, vc, ov, ksa, vsa, kwa, vwa, gt, gsel)


def _nsa_layer_attention(hx2, norm_g, positions, w_in, pe_k, pe_v, w1k, w2k, w1v, w2v, b, s):
    t = b * s
    G = NSA_KV_GROUPS
    w_pad = jnp.pad(w_in, ((0, 0), (0, NSA_W_COLS - w_in.shape[1]))).astype(BF16)
    inv = ROPE_THETA ** (-jnp.arange(HALF, dtype=F32) / HALF)
    inv2 = jnp.tile(inv, LANES // HALF)[None, :]
    sgn2 = jnp.tile(jnp.concatenate([-jnp.ones(HALF, F32), jnp.ones(HALF, F32)]), LANES // HEAD_DIM)[None, :]
    q, kc_tok, vc_tok, ksa, vsa, kwa, vwa, gt = _nsa_in_proj(
        hx2, norm_g, w_pad, positions.reshape(t, 1), inv2, sgn2, s)

    nrow = s // CMP_STRIDE
    wide = CMP_STRIDE * HEAD_DIM
    k16 = kc_tok.reshape(G, b, nrow, wide)
    v16 = vc_tok.reshape(G, b, nrow, wide)
    pek = jnp.broadcast_to(pe_k.reshape(1, CMP_LEN * HEAD_DIM), (8, CMP_LEN * HEAD_DIM))
    pev = jnp.broadcast_to(pe_v.reshape(1, CMP_LEN * HEAD_DIM), (8, CMP_LEN * HEAD_DIM))
    w2k_rot = jnp.concatenate([-w2k[:, HALF:], w2k[:, :HALF]], axis=1)
    end_idx = jnp.minimum(jnp.arange(nrow) * CMP_STRIDE + CMP_LEN - 1, s - 1)
    pos_cmp = positions[:, end_idx][:, :, None]
    inv64 = jnp.tile(inv, 2)[None, :]
    kct, vc = _nsa_compress(k16, v16, pek, pev, w1k, w2k, w2k_rot, w1v, w2v, pos_cmp, inv64)

    n_slc = s // SLC_LEN
    c0 = np.arange(nrow)[:, None] * CMP_STRIDE
    s0 = np.arange(n_slc)[None, :] * SLC_LEN
    ov = np.clip(np.minimum(c0 + CMP_LEN, s0 + SLC_LEN) - np.maximum(c0, s0), 0, None) / CMP_LEN
    ov[(s - CMP_LEN) // CMP_STRIDE + 1:, :] = 0.0
    ov = np.pad(ov, ((0, 0), (0, LANES - n_slc)))
    ov = jnp.asarray(np.concatenate([ov, ov], axis=0), BF16)

    aug4 = lambda a: a.reshape(G, b, s, a.shape[-1])
    return _nsa_attention(q.reshape(b, s, D_MODEL), kct, vc, ov, aug4(ksa), aug4(vsa),
                          aug4(kwa), aug4(vwa), gt)


def kernel(x, positions, norm_mix, sba_w_in, sba_w_out, nsa_w_in, nsa_cmp_pos_k, nsa_cmp_pos_v,
           nsa_cmp_k_w1, nsa_cmp_k_w2, nsa_cmp_v_w1, nsa_cmp_v_w2, nsa_w_out, norm_ffn,
           ffn_w_up, ffn_conv_w, ffn_conv_b, ffn_w_down, norm_final):
    b, s, d = x.shape
    t = b * s
    depth = norm_mix.shape[0]
    x2 = x.reshape(t, d)
    g_final = norm_final.reshape(1, d)
    for layer in range(depth):
        j = layer // 2
        g_mix = norm_mix[layer].reshape(1, d)
        if layer % 2 == 0:
            qkv = _sba_in_proj(x2, g_mix, sba_w_in[j].astype(BF16))
            o = _sba_attention(qkv.reshape(b, s, 3 * d), b, s)
            w_out = sba_w_out[j]
        else:
            o = _nsa_layer_attention(x2, g_mix, positions, nsa_w_in[j], nsa_cmp_pos_k[j],
                                     nsa_cmp_pos_v[j], nsa_cmp_k_w1[j], nsa_cmp_k_w2[j],
                                     nsa_cmp_v_w1[j], nsa_cmp_v_w2[j], b, s)
            w_out = nsa_w_out[j]
        x2 = _conv_ffn(x2, o.reshape(t, d), w_out.astype(BF16),
                       norm_ffn[layer].reshape(1, d), ffn_w_up[layer].astype(BF16),
                       ffn_conv_w[layer], ffn_conv_b[layer].reshape(1, -1),
                       ffn_w_down[layer].astype(BF16), g_final, s,
                       final_norm=(layer == depth - 1))
    return x2.reshape(b, s, d)
```
